```python
import jax, jax.numpy as jnp
from jax import lax
import numpy as np

D_MODEL = 1024
BATCH = 4
SEQ = 4096
DEPTH = 1

HEAD_DIM = 128
DN_HEADS = 4
DN_WIDTH = DN_HEADS * HEAD_DIM
DN_CONV = 4
DN_CHUNK = 64
MB_HEADS = 4
MB_WIDTH = MB_HEADS * HEAD_DIM
MB_BLOCK = 256
MB_TOPK = 3
MB_QCHUNK = 32
ROPE_THETA = 500000.0
ROPE_DIM = HEAD_DIM // 4
N_EXPERTS = 32
TOP_K = 4
D_FF = D_MODEL
SWIGLU_LIMIT = 7.0
SWIGLU_ALPHA = 1.702
MOE_ROWS = 256
NORM_EPS = 1e-6
IN_SPLITS = (DN_WIDTH, DN_WIDTH, DN_WIDTH, DN_WIDTH, DN_HEADS, DN_HEADS,
             MB_WIDTH, MB_WIDTH, MB_WIDTH, D_MODEL, D_MODEL)
IN_WIDTH = 4 * DN_WIDTH + 2 * DN_HEADS + 3 * MB_WIDTH + 2 * D_MODEL

kernel_name = 'hybrid_deltanet_moba_moe_block'


def rmsnorm(x, g):
    xf = x.astype(jnp.float32)
    y = xf * lax.rsqrt(jnp.mean(xf * xf, axis=-1, keepdims=True) + NORM_EPS)
    return (y * g.astype(jnp.float32)).astype(x.dtype)


def l2norm(x):
    return x * lax.rsqrt(jnp.sum(x * x, axis=-1, keepdims=True) + 1e-6)


def partial_rope(x):
    S = x.shape[1]
    half = ROPE_DIM // 2
    inv_freq = ROPE_THETA ** (-jnp.arange(half, dtype=jnp.float32) / half)
    ang = jnp.arange(S, dtype=jnp.float32)[:, None] * inv_freq[None, :]
    cos = jnp.cos(ang)[None, :, None, :]
    sin = jnp.sin(ang)[None, :, None, :]
    x1 = x[..., :half]
    x2 = x[..., half:ROPE_DIM]
    return jnp.concatenate([x1 * cos - x2 * sin, x2 * cos + x1 * sin, x[..., ROPE_DIM:]], axis=-1)


def causal_depthwise_conv(x, w):
    K, C = w.shape
    return lax.conv_general_dilated(x, w[:, None, :], window_strides=(1,), padding=[(K - 1, 0)],
                                    dimension_numbers=('NWC', 'WIO', 'NWC'), feature_group_count=C)


def gated_delta_rule(q, k, v, g, beta):
    B, S, H, dk = q.shape
    dv = v.shape[-1]
    C = DN_CHUNK
    N = S // C

    def chunks(t):
        t = t.reshape((B, N, C, H) + t.shape[3:])
        return jnp.moveaxis(t, [1, 0, 3, 2], [0, 1, 2, 3])

    q = chunks(q) * (dk ** -0.5)
    k = chunks(k)
    v = chunks(v)
    g = chunks(g)
    beta = chunks(beta)
    gc = jnp.cumsum(g, axis=-1)
    tril = jnp.tril(jnp.ones((C, C), dtype=bool))
    strict = jnp.tril(jnp.ones((C, C), dtype=bool), -1)
    decay = jnp.exp(jnp.where(tril, gc[..., :, None] - gc[..., None, :], -jnp.inf))
    kb = k * beta[..., None]
    L = jnp.where(strict, jnp.einsum('nbhcd,nbhmd->nbhcm', kb, k) * decay, 0.0)
    A = L + jnp.eye(C, dtype=L.dtype)
    u = lax.linalg.triangular_solve(A, v * beta[..., None], left_side=True, lower=True)
    w = lax.linalg.triangular_solve(A, kb * jnp.exp(gc)[..., None], left_side=True, lower=True)
    qk = jnp.einsum('nbhcd,nbhmd->nbhcm', q, k) * decay
    g_last = gc[..., -1]
    k_tail = k * jnp.exp(g_last[..., None] - gc)[..., None]
    q_dec = q * jnp.exp(gc)[..., None]

    def step(state, xs):
        q_n, qk_n, u_n, w_n, kt_n, gl_n = xs
        v_new = u_n - jnp.einsum('bhcd,bhde->bhce', w_n, state)
        o = jnp.einsum('bhcd,bhde->bhce', q_n, state) + jnp.einsum('bhcm,bhme->bhce', qk_n, v_new)
        state = state * jnp.exp(gl_n)[..., None, None] + jnp.einsum('bhcd,bhce->bhde', kt_n, v_new)
        return state, o

    s0 = jnp.zeros((B, H, dk, dv), jnp.float32)
    _, o = lax.scan(step, s0, (q_dec, qk, u, w, k_tail, g_last))
    return jnp.moveaxis(o, [1, 0, 3, 2], [0, 1, 2, 3]).reshape(B, S, H, dv)


def moba_attention(q, k, v):
    B, S, H, d = q.shape
    nb = -(-S // MB_BLOCK)
    Sp = nb * MB_BLOCK
    pad = ((0, 0), (0, Sp - S), (0, 0), (0, 0))
    q, k, v = [jnp.pad(t, pad).transpose(0, 2, 1, 3) for t in (q, k, v)]
    k_blk = k.reshape(B, H, nb, MB_BLOCK, d)
    v_blk = v.reshape(B, H, nb, MB_BLOCK, d)
    k_mean = jnp.mean(k_blk, axis=3)
    pos = jnp.arange(Sp, dtype=jnp.int32)
    q_blk = pos // MB_BLOCK
    past = jnp.arange(nb, dtype=jnp.int32)[None, :] < q_blk[:, None]
    scores = jnp.where(past, jnp.einsum('bhsd,bhnd->bhsn', q, k_mean), -jnp.inf)
    topk = min(MB_TOPK, nb)
    _, top_idx = lax.top_k(scores, topk)
    sel = jnp.concatenate([top_idx.astype(jnp.int32),
                           jnp.broadcast_to(q_blk[:, None], (B, H, Sp, 1))], axis=-1)
    sel_ok = jnp.concatenate([jnp.arange(topk, dtype=jnp.int32)[None, :] < q_blk[:, None],
                              jnp.ones((Sp, 1), dtype=bool)], axis=-1)
    K1 = topk + 1
    nq = Sp // MB_QCHUNK

    def to_chunks(t):
        t = t.reshape(t.shape[:2] + (nq, MB_QCHUNK) + t.shape[3:])
        return jnp.moveaxis(t, 2, 0)

    b_idx = jnp.arange(B)[:, None, None, None]
    h_idx = jnp.arange(H)[None, :, None, None]
    offs = jnp.arange(MB_BLOCK, dtype=jnp.int32)
    scale = d ** -0.5

    def attend(args):
        q_i, sel_i, pos_i, ok_i = args
        k_g = k_blk[b_idx, h_idx, sel_i]
        v_g = v_blk[b_idx, h_idx, sel_i]
        logits = jnp.einsum('bhqd,bhqjkd->bhqjk', q_i, k_g) * scale
        k_pos = sel_i[..., None] * MB_BLOCK + offs
        allowed = (k_pos <= pos_i[:, None, None]) & ok_i[:, :, None]
        logits = jnp.where(allowed, logits, -jnp.inf)
        p = jax.nn.softmax(logits.reshape(logits.shape[:3] + (-1,)), axis=-1).reshape(logits.shape)
        return jnp.einsum('bhqjk,bhqjkd->bhqd', p, v_g)

    o = lax.map(attend, (to_chunks(q), to_chunks(sel), pos.reshape(nq, MB_QCHUNK),
                         sel_ok.reshape(nq, MB_QCHUNK, K1)))
    o = jnp.moveaxis(o, 0, 2).reshape(B, H, Sp, d)[:, :, :S]
    return o.transpose(0, 2, 1, 3)


def hybrid_mixer(h, w_in, conv_w, a_log, dt_bias, g_dn_out, w_br_a, w_br_b, w_o):
    B, S, _ = h.shape
    f32 = jnp.float32
    proj = h @ w_in
    cuts = [int(i) for i in np.cumsum(IN_SPLITS)[:-1]]
    dn_q, dn_k, dn_v, dn_z, dn_b, dn_a, mb_q, mb_k, mb_v, gate_a, gate_b = jnp.split(proj, cuts, axis=-1)

    def heads(t, n):
        return t.reshape(B, S, n, HEAD_DIM)

    qkv = jax.nn.silu(causal_depthwise_conv(jnp.concatenate([dn_q, dn_k, dn_v], axis=-1), conv_w))
    q_a, k_a, v_a = jnp.split(qkv.astype(f32), 3, axis=-1)
    q_a = l2norm(heads(q_a, DN_HEADS))
    k_a = l2norm(heads(k_a, DN_HEADS))
    v_a = heads(v_a, DN_HEADS)
    beta = jax.nn.sigmoid(dn_b.astype(f32))
    g = -jnp.exp(a_log.astype(f32)) * jax.nn.softplus(dn_a.astype(f32) + dt_bias.astype(f32))
    o_a = gated_delta_rule(q_a, k_a, v_a, g, beta)
    o_a = rmsnorm(o_a, g_dn_out) * jax.nn.silu(heads(dn_z.astype(f32), DN_HEADS))
    y_a = o_a.reshape(B, S, DN_WIDTH).astype(h.dtype) @ w_br_a

    q_b = partial_rope(heads(mb_q.astype(f32), MB_HEADS))
    k_b = partial_rope(heads(mb_k.astype(f32), MB_HEADS))
    v_b = heads(mb_v.astype(f32), MB_HEADS)
    o_b = moba_attention(q_b, k_b, v_b)
    y_b = o_b.reshape(B, S, MB_WIDTH).astype(h.dtype) @ w_br_b

    merged = jax.nn.sigmoid(gate_a) * y_a + jax.nn.sigmoid(gate_b) * y_b
    return merged @ w_o


def routed_moe(h, w_router, b_router, w_gu, b_gu, w_down, b_down):
    B, S, D = h.shape
    T = B * S
    x = h.reshape(T, D)
    logits = (x @ w_router).astype(jnp.float32) + b_router.astype(jnp.float32)
    top_val, top_idx = lax.top_k(logits, TOP_K)
    top_w = jax.nn.softmax(top_val, axis=-1)
    A = T * TOP_K
    flat_e = top_idx.reshape(A)
    order = jnp.argsort(flat_e)
    e_sorted = flat_e[order]
    tok_sorted = (order // TOP_K).astype(jnp.int32)
    w_sorted = top_w.reshape(A)[order]
    counts = jnp.bincount(flat_e, length=N_EXPERTS)
    padded = (counts + MOE_ROWS - 1) // MOE_ROWS * MOE_ROWS
    pad_end = jnp.cumsum(padded)
    pad_start = pad_end - padded
    grp_start = jnp.cumsum(counts) - counts
    dest = pad_start[e_sorted] + jnp.arange(A) - grp_start[e_sorted]
    n_blocks = -(-A // MOE_ROWS) + N_EXPERTS
    n_rows = n_blocks * MOE_ROWS
    row_tok = jnp.zeros((n_rows,), jnp.int32).at[dest].set(tok_sorted)
    row_w = jnp.zeros((n_rows,), jnp.float32).at[dest].set(w_sorted)
    blk_exp = jnp.minimum(jnp.searchsorted(pad_end, jnp.arange(n_blocks) * MOE_ROWS, side='right'),
                          N_EXPERTS - 1)

    def expert_block(args):
        tok, e = args
        gu = x[tok] @ w_gu[e] + b_gu[e]
        gate = jnp.minimum(gu[:, :D_FF], SWIGLU_LIMIT)
        up = jnp.clip(gu[:, D_FF:], -SWIGLU_LIMIT, SWIGLU_LIMIT)
        act = (up + 1) * gate * jax.nn.sigmoid(SWIGLU_ALPHA * gate)
        return act @ w_down[e] + b_down[e]

    y_rows = lax.map(expert_block, (row_tok.reshape(n_blocks, MOE_ROWS), blk_exp))
    y_rows = y_rows.reshape(n_rows, D).astype(jnp.float32) * row_w[:, None]
    out = jax.ops.segment_sum(y_rows, row_tok, num_segments=T)
    return out.reshape(B, S, D).astype(h.dtype)


def setup_inputs(seed: int = 0) -> dict:
    key = jax.random.key(seed)
    ks = jax.random.split(key, 24)
    f32 = jnp.float32
    L, D = DEPTH, D_MODEL

    def nrm(k, shape, fan_in):
        return jax.random.normal(k, shape, f32) * (fan_in ** -0.5)

    def gain(k, shape):
        return 1.0 + 0.05 * jax.random.normal(k, shape, f32)

    dt = jnp.exp(jax.random.uniform(ks[12], (L, DN_HEADS), f32, np.log(1e-3), np.log(1e-1)))
    return {
        'x': jax.random.normal(ks[0], (BATCH, SEQ, D), f32),
        'c': jax.random.normal(ks[1], (BATCH, D), f32),
        'w_ada': nrm(ks[2], (L, D, 6 * D), D),
        'b_ada': 0.02 * jax.random.normal(ks[3], (L, 6 * D), f32),
        'g_pre_mix': gain(ks[4], (L, D)),
        'g_post_mix': gain(ks[5], (L, D)),
        'g_pre_ffn': gain(ks[6], (L, D)),
        'g_post_ffn': gain(ks[7], (L, D)),
        'w_in': nrm(ks[8], (L, D, IN_WIDTH), D),
        'conv_w': nrm(ks[9], (L, DN_CONV, 3 * DN_WIDTH), DN_CONV),
        'a_log': jnp.log(jax.random.uniform(ks[10], (L, DN_HEADS), f32, 1.0, 16.0)),
        'dt_bias': dt + jnp.log(-jnp.expm1(-dt)),
        'g_dn_out': gain(ks[11], (L, HEAD_DIM)),
        'w_br_a': nrm(ks[13], (L, DN_WIDTH, D), DN_WIDTH),
        'w_br_b': nrm(ks[14], (L, MB_WIDTH, D), MB_WIDTH),
        'w_o': nrm(ks[15], (L, D, D), D),
        'w_router': nrm(ks[16], (L, D, N_EXPERTS), D),
        'b_router': 0.01 * jax.random.normal(ks[17], (L, N_EXPERTS), f32),
        'w_gu': nrm(ks[18], (L, N_EXPERTS, D, 2 * D_FF), D),
        'b_gu': 0.01 * jax.random.normal(ks[19], (L, N_EXPERTS, 2 * D_FF), f32),
        'w_down': nrm(ks[20], (L, N_EXPERTS, D_FF, D), D_FF),
        'b_down': 0.01 * jax.random.normal(ks[21], (L, N_EXPERTS, D), f32),
    }


def reference(x, c, w_ada, b_ada, g_pre_mix, g_post_mix, g_pre_ffn, g_post_ffn, w_in, conv_w,
              a_log, dt_bias, g_dn_out, w_br_a, w_br_b, w_o, w_router, b_router, w_gu, b_gu,
              w_down, b_down):
    for l in range(DEPTH):
        mod = jax.nn.silu(c) @ w_ada[l] + b_ada[l]
        shift_m, scale_m, gate_m, shift_f, scale_f, gate_f = [m[:, None, :] for m in jnp.split(mod, 6, axis=-1)]
        h = rmsnorm(x, g_pre_mix[l]) * (1 + scale_m) + shift_m
        y = hybrid_mixer(h, w_in[l], conv_w[l], a_log[l], dt_bias[l], g_dn_out[l],
                         w_br_a[l], w_br_b[l], w_o[l])
        x = x + gate_m * rmsnorm(y, g_post_mix[l])
        h = rmsnorm(x, g_pre_ffn[l]) * (1 + scale_f) + shift_f
        y = routed_moe(h, w_router[l], b_router[l], w_gu[l], b_gu[l], w_down[l], b_down[l])
        x = x + gate_f * rmsnorm(y, g_post_ffn[l])
    return x
```

```python
import functools

import jax
import jax.numpy as jnp
import numpy as np
from jax import lax
from jax.experimental import pallas as pl
from jax.experimental.pallas import tpu as pltpu

F32 = jnp.float32
BF16 = jnp.bfloat16
HIGHEST = lax.Precision.HIGHEST

HEAD_DIM = 128
DN_HEADS = 4
DN_CONV = 4
DN_CHUNK = 64
MB_HEADS = 4
MB_BLOCK = 256
MB_TOPK = 3
ROPE_THETA = 500000.0
ROPE_DIM = HEAD_DIM // 4
N_EXPERTS = 32
TOP_K = 4
SWIGLU_LIMIT = 7.0
SWIGLU_ALPHA = 1.702
NORM_EPS = 1e-6
LANES = 128
NEG_BIG = -1e30

_W = DN_HEADS * HEAD_DIM // LANES
COL_GA = 0
COL_GB = 8
COL_DQ = 16
COL_DK = 20
COL_DV = 24
COL_DZ = 28
COL_MQ = 32
COL_MK = 36
COL_MV = 40
COL_SMALL = 44
N_PROJ = 45 * LANES

VMEM_LIMIT = 56 * 1024 * 1024


def _cparams(sem):
    return pltpu.CompilerParams(dimension_semantics=sem, vmem_limit_bytes=VMEM_LIMIT)


def _silu(v):
    return v * (1.0 / (1.0 + jnp.exp(-v)))


def _sigmoid(v):
    return 1.0 / (1.0 + jnp.exp(-v))


def _adaln_kernel(c_ref, w_ref, b_ref, o_ref):
    a = _silu(c_ref[...])
    o_ref[...] = jnp.dot(a, w_ref[...], preferred_element_type=F32, precision=HIGHEST) + b_ref[...]


def _adaln(c, w_ada, b_ada):
    B, D = c.shape
    N = w_ada.shape[1]
    tn = D
    return pl.pallas_call(
        _adaln_kernel,
        grid=(N // tn,),
        in_specs=[pl.BlockSpec((B, D), lambda j: (0, 0)),
                  pl.BlockSpec((D, tn), lambda j: (0, j)),
                  pl.BlockSpec((1, tn), lambda j: (0, j))],
        out_specs=pl.BlockSpec((B, tn), lambda j: (0, j)),
        out_shape=jax.ShapeDtypeStruct((B, N), F32),
        compiler_params=_cparams(("arbitrary",)),
        name="adaln",
    )(c, w_ada, b_ada.reshape(1, N))


def _inproj_kernel(x_ref, mod_ref, g_ref, w_ref, o_ref, h_scr):
    @pl.when(pl.program_id(2) == 0)
    def _():
        x = x_ref[0]
        y = x * lax.rsqrt(jnp.mean(x * x, axis=-1, keepdims=True) + NORM_EPS) * g_ref[...]
        h = y * (1.0 + mod_ref[0, 1:2, :]) + mod_ref[0, 0:1, :]
        h_scr[...] = h.astype(BF16)

    o_ref[...] = jnp.dot(h_scr[...], w_ref[...], preferred_element_type=F32)


def _inproj(x, mod3, g_pre, w_all):
    B, S, D = x.shape
    tm = min(512, S)
    tn = N_PROJ // 3
    nrow = S // tm
    return pl.pallas_call(
        _inproj_kernel,
        grid=(B, nrow, N_PROJ // tn),
        in_specs=[pl.BlockSpec((1, tm, D), lambda b, i, j: (b, i, 0)),
                  pl.BlockSpec((1, 6, D), lambda b, i, j: (b, 0, 0)),
                  pl.BlockSpec((1, D), lambda b, i, j: (0, 0)),
                  pl.BlockSpec((D, tn), lambda b, i, j: (0, j))],
        out_specs=pl.BlockSpec((tm, tn), lambda b, i, j: (b * nrow + i, j)),
        out_shape=jax.ShapeDtypeStruct((B * S, N_PROJ), F32),
        scratch_shapes=[pltpu.VMEM((tm, D), BF16)],
        compiler_params=_cparams(("arbitrary", "arbitrary", "arbitrary")),
        name="inproj",
    )(x, mod3, g_pre.reshape(1, D), w_all)


def _rope(v, cosf, sinf, lane):
    rot = jnp.where(lane < ROPE_DIM // 2, pltpu.roll(v, LANES - ROPE_DIM // 2, 1), pltpu.roll(v, ROPE_DIM // 2, 1))
    return v * cosf + rot * sinf


def _mobaprep_kernel(q_ref, k_ref, v_ref, cos_ref, sin_ref, qo_ref, ko_ref, vo_ref, km_ref):
    cosf = cos_ref[...]
    sinf = sin_ref[...]
    lane = lax.broadcasted_iota(jnp.int32, cosf.shape, 1)
    for h in range(MB_HEADS):
        sl = slice(h * HEAD_DIM, (h + 1) * HEAD_DIM)
        qo_ref[:, sl] = _rope(q_ref[:, sl], cosf, sinf, lane).astype(BF16)
        kr = _rope(k_ref[:, sl], cosf, sinf, lane)
        ko_ref[:, sl] = kr.astype(BF16)
        km_ref[0, :, sl] = jnp.mean(kr, axis=0, keepdims=True)
    vo_ref[...] = v_ref[...].astype(BF16)


def _rope_tables(S):
    half = ROPE_DIM // 2
    inv_freq = ROPE_THETA ** (-jnp.arange(half, dtype=F32) / half)
    ang = jnp.arange(S, dtype=F32)[:, None] * inv_freq[None, :]
    cos, sin = jnp.cos(ang), jnp.sin(ang)
    rest = HEAD_DIM - ROPE_DIM
    cosf = jnp.concatenate([cos, cos, jnp.ones((S, rest), F32)], axis=1)
    sinf = jnp.concatenate([-sin, sin, jnp.zeros((S, rest), F32)], axis=1)
    return cosf, sinf


def _mobaprep(proj, B, S):
    T = B * S
    tb = MB_BLOCK
    nb = S // tb
    W = MB_HEADS * HEAD_DIM
    cosf, sinf = _rope_tables(S)
    col = lambda c: (lambda i: (i, c))
    return pl.pallas_call(
        _mobaprep_kernel,
        grid=(T // tb,),
        in_specs=[pl.BlockSpec((tb, W), col(COL_MQ // _W)),
                  pl.BlockSpec((tb, W), col(COL_MK // _W)),
                  pl.BlockSpec((tb, W), col(COL_MV // _W)),
                  pl.BlockSpec((tb, LANES), lambda i: (i % nb, 0)),
                  pl.BlockSpec((tb, LANES), lambda i: (i % nb, 0))],
        out_specs=[pl.BlockSpec((tb, W), lambda i: (i, 0)),
                   pl.BlockSpec((tb, W), lambda i: (i, 0)),
                   pl.BlockSpec((tb, W), lambda i: (i, 0)),
                   pl.BlockSpec((1, 1, W), lambda i: (i, 0, 0))],
        out_shape=[jax.ShapeDtypeStruct((T, W), BF16),
                   jax.ShapeDtypeStruct((T, W), BF16),
                   jax.ShapeDtypeStruct((T, W), BF16),
                   jax.ShapeDtypeStruct((T // tb, 1, W), F32)],
        compiler_params=_cparams(("arbitrary",)),
        name="mobaprep",
    )(proj, proj, proj, cosf, sinf)


def _moba_kernel(q_ref, k_ref, v_ref, km_ref, o_ref):
    qi = pl.program_id(2)
    tb = MB_BLOCK
    nb = km_ref.shape[0]
    q = q_ref[...]
    scale = HEAD_DIM ** -0.5

    st = lax.dot_general(km_ref[...], q.astype(F32), (((1,), (1,)), ((), ())),
                         preferred_element_type=F32, precision=HIGHEST)
    blk = lax.broadcasted_iota(jnp.int32, st.shape, 0)
    st = jnp.where(blk < qi, st, -jnp.inf)
    rowid = lax.broadcasted_iota(jnp.int32, (LANES, tb), 0)
    sel_t = jnp.full((LANES, tb), -1.0, F32)
    for r in range(MB_TOPK):
        m = jnp.max(st, axis=0, keepdims=True)
        idx = jnp.min(jnp.where(st == m, blk, nb), axis=0, keepdims=True)
        sel_t = jnp.where(rowid == r, jnp.where(r < qi, idx, -1).astype(F32), sel_t)
        st = jnp.where(blk == idx, -jnp.inf, st)
    sel = jnp.transpose(sel_t)
    sel_ids = [sel[:, r:r + 1] for r in range(MB_TOPK)]

    def block_logits(kb):
        kblk = k_ref[pl.ds(pl.multiple_of(kb * tb, tb), tb), :]
        return lax.dot_general(q, kblk, (((1,), (1,)), ((), ())), preferred_element_type=F32) * scale

    def pv(p, kb):
        vblk = v_ref[pl.ds(pl.multiple_of(kb * tb, tb), tb), :]
        return jnp.dot(p.astype(BF16), vblk, preferred_element_type=F32)

    s = block_logits(qi)
    r_i = lax.broadcasted_iota(jnp.int32, s.shape, 0)
    c_i = lax.broadcasted_iota(jnp.int32, s.shape, 1)
    s = jnp.where(c_i <= r_i, s, NEG_BIG)
    m0 = jnp.max(s, axis=-1, keepdims=True)
    p = jnp.exp(s - m0)
    l0 = jnp.sum(p, axis=-1, keepdims=True)
    acc0 = pv(p, qi)

    def body(kb, carry):
        m, l, acc = carry
        kbf = kb.astype(F32)
        hit = (sel_ids[0] == kbf) | (sel_ids[1] == kbf) | (sel_ids[2] == kbf)
        s = jnp.where(hit, block_logits(kb), NEG_BIG)
        m_new = jnp.maximum(m, jnp.max(s, axis=-1, keepdims=True))
        alpha = jnp.exp(m - m_new)
        p = jnp.exp(s - m_new)
        l = alpha * l + jnp.sum(p, axis=-1, keepdims=True)
        acc = alpha * acc + pv(p, kb)
        return m_new, l, acc

    m, l, acc = lax.fori_loop(0, qi, body, (m0, l0, acc0))
    o_ref[...] = (acc / l).astype(o_ref.dtype)


def _moba(qr, kr, vb, kmean, B, S):
    tb = MB_BLOCK
    nb = S // tb
    T = B * S
    km = kmean.reshape(B, nb, MB_HEADS * HEAD_DIM)
    return pl.pallas_call(
        _moba_kernel,
        grid=(B, MB_HEADS, nb),
        in_specs=[pl.BlockSpec((tb, HEAD_DIM), lambda b, h, i: (b * nb + i, h)),
                  pl.BlockSpec((S, HEAD_DIM), lambda b, h, i: (b, h)),
                  pl.BlockSpec((S, HEAD_DIM), lambda b, h, i: (b, h)),
                  pl.BlockSpec((None, nb, HEAD_DIM), lambda b, h, i: (b, 0, h))],
        out_specs=pl.BlockSpec((tb, HEAD_DIM), lambda b, h, i: (b * nb + i, h)),
        out_shape=jax.ShapeDtypeStruct((T, MB_HEADS * HEAD_DIM), BF16),
        compiler_params=_cparams(("arbitrary", "arbitrary", "arbitrary")),
        name="moba",
    )(qr, kr, vb, km)


DN_TILE_CHUNKS = 8


def _softplus(v):
    return jnp.maximum(v, 0.0) + jnp.log1p(jnp.exp(-jnp.abs(v)))


def _dot_hi(a, b):
    return jnp.dot(a, b, preferred_element_type=F32, precision=HIGHEST)


def _dot_bf(a, b):
    return jnp.dot(a.astype(BF16), b.astype(BF16), preferred_element_type=F32)


def _deltanet_kernel(q_ref, k_ref, v_ref, z_ref, sm_ref, wq_ref, wk_ref, wv_ref, alog_ref, dtb_ref, gout_ref,
                     o_ref, xp_scr, state_scr):
    h = pl.program_id(1)
    i = pl.program_id(2)
    C = DN_CHUNK
    TR = q_ref.shape[0]
    HALO = 8

    @pl.when(i == 0)
    def _():
        xp_scr[...] = jnp.zeros(xp_scr.shape, F32)
        state_scr[...] = jnp.zeros(state_scr.shape, F32)

    def conv_silu(slot, x_ref, w_ref):
        xp_scr[slot, 0:HALO, :] = xp_scr[slot, TR:TR + HALO, :]
        xp_scr[slot, HALO:HALO + TR, :] = x_ref[...]
        acc = w_ref[DN_CONV - 1:DN_CONV, :] * xp_scr[slot, HALO:HALO + TR, :]
        for j in range(1, DN_CONV):
            acc = acc + w_ref[DN_CONV - 1 - j:DN_CONV - j, :] * xp_scr[slot, HALO - j:HALO - j + TR, :]
        return _silu(acc)

    q = conv_silu(0, q_ref, wq_ref)
    k = conv_silu(1, k_ref, wk_ref)
    v = conv_silu(2, v_ref, wv_ref)
    q = q * lax.rsqrt(jnp.sum(q * q, axis=-1, keepdims=True) + 1e-6) * (HEAD_DIM ** -0.5)
    k = k * lax.rsqrt(jnp.sum(k * k, axis=-1, keepdims=True) + 1e-6)

    sm = sm_ref[...]
    lane = lax.broadcasted_iota(jnp.int32, sm.shape, 1)
    row = lax.broadcasted_iota(jnp.int32, sm.shape, 0)
    beta_all = _sigmoid(sm)
    g_all = -jnp.exp(alog_ref[...]) * _softplus(sm + dtb_ref[...])
    pos = row % C
    gc_all = g_all
    shift = 1
    while shift < C:
        gc_all = gc_all + jnp.where(pos >= shift, pltpu.roll(gc_all, shift, 0), 0.0)
        shift *= 2
    beta = jnp.sum(jnp.where(lane == h, beta_all, 0.0), axis=1, keepdims=True)
    gc = jnp.sum(jnp.where(lane == DN_HEADS + h, gc_all, 0.0), axis=1, keepdims=True)
    gc_t = jnp.transpose(gc_all)
    row_t = lax.broadcasted_iota(jnp.int32, gc_t.shape, 0)
    gc_row = jnp.sum(jnp.where(row_t == DN_HEADS + h, gc_t, 0.0), axis=0, keepdims=True)
    k_t = jnp.transpose(k)

    ri = lax.broadcasted_iota(jnp.int32, (C, C), 0)
    ci = lax.broadcasted_iota(jnp.int32, (C, C), 1)
    tril = ci <= ri
    strict = ci < ri
    eye = (ci == ri).astype(F32)

    state = state_scr[...]
    gout = gout_ref[...]
    for n in range(TR // C):
        r0 = n * C
        qn, kn, vn = q[r0:r0 + C], k[r0:r0 + C], v[r0:r0 + C]
        bn = beta[r0:r0 + C]
        gcn = gc[r0:r0 + C]
        gcr = gc_row[:, r0:r0 + C]
        knt = k_t[:, r0:r0 + C]
        g_last = gcn[C - 1:C, :]
        decay = jnp.where(tril, jnp.exp(jnp.where(tril, gcn - gcr, 0.0)), 0.0)
        kb = kn * bn
        m_neg = jnp.where(strict, -(_dot_bf(kb, knt) * decay), 0.0)
        t_inv = eye + m_neg
        p = m_neg
        for _ in range(5):
            p = _dot_hi(p, p)
            t_inv = t_inv + _dot_hi(t_inv, p)
        egc = jnp.exp(gcn)
        u = _dot_hi(t_inv, vn * bn)
        w = _dot_hi(t_inv, kb * egc)
        qk = _dot_bf(qn, knt) * decay
        kt_tail = knt * jnp.exp(g_last - gcr)
        q_dec = qn * egc
        v_new = u - _dot_bf(w, state)
        o = _dot_bf(q_dec, state) + _dot_bf(qk, v_new)
        state = state * jnp.exp(g_last) + _dot_bf(kt_tail, v_new)
        y = o * lax.rsqrt(jnp.mean(o * o, axis=-1, keepdims=True) + NORM_EPS) * gout
        o_ref[r0:r0 + C, :] = (y * _silu(z_ref[r0:r0 + C, :])).astype(o_ref.dtype)
    state_scr[...] = state


def _deltanet(proj, conv_w, a_log, dt_bias, g_dn_out, B, S):
    T = B * S
    TR = min(DN_TILE_CHUNKS * DN_CHUNK, S)
    nt = S // TR
    pad = jnp.zeros((DN_HEADS,), F32)
    rest = jnp.zeros((LANES - 2 * DN_HEADS,), F32)
    alog_lane = jnp.concatenate([pad, a_log.astype(F32), rest]).reshape(1, LANES)
    dtb_lane = jnp.concatenate([pad, dt_bias.astype(F32), rest]).reshape(1, LANES)
    rows = lambda c0: (lambda b, h, i: (b * nt + i, c0 + h))
    wcol = lambda c0: (lambda b, h, i: (0, c0 + h))
    const = lambda b, h, i: (0, 0)
    return pl.pallas_call(
        _deltanet_kernel,
        grid=(B, DN_HEADS, nt),
        in_specs=[pl.BlockSpec((TR, HEAD_DIM), rows(COL_DQ)),
                  pl.BlockSpec((TR, HEAD_DIM), rows(COL_DK)),
                  pl.BlockSpec((TR, HEAD_DIM), rows(COL_DV)),
                  pl.BlockSpec((TR, HEAD_DIM), rows(COL_DZ)),
                  pl.BlockSpec((TR, LANES), lambda b, h, i: (b * nt + i, COL_SMALL)),
                  pl.BlockSpec((DN_CONV, HEAD_DIM), wcol(0)),
                  pl.BlockSpec((DN_CONV, HEAD_DIM), wcol(DN_HEADS)),
                  pl.BlockSpec((DN_CONV, HEAD_DIM), wcol(2 * DN_HEADS)),
                  pl.BlockSpec((1, LANES), const),
                  pl.BlockSpec((1, LANES), const),
                  pl.BlockSpec((1, HEAD_DIM), const)],
        out_specs=pl.BlockSpec((TR, HEAD_DIM), lambda b, h, i: (b * nt + i, h)),
        out_shape=jax.ShapeDtypeStruct((T, DN_HEADS * HEAD_DIM), BF16),
        scratch_shapes=[pltpu.VMEM((3, TR + 8, HEAD_DIM), F32), pltpu.VMEM((HEAD_DIM, HEAD_DIM), F32)],
        compiler_params=_cparams(("arbitrary", "arbitrary", "arbitrary")),
        name="deltanet",
    )(proj, proj, proj, proj, proj, conv_w, conv_w, conv_w, alog_lane, dtb_lane, g_dn_out.reshape(1, HEAD_DIM))


def _rms(v):
    return v * lax.rsqrt(jnp.mean(v * v, axis=-1, keepdims=True) + NORM_EPS)


def _merge_kernel(oa_ref, ob_ref, ga_ref, gb_ref, x_ref, mod_ref, gpost_ref, gpre_ref, wa_ref, wb_ref, wo_ref,
                  wrt_ref, br_ref, x1_ref, hp_ref, idx_ref, wrow_ref, rank_ref, cnt_ref, carry_scr):
    E = N_EXPERTS
    tm = x_ref.shape[0]
    half = x_ref.shape[1] // 2

    @pl.when(pl.program_id(0) == 0)
    def _():
        carry_scr[...] = jnp.zeros(carry_scr.shape, F32)

    ya = jnp.dot(oa_ref[...], wa_ref[...], preferred_element_type=F32)
    yb = jnp.dot(ob_ref[...], wb_ref[...], preferred_element_type=F32)
    merged = _sigmoid(ga_ref[...]) * ya + _sigmoid(gb_ref[...]) * yb
    mix = jnp.dot(merged.astype(BF16), wo_ref[...], preferred_element_type=F32)
    x1 = x_ref[...] + mod_ref[0, 2:3, :] * (_rms(mix) * gpost_ref[...])
    x1_ref[...] = x1
    h2 = (_rms(x1) * gpre_ref[...]) * (1.0 + mod_ref[0, 4:5, :]) + mod_ref[0, 3:4, :]

    lo_bits = pltpu.bitcast(h2[:, :half].astype(BF16).astype(F32), jnp.uint32) >> 16
    hi_bits = pltpu.bitcast(h2[:, half:].astype(BF16).astype(F32), jnp.uint32) & jnp.uint32(0xFFFF0000)
    hp_ref[...] = hi_bits | lo_bits

    lt = lax.dot_general(wrt_ref[...], h2, (((1,), (1,)), ((), ())), preferred_element_type=F32,
                         precision=HIGHEST) + br_ref[...]
    eid = lax.broadcasted_iota(jnp.int32, (E, tm), 0)
    vals, idxs = [], []
    for _ in range(TOP_K):
        m = jnp.max(lt, axis=0, keepdims=True)
        idx = jnp.min(jnp.where(lt == m, eid, E), axis=0, keepdims=True)
        vals.append(m)
        idxs.append(idx)
        lt = jnp.where(eid == idx, -jnp.inf, lt)
    exps = [jnp.exp(v - vals[0]) for v in vals]
    den = exps[0] + exps[1] + exps[2] + exps[3]
    wts = [e / den for e in exps]

    hot = jnp.zeros((E, tm), F32)
    for idx in idxs:
        hot = hot + (eid == idx).astype(F32)
    ti = lax.broadcasted_iota(jnp.int32, (tm, tm), 0)
    tj = lax.broadcasted_iota(jnp.int32, (tm, tm), 1)
    before = (ti < tj).astype(BF16)
    prior = carry_scr[...][:, 0:1] + jnp.dot(hot.astype(BF16), before, preferred_element_type=F32)
    row8 = lax.broadcasted_iota(jnp.int32, (8, tm), 0)
    row128 = lax.broadcasted_iota(jnp.int32, (LANES, tm), 0)
    idx8 = jnp.zeros((8, tm), jnp.int32)
    rank8 = jnp.zeros((8, tm), jnp.int32)
    w128 = jnp.zeros((LANES, tm), F32)
    for r in range(TOP_K):
        rank_r = jnp.sum(jnp.where(eid == idxs[r], prior, 0.0), axis=0, keepdims=True)
        idx8 = jnp.where(row8 == r, idxs[r], idx8)
        rank8 = jnp.where(row8 == r, rank_r.astype(jnp.int32), rank8)
        w128 = jnp.where(row128 == r, wts[r], w128)
    idx_ref[...] = idx8
    rank_ref[...] = rank8
    wrow_ref[...] = jnp.transpose(w128)
    carry = carry_scr[...] + jnp.sum(hot, axis=1, keepdims=True)
    carry_scr[...] = carry
    cnt_ref[...] = carry


def _merge(oa, ob, proj, x2, mod3, g_post_mix, g_pre_ffn, w_br_a, w_br_b, w_o, w_router, b_router, S):
    T, D = x2.shape
    E = N_EXPERTS
    tm = min(512, S)
    per_b = S // tm
    W = DN_HEADS * HEAD_DIM
    row = lambda i: (i, 0)
    const = lambda i: (0, 0)
    lane_t = lambda i: (0, i)
    return pl.pallas_call(
        _merge_kernel,
        grid=(T // tm,),
        in_specs=[pl.BlockSpec((tm, W), row),
                  pl.BlockSpec((tm, W), row),
                  pl.BlockSpec((tm, D), lambda i: (i, COL_GA * LANES // D)),
                  pl.BlockSpec((tm, D), lambda i: (i, COL_GB * LANES // D)),
                  pl.BlockSpec((tm, D), row),
                  pl.BlockSpec((1, 6, D), lambda i: (i // per_b, 0, 0)),
                  pl.BlockSpec((1, D), const),
                  pl.BlockSpec((1, D), const),
                  pl.BlockSpec((W, D), const),
                  pl.BlockSpec((W, D), const),
                  pl.BlockSpec((D, D), const),
                  pl.BlockSpec((E, D), const),
                  pl.BlockSpec((E, 1), const)],
        out_specs=[pl.BlockSpec((tm, D), row),
                   pl.BlockSpec((tm, D // 2), row),
                   pl.BlockSpec((8, tm), lane_t),
                   pl.BlockSpec((tm, LANES), row),
                   pl.BlockSpec((8, tm), lane_t),
                   pl.BlockSpec((E, LANES), const)],
        out_shape=[jax.ShapeDtypeStruct((T, D), F32),
                   jax.ShapeDtypeStruct((T, D // 2), jnp.uint32),
                   jax.ShapeDtypeStruct((8, T), jnp.int32),
                   jax.ShapeDtypeStruct((T, LANES), F32),
                   jax.ShapeDtypeStruct((8, T), jnp.int32),
                   jax.ShapeDtypeStruct((E, LANES), F32)],
        scratch_shapes=[pltpu.VMEM((E, LANES), F32)],
        compiler_params=_cparams(("arbitrary",)),
        name="merge",
    )(oa, ob, proj, proj, x2, mod3, g_post_mix.reshape(1, D), g_pre_ffn.reshape(1, D),
      w_br_a.astype(BF16), w_br_b.astype(BF16), w_o.astype(BF16),
      jnp.transpose(w_router).astype(F32), b_router.reshape(E, 1).astype(F32))


DISPATCH_TOKENS = 256


def _dispatch_kernel(dest_ref, src_ref, dst_ref, sem):
    n = dest_ref.shape[1]
    base = pl.program_id(0) * n

    def row_copy(t, d):
        return pltpu.make_async_copy(src_ref.at[pl.ds(t, 1)], dst_ref.at[pl.ds(d, 1)], sem)

    def issue(j, carry):
        for r in range(TOP_K):
            row_copy(base + j, dest_ref[r, j]).start()
        return carry

    lax.fori_loop(0, n, issue, 0)

    def drain(j, carry):
        for r in range(TOP_K):
            row_copy(base + j, dest_ref[r, j]).wait()
        return carry

    lax.fori_loop(0, n, drain, 0)


def _dispatch(dest, hp):
    T, Wd = hp.shape
    n = min(DISPATCH_TOKENS, T)
    return pl.pallas_call(
        _dispatch_kernel,
        grid=(T // n,),
        in_specs=[pl.BlockSpec((TOP_K, n), lambda i: (0, i), memory_space=pltpu.SMEM),
                  pl.BlockSpec(memory_space=pl.ANY)],
        out_specs=pl.BlockSpec(memory_space=pl.ANY),
        out_shape=jax.ShapeDtypeStruct((T * TOP_K, Wd), hp.dtype),
        scratch_shapes=[pltpu.SemaphoreType.DMA(())],
        compiler_params=_cparams(("arbitrary",)),
        name="dispatch",
    )(dest, hp)


EXPERT_ROWS = 512


def _experts_kernel(blk_ref, exp_ref, lo_ref, hi_ref, x_ref, wgu_ref, bgu_ref, wdn_ref, bdn_ref, o_ref):
    k = pl.program_id(0)
    lo = lo_ref[k]
    hi = hi_ref[k]
    tm = x_ref.shape[0]
    half = x_ref.shape[1]
    F = wdn_ref.shape[0]

    @pl.when(hi > lo)
    def _():
        word = x_ref[...]
        x_lo = pltpu.bitcast(word << 16, F32).astype(BF16)
        x_hi = pltpu.bitcast(word & jnp.uint32(0xFFFF0000), F32).astype(BF16)
        gu = (jnp.dot(x_lo, wgu_ref[0:half, :], preferred_element_type=F32)
              + jnp.dot(x_hi, wgu_ref[half:2 * half, :], preferred_element_type=F32) + bgu_ref[...])
        gate = jnp.minimum(gu[:, :F], SWIGLU_LIMIT)
        up = jnp.clip(gu[:, F:], -SWIGLU_LIMIT, SWIGLU_LIMIT)
        act = (up + 1.0) * gate * _sigmoid(SWIGLU_ALPHA * gate)
        y = jnp.dot(act.astype(BF16), wdn_ref[...], preferred_element_type=F32) + bdn_ref[...]
        rows = lax.broadcasted_iota(jnp.int32, (tm, 1), 0)
        keep = (rows >= lo) & (rows < hi)

        @pl.when(lo == 0)
        def _():
            o_ref[...] = jnp.where(keep, y, 0.0)

        @pl.when(lo > 0)
        def _():
            o_ref[...] = jnp.where(keep, y, o_ref[...])


def _experts(xs, item_blk, item_exp, item_lo, item_hi, w_gu, b_gu, w_down, b_down):
    A, half = xs.shape
    E, D, F2 = w_gu.shape
    F = F2 // 2
    tm = EXPERT_ROWS
    n_items = item_blk.shape[0]
    grid_spec = pltpu.PrefetchScalarGridSpec(
        num_scalar_prefetch=4,
        grid=(n_items,),
        in_specs=[pl.BlockSpec((tm, half), lambda k, blk, ex, lo, hi: (blk[k], 0)),
                  pl.BlockSpec((None, D, F2), lambda k, blk, ex, lo, hi: (ex[k], 0, 0)),
                  pl.BlockSpec((None, 1, F2), lambda k, blk, ex, lo, hi: (ex[k], 0, 0)),
                  pl.BlockSpec((None, F, D), lambda k, blk, ex, lo, hi: (ex[k], 0, 0)),
                  pl.BlockSpec((None, 1, D), lambda k, blk, ex, lo, hi: (ex[k], 0, 0))],
        out_specs=pl.BlockSpec((tm, D), lambda k, blk, ex, lo, hi: (blk[k], 0)),
    )
    return pl.pallas_call(
        _experts_kernel,
        grid_spec=grid_spec,
        out_shape=jax.ShapeDtypeStruct((A, D), F32),
        compiler_params=_cparams(("arbitrary",)),
        name="experts",
    )(item_blk, item_exp, item_lo, item_hi, xs, w_gu.astype(BF16), b_gu.reshape(E, 1, F2).astype(F32),
      w_down.astype(BF16), b_down.reshape(E, 1, D).astype(F32))


def _work_items(counts, A):
    E = N_EXPERTS
    tm = EXPERT_ROWS
    n_items = A // tm + E - 1
    end = jnp.cumsum(counts)
    start = end - counts
    first_blk = start // tm
    last_blk = jnp.maximum(end - 1, 0) // tm
    n_e = jnp.where(counts > 0, last_blk - first_blk + 1, 0)
    off_end = jnp.cumsum(n_e)
    off = off_end - n_e
    total = off_end[-1]
    k = jnp.arange(n_items, dtype=jnp.int32)
    kk = jnp.minimum(k, total - 1)
    e = jnp.searchsorted(off_end, kk, side='right').astype(jnp.int32)
    blk = (first_blk[e] + (kk - off[e])).astype(jnp.int32)
    lo = jnp.maximum(start[e], blk * tm) - blk * tm
    hi = jnp.minimum(end[e], (blk + 1) * tm) - blk * tm
    valid = k < total
    lo = jnp.where(valid, lo, 0).astype(jnp.int32)
    hi = jnp.where(valid, hi, 0).astype(jnp.int32)
    return blk, e, lo, hi


COMBINE_TOKENS = 256


def _combine_kernel(dest_ref, y_ref, wrow_ref, x1_ref, mod_ref, gpost_ref, o_ref, stage, sem):
    n = x1_ref.shape[0]

    def row_copy(r, j):
        return pltpu.make_async_copy(y_ref.at[pl.ds(dest_ref[r, j], 1)], stage.at[r, pl.ds(j, 1)], sem)

    def issue(j, carry):
        for r in range(TOP_K):
            row_copy(r, j).start()
        return carry

    lax.fori_loop(0, n, issue, 0)

    def drain(j, carry):
        for r in range(TOP_K):
            row_copy(r, j).wait()
        return carry

    lax.fori_loop(0, n, drain, 0)

    wrow = wrow_ref[...]
    moe = wrow[:, 0:1] * stage[0]
    for r in range(1, TOP_K):
        moe = moe + wrow[:, r:r + 1] * stage[r]
    o_ref[...] = x1_ref[...] + mod_ref[0, 5:6, :] * (_rms(moe) * gpost_ref[...])


def _combine(dest, y, wrow, x1, mod3, g_post_ffn, S):
    T, D = x1.shape
    n = min(COMBINE_TOKENS, S)
    per_b = S // n
    return pl.pallas_call(
        _combine_kernel,
        grid=(T // n,),
        in_specs=[pl.BlockSpec((TOP_K, n), lambda i: (0, i), memory_space=pltpu.SMEM),
                  pl.BlockSpec(memory_space=pl.ANY),
                  pl.BlockSpec((n, LANES), lambda i: (i, 0)),
                  pl.BlockSpec((n, D), lambda i: (i, 0)),
                  pl.BlockSpec((1, 6, D), lambda i: (i // per_b, 0, 0)),
                  pl.BlockSpec((1, D), lambda i: (0, 0))],
        out_specs=pl.BlockSpec((n, D), lambda i: (i, 0)),
        out_shape=jax.ShapeDtypeStruct((T, D), F32),
        scratch_shapes=[pltpu.VMEM((TOP_K, n, D), F32), pltpu.SemaphoreType.DMA(())],
        compiler_params=_cparams(("arbitrary",)),
        name="combine",
    )(dest, y, wrow, x1, mod3, g_post_ffn.reshape(1, D))


def _regroup_w_in(w_in):
    D = w_in.shape[0]
    dw = DN_HEADS * HEAD_DIM
    mw = MB_HEADS * HEAD_DIM
    cuts = np.cumsum([dw, dw, dw, dw, DN_HEADS, DN_HEADS, mw, mw, mw, D, D])[:-1]
    dq, dk, dv, dz, db, da, mq, mk, mv, ga, gb = jnp.split(w_in, [int(c) for c in cuts], axis=1)
    small = jnp.concatenate([db, da, jnp.zeros((D, LANES - 2 * DN_HEADS), w_in.dtype)], axis=1)
    return jnp.concatenate([ga, gb, dq, dk, dv, dz, mq, mk, mv, small], axis=1).astype(BF16)


def kernel(x, c, w_ada, b_ada, g_pre_mix, g_post_mix, g_pre_ffn, g_post_ffn, w_in, conv_w, a_log, dt_bias,
           g_dn_out, w_br_a, w_br_b, w_o, w_router, b_router, w_gu, b_gu, w_down, b_down):
    B, S, D = x.shape
    l = 0
    mod = _adaln(c, w_ada[l], b_ada[l]).reshape(B, 6, D)
    proj = _inproj(x, mod, g_pre_mix[l], _regroup_w_in(w_in[l]))
    qr, kr, vb, kmean = _mobaprep(proj, B, S)
    ob = _moba(qr, kr, vb, kmean, B, S)
    oa = _deltanet(proj, conv_w[l], a_log[l], dt_bias[l], g_dn_out[l], B, S)
    x1, hp, idx8, wrow, rank8, cnt = _merge(oa, ob, proj, x.reshape(B * S, D), mod, g_post_mix[l], g_pre_ffn[l],
                                            w_br_a[l], w_br_b[l], w_o[l], w_router[l], b_router[l], S)
    out = _moe(x1, hp, idx8, wrow, rank8, cnt, mod, g_post_ffn[l], w_gu[l], b_gu[l], w_down[l], b_down[l], S)
    return out.reshape(B, S, D)


def _moe(x1, hp, idx8, wrow, rank8, cnt, mod, g_post_ffn, w_gu, b_gu, w_down, b_down, S):
    T = x1.shape[0]
    counts = cnt[:, 0].astype(jnp.int32)
    start = jnp.cumsum(counts) - counts
    dest = rank8[:TOP_K] + jnp.take(start, idx8[:TOP_K])
    xs = _dispatch(dest, hp)
    blk, e, lo, hi = _work_items(counts, T * TOP_K)
    y = _experts(xs, blk, e, lo, hi, w_gu, b_gu, w_down, b_down)
    return _combine(dest, y, wrow, x1, mod, g_post_ffn, S)
```

```python
import functools

import jax
import jax.numpy as jnp
import numpy as np
from jax import lax
from jax.experimental import pallas as pl
from jax.experimental.pallas import tpu as pltpu

F32 = jnp.float32
BF16 = jnp.bfloat16
HIGHEST = lax.Precision.HIGHEST

HEAD_DIM = 128
DN_HEADS = 4
DN_CONV = 4
DN_CHUNK = 64
MB_HEADS = 4
MB_BLOCK = 256
MB_TOPK = 3
ROPE_THETA = 500000.0
ROPE_DIM = HEAD_DIM // 4
N_EXPERTS = 32
TOP_K = 4
SWIGLU_LIMIT = 7.0
SWIGLU_ALPHA = 1.702
NORM_EPS = 1e-6
LANES = 128
NEG_BIG = -1e30

_W = DN_HEADS * HEAD_DIM // LANES
COL_GA = 0
COL_GB = 8
COL_DQ = 16
COL_DK = 20
COL_DV = 24
COL_DZ = 28
COL_MQ = 32
COL_MK = 36
COL_MV = 40
COL_SMALL = 44
N_PROJ = 45 * LANES

VMEM_LIMIT = 56 * 1024 * 1024


def _cparams(sem):
    return pltpu.CompilerParams(dimension_semantics=sem, vmem_limit_bytes=VMEM_LIMIT)


def _silu(v):
    return v * (1.0 / (1.0 + jnp.exp(-v)))


def _sigmoid(v):
    return 1.0 / (1.0 + jnp.exp(-v))


def _adaln_kernel(c_ref, w_ref, b_ref, o_ref):
    a = _silu(c_ref[...])
    o_ref[...] = jnp.dot(a, w_ref[...], preferred_element_type=F32, precision=HIGHEST) + b_ref[...]


def _adaln(c, w_ada, b_ada):
    B, D = c.shape
    N = w_ada.shape[1]
    tn = D
    return pl.pallas_call(
        _adaln_kernel,
        grid=(N // tn,),
        in_specs=[pl.BlockSpec((B, D), lambda j: (0, 0)),
                  pl.BlockSpec((D, tn), lambda j: (0, j)),
                  pl.BlockSpec((1, tn), lambda j: (0, j))],
        out_specs=pl.BlockSpec((B, tn), lambda j: (0, j)),
        out_shape=jax.ShapeDtypeStruct((B, N), F32),
        compiler_params=_cparams(("arbitrary",)),
        name="adaln",
    )(c, w_ada, b_ada.reshape(1, N))


def _inproj_kernel(x_ref, mod_ref, g_ref, w_ref, o_ref, h_scr):
    @pl.when(pl.program_id(2) == 0)
    def _():
        x = x_ref[0]
        y = x * lax.rsqrt(jnp.mean(x * x, axis=-1, keepdims=True) + NORM_EPS) * g_ref[...]
        h = y * (1.0 + mod_ref[0, 1:2, :]) + mod_ref[0, 0:1, :]
        h_scr[...] = h.astype(BF16)

    o_ref[...] = jnp.dot(h_scr[...], w_ref[...], preferred_element_type=F32)


def _inproj(x, mod3, g_pre, w_all):
    B, S, D = x.shape
    tm = min(512, S)
    tn = N_PROJ // 3
    nrow = S // tm
    return pl.pallas_call(
        _inproj_kernel,
        grid=(B, nrow, N_PROJ // tn),
        in_specs=[pl.BlockSpec((1, tm, D), lambda b, i, j: (b, i, 0)),
                  pl.BlockSpec((1, 6, D), lambda b, i, j: (b, 0, 0)),
                  pl.BlockSpec((1, D), lambda b, i, j: (0, 0)),
                  pl.BlockSpec((D, tn), lambda b, i, j: (0, j))],
        out_specs=pl.BlockSpec((tm, tn), lambda b, i, j: (b * nrow + i, j)),
        out_shape=jax.ShapeDtypeStruct((B * S, N_PROJ), F32),
        scratch_shapes=[pltpu.VMEM((tm, D), BF16)],
        compiler_params=_cparams(("arbitrary", "arbitrary", "arbitrary")),
        name="inproj",
    )(x, mod3, g_pre.reshape(1, D), w_all)


def _rope(v, cosf, sinf, lane):
    rot = jnp.where(lane < ROPE_DIM // 2, pltpu.roll(v, LANES - ROPE_DIM // 2, 1), pltpu.roll(v, ROPE_DIM // 2, 1))
    return v * cosf + rot * sinf


def _mobaprep_kernel(q_ref, k_ref, v_ref, cos_ref, sin_ref, qo_ref, ko_ref, vo_ref, km_ref):
    cosf = cos_ref[...]
    sinf = sin_ref[...]
    lane = lax.broadcasted_iota(jnp.int32, cosf.shape, 1)
    for h in range(MB_HEADS):
        sl = slice(h * HEAD_DIM, (h + 1) * HEAD_DIM)
        qo_ref[:, sl] = (_rope(q_ref[:, sl], cosf, sinf, lane) * (HEAD_DIM ** -0.5)).astype(BF16)
        kr = _rope(k_ref[:, sl], cosf, sinf, lane)
        ko_ref[:, sl] = kr.astype(BF16)
        km_ref[0, :, sl] = jnp.mean(kr, axis=0, keepdims=True)
    vo_ref[...] = v_ref[...].astype(BF16)


def _rope_tables(S):
    half = ROPE_DIM // 2
    inv_freq = ROPE_THETA ** (-jnp.arange(half, dtype=F32) / half)
    ang = jnp.arange(S, dtype=F32)[:, None] * inv_freq[None, :]
    cos, sin = jnp.cos(ang), jnp.sin(ang)
    rest = HEAD_DIM - ROPE_DIM
    cosf = jnp.concatenate([cos, cos, jnp.ones((S, rest), F32)], axis=1)
    sinf = jnp.concatenate([-sin, sin, jnp.zeros((S, rest), F32)], axis=1)
    return cosf, sinf


def _mobaprep(proj, B, S):
    T = B * S
    tb = MB_BLOCK
    nb = S // tb
    W = MB_HEADS * HEAD_DIM
    cosf, sinf = _rope_tables(S)
    col = lambda c: (lambda i: (i, c))
    return pl.pallas_call(
        _mobaprep_kernel,
        grid=(T // tb,),
        in_specs=[pl.BlockSpec((tb, W), col(COL_MQ // _W)),
                  pl.BlockSpec((tb, W), col(COL_MK // _W)),
                  pl.BlockSpec((tb, W), col(COL_MV // _W)),
                  pl.BlockSpec((tb, LANES), lambda i: (i % nb, 0)),
                  pl.BlockSpec((tb, LANES), lambda i: (i % nb, 0))],
        out_specs=[pl.BlockSpec((tb, W), lambda i: (i, 0)),
                   pl.BlockSpec((tb, W), lambda i: (i, 0)),
                   pl.BlockSpec((tb, W), lambda i: (i, 0)),
                   pl.BlockSpec((1, 1, W), lambda i: (i, 0, 0))],
        out_shape=[jax.ShapeDtypeStruct((T, W), BF16),
                   jax.ShapeDtypeStruct((T, W), BF16),
                   jax.ShapeDtypeStruct((T, W), BF16),
                   jax.ShapeDtypeStruct((T // tb, 1, W), F32)],
        compiler_params=_cparams(("arbitrary",)),
        name="mobaprep",
    )(proj, proj, proj, cosf, sinf)


def _moba_kernel(q_ref, k_ref, v_ref, km_ref, o_ref, s_scr, sd_scr):
    qi = pl.program_id(2)
    tb = MB_BLOCK
    nb = km_ref.shape[0]
    nt = tb // LANES
    q = q_ref[...]

    st = lax.dot_general(km_ref[...], q.astype(F32), (((1,), (1,)), ((), ())),
                         preferred_element_type=F32, precision=HIGHEST)
    blk = lax.broadcasted_iota(jnp.int32, st.shape, 0)
    st = jnp.where(blk < qi, st, -jnp.inf)
    rowid = lax.broadcasted_iota(jnp.int32, (LANES, tb), 0)
    sel_t = jnp.full((LANES, tb), -1.0, F32)
    for r in range(MB_TOPK):
        m = jnp.max(st, axis=0, keepdims=True)
        idx = jnp.min(jnp.where(st == m, blk, nb), axis=0, keepdims=True)
        sel_t = jnp.where(rowid == r, jnp.where(r < qi, idx, -1).astype(F32), sel_t)
        st = jnp.where(blk == idx, -jnp.inf, st)
    sel = jnp.transpose(sel_t)
    sel_rep = [jnp.broadcast_to(sel[:, r:r + 1], (tb, LANES)) for r in range(MB_TOPK)]

    def logits(start, width):
        kslab = k_ref[pl.ds(pl.multiple_of(start, tb), width), :]
        return lax.dot_general(q, kslab, (((1,), (1,)), ((), ())), preferred_element_type=F32)

    s = logits(qi * tb, tb)
    r_i = lax.broadcasted_iota(jnp.int32, (tb, LANES), 0)
    c_i = lax.broadcasted_iota(jnp.int32, (tb, LANES), 1)
    mx = jnp.full((tb, LANES), NEG_BIG, F32)
    for t in range(nt):
        piece = jnp.where(c_i + t * LANES <= r_i, s[:, t * LANES:(t + 1) * LANES], NEG_BIG)
        sd_scr[:, t * LANES:(t + 1) * LANES] = piece
        mx = jnp.maximum(mx, piece)

    n_pairs = (qi + 1) // 2

    def pass1(i, mx):
        kb0 = 2 * i
        s2 = logits(kb0 * tb, 2 * tb)
        for half in range(2):
            kbf = (kb0 + half).astype(F32)
            hit = (sel_rep[0] == kbf) | (sel_rep[1] == kbf) | (sel_rep[2] == kbf)
            for t in range(nt):
                c0 = (half * nt + t) * LANES
                piece = jnp.where(hit, s2[:, c0:c0 + LANES], NEG_BIG)
                s_scr[i, :, c0:c0 + LANES] = piece
                mx = jnp.maximum(mx, piece)
        return mx

    mx = lax.fori_loop(0, n_pairs, pass1, mx)
    m_rep = jnp.broadcast_to(jnp.max(mx, axis=-1, keepdims=True), (tb, LANES))

    def probs(load, width):
        ps, lsum = [], jnp.zeros((tb, LANES), F32)
        for t in range(width // LANES):
            p = jnp.exp(load(t) - m_rep)
            lsum = lsum + p
            ps.append(p.astype(BF16))
        return jnp.concatenate(ps, axis=1), lsum

    pd, l0 = probs(lambda t: sd_scr[:, t * LANES:(t + 1) * LANES], tb)
    acc0 = jnp.dot(pd, v_ref[pl.ds(pl.multiple_of(qi * tb, tb), tb), :], preferred_element_type=F32)

    def pass2(i, carry):
        lsum, acc = carry
        p2, l2 = probs(lambda t: s_scr[i, :, t * LANES:(t + 1) * LANES], 2 * tb)
        vslab = v_ref[pl.ds(pl.multiple_of(2 * i * tb, tb), 2 * tb), :]
        return lsum + l2, acc + jnp.dot(p2, vslab, preferred_element_type=F32)

    lsum, acc = lax.fori_loop(0, n_pairs, pass2, (l0, acc0))
    o_ref[...] = (acc / jnp.sum(lsum, axis=-1, keepdims=True)).astype(o_ref.dtype)


def _moba(qr, kr, vb, kmean, B, S):
    tb = MB_BLOCK
    nb = S // tb
    T = B * S
    km = kmean.reshape(B, nb, MB_HEADS * HEAD_DIM)
    return pl.pallas_call(
        _moba_kernel,
        grid=(B, MB_HEADS, nb),
        in_specs=[pl.BlockSpec((tb, HEAD_DIM), lambda b, h, i: (b * nb + i, h)),
                  pl.BlockSpec((S, HEAD_DIM), lambda b, h, i: (b, h)),
                  pl.BlockSpec((S, HEAD_DIM), lambda b, h, i: (b, h)),
                  pl.BlockSpec((None, nb, HEAD_DIM), lambda b, h, i: (b, 0, h))],
        out_specs=pl.BlockSpec((tb, HEAD_DIM), lambda b, h, i: (b * nb + i, h)),
        out_shape=jax.ShapeDtypeStruct((T, MB_HEADS * HEAD_DIM), BF16),
        scratch_shapes=[pltpu.VMEM(((nb + 1) // 2, tb, 2 * tb), F32), pltpu.VMEM((tb, tb), F32)],
        compiler_params=_cparams(("arbitrary", "arbitrary", "arbitrary")),
        name="moba",
    )(qr, kr, vb, km)


DN_TILE_CHUNKS = 8


def _softplus(v):
    return jnp.maximum(v, 0.0) + jnp.log1p(jnp.exp(-jnp.abs(v)))


def _dot_hi(a, b):
    return jnp.dot(a, b, preferred_element_type=F32, precision=HIGHEST)


def _dot_bf(a, b):
    return jnp.dot(a.astype(BF16), b.astype(BF16), preferred_element_type=F32)


def _deltanet_kernel(q_ref, k_ref, v_ref, z_ref, sm_ref, wq_ref, wk_ref, wv_ref, alog_ref, dtb_ref, gout_ref,
                     o_ref, xp_scr, state_scr):
    h = pl.program_id(1)
    i = pl.program_id(2)
    C = DN_CHUNK
    TR = q_ref.shape[0]
    HALO = 8

    @pl.when(i == 0)
    def _():
        xp_scr[...] = jnp.zeros(xp_scr.shape, F32)
        state_scr[...] = jnp.zeros(state_scr.shape, F32)

    def conv_silu(slot, x_ref, w_ref):
        xp_scr[slot, 0:HALO, :] = xp_scr[slot, TR:TR + HALO, :]
        xp_scr[slot, HALO:HALO + TR, :] = x_ref[...]
        acc = w_ref[DN_CONV - 1:DN_CONV, :] * xp_scr[slot, HALO:HALO + TR, :]
        for j in range(1, DN_CONV):
            acc = acc + w_ref[DN_CONV - 1 - j:DN_CONV - j, :] * xp_scr[slot, HALO - j:HALO - j + TR, :]
        return _silu(acc)

    q = conv_silu(0, q_ref, wq_ref)
    k = conv_silu(1, k_ref, wk_ref)
    v = conv_silu(2, v_ref, wv_ref)
    q = q * lax.rsqrt(jnp.sum(q * q, axis=-1, keepdims=True) + 1e-6) * (HEAD_DIM ** -0.5)
    k = k * lax.rsqrt(jnp.sum(k * k, axis=-1, keepdims=True) + 1e-6)

    sm = sm_ref[...]
    lane = lax.broadcasted_iota(jnp.int32, sm.shape, 1)
    row = lax.broadcasted_iota(jnp.int32, sm.shape, 0)
    beta_all = _sigmoid(sm)
    g_all = -jnp.exp(alog_ref[...]) * _softplus(sm + dtb_ref[...])
    pos = row % C
    gc_all = g_all
    shift = 1
    while shift < C:
        gc_all = gc_all + jnp.where(pos >= shift, pltpu.roll(gc_all, shift, 0), 0.0)
        shift *= 2
    beta = jnp.sum(jnp.where(lane == h, beta_all, 0.0), axis=1, keepdims=True)
    gc = jnp.sum(jnp.where(lane == DN_HEADS + h, gc_all, 0.0), axis=1, keepdims=True)
    gc_t = jnp.transpose(gc_all)
    row_t = lax.broadcasted_iota(jnp.int32, gc_t.shape, 0)
    gc_row = jnp.sum(jnp.where(row_t == DN_HEADS + h, gc_t, 0.0), axis=0, keepdims=True)
    k_t = jnp.transpose(k)

    ri = lax.broadcasted_iota(jnp.int32, (C, C), 0)
    ci = lax.broadcasted_iota(jnp.int32, (C, C), 1)
    tril = ci <= ri
    strict = ci < ri
    eye = (ci == ri).astype(F32)

    state = state_scr[...]
    gout = gout_ref[...]
    for n in range(TR // C):
        r0 = n * C
        qn, kn, vn = q[r0:r0 + C], k[r0:r0 + C], v[r0:r0 + C]
        bn = beta[r0:r0 + C]
        gcn = gc[r0:r0 + C]
        gcr = gc_row[:, r0:r0 + C]
        knt = k_t[:, r0:r0 + C]
        g_last = gcn[C - 1:C, :]
        decay = jnp.where(tril, jnp.exp(jnp.where(tril, gcn - gcr, 0.0)), 0.0)
        kb = kn * bn
        m_neg = jnp.where(strict, -(_dot_bf(kb, knt) * decay), 0.0)
        t_inv = eye + m_neg
        p = m_neg
        for _ in range(5):
            p = _dot_hi(p, p)
            t_inv = t_inv + _dot_hi(t_inv, p)
        egc = jnp.exp(gcn)
        u = _dot_hi(t_inv, vn * bn)
        w = _dot_hi(t_inv, kb * egc)
        qk = _dot_bf(qn, knt) * decay
        kt_tail = knt * jnp.exp(g_last - gcr)
        q_dec = qn * egc
        v_new = u - _dot_bf(w, state)
        o = _dot_bf(q_dec, state) + _dot_bf(qk, v_new)
        state = state * jnp.exp(g_last) + _dot_bf(kt_tail, v_new)
        y = o * lax.rsqrt(jnp.mean(o * o, axis=-1, keepdims=True) + NORM_EPS) * gout
        o_ref[r0:r0 + C, :] = (y * _silu(z_ref[r0:r0 + C, :])).astype(o_ref.dtype)
    state_scr[...] = state


def _deltanet(proj, conv_w, a_log, dt_bias, g_dn_out, B, S):
    T = B * S
    TR = min(DN_TILE_CHUNKS * DN_CHUNK, S)
    nt = S // TR
    pad = jnp.zeros((DN_HEADS,), F32)
    rest = jnp.zeros((LANES - 2 * DN_HEADS,), F32)
    alog_lane = jnp.concatenate([pad, a_log.astype(F32), rest]).reshape(1, LANES)
    dtb_lane = jnp.concatenate([pad, dt_bias.astype(F32), rest]).reshape(1, LANES)
    rows = lambda c0: (lambda b, h, i: (b * nt + i, c0 + h))
    wcol = lambda c0: (lambda b, h, i: (0, c0 + h))
    const = lambda b, h, i: (0, 0)
    return pl.pallas_call(
        _deltanet_kernel,
        grid=(B, DN_HEADS, nt),
        in_specs=[pl.BlockSpec((TR, HEAD_DIM), rows(COL_DQ)),
                  pl.BlockSpec((TR, HEAD_DIM), rows(COL_DK)),
                  pl.BlockSpec((TR, HEAD_DIM), rows(COL_DV)),
                  pl.BlockSpec((TR, HEAD_DIM), rows(COL_DZ)),
                  pl.BlockSpec((TR, LANES), lambda b, h, i: (b * nt + i, COL_SMALL)),
                  pl.BlockSpec((DN_CONV, HEAD_DIM), wcol(0)),
                  pl.BlockSpec((DN_CONV, HEAD_DIM), wcol(DN_HEADS)),
                  pl.BlockSpec((DN_CONV, HEAD_DIM), wcol(2 * DN_HEADS)),
                  pl.BlockSpec((1, LANES), const),
                  pl.BlockSpec((1, LANES), const),
                  pl.BlockSpec((1, HEAD_DIM), const)],
        out_specs=pl.BlockSpec((TR, HEAD_DIM), lambda b, h, i: (b * nt + i, h)),
        out_shape=jax.ShapeDtypeStruct((T, DN_HEADS * HEAD_DIM), BF16),
        scratch_shapes=[pltpu.VMEM((3, TR + 8, HEAD_DIM), F32), pltpu.VMEM((HEAD_DIM, HEAD_DIM), F32)],
        compiler_params=_cparams(("arbitrary", "arbitrary", "arbitrary")),
        name="deltanet",
    )(proj, proj, proj, proj, proj, conv_w, conv_w, conv_w, alog_lane, dtb_lane, g_dn_out.reshape(1, HEAD_DIM))


def _rms(v):
    return v * lax.rsqrt(jnp.mean(v * v, axis=-1, keepdims=True) + NORM_EPS)


def _merge_kernel(oa_ref, ob_ref, ga_ref, gb_ref, x_ref, mod_ref, gpost_ref, gpre_ref, wa_ref, wb_ref, wo_ref,
                  wrt_ref, br_ref, x1_ref, hp_ref, idx_ref, wrow_ref, rank_ref, cnt_ref, carry_scr):
    E = N_EXPERTS
    tm = x_ref.shape[0]
    half = x_ref.shape[1] // 2

    @pl.when(pl.program_id(0) == 0)
    def _():
        carry_scr[...] = jnp.zeros(carry_scr.shape, F32)

    ya = jnp.dot(oa_ref[...], wa_ref[...], preferred_element_type=F32)
    yb = jnp.dot(ob_ref[...], wb_ref[...], preferred_element_type=F32)
    merged = _sigmoid(ga_ref[...]) * ya + _sigmoid(gb_ref[...]) * yb
    mix = jnp.dot(merged.astype(BF16), wo_ref[...], preferred_element_type=F32)
    x1 = x_ref[...] + mod_ref[0, 2:3, :] * (_rms(mix) * gpost_ref[...])
    x1_ref[...] = x1
    h2 = (_rms(x1) * gpre_ref[...]) * (1.0 + mod_ref[0, 4:5, :]) + mod_ref[0, 3:4, :]

    lo_bits = pltpu.bitcast(h2[:, :half].astype(BF16).astype(F32), jnp.uint32) >> 16
    hi_bits = pltpu.bitcast(h2[:, half:].astype(BF16).astype(F32), jnp.uint32) & jnp.uint32(0xFFFF0000)
    hp_ref[...] = hi_bits | lo_bits

    lt = lax.dot_general(wrt_ref[...], h2, (((1,), (1,)), ((), ())), preferred_element_type=F32,
                         precision=HIGHEST) + br_ref[...]
    eid = lax.broadcasted_iota(jnp.int32, (E, tm), 0)
    vals, idxs = [], []
    for _ in range(TOP_K):
        m = jnp.max(lt, axis=0, keepdims=True)
        idx = jnp.min(jnp.where(lt == m, eid, E), axis=0, keepdims=True)
        vals.append(m)
        idxs.append(idx)
        lt = jnp.where(eid == idx, -jnp.inf, lt)
    exps = [jnp.exp(v - vals[0]) for v in vals]
    den = exps[0] + exps[1] + exps[2] + exps[3]
    wts = [e / den for e in exps]

    hot = jnp.zeros((E, tm), F32)
    for idx in idxs:
        hot = hot + (eid == idx).astype(F32)
    ti = lax.broadcasted_iota(jnp.int32, (tm, tm), 0)
    tj = lax.broadcasted_iota(jnp.int32, (tm, tm), 1)
    before = (ti < tj).astype(BF16)
    prior = carry_scr[...][:, 0:1] + jnp.dot(hot.astype(BF16), before, preferred_element_type=F32)
    row8 = lax.broadcasted_iota(jnp.int32, (8, tm), 0)
    row128 = lax.broadcasted_iota(jnp.int32, (LANES, tm), 0)
    idx8 = jnp.zeros((8, tm), jnp.int32)
    rank8 = jnp.zeros((8, tm), jnp.int32)
    w128 = jnp.zeros((LANES, tm), F32)
    for r in range(TOP_K):
        rank_r = jnp.sum(jnp.where(eid == idxs[r], prior, 0.0), axis=0, keepdims=True)
        idx8 = jnp.where(row8 == r, idxs[r], idx8)
        rank8 = jnp.where(row8 == r, rank_r.astype(jnp.int32), rank8)
        w128 = jnp.where(row128 == r, wts[r], w128)
    idx_ref[...] = idx8
    rank_ref[...] = rank8
    wrow_ref[...] = jnp.transpose(w128)
    carry = carry_scr[...] + jnp.sum(hot, axis=1, keepdims=True)
    carry_scr[...] = carry
    cnt_ref[...] = carry


def _merge(oa, ob, proj, x2, mod3, g_post_mix, g_pre_ffn, w_br_a, w_br_b, w_o, w_router, b_router, S):
    T, D = x2.shape
    E = N_EXPERTS
    tm = min(512, S)
    per_b = S // tm
    W = DN_HEADS * HEAD_DIM
    row = lambda i: (i, 0)
    const = lambda i: (0, 0)
    lane_t = lambda i: (0, i)
    return pl.pallas_call(
        _merge_kernel,
        grid=(T // tm,),
        in_specs=[pl.BlockSpec((tm, W), row),
                  pl.BlockSpec((tm, W), row),
                  pl.BlockSpec((tm, D), lambda i: (i, COL_GA * LANES // D)),
                  pl.BlockSpec((tm, D), lambda i: (i, COL_GB * LANES // D)),
                  pl.BlockSpec((tm, D), row),
                  pl.BlockSpec((1, 6, D), lambda i: (i // per_b, 0, 0)),
                  pl.BlockSpec((1, D), const),
                  pl.BlockSpec((1, D), const),
                  pl.BlockSpec((W, D), const),
                  pl.BlockSpec((W, D), const),
                  pl.BlockSpec((D, D), const),
                  pl.BlockSpec((E, D), const),
                  pl.BlockSpec((E, 1), const)],
        out_specs=[pl.BlockSpec((tm, D), row),
                   pl.BlockSpec((tm, D // 2), row),
                   pl.BlockSpec((8, tm), lane_t),
                   pl.BlockSpec((tm, LANES), row),
                   pl.BlockSpec((8, tm), lane_t),
                   pl.BlockSpec((E, LANES), const)],
        out_shape=[jax.ShapeDtypeStruct((T, D), F32),
                   jax.ShapeDtypeStruct((T, D // 2), jnp.uint32),
                   jax.ShapeDtypeStruct((8, T), jnp.int32),
                   jax.ShapeDtypeStruct((T, LANES), F32),
                   jax.ShapeDtypeStruct((8, T), jnp.int32),
                   jax.ShapeDtypeStruct((E, LANES), F32)],
        scratch_shapes=[pltpu.VMEM((E, LANES), F32)],
        compiler_params=_cparams(("arbitrary",)),
        name="merge",
    )(oa, ob, proj, proj, x2, mod3, g_post_mix.reshape(1, D), g_pre_ffn.reshape(1, D),
      w_br_a.astype(BF16), w_br_b.astype(BF16), w_o.astype(BF16),
      jnp.transpose(w_router).astype(F32), b_router.reshape(E, 1).astype(F32))


DISPATCH_TOKENS = 256


def _dispatch_kernel(dest_ref, src_ref, dst_ref, sem):
    n = dest_ref.shape[1]

    def issue(j, carry):
        for r in range(TOP_K):
            pltpu.make_async_copy(src_ref.at[pl.ds(j, 1)], dst_ref.at[pl.ds(dest_ref[r, j], 1)], sem).start()
        return carry

    lax.fori_loop(0, n, issue, 0, unroll=8)
    for r in range(TOP_K):
        pltpu.make_async_copy(src_ref, dst_ref.at[pl.ds(0, n)], sem).wait()


def _dispatch(dest, hp):
    T, Wd = hp.shape
    n = min(DISPATCH_TOKENS, T)
    return pl.pallas_call(
        _dispatch_kernel,
        grid=(T // n,),
        in_specs=[pl.BlockSpec((TOP_K, n), lambda i: (0, i), memory_space=pltpu.SMEM),
                  pl.BlockSpec((n, Wd), lambda i: (i, 0))],
        out_specs=pl.BlockSpec(memory_space=pl.ANY),
        out_shape=jax.ShapeDtypeStruct((T * TOP_K, Wd), hp.dtype),
        scratch_shapes=[pltpu.SemaphoreType.DMA(())],
        compiler_params=_cparams(("arbitrary",)),
        name="dispatch",
    )(dest, hp)


EXPERT_ROWS = 512


def _experts_kernel(blk_ref, exp_ref, lo_ref, hi_ref, x_ref, wgu_ref, bgu_ref, wdn_ref, bdn_ref, o_ref):
    k = pl.program_id(0)
    lo = lo_ref[k]
    hi = hi_ref[k]
    tm = x_ref.shape[0]
    half = x_ref.shape[1]
    F = wdn_ref.shape[0]

    @pl.when(hi > lo)
    def _():
        word = x_ref[...]
        x_lo = pltpu.bitcast(word << 16, F32).astype(BF16)
        x_hi = pltpu.bitcast(word & jnp.uint32(0xFFFF0000), F32).astype(BF16)
        gu = (jnp.dot(x_lo, wgu_ref[0:half, :], preferred_element_type=F32)
              + jnp.dot(x_hi, wgu_ref[half:2 * half, :], preferred_element_type=F32) + bgu_ref[...])
        gate = jnp.minimum(gu[:, :F], SWIGLU_LIMIT)
        up = jnp.clip(gu[:, F:], -SWIGLU_LIMIT, SWIGLU_LIMIT)
        act = (up + 1.0) * gate * _sigmoid(SWIGLU_ALPHA * gate)
        y = jnp.dot(act.astype(BF16), wdn_ref[...], preferred_element_type=F32) + bdn_ref[...]
        rows = lax.broadcasted_iota(jnp.int32, (tm, 1), 0)
        keep = (rows >= lo) & (rows < hi)

        @pl.when(lo == 0)
        def _():
            o_ref[...] = jnp.where(keep, y, 0.0)

        @pl.when(lo > 0)
        def _():
            o_ref[...] = jnp.where(keep, y, o_ref[...])


def _experts(xs, item_blk, item_exp, item_lo, item_hi, w_gu, b_gu, w_down, b_down):
    A, half = xs.shape
    E, D, F2 = w_gu.shape
    F = F2 // 2
    tm = EXPERT_ROWS
    n_items = item_blk.shape[0]
    grid_spec = pltpu.PrefetchScalarGridSpec(
        num_scalar_prefetch=4,
        grid=(n_items,),
        in_specs=[pl.BlockSpec((tm, half), lambda k, blk, ex, lo, hi: (blk[k], 0)),
                  pl.BlockSpec((None, D, F2), lambda k, blk, ex, lo, hi: (ex[k], 0, 0)),
                  pl.BlockSpec((None, 1, F2), lambda k, blk, ex, lo, hi: (ex[k], 0, 0)),
                  pl.BlockSpec((None, F, D), lambda k, blk, ex, lo, hi: (ex[k], 0, 0)),
                  pl.BlockSpec((None, 1, D), lambda k, blk, ex, lo, hi: (ex[k], 0, 0))],
        out_specs=pl.BlockSpec((tm, D), lambda k, blk, ex, lo, hi: (blk[k], 0)),
    )
    return pl.pallas_call(
        _experts_kernel,
        grid_spec=grid_spec,
        out_shape=jax.ShapeDtypeStruct((A, D), F32),
        compiler_params=_cparams(("arbitrary",)),
        name="experts",
    )(item_blk, item_exp, item_lo, item_hi, xs, w_gu.astype(BF16), b_gu.reshape(E, 1, F2).astype(F32),
      w_down.astype(BF16), b_down.reshape(E, 1, D).astype(F32))


def _work_items(counts, A):
    E = N_EXPERTS
    tm = EXPERT_ROWS
    n_items = A // tm + E - 1
    end = jnp.cumsum(counts)
    start = end - counts
    first_blk = start // tm
    last_blk = jnp.maximum(end - 1, 0) // tm
    n_e = jnp.where(counts > 0, last_blk - first_blk + 1, 0)
    off_end = jnp.cumsum(n_e)
    off = off_end - n_e
    total = off_end[-1]
    k = jnp.arange(n_items, dtype=jnp.int32)
    kk = jnp.minimum(k, total - 1)
    e = jnp.searchsorted(off_end, kk, side='right').astype(jnp.int32)
    blk = (first_blk[e] + (kk - off[e])).astype(jnp.int32)
    lo = jnp.maximum(start[e], blk * tm) - blk * tm
    hi = jnp.minimum(end[e], (blk + 1) * tm) - blk * tm
    valid = k < total
    lo = jnp.where(valid, lo, 0).astype(jnp.int32)
    hi = jnp.where(valid, hi, 0).astype(jnp.int32)
    return blk, e, lo, hi


COMBINE_TOKENS = 256


def _combine_kernel(dest_ref, y_ref, wrow_ref, x1_ref, mod_ref, gpost_ref, o_ref, stage, sem):
    n = x1_ref.shape[0]

    def issue(j, carry):
        for r in range(TOP_K):
            pltpu.make_async_copy(y_ref.at[pl.ds(dest_ref[r, j], 1)], stage.at[r, pl.ds(j, 1)], sem).start()
        return carry

    lax.fori_loop(0, n, issue, 0, unroll=8)
    for r in range(TOP_K):
        pltpu.make_async_copy(y_ref.at[pl.ds(0, n)], stage.at[r], sem).wait()

    wrow = wrow_ref[...]
    moe = wrow[:, 0:1] * stage[0]
    for r in range(1, TOP_K):
        moe = moe + wrow[:, r:r + 1] * stage[r]
    o_ref[...] = x1_ref[...] + mod_ref[0, 5:6, :] * (_rms(moe) * gpost_ref[...])


def _combine(dest, y, wrow, x1, mod3, g_post_ffn, S):
    T, D = x1.shape
    n = min(COMBINE_TOKENS, S)
    per_b = S // n
    return pl.pallas_call(
        _combine_kernel,
        grid=(T // n,),
        in_specs=[pl.BlockSpec((TOP_K, n), lambda i: (0, i), memory_space=pltpu.SMEM),
                  pl.BlockSpec(memory_space=pl.ANY),
                  pl.BlockSpec((n, LANES), lambda i: (i, 0)),
                  pl.BlockSpec((n, D), lambda i: (i, 0)),
                  pl.BlockSpec((1, 6, D), lambda i: (i // per_b, 0, 0)),
                  pl.BlockSpec((1, D), lambda i: (0, 0))],
        out_specs=pl.BlockSpec((n, D), lambda i: (i, 0)),
        out_shape=jax.ShapeDtypeStruct((T, D), F32),
        scratch_shapes=[pltpu.VMEM((TOP_K, n, D), F32), pltpu.SemaphoreType.DMA(())],
        compiler_params=_cparams(("arbitrary",)),
        name="combine",
    )(dest, y, wrow, x1, mod3, g_post_ffn.reshape(1, D))


def _regroup_w_in(w_in):
    D = w_in.shape[0]
    dw = DN_HEADS * HEAD_DIM
    mw = MB_HEADS * HEAD_DIM
    cuts = np.cumsum([dw, dw, dw, dw, DN_HEADS, DN_HEADS, mw, mw, mw, D, D])[:-1]
    dq, dk, dv, dz, db, da, mq, mk, mv, ga, gb = jnp.split(w_in, [int(c) for c in cuts], axis=1)
    small = jnp.concatenate([db, da, jnp.zeros((D, LANES - 2 * DN_HEADS), w_in.dtype)], axis=1)
    return jnp.concatenate([ga, gb, dq, dk, dv, dz, mq, mk, mv, small], axis=1).astype(BF16)


def kernel(x, c, w_ada, b_ada, g_pre_mix, g_post_mix, g_pre_ffn, g_post_ffn, w_in, conv_w, a_log, dt_bias,
           g_dn_out, w_br_a, w_br_b, w_o, w_router, b_router, w_gu, b_gu, w_down, b_down):
    B, S, D = x.shape
    l = 0
    mod = _adaln(c, w_ada[l], b_ada[l]).reshape(B, 6, D)
    proj = _inproj(x, mod, g_pre_mix[l], _regroup_w_in(w_in[l]))
    qr, kr, vb, kmean = _mobaprep(proj, B, S)
    ob = _moba(qr, kr, vb, kmean, B, S)
    oa = _deltanet(proj, conv_w[l], a_log[l], dt_bias[l], g_dn_out[l], B, S)
    x1, hp, idx8, wrow, rank8, cnt = _merge(oa, ob, proj, x.reshape(B * S, D), mod, g_post_mix[l], g_pre_ffn[l],
                                            w_br_a[l], w_br_b[l], w_o[l], w_router[l], b_router[l], S)
    out = _moe(x1, hp, idx8, wrow, rank8, cnt, mod, g_post_ffn[l], w_gu[l], b_gu[l], w_down[l], b_down[l], S)
    return out.reshape(B, S, D)


def _moe(x1, hp, idx8, wrow, rank8, cnt, mod, g_post_ffn, w_gu, b_gu, w_down, b_down, S):
    T = x1.shape[0]
    counts = cnt[:, 0].astype(jnp.int32)
    start = jnp.cumsum(counts) - counts
    hot = idx8[:TOP_K, :, None] == jnp.arange(N_EXPERTS, dtype=jnp.int32)
    dest = rank8[:TOP_K] + jnp.sum(jnp.where(hot, start, 0), axis=-1)
    xs = _dispatch(dest, hp)
    blk, e, lo, hi = _work_items(counts, T * TOP_K)
    y = _experts(xs, blk, e, lo, hi, w_gu, b_gu, w_down, b_down)
    return _combine(dest, y, wrow, x1, mod, g_post_ffn, S)
```

```python
import functools

import jax
import jax.numpy as jnp
import numpy as np
from jax import lax
from jax.experimental import pallas as pl
from jax.experimental.pallas import tpu as pltpu

F32 = jnp.float32
BF16 = jnp.bfloat16
HIGHEST = lax.Precision.HIGHEST

HEAD_DIM = 128
DN_HEADS = 4
DN_CONV = 4
DN_CHUNK = 64
MB_HEADS = 4
MB_BLOCK = 256
MB_TOPK = 3
ROPE_THETA = 500000.0
ROPE_DIM = HEAD_DIM // 4
N_EXPERTS = 32
TOP_K = 4
SWIGLU_LIMIT = 7.0
SWIGLU_ALPHA = 1.702
NORM_EPS = 1e-6
LANES = 128
NEG_BIG = -1e30

_W = DN_HEADS * HEAD_DIM // LANES
COL_GA = 0
COL_GB = 8
COL_DQ = 16
COL_DK = 20
COL_DV = 24
COL_DZ = 28
COL_MQ = 32
COL_MK = 36
COL_MV = 40
COL_SMALL = 44
N_PROJ = 45 * LANES

VMEM_LIMIT = 56 * 1024 * 1024


def _cparams(sem):
    return pltpu.CompilerParams(dimension_semantics=sem, vmem_limit_bytes=VMEM_LIMIT)


def _silu(v):
    return v * (1.0 / (1.0 + jnp.exp(-v)))


def _sigmoid(v):
    return 1.0 / (1.0 + jnp.exp(-v))


def _adaln_kernel(c_ref, w_ref, b_ref, o_ref):
    a = _silu(c_ref[...])
    o_ref[...] = jnp.dot(a, w_ref[...], preferred_element_type=F32, precision=HIGHEST) + b_ref[...]


def _adaln(c, w_ada, b_ada):
    B, D = c.shape
    N = w_ada.shape[1]
    tn = D
    return pl.pallas_call(
        _adaln_kernel,
        grid=(N // tn,),
        in_specs=[pl.BlockSpec((B, D), lambda j: (0, 0)),
                  pl.BlockSpec((D, tn), lambda j: (0, j)),
                  pl.BlockSpec((1, tn), lambda j: (0, j))],
        out_specs=pl.BlockSpec((B, tn), lambda j: (0, j)),
        out_shape=jax.ShapeDtypeStruct((B, N), F32),
        compiler_params=_cparams(("arbitrary",)),
        name="adaln",
    )(c, w_ada, b_ada.reshape(1, N))


def _inproj_kernel(x_ref, mod_ref, g_ref, w_ref, o_ref, h_scr):
    @pl.when(pl.program_id(2) == 0)
    def _():
        x = x_ref[0]
        y = x * lax.rsqrt(jnp.mean(x * x, axis=-1, keepdims=True) + NORM_EPS) * g_ref[...]
        h = y * (1.0 + mod_ref[0, 1:2, :]) + mod_ref[0, 0:1, :]
        h_scr[...] = h.astype(BF16)

    o_ref[...] = jnp.dot(h_scr[...], w_ref[...], preferred_element_type=F32)


def _inproj(x, mod3, g_pre, w_all):
    B, S, D = x.shape
    tm = min(512, S)
    tn = N_PROJ // 3
    nrow = S // tm
    return pl.pallas_call(
        _inproj_kernel,
        grid=(B, nrow, N_PROJ // tn),
        in_specs=[pl.BlockSpec((1, tm, D), lambda b, i, j: (b, i, 0)),
                  pl.BlockSpec((1, 6, D), lambda b, i, j: (b, 0, 0)),
                  pl.BlockSpec((1, D), lambda b, i, j: (0, 0)),
                  pl.BlockSpec((D, tn), lambda b, i, j: (0, j))],
        out_specs=pl.BlockSpec((tm, tn), lambda b, i, j: (b * nrow + i, j)),
        out_shape=jax.ShapeDtypeStruct((B * S, N_PROJ), F32),
        scratch_shapes=[pltpu.VMEM((tm, D), BF16)],
        compiler_params=_cparams(("arbitrary", "arbitrary", "arbitrary")),
        name="inproj",
    )(x, mod3, g_pre.reshape(1, D), w_all)


def _rope(v, cosf, sinf, lane):
    rot = jnp.where(lane < ROPE_DIM // 2, pltpu.roll(v, LANES - ROPE_DIM // 2, 1), pltpu.roll(v, ROPE_DIM // 2, 1))
    return v * cosf + rot * sinf


def _mobaprep_kernel(q_ref, k_ref, v_ref, cos_ref, sin_ref, qo_ref, ko_ref, vo_ref, km_ref):
    cosf = cos_ref[...]
    sinf = sin_ref[...]
    lane = lax.broadcasted_iota(jnp.int32, cosf.shape, 1)
    for h in range(MB_HEADS):
        sl = slice(h * HEAD_DIM, (h + 1) * HEAD_DIM)
        qo_ref[:, sl] = (_rope(q_ref[:, sl], cosf, sinf, lane) * (HEAD_DIM ** -0.5)).astype(BF16)
        kr = _rope(k_ref[:, sl], cosf, sinf, lane)
        ko_ref[:, sl] = kr.astype(BF16)
        km_ref[0, :, sl] = jnp.mean(kr, axis=0, keepdims=True)
    vo_ref[...] = v_ref[...].astype(BF16)


def _rope_tables(S):
    half = ROPE_DIM // 2
    inv_freq = ROPE_THETA ** (-jnp.arange(half, dtype=F32) / half)
    ang = jnp.arange(S, dtype=F32)[:, None] * inv_freq[None, :]
    cos, sin = jnp.cos(ang), jnp.sin(ang)
    rest = HEAD_DIM - ROPE_DIM
    cosf = jnp.concatenate([cos, cos, jnp.ones((S, rest), F32)], axis=1)
    sinf = jnp.concatenate([-sin, sin, jnp.zeros((S, rest), F32)], axis=1)
    return cosf, sinf


def _mobaprep(proj, B, S):
    T = B * S
    tb = MB_BLOCK
    nb = S // tb
    W = MB_HEADS * HEAD_DIM
    cosf, sinf = _rope_tables(S)
    col = lambda c: (lambda i: (i, c))
    return pl.pallas_call(
        _mobaprep_kernel,
        grid=(T // tb,),
        in_specs=[pl.BlockSpec((tb, W), col(COL_MQ // _W)),
                  pl.BlockSpec((tb, W), col(COL_MK // _W)),
                  pl.BlockSpec((tb, W), col(COL_MV // _W)),
                  pl.BlockSpec((tb, LANES), lambda i: (i % nb, 0)),
                  pl.BlockSpec((tb, LANES), lambda i: (i % nb, 0))],
        out_specs=[pl.BlockSpec((tb, W), lambda i: (i, 0)),
                   pl.BlockSpec((tb, W), lambda i: (i, 0)),
                   pl.BlockSpec((tb, W), lambda i: (i, 0)),
                   pl.BlockSpec((1, 1, W), lambda i: (i, 0, 0))],
        out_shape=[jax.ShapeDtypeStruct((T, W), BF16),
                   jax.ShapeDtypeStruct((T, W), BF16),
                   jax.ShapeDtypeStruct((T, W), BF16),
                   jax.ShapeDtypeStruct((T // tb, 1, W), F32)],
        compiler_params=_cparams(("arbitrary",)),
        name="mobaprep",
    )(proj, proj, proj, cosf, sinf)


def _moba_kernel(q_ref, k_ref, v_ref, km_ref, o_ref, s_scr, sd_scr):
    qi = pl.program_id(2)
    tb = MB_BLOCK
    nb = km_ref.shape[0]
    nt = tb // LANES
    q = q_ref[...]

    st = lax.dot_general(km_ref[...], q.astype(F32), (((1,), (1,)), ((), ())),
                         preferred_element_type=F32, precision=HIGHEST)
    blk = lax.broadcasted_iota(jnp.int32, st.shape, 0)
    st = jnp.where(blk < qi, st, -jnp.inf)
    rowid = lax.broadcasted_iota(jnp.int32, (LANES, tb), 0)
    sel_t = jnp.full((LANES, tb), -1.0, F32)
    for r in range(MB_TOPK):
        m = jnp.max(st, axis=0, keepdims=True)
        idx = jnp.min(jnp.where(st == m, blk, nb), axis=0, keepdims=True)
        sel_t = jnp.where(rowid == r, jnp.where(r < qi, idx, -1).astype(F32), sel_t)
        st = jnp.where(blk == idx, -jnp.inf, st)
    sel = jnp.transpose(sel_t)
    sel_rep = [jnp.broadcast_to(sel[:, r:r + 1], (tb, LANES)) for r in range(MB_TOPK)]

    def logits(start, width):
        kslab = k_ref[pl.ds(pl.multiple_of(start, tb), width), :]
        return lax.dot_general(q, kslab, (((1,), (1,)), ((), ())), preferred_element_type=F32)

    s = logits(qi * tb, tb)
    r_i = lax.broadcasted_iota(jnp.int32, (tb, LANES), 0)
    c_i = lax.broadcasted_iota(jnp.int32, (tb, LANES), 1)
    mx = jnp.full((tb, LANES), NEG_BIG, F32)
    for t in range(nt):
        piece = jnp.where(c_i + t * LANES <= r_i, s[:, t * LANES:(t + 1) * LANES], NEG_BIG)
        sd_scr[:, t * LANES:(t + 1) * LANES] = piece
        mx = jnp.maximum(mx, piece)

    n_pairs = (qi + 1) // 2

    def pass1(i, mx):
        kb0 = 2 * i
        s2 = logits(kb0 * tb, 2 * tb)
        for half in range(2):
            kbf = (kb0 + half).astype(F32)
            hit = (sel_rep[0] == kbf) | (sel_rep[1] == kbf) | (sel_rep[2] == kbf)
            for t in range(nt):
                c0 = (half * nt + t) * LANES
                piece = jnp.where(hit, s2[:, c0:c0 + LANES], NEG_BIG)
                s_scr[i, :, c0:c0 + LANES] = piece
                mx = jnp.maximum(mx, piece)
        return mx

    mx = lax.fori_loop(0, n_pairs, pass1, mx)
    m_rep = jnp.broadcast_to(jnp.max(mx, axis=-1, keepdims=True), (tb, LANES))

    def probs(load, width):
        ps, lsum = [], jnp.zeros((tb, LANES), F32)
        for t in range(width // LANES):
            p = jnp.exp(load(t) - m_rep)
            lsum = lsum + p
            ps.append(p.astype(BF16))
        return jnp.concatenate(ps, axis=1), lsum

    pd, l0 = probs(lambda t: sd_scr[:, t * LANES:(t + 1) * LANES], tb)
    acc0 = jnp.dot(pd, v_ref[pl.ds(pl.multiple_of(qi * tb, tb), tb), :], preferred_element_type=F32)

    def pass2(i, carry):
        lsum, acc = carry
        p2, l2 = probs(lambda t: s_scr[i, :, t * LANES:(t + 1) * LANES], 2 * tb)
        vslab = v_ref[pl.ds(pl.multiple_of(2 * i * tb, tb), 2 * tb), :]
        return lsum + l2, acc + jnp.dot(p2, vslab, preferred_element_type=F32)

    lsum, acc = lax.fori_loop(0, n_pairs, pass2, (l0, acc0))
    o_ref[...] = (acc / jnp.sum(lsum, axis=-1, keepdims=True)).astype(o_ref.dtype)


def _moba(qr, kr, vb, kmean, B, S):
    tb = MB_BLOCK
    nb = S // tb
    T = B * S
    km = kmean.reshape(B, nb, MB_HEADS * HEAD_DIM)
    return pl.pallas_call(
        _moba_kernel,
        grid=(B, MB_HEADS, nb),
        in_specs=[pl.BlockSpec((tb, HEAD_DIM), lambda b, h, i: (b * nb + i, h)),
                  pl.BlockSpec((S, HEAD_DIM), lambda b, h, i: (b, h)),
                  pl.BlockSpec((S, HEAD_DIM), lambda b, h, i: (b, h)),
                  pl.BlockSpec((None, nb, HEAD_DIM), lambda b, h, i: (b, 0, h))],
        out_specs=pl.BlockSpec((tb, HEAD_DIM), lambda b, h, i: (b * nb + i, h)),
        out_shape=jax.ShapeDtypeStruct((T, MB_HEADS * HEAD_DIM), BF16),
        scratch_shapes=[pltpu.VMEM(((nb + 1) // 2, tb, 2 * tb), F32), pltpu.VMEM((tb, tb), F32)],
        compiler_params=_cparams(("arbitrary", "arbitrary", "arbitrary")),
        name="moba",
    )(qr, kr, vb, km)


DN_TILE_CHUNKS = 8


def _softplus(v):
    return jnp.maximum(v, 0.0) + jnp.log1p(jnp.exp(-jnp.abs(v)))


DN_GROUP = 4
DN_HPS = 4


def _split(a):
    hi = a.astype(BF16)
    return hi, (a - hi.astype(F32)).astype(BF16)


def _dot3(ah, al, bh, bl):
    lhs = jnp.concatenate([ah, ah, al], axis=1)
    rhs = jnp.concatenate([bh, bl, bh], axis=0)
    return jnp.dot(lhs, rhs, preferred_element_type=F32)


def _dot_bf(a, b):
    return jnp.dot(a.astype(BF16), b.astype(BF16), preferred_element_type=F32)


def _deltanet_kernel(q_ref, k_ref, v_ref, z_ref, sm_ref, wq_ref, wk_ref, wv_ref, alog_ref, dtb_ref, gout_ref,
                     o_ref, xp_scr, state_scr):
    i = pl.program_id(2)
    C = DN_CHUNK
    TR = q_ref.shape[0]
    HALO = 8

    @pl.when(i == 0)
    def _():
        xp_scr[...] = jnp.zeros(xp_scr.shape, F32)
        state_scr[...] = jnp.zeros(state_scr.shape, F32)

    def conv_silu(slot, x_ref, w_ref):
        xp_scr[slot, 0:HALO, :] = xp_scr[slot, TR:TR + HALO, :]
        xp_scr[slot, HALO:HALO + TR, :] = x_ref[...]
        acc = w_ref[DN_CONV - 1:DN_CONV, :] * xp_scr[slot, HALO:HALO + TR, :]
        for j in range(1, DN_CONV):
            acc = acc + w_ref[DN_CONV - 1 - j:DN_CONV - j, :] * xp_scr[slot, HALO - j:HALO - j + TR, :]
        return _silu(acc)

    q_all = conv_silu(0, q_ref, wq_ref)
    k_all = conv_silu(1, k_ref, wk_ref)
    v_all = conv_silu(2, v_ref, wv_ref)

    sm = sm_ref[...]
    lane = lax.broadcasted_iota(jnp.int32, sm.shape, 1)
    row = lax.broadcasted_iota(jnp.int32, sm.shape, 0)
    beta_all = _sigmoid(sm)
    g_all = -jnp.exp(alog_ref[...]) * _softplus(sm + dtb_ref[...])
    pos = row % C
    gc_all = g_all
    shift = 1
    while shift < C:
        gc_all = gc_all + jnp.where(pos >= shift, pltpu.roll(gc_all, shift, 0), 0.0)
        shift *= 2
    gc_t = jnp.transpose(gc_all)
    row_t = lax.broadcasted_iota(jnp.int32, gc_t.shape, 0)

    G = DN_GROUP * C
    ri = lax.broadcasted_iota(jnp.int32, (G, G), 0)
    ci = lax.broadcasted_iota(jnp.int32, (G, G), 1)
    same = (ri // C) == (ci // C)
    tril = same & (ci <= ri)
    strict = same & (ci < ri)
    eye = (ci == ri).astype(F32)
    D = HEAD_DIM
    gout = gout_ref[...]

    heads = []
    for hh in range(DN_HPS):
        h = pl.program_id(1) * DN_HPS + hh
        sl = slice(hh * D, (hh + 1) * D)
        q, k, v = q_all[:, sl], k_all[:, sl], v_all[:, sl]
        q = q * lax.rsqrt(jnp.sum(q * q, axis=-1, keepdims=True) + 1e-6) * (D ** -0.5)
        k = k * lax.rsqrt(jnp.sum(k * k, axis=-1, keepdims=True) + 1e-6)
        beta = jnp.sum(jnp.where(lane == h, beta_all, 0.0), axis=1, keepdims=True)
        gc = jnp.sum(jnp.where(lane == DN_HEADS + h, gc_all, 0.0), axis=1, keepdims=True)
        gc_row = jnp.sum(jnp.where(row_t == DN_HEADS + h, gc_t, 0.0), axis=0, keepdims=True)
        heads.append((q, k, v, beta, gc, gc_row, jnp.transpose(k)))

    steps = [[None] * (TR // C) for _ in range(DN_HPS)]
    preps = []
    for hh in range(DN_HPS):
        for g in range(TR // G):
            preps.append(_deltanet_group(heads[hh], g, tril, strict, eye, steps[hh]))
    _round_robin(preps)

    outs = [[] for _ in range(DN_HPS)]
    _round_robin([_deltanet_chain(hh, steps[hh], state_scr, outs[hh]) for hh in range(DN_HPS)])
    for hh in range(DN_HPS):
        sl = slice(hh * D, (hh + 1) * D)
        o = jnp.concatenate(outs[hh], axis=0)
        y = o * lax.rsqrt(jnp.mean(o * o, axis=-1, keepdims=True) + NORM_EPS) * gout
        o_ref[:, sl] = (y * _silu(z_ref[:, sl])).astype(o_ref.dtype)


def _round_robin(gens):
    active = list(gens)
    while active:
        still = []
        for gen in active:
            try:
                next(gen)
                still.append(gen)
            except StopIteration:
                pass
        active = still


def _deltanet_group(head, g, tril, strict, eye, steps_out):
    C = DN_CHUNK
    D = HEAD_DIM
    G = DN_GROUP * C
    q, k, v, beta, gc, gc_row, k_t = head
    r0 = g * G
    qg, kg, vg = q[r0:r0 + G], k[r0:r0 + G], v[r0:r0 + G]
    bg = beta[r0:r0 + G]
    gcg = gc[r0:r0 + G]
    gcr = gc_row[:, r0:r0 + G]
    ktg = k_t[:, r0:r0 + G]
    decay = jnp.where(tril, jnp.exp(jnp.where(tril, gcg - gcr, 0.0)), 0.0)
    kb = kg * bg
    aq = _dot_bf(jnp.concatenate([kb, qg], axis=0), ktg)
    yield
    m_neg = jnp.where(strict, -(aq[:G] * decay), 0.0)
    qk = (aq[G:] * decay).astype(BF16)
    t_inv = eye + m_neg
    ph, pl_ = _split(m_neg)
    for _ in range(5):
        th, tl = _split(t_inv)
        ph, pl_ = _split(_dot3(ph, pl_, ph, pl_))
        yield
        t_inv = t_inv + _dot3(th, tl, ph, pl_)
        yield
    egc = jnp.exp(gcg)
    th, tl = _split(t_inv)
    rh, rl = _split(jnp.concatenate([kb * egc, vg * bg], axis=1))
    wu = _dot3(th, tl, rh, rl).astype(BF16)
    yield
    qr = jnp.dot(qk, wu, preferred_element_type=F32)
    qp = qg * egc - qr[:, :D]
    r_all = qr[:, D:]
    yield
    for c in range(DN_GROUP):
        c0 = c * C
        g_last = gcg[c0 + C - 1:c0 + C, :]
        kt_tail = ktg[:, c0:c0 + C] * jnp.exp(g_last - gcr[:, c0:c0 + C])
        gh = jnp.dot(kt_tail.astype(BF16), wu[c0:c0 + C, :], preferred_element_type=F32)
        lhs = jnp.concatenate([gh[:, :D], qp[c0:c0 + C]], axis=0).astype(BF16)
        steps_out[g * DN_GROUP + c] = (lhs, gh[:, D:], r_all[c0:c0 + C], jnp.exp(g_last))
        yield


def _deltanet_chain(hh, steps, state_scr, outs):
    D = HEAD_DIM
    state = state_scr[hh]
    for lhs, h_add, r_add, dec in steps:
        res = jnp.dot(lhs, state.astype(BF16), preferred_element_type=F32)
        outs.append(res[D:] + r_add)
        state = state * dec - res[:D] + h_add
        yield
    state_scr[hh] = state


def _deltanet(proj, conv_w, a_log, dt_bias, g_dn_out, B, S):
    T = B * S
    TR = min(DN_TILE_CHUNKS * DN_CHUNK, S)
    nt = S // TR
    pad = jnp.zeros((DN_HEADS,), F32)
    rest = jnp.zeros((LANES - 2 * DN_HEADS,), F32)
    alog_lane = jnp.concatenate([pad, a_log.astype(F32), rest]).reshape(1, LANES)
    dtb_lane = jnp.concatenate([pad, dt_bias.astype(F32), rest]).reshape(1, LANES)
    HW = DN_HPS * HEAD_DIM
    per = HW // LANES
    rows = lambda c0: (lambda b, h, i: (b * nt + i, c0 // per + h))
    wcol = lambda c0: (lambda b, h, i: (0, c0 // per + h))
    const = lambda b, h, i: (0, 0)
    return pl.pallas_call(
        _deltanet_kernel,
        grid=(B, DN_HEADS // DN_HPS, nt),
        in_specs=[pl.BlockSpec((TR, HW), rows(COL_DQ)),
                  pl.BlockSpec((TR, HW), rows(COL_DK)),
                  pl.BlockSpec((TR, HW), rows(COL_DV)),
                  pl.BlockSpec((TR, HW), rows(COL_DZ)),
                  pl.BlockSpec((TR, LANES), lambda b, h, i: (b * nt + i, COL_SMALL)),
                  pl.BlockSpec((DN_CONV, HW), wcol(0)),
                  pl.BlockSpec((DN_CONV, HW), wcol(DN_HEADS)),
                  pl.BlockSpec((DN_CONV, HW), wcol(2 * DN_HEADS)),
                  pl.BlockSpec((1, LANES), const),
                  pl.BlockSpec((1, LANES), const),
                  pl.BlockSpec((1, HEAD_DIM), const)],
        out_specs=pl.BlockSpec((TR, HW), lambda b, h, i: (b * nt + i, h)),
        out_shape=jax.ShapeDtypeStruct((T, DN_HEADS * HEAD_DIM), BF16),
        scratch_shapes=[pltpu.VMEM((3, TR + 8, HW), F32), pltpu.VMEM((DN_HPS, HEAD_DIM, HEAD_DIM), F32)],
        compiler_params=_cparams(("arbitrary", "arbitrary", "arbitrary")),
        name="deltanet",
    )(proj, proj, proj, proj, proj, conv_w, conv_w, conv_w, alog_lane, dtb_lane, g_dn_out.reshape(1, HEAD_DIM))


def _rms(v):
    return v * lax.rsqrt(jnp.mean(v * v, axis=-1, keepdims=True) + NORM_EPS)


def _merge_kernel(oa_ref, ob_ref, ga_ref, gb_ref, x_ref, mod_ref, gpost_ref, gpre_ref, wa_ref, wb_ref, wo_ref,
                  wrt_ref, br_ref, x1_ref, hp_ref, idx_ref, wrow_ref, rank_ref, cnt_ref, carry_scr):
    E = N_EXPERTS
    tm = x_ref.shape[0]
    half = x_ref.shape[1] // 2

    @pl.when(pl.program_id(0) == 0)
    def _():
        carry_scr[...] = jnp.zeros(carry_scr.shape, F32)

    ya = jnp.dot(oa_ref[...], wa_ref[...], preferred_element_type=F32)
    yb = jnp.dot(ob_ref[...], wb_ref[...], preferred_element_type=F32)
    merged = _sigmoid(ga_ref[...]) * ya + _sigmoid(gb_ref[...]) * yb
    mix = jnp.dot(merged.astype(BF16), wo_ref[...], preferred_element_type=F32)
    x1 = x_ref[...] + mod_ref[0, 2:3, :] * (_rms(mix) * gpost_ref[...])
    x1_ref[...] = x1
    h2 = (_rms(x1) * gpre_ref[...]) * (1.0 + mod_ref[0, 4:5, :]) + mod_ref[0, 3:4, :]

    lo_bits = pltpu.bitcast(h2[:, :half].astype(BF16).astype(F32), jnp.uint32) >> 16
    hi_bits = pltpu.bitcast(h2[:, half:].astype(BF16).astype(F32), jnp.uint32) & jnp.uint32(0xFFFF0000)
    hp_ref[...] = hi_bits | lo_bits

    lt = lax.dot_general(wrt_ref[...], h2, (((1,), (1,)), ((), ())), preferred_element_type=F32,
                         precision=HIGHEST) + br_ref[...]
    eid = lax.broadcasted_iota(jnp.int32, (E, tm), 0)
    vals, idxs = [], []
    for _ in range(TOP_K):
        m = jnp.max(lt, axis=0, keepdims=True)
        idx = jnp.min(jnp.where(lt == m, eid, E), axis=0, keepdims=True)
        vals.append(m)
        idxs.append(idx)
        lt = jnp.where(eid == idx, -jnp.inf, lt)
    exps = [jnp.exp(v - vals[0]) for v in vals]
    den = exps[0] + exps[1] + exps[2] + exps[3]
    wts = [e / den for e in exps]

    hot = jnp.zeros((E, tm), F32)
    for idx in idxs:
        hot = hot + (eid == idx).astype(F32)
    ti = lax.broadcasted_iota(jnp.int32, (tm, tm), 0)
    tj = lax.broadcasted_iota(jnp.int32, (tm, tm), 1)
    before = (ti < tj).astype(BF16)
    prior = carry_scr[...][:, 0:1] + jnp.dot(hot.astype(BF16), before, preferred_element_type=F32)
    row8 = lax.broadcasted_iota(jnp.int32, (8, tm), 0)
    row128 = lax.broadcasted_iota(jnp.int32, (LANES, tm), 0)
    idx8 = jnp.zeros((8, tm), jnp.int32)
    rank8 = jnp.zeros((8, tm), jnp.int32)
    w128 = jnp.zeros((LANES, tm), F32)
    for r in range(TOP_K):
        rank_r = jnp.sum(jnp.where(eid == idxs[r], prior, 0.0), axis=0, keepdims=True)
        idx8 = jnp.where(row8 == r, idxs[r], idx8)
        rank8 = jnp.where(row8 == r, rank_r.astype(jnp.int32), rank8)
        w128 = jnp.where(row128 == r, wts[r], w128)
    idx_ref[...] = idx8
    rank_ref[...] = rank8
    wrow_ref[...] = jnp.transpose(w128)
    carry = carry_scr[...] + jnp.sum(hot, axis=1, keepdims=True)
    carry_scr[...] = carry
    cnt_ref[...] = carry


def _merge(oa, ob, proj, x2, mod3, g_post_mix, g_pre_ffn, w_br_a, w_br_b, w_o, w_router, b_router, S):
    T, D = x2.shape
    E = N_EXPERTS
    tm = min(512, S)
    per_b = S // tm
    W = DN_HEADS * HEAD_DIM
    row = lambda i: (i, 0)
    const = lambda i: (0, 0)
    lane_t = lambda i: (0, i)
    return pl.pallas_call(
        _merge_kernel,
        grid=(T // tm,),
        in_specs=[pl.BlockSpec((tm, W), row),
                  pl.BlockSpec((tm, W), row),
                  pl.BlockSpec((tm, D), lambda i: (i, COL_GA * LANES // D)),
                  pl.BlockSpec((tm, D), lambda i: (i, COL_GB * LANES // D)),
                  pl.BlockSpec((tm, D), row),
                  pl.BlockSpec((1, 6, D), lambda i: (i // per_b, 0, 0)),
                  pl.BlockSpec((1, D), const),
                  pl.BlockSpec((1, D), const),
                  pl.BlockSpec((W, D), const),
                  pl.BlockSpec((W, D), const),
                  pl.BlockSpec((D, D), const),
                  pl.BlockSpec((E, D), const),
                  pl.BlockSpec((E, 1), const)],
        out_specs=[pl.BlockSpec((tm, D), row),
                   pl.BlockSpec((tm, D // 2), row),
                   pl.BlockSpec((8, tm), lane_t),
                   pl.BlockSpec((tm, LANES), row),
                   pl.BlockSpec((8, tm), lane_t),
                   pl.BlockSpec((E, LANES), const)],
        out_shape=[jax.ShapeDtypeStruct((T, D), F32),
                   jax.ShapeDtypeStruct((T, D // 2), jnp.uint32),
                   jax.ShapeDtypeStruct((8, T), jnp.int32),
                   jax.ShapeDtypeStruct((T, LANES), F32),
                   jax.ShapeDtypeStruct((8, T), jnp.int32),
                   jax.ShapeDtypeStruct((E, LANES), F32)],
        scratch_shapes=[pltpu.VMEM((E, LANES), F32)],
        compiler_params=_cparams(("arbitrary",)),
        name="merge",
    )(oa, ob, proj, proj, x2, mod3, g_post_mix.reshape(1, D), g_pre_ffn.reshape(1, D),
      w_br_a.astype(BF16), w_br_b.astype(BF16), w_o.astype(BF16),
      jnp.transpose(w_router).astype(F32), b_router.reshape(E, 1).astype(F32))


DISPATCH_TOKENS = 256


def _dispatch_kernel(dest_ref, src_ref, dst_ref, sem):
    n = dest_ref.shape[1]

    def issue(j, carry):
        for r in range(TOP_K):
            pltpu.make_async_copy(src_ref.at[pl.ds(j, 1)], dst_ref.at[pl.ds(dest_ref[r, j], 1)], sem).start()
        return carry

    lax.fori_loop(0, n, issue, 0, unroll=8)
    for r in range(TOP_K):
        pltpu.make_async_copy(src_ref, dst_ref.at[pl.ds(0, n)], sem).wait()


def _dispatch(dest, hp):
    T, Wd = hp.shape
    n = min(DISPATCH_TOKENS, T)
    return pl.pallas_call(
        _dispatch_kernel,
        grid=(T // n,),
        in_specs=[pl.BlockSpec((TOP_K, n), lambda i: (0, i), memory_space=pltpu.SMEM),
                  pl.BlockSpec((n, Wd), lambda i: (i, 0))],
        out_specs=pl.BlockSpec(memory_space=pl.ANY),
        out_shape=jax.ShapeDtypeStruct((T * TOP_K, Wd), hp.dtype),
        scratch_shapes=[pltpu.SemaphoreType.DMA(())],
        compiler_params=_cparams(("arbitrary",)),
        name="dispatch",
    )(dest, hp)


EXPERT_ROWS = 512


def _experts_kernel(blk_ref, exp_ref, lo_ref, hi_ref, x_ref, wgu_ref, bgu_ref, wdn_ref, bdn_ref, o_ref):
    k = pl.program_id(0)
    lo = lo_ref[k]
    hi = hi_ref[k]
    tm = x_ref.shape[0]
    half = x_ref.shape[1]
    F = wdn_ref.shape[0]

    @pl.when(hi > lo)
    def _():
        word = x_ref[...]
        x_lo = pltpu.bitcast(word << 16, F32).astype(BF16)
        x_hi = pltpu.bitcast(word & jnp.uint32(0xFFFF0000), F32).astype(BF16)
        gu = (jnp.dot(x_lo, wgu_ref[0:half, :], preferred_element_type=F32)
              + jnp.dot(x_hi, wgu_ref[half:2 * half, :], preferred_element_type=F32) + bgu_ref[...])
        gate = jnp.minimum(gu[:, :F], SWIGLU_LIMIT)
        up = jnp.clip(gu[:, F:], -SWIGLU_LIMIT, SWIGLU_LIMIT)
        act = (up + 1.0) * gate * _sigmoid(SWIGLU_ALPHA * gate)
        y = jnp.dot(act.astype(BF16), wdn_ref[...], preferred_element_type=F32) + bdn_ref[...]
        rows = lax.broadcasted_iota(jnp.int32, (tm, 1), 0)
        keep = (rows >= lo) & (rows < hi)

        @pl.when(lo == 0)
        def _():
            o_ref[...] = jnp.where(keep, y, 0.0)

        @pl.when(lo > 0)
        def _():
            o_ref[...] = jnp.where(keep, y, o_ref[...])


def _experts(xs, item_blk, item_exp, item_lo, item_hi, w_gu, b_gu, w_down, b_down):
    A, half = xs.shape
    E, D, F2 = w_gu.shape
    F = F2 // 2
    tm = EXPERT_ROWS
    n_items = item_blk.shape[0]
    grid_spec = pltpu.PrefetchScalarGridSpec(
        num_scalar_prefetch=4,
        grid=(n_items,),
        in_specs=[pl.BlockSpec((tm, half), lambda k, blk, ex, lo, hi: (blk[k], 0)),
                  pl.BlockSpec((None, D, F2), lambda k, blk, ex, lo, hi: (ex[k], 0, 0)),
                  pl.BlockSpec((None, 1, F2), lambda k, blk, ex, lo, hi: (ex[k], 0, 0)),
                  pl.BlockSpec((None, F, D), lambda k, blk, ex, lo, hi: (ex[k], 0, 0)),
                  pl.BlockSpec((None, 1, D), lambda k, blk, ex, lo, hi: (ex[k], 0, 0))],
        out_specs=pl.BlockSpec((tm, D), lambda k, blk, ex, lo, hi: (blk[k], 0)),
    )
    return pl.pallas_call(
        _experts_kernel,
        grid_spec=grid_spec,
        out_shape=jax.ShapeDtypeStruct((A, D), F32),
        compiler_params=_cparams(("arbitrary",)),
        name="experts",
    )(item_blk, item_exp, item_lo, item_hi, xs, w_gu.astype(BF16), b_gu.reshape(E, 1, F2).astype(F32),
      w_down.astype(BF16), b_down.reshape(E, 1, D).astype(F32))


def _work_items(counts, A):
    E = N_EXPERTS
    tm = EXPERT_ROWS
    n_items = A // tm + E - 1
    end = jnp.cumsum(counts)
    start = end - counts
    first_blk = start // tm
    last_blk = jnp.maximum(end - 1, 0) // tm
    n_e = jnp.where(counts > 0, last_blk - first_blk + 1, 0)
    off_end = jnp.cumsum(n_e)
    off = off_end - n_e
    total = off_end[-1]
    k = jnp.arange(n_items, dtype=jnp.int32)
    kk = jnp.minimum(k, total - 1)
    e = jnp.searchsorted(off_end, kk, side='right').astype(jnp.int32)
    blk = (first_blk[e] + (kk - off[e])).astype(jnp.int32)
    lo = jnp.maximum(start[e], blk * tm) - blk * tm
    hi = jnp.minimum(end[e], (blk + 1) * tm) - blk * tm
    valid = k < total
    lo = jnp.where(valid, lo, 0).astype(jnp.int32)
    hi = jnp.where(valid, hi, 0).astype(jnp.int32)
    return blk, e, lo, hi


COMBINE_TOKENS = 256


def _combine_kernel(dest_ref, y_ref, wrow_ref, x1_ref, mod_ref, gpost_ref, o_ref, stage, sem):
    n = x1_ref.shape[0]

    def issue(j, carry):
        for r in range(TOP_K):
            pltpu.make_async_copy(y_ref.at[pl.ds(dest_ref[r, j], 1)], stage.at[r, pl.ds(j, 1)], sem).start()
        return carry

    lax.fori_loop(0, n, issue, 0, unroll=8)
    for r in range(TOP_K):
        pltpu.make_async_copy(y_ref.at[pl.ds(0, n)], stage.at[r], sem).wait()

    wrow = wrow_ref[...]
    moe = wrow[:, 0:1] * stage[0]
    for r in range(1, TOP_K):
        moe = moe + wrow[:, r:r + 1] * stage[r]
    o_ref[...] = x1_ref[...] + mod_ref[0, 5:6, :] * (_rms(moe) * gpost_ref[...])


def _combine(dest, y, wrow, x1, mod3, g_post_ffn, S):
    T, D = x1.shape
    n = min(COMBINE_TOKENS, S)
    per_b = S // n
    return pl.pallas_call(
        _combine_kernel,
        grid=(T // n,),
        in_specs=[pl.BlockSpec((TOP_K, n), lambda i: (0, i), memory_space=pltpu.SMEM),
                  pl.BlockSpec(memory_space=pl.ANY),
                  pl.BlockSpec((n, LANES), lambda i: (i, 0)),
                  pl.BlockSpec((n, D), lambda i: (i, 0)),
                  pl.BlockSpec((1, 6, D), lambda i: (i // per_b, 0, 0)),
                  pl.BlockSpec((1, D), lambda i: (0, 0))],
        out_specs=pl.BlockSpec((n, D), lambda i: (i, 0)),
        out_shape=jax.ShapeDtypeStruct((T, D), F32),
        scratch_shapes=[pltpu.VMEM((TOP_K, n, D), F32), pltpu.SemaphoreType.DMA(())],
        compiler_params=_cparams(("arbitrary",)),
        name="combine",
    )(dest, y, wrow, x1, mod3, g_post_ffn.reshape(1, D))


def _regroup_w_in(w_in):
    D = w_in.shape[0]
    dw = DN_HEADS * HEAD_DIM
    mw = MB_HEADS * HEAD_DIM
    cuts = np.cumsum([dw, dw, dw, dw, DN_HEADS, DN_HEADS, mw, mw, mw, D, D])[:-1]
    dq, dk, dv, dz, db, da, mq, mk, mv, ga, gb = jnp.split(w_in, [int(c) for c in cuts], axis=1)
    small = jnp.concatenate([db, da, jnp.zeros((D, LANES - 2 * DN_HEADS), w_in.dtype)], axis=1)
    return jnp.concatenate([ga, gb, dq, dk, dv, dz, mq, mk, mv, small], axis=1).astype(BF16)


def kernel(x, c, w_ada, b_ada, g_pre_mix, g_post_mix, g_pre_ffn, g_post_ffn, w_in, conv_w, a_log, dt_bias,
           g_dn_out, w_br_a, w_br_b, w_o, w_router, b_router, w_gu, b_gu, w_down, b_down):
    B, S, D = x.shape
    l = 0
    mod = _adaln(c, w_ada[l], b_ada[l]).reshape(B, 6, D)
    proj = _inproj(x, mod, g_pre_mix[l], _regroup_w_in(w_in[l]))
    qr, kr, vb, kmean = _mobaprep(proj, B, S)
    ob = _moba(qr, kr, vb, kmean, B, S)
    oa = _deltanet(proj, conv_w[l], a_log[l], dt_bias[l], g_dn_out[l], B, S)
    x1, hp, idx8, wrow, rank8, cnt = _merge(oa, ob, proj, x.reshape(B * S, D), mod, g_post_mix[l], g_pre_ffn[l],
                                            w_br_a[l], w_br_b[l], w_o[l], w_router[l], b_router[l], S)
    out = _moe(x1, hp, idx8, wrow, rank8, cnt, mod, g_post_ffn[l], w_gu[l], b_gu[l], w_down[l], b_down[l], S)
    return out.reshape(B, S, D)


def _moe(x1, hp, idx8, wrow, rank8, cnt, mod, g_post_ffn, w_gu, b_gu, w_down, b_down, S):
    T = x1.shape[0]
    counts = cnt[:, 0].astype(jnp.int32)
    start = jnp.cumsum(counts) - counts
    hot = idx8[:TOP_K, :, None] == jnp.arange(N_EXPERTS, dtype=jnp.int32)
    dest = rank8[:TOP_K] + jnp.sum(jnp.where(hot, start, 0), axis=-1)
    xs = _dispatch(dest, hp)
    blk, e, lo, hi = _work_items(counts, T * TOP_K)
    y = _experts(xs, blk, e, lo, hi, w_gu, b_gu, w_down, b_down)
    return _combine(dest, y, wrow, x1, mod, g_post_ffn, S)
```

```python
import functools

import jax
import jax.numpy as jnp
import numpy as np
from jax import lax
from jax.experimental import pallas as pl
from jax.experimental.pallas import tpu as pltpu

F32 = jnp.float32
BF16 = jnp.bfloat16
HIGHEST = lax.Precision.HIGHEST

HEAD_DIM = 128
DN_HEADS = 4
DN_CONV = 4
DN_CHUNK = 64
MB_HEADS = 4
MB_BLOCK = 256
MB_TOPK = 3
ROPE_THETA = 500000.0
ROPE_DIM = HEAD_DIM // 4
N_EXPERTS = 32
TOP_K = 4
SWIGLU_LIMIT = 7.0
SWIGLU_ALPHA = 1.702
NORM_EPS = 1e-6
LANES = 128
NEG_BIG = -1e30

_W = DN_HEADS * HEAD_DIM // LANES
COL_GA = 0
COL_GB = 8
COL_DQ = 16
COL_DK = 20
COL_DV = 24
COL_DZ = 28
COL_MQ = 32
COL_MK = 36
COL_MV = 40
COL_SMALL = 44
N_PROJ = 45 * LANES

VMEM_LIMIT = 56 * 1024 * 1024


def _cparams(sem):
    return pltpu.CompilerParams(dimension_semantics=sem, vmem_limit_bytes=VMEM_LIMIT)


def _silu(v):
    return v * (1.0 / (1.0 + jnp.exp(-v)))


def _sigmoid(v):
    return 1.0 / (1.0 + jnp.exp(-v))


def _adaln_kernel(c_ref, w_ref, b_ref, o_ref):
    a = _silu(c_ref[...])
    o_ref[...] = jnp.dot(a, w_ref[...], preferred_element_type=F32, precision=HIGHEST) + b_ref[...]


def _adaln(c, w_ada, b_ada):
    B, D = c.shape
    N = w_ada.shape[1]
    tn = D
    return pl.pallas_call(
        _adaln_kernel,
        grid=(N // tn,),
        in_specs=[pl.BlockSpec((B, D), lambda j: (0, 0)),
                  pl.BlockSpec((D, tn), lambda j: (0, j)),
                  pl.BlockSpec((1, tn), lambda j: (0, j))],
        out_specs=pl.BlockSpec((B, tn), lambda j: (0, j)),
        out_shape=jax.ShapeDtypeStruct((B, N), F32),
        compiler_params=_cparams(("arbitrary",)),
        name="adaln",
    )(c, w_ada, b_ada.reshape(1, N))


def _inproj_kernel(x_ref, mod_ref, g_ref, w_ref, o_ref, h_scr):
    @pl.when(pl.program_id(2) == 0)
    def _():
        x = x_ref[0]
        y = x * lax.rsqrt(jnp.mean(x * x, axis=-1, keepdims=True) + NORM_EPS) * g_ref[...]
        h = y * (1.0 + mod_ref[0, 1:2, :]) + mod_ref[0, 0:1, :]
        h_scr[...] = h.astype(BF16)

    o_ref[...] = jnp.dot(h_scr[...], w_ref[...], preferred_element_type=F32)


def _inproj(x, mod3, g_pre, w_all):
    B, S, D = x.shape
    tm = min(1024, S)
    tn = N_PROJ // 3
    nrow = S // tm
    return pl.pallas_call(
        _inproj_kernel,
        grid=(B, nrow, N_PROJ // tn),
        in_specs=[pl.BlockSpec((1, tm, D), lambda b, i, j: (b, i, 0)),
                  pl.BlockSpec((1, 6, D), lambda b, i, j: (b, 0, 0)),
                  pl.BlockSpec((1, D), lambda b, i, j: (0, 0)),
                  pl.BlockSpec((D, tn), lambda b, i, j: (0, j))],
        out_specs=pl.BlockSpec((tm, tn), lambda b, i, j: (b * nrow + i, j)),
        out_shape=jax.ShapeDtypeStruct((B * S, N_PROJ), F32),
        scratch_shapes=[pltpu.VMEM((tm, D), BF16)],
        compiler_params=_cparams(("arbitrary", "arbitrary", "arbitrary")),
        name="inproj",
    )(x, mod3, g_pre.reshape(1, D), w_all)


def _rope(v, cosf, sinf, lane):
    rot = jnp.where(lane < ROPE_DIM // 2, pltpu.roll(v, LANES - ROPE_DIM // 2, 1), pltpu.roll(v, ROPE_DIM // 2, 1))
    return v * cosf + rot * sinf


def _mobaprep_kernel(q_ref, k_ref, v_ref, cos_ref, sin_ref, qo_ref, ko_ref, vo_ref, km_ref):
    cosf = cos_ref[...]
    sinf = sin_ref[...]
    lane = lax.broadcasted_iota(jnp.int32, cosf.shape, 1)
    for h in range(MB_HEADS):
        sl = slice(h * HEAD_DIM, (h + 1) * HEAD_DIM)
        qo_ref[:, sl] = (_rope(q_ref[:, sl], cosf, sinf, lane) * (HEAD_DIM ** -0.5)).astype(BF16)
        kr = _rope(k_ref[:, sl], cosf, sinf, lane)
        ko_ref[:, sl] = kr.astype(BF16)
        km_ref[0, :, sl] = jnp.mean(kr, axis=0, keepdims=True)
    vo_ref[...] = v_ref[...].astype(BF16)


def _rope_tables(S):
    half = ROPE_DIM // 2
    inv_freq = ROPE_THETA ** (-jnp.arange(half, dtype=F32) / half)
    ang = jnp.arange(S, dtype=F32)[:, None] * inv_freq[None, :]
    cos, sin = jnp.cos(ang), jnp.sin(ang)
    rest = HEAD_DIM - ROPE_DIM
    cosf = jnp.concatenate([cos, cos, jnp.ones((S, rest), F32)], axis=1)
    sinf = jnp.concatenate([-sin, sin, jnp.zeros((S, rest), F32)], axis=1)
    return cosf, sinf


def _mobaprep(proj, B, S):
    T = B * S
    tb = MB_BLOCK
    nb = S // tb
    W = MB_HEADS * HEAD_DIM
    cosf, sinf = _rope_tables(S)
    col = lambda c: (lambda i: (i, c))
    return pl.pallas_call(
        _mobaprep_kernel,
        grid=(T // tb,),
        in_specs=[pl.BlockSpec((tb, W), col(COL_MQ // _W)),
                  pl.BlockSpec((tb, W), col(COL_MK // _W)),
                  pl.BlockSpec((tb, W), col(COL_MV // _W)),
                  pl.BlockSpec((tb, LANES), lambda i: (i % nb, 0)),
                  pl.BlockSpec((tb, LANES), lambda i: (i % nb, 0))],
        out_specs=[pl.BlockSpec((tb, W), lambda i: (i, 0)),
                   pl.BlockSpec((tb, W), lambda i: (i, 0)),
                   pl.BlockSpec((tb, W), lambda i: (i, 0)),
                   pl.BlockSpec((1, 1, W), lambda i: (i, 0, 0))],
        out_shape=[jax.ShapeDtypeStruct((T, W), BF16),
                   jax.ShapeDtypeStruct((T, W), BF16),
                   jax.ShapeDtypeStruct((T, W), BF16),
                   jax.ShapeDtypeStruct((T // tb, 1, W), F32)],
        compiler_params=_cparams(("arbitrary",)),
        name="mobaprep",
    )(proj, proj, proj, cosf, sinf)


def _moba_kernel(q_ref, k_ref, v_ref, km_ref, o_ref, s_scr, sd_scr):
    qi = pl.program_id(2)
    tb = MB_BLOCK
    nb = km_ref.shape[0]
    nt = tb // LANES
    q = q_ref[...]

    st = lax.dot_general(km_ref[...], q.astype(F32), (((1,), (1,)), ((), ())),
                         preferred_element_type=F32, precision=HIGHEST)
    blk = lax.broadcasted_iota(jnp.int32, st.shape, 0)
    st = jnp.where(blk < qi, st, -jnp.inf)
    rowid = lax.broadcasted_iota(jnp.int32, (LANES, tb), 0)
    sel_t = jnp.full((LANES, tb), -1.0, F32)
    for r in range(MB_TOPK):
        m = jnp.max(st, axis=0, keepdims=True)
        idx = jnp.min(jnp.where(st == m, blk, nb), axis=0, keepdims=True)
        sel_t = jnp.where(rowid == r, jnp.where(r < qi, idx, -1).astype(F32), sel_t)
        st = jnp.where(blk == idx, -jnp.inf, st)
    sel = jnp.transpose(sel_t)
    sel_rep = [jnp.broadcast_to(sel[:, r:r + 1], (tb, LANES)) for r in range(MB_TOPK)]

    def logits(start, width):
        kslab = k_ref[pl.ds(pl.multiple_of(start, tb), width), :]
        return lax.dot_general(q, kslab, (((1,), (1,)), ((), ())), preferred_element_type=F32)

    s = logits(qi * tb, tb)
    r_i = lax.broadcasted_iota(jnp.int32, (tb, LANES), 0)
    c_i = lax.broadcasted_iota(jnp.int32, (tb, LANES), 1)
    mx = jnp.full((tb, LANES), NEG_BIG, F32)
    for t in range(nt):
        piece = jnp.where(c_i + t * LANES <= r_i, s[:, t * LANES:(t + 1) * LANES], NEG_BIG)
        sd_scr[:, t * LANES:(t + 1) * LANES] = piece
        mx = jnp.maximum(mx, piece)

    n_pairs = (qi + 1) // 2

    def pass1(i, mx):
        kb0 = 2 * i
        s2 = logits(kb0 * tb, 2 * tb)
        for half in range(2):
            kbf = (kb0 + half).astype(F32)
            hit = (sel_rep[0] == kbf) | (sel_rep[1] == kbf) | (sel_rep[2] == kbf)
            for t in range(nt):
                c0 = (half * nt + t) * LANES
                piece = jnp.where(hit, s2[:, c0:c0 + LANES], NEG_BIG)
                s_scr[i, :, c0:c0 + LANES] = piece
                mx = jnp.maximum(mx, piece)
        return mx

    mx = lax.fori_loop(0, n_pairs, pass1, mx)
    m_rep = jnp.broadcast_to(jnp.max(mx, axis=-1, keepdims=True), (tb, LANES))

    def probs(load, width):
        ps, lsum = [], jnp.zeros((tb, LANES), F32)
        for t in range(width // LANES):
            p = jnp.exp(load(t) - m_rep)
            lsum = lsum + p
            ps.append(p.astype(BF16))
        return jnp.concatenate(ps, axis=1), lsum

    pd, l0 = probs(lambda t: sd_scr[:, t * LANES:(t + 1) * LANES], tb)
    acc0 = jnp.dot(pd, v_ref[pl.ds(pl.multiple_of(qi * tb, tb), tb), :], preferred_element_type=F32)

    def pass2(i, carry):
        lsum, acc = carry
        p2, l2 = probs(lambda t: s_scr[i, :, t * LANES:(t + 1) * LANES], 2 * tb)
        vslab = v_ref[pl.ds(pl.multiple_of(2 * i * tb, tb), 2 * tb), :]
        return lsum + l2, acc + jnp.dot(p2, vslab, preferred_element_type=F32)

    lsum, acc = lax.fori_loop(0, n_pairs, pass2, (l0, acc0))
    o_ref[...] = (acc / jnp.sum(lsum, axis=-1, keepdims=True)).astype(o_ref.dtype)


def _moba(qr, kr, vb, kmean, B, S):
    tb = MB_BLOCK
    nb = S // tb
    T = B * S
    km = kmean.reshape(B, nb, MB_HEADS * HEAD_DIM)
    return pl.pallas_call(
        _moba_kernel,
        grid=(B, MB_HEADS, nb),
        in_specs=[pl.BlockSpec((tb, HEAD_DIM), lambda b, h, i: (b * nb + i, h)),
                  pl.BlockSpec((S, HEAD_DIM), lambda b, h, i: (b, h)),
                  pl.BlockSpec((S, HEAD_DIM), lambda b, h, i: (b, h)),
                  pl.BlockSpec((None, nb, HEAD_DIM), lambda b, h, i: (b, 0, h))],
        out_specs=pl.BlockSpec((tb, HEAD_DIM), lambda b, h, i: (b * nb + i, h)),
        out_shape=jax.ShapeDtypeStruct((T, MB_HEADS * HEAD_DIM), BF16),
        scratch_shapes=[pltpu.VMEM(((nb + 1) // 2, tb, 2 * tb), F32), pltpu.VMEM((tb, tb), F32)],
        compiler_params=_cparams(("arbitrary", "arbitrary", "arbitrary")),
        name="moba",
    )(qr, kr, vb, km)


DN_TILE_CHUNKS = 8


def _softplus(v):
    return jnp.maximum(v, 0.0) + jnp.log1p(jnp.exp(-jnp.abs(v)))


DN_GROUP = 4
DN_HPS = 4


def _split(a):
    hi = a.astype(BF16)
    return hi, (a - hi.astype(F32)).astype(BF16)


def _dot3(ah, al, bh, bl):
    lhs = jnp.concatenate([ah, ah, al], axis=1)
    rhs = jnp.concatenate([bh, bl, bh], axis=0)
    return jnp.dot(lhs, rhs, preferred_element_type=F32)


def _dot_bf(a, b):
    return jnp.dot(a.astype(BF16), b.astype(BF16), preferred_element_type=F32)


def _deltanet_kernel(q_ref, k_ref, v_ref, z_ref, sm_ref, wq_ref, wk_ref, wv_ref, alog_ref, dtb_ref, gout_ref,
                     o_ref, xp_scr, state_scr):
    i = pl.program_id(2)
    C = DN_CHUNK
    TR = q_ref.shape[0]
    HALO = 8

    @pl.when(i == 0)
    def _():
        xp_scr[...] = jnp.zeros(xp_scr.shape, F32)
        state_scr[...] = jnp.zeros(state_scr.shape, F32)

    def conv_silu(slot, x_ref, w_ref):
        xp_scr[slot, 0:HALO, :] = xp_scr[slot, TR:TR + HALO, :]
        xp_scr[slot, HALO:HALO + TR, :] = x_ref[...]
        acc = w_ref[DN_CONV - 1:DN_CONV, :] * xp_scr[slot, HALO:HALO + TR, :]
        for j in range(1, DN_CONV):
            acc = acc + w_ref[DN_CONV - 1 - j:DN_CONV - j, :] * xp_scr[slot, HALO - j:HALO - j + TR, :]
        return _silu(acc)

    q_all = conv_silu(0, q_ref, wq_ref)
    k_all = conv_silu(1, k_ref, wk_ref)
    v_all = conv_silu(2, v_ref, wv_ref)

    sm = sm_ref[...]
    lane = lax.broadcasted_iota(jnp.int32, sm.shape, 1)
    row = lax.broadcasted_iota(jnp.int32, sm.shape, 0)
    beta_all = _sigmoid(sm)
    g_all = -jnp.exp(alog_ref[...]) * _softplus(sm + dtb_ref[...])
    pos = row % C
    gc_all = g_all
    shift = 1
    while shift < C:
        gc_all = gc_all + jnp.where(pos >= shift, pltpu.roll(gc_all, shift, 0), 0.0)
        shift *= 2
    gc_t = jnp.transpose(gc_all)
    row_t = lax.broadcasted_iota(jnp.int32, gc_t.shape, 0)

    G = DN_GROUP * C
    ri = lax.broadcasted_iota(jnp.int32, (G, G), 0)
    ci = lax.broadcasted_iota(jnp.int32, (G, G), 1)
    same = (ri // C) == (ci // C)
    tril = same & (ci <= ri)
    strict = same & (ci < ri)
    eye = (ci == ri).astype(F32)
    D = HEAD_DIM
    gout = gout_ref[...]

    heads = []
    for hh in range(DN_HPS):
        h = pl.program_id(1) * DN_HPS + hh
        sl = slice(hh * D, (hh + 1) * D)
        q, k, v = q_all[:, sl], k_all[:, sl], v_all[:, sl]
        q = q * lax.rsqrt(jnp.sum(q * q, axis=-1, keepdims=True) + 1e-6) * (D ** -0.5)
        k = k * lax.rsqrt(jnp.sum(k * k, axis=-1, keepdims=True) + 1e-6)
        beta = jnp.sum(jnp.where(lane == h, beta_all, 0.0), axis=1, keepdims=True)
        gc = jnp.sum(jnp.where(lane == DN_HEADS + h, gc_all, 0.0), axis=1, keepdims=True)
        gc_row = jnp.sum(jnp.where(row_t == DN_HEADS + h, gc_t, 0.0), axis=0, keepdims=True)
        heads.append((q, k, v, beta, gc, gc_row, jnp.transpose(k)))

    steps = [[None] * (TR // C) for _ in range(DN_HPS)]
    preps = []
    for hh in range(DN_HPS):
        for g in range(TR // G):
            preps.append(_deltanet_group(heads[hh], g, tril, strict, eye, steps[hh]))
    _round_robin(preps)

    outs = [[] for _ in range(DN_HPS)]
    _round_robin([_deltanet_chain(hh, steps[hh], state_scr, outs[hh]) for hh in range(DN_HPS)])
    for hh in range(DN_HPS):
        sl = slice(hh * D, (hh + 1) * D)
        o = jnp.concatenate(outs[hh], axis=0)
        y = o * lax.rsqrt(jnp.mean(o * o, axis=-1, keepdims=True) + NORM_EPS) * gout
        o_ref[:, sl] = (y * _silu(z_ref[:, sl])).astype(o_ref.dtype)


def _round_robin(gens):
    active = list(gens)
    while active:
        still = []
        for gen in active:
            try:
                next(gen)
                still.append(gen)
            except StopIteration:
                pass
        active = still


def _deltanet_group(head, g, tril, strict, eye, steps_out):
    C = DN_CHUNK
    D = HEAD_DIM
    G = DN_GROUP * C
    q, k, v, beta, gc, gc_row, k_t = head
    r0 = g * G
    qg, kg, vg = q[r0:r0 + G], k[r0:r0 + G], v[r0:r0 + G]
    bg = beta[r0:r0 + G]
    gcg = gc[r0:r0 + G]
    gcr = gc_row[:, r0:r0 + G]
    ktg = k_t[:, r0:r0 + G]
    decay = jnp.where(tril, jnp.exp(jnp.where(tril, gcg - gcr, 0.0)), 0.0)
    kb = kg * bg
    aq = _dot_bf(jnp.concatenate([kb, qg], axis=0), ktg)
    yield
    m_neg = jnp.where(strict, -(aq[:G] * decay), 0.0)
    qk = (aq[G:] * decay).astype(BF16)
    t_inv = eye + m_neg
    ph, pl_ = _split(m_neg)
    for _ in range(5):
        th, tl = _split(t_inv)
        ph, pl_ = _split(_dot3(ph, pl_, ph, pl_))
        yield
        t_inv = t_inv + _dot3(th, tl, ph, pl_)
        yield
    egc = jnp.exp(gcg)
    th, tl = _split(t_inv)
    rh, rl = _split(jnp.concatenate([kb * egc, vg * bg], axis=1))
    wu = _dot3(th, tl, rh, rl).astype(BF16)
    yield
    qr = jnp.dot(qk, wu, preferred_element_type=F32)
    qp = qg * egc - qr[:, :D]
    r_all = qr[:, D:]
    yield
    for c in range(DN_GROUP):
        c0 = c * C
        g_last = gcg[c0 + C - 1:c0 + C, :]
        kt_tail = ktg[:, c0:c0 + C] * jnp.exp(g_last - gcr[:, c0:c0 + C])
        gh = jnp.dot(kt_tail.astype(BF16), wu[c0:c0 + C, :], preferred_element_type=F32)
        lhs = jnp.concatenate([gh[:, :D], qp[c0:c0 + C]], axis=0).astype(BF16)
        steps_out[g * DN_GROUP + c] = (lhs, gh[:, D:], r_all[c0:c0 + C], jnp.exp(g_last))
        yield


def _deltanet_chain(hh, steps, state_scr, outs):
    D = HEAD_DIM
    state = state_scr[hh]
    for lhs, h_add, r_add, dec in steps:
        res = jnp.dot(lhs, state.astype(BF16), preferred_element_type=F32)
        outs.append(res[D:] + r_add)
        state = state * dec - res[:D] + h_add
        yield
    state_scr[hh] = state


def _deltanet(proj, conv_w, a_log, dt_bias, g_dn_out, B, S):
    T = B * S
    TR = min(DN_TILE_CHUNKS * DN_CHUNK, S)
    nt = S // TR
    pad = jnp.zeros((DN_HEADS,), F32)
    rest = jnp.zeros((LANES - 2 * DN_HEADS,), F32)
    alog_lane = jnp.concatenate([pad, a_log.astype(F32), rest]).reshape(1, LANES)
    dtb_lane = jnp.concatenate([pad, dt_bias.astype(F32), rest]).reshape(1, LANES)
    HW = DN_HPS * HEAD_DIM
    per = HW // LANES
    rows = lambda c0: (lambda b, h, i: (b * nt + i, c0 // per + h))
    wcol = lambda c0: (lambda b, h, i: (0, c0 // per + h))
    const = lambda b, h, i: (0, 0)
    return pl.pallas_call(
        _deltanet_kernel,
        grid=(B, DN_HEADS // DN_HPS, nt),
        in_specs=[pl.BlockSpec((TR, HW), rows(COL_DQ)),
                  pl.BlockSpec((TR, HW), rows(COL_DK)),
                  pl.BlockSpec((TR, HW), rows(COL_DV)),
                  pl.BlockSpec((TR, HW), rows(COL_DZ)),
                  pl.BlockSpec((TR, LANES), lambda b, h, i: (b * nt + i, COL_SMALL)),
                  pl.BlockSpec((DN_CONV, HW), wcol(0)),
                  pl.BlockSpec((DN_CONV, HW), wcol(DN_HEADS)),
                  pl.BlockSpec((DN_CONV, HW), wcol(2 * DN_HEADS)),
                  pl.BlockSpec((1, LANES), const),
                  pl.BlockSpec((1, LANES), const),
                  pl.BlockSpec((1, HEAD_DIM), const)],
        out_specs=pl.BlockSpec((TR, HW), lambda b, h, i: (b * nt + i, h)),
        out_shape=jax.ShapeDtypeStruct((T, DN_HEADS * HEAD_DIM), BF16),
        scratch_shapes=[pltpu.VMEM((3, TR + 8, HW), F32), pltpu.VMEM((DN_HPS, HEAD_DIM, HEAD_DIM), F32)],
        compiler_params=_cparams(("arbitrary", "arbitrary", "arbitrary")),
        name="deltanet",
    )(proj, proj, proj, proj, proj, conv_w, conv_w, conv_w, alog_lane, dtb_lane, g_dn_out.reshape(1, HEAD_DIM))


def _rms(v):
    return v * lax.rsqrt(jnp.mean(v * v, axis=-1, keepdims=True) + NORM_EPS)


def _merge_kernel(oa_ref, ob_ref, ga_ref, gb_ref, x_ref, mod_ref, gpost_ref, gpre_ref, wa_ref, wb_ref, wo_ref,
                  wrt_ref, br_ref, x1_ref, hp_ref, idx_ref, wrow_ref, rank_ref, cnt_ref, carry_scr):
    E = N_EXPERTS
    tm = x_ref.shape[0]
    half = x_ref.shape[1] // 2

    @pl.when(pl.program_id(0) == 0)
    def _():
        carry_scr[...] = jnp.zeros(carry_scr.shape, F32)

    ya = jnp.dot(oa_ref[...], wa_ref[...], preferred_element_type=F32)
    yb = jnp.dot(ob_ref[...], wb_ref[...], preferred_element_type=F32)
    merged = _sigmoid(ga_ref[...]) * ya + _sigmoid(gb_ref[...]) * yb
    mix = jnp.dot(merged.astype(BF16), wo_ref[...], preferred_element_type=F32)
    x1 = x_ref[...] + mod_ref[0, 2:3, :] * (_rms(mix) * gpost_ref[...])
    x1_ref[...] = x1
    h2 = (_rms(x1) * gpre_ref[...]) * (1.0 + mod_ref[0, 4:5, :]) + mod_ref[0, 3:4, :]

    lo_bits = pltpu.bitcast(h2[:, :half].astype(BF16).astype(F32), jnp.uint32) >> 16
    hi_bits = pltpu.bitcast(h2[:, half:].astype(BF16).astype(F32), jnp.uint32) & jnp.uint32(0xFFFF0000)
    hp_ref[...] = hi_bits | lo_bits

    lt = lax.dot_general(wrt_ref[...], h2, (((1,), (1,)), ((), ())), preferred_element_type=F32,
                         precision=HIGHEST) + br_ref[...]
    eid = lax.broadcasted_iota(jnp.int32, (E, tm), 0)
    vals, idxs = [], []
    for _ in range(TOP_K):
        m = jnp.max(lt, axis=0, keepdims=True)
        idx = jnp.min(jnp.where(lt == m, eid, E), axis=0, keepdims=True)
        vals.append(m)
        idxs.append(idx)
        lt = jnp.where(eid == idx, -jnp.inf, lt)
    exps = [jnp.exp(v - vals[0]) for v in vals]
    den = exps[0] + exps[1] + exps[2] + exps[3]
    wts = [e / den for e in exps]

    hot = jnp.zeros((E, tm), F32)
    for idx in idxs:
        hot = hot + (eid == idx).astype(F32)
    ti = lax.broadcasted_iota(jnp.int32, (tm, tm), 0)
    tj = lax.broadcasted_iota(jnp.int32, (tm, tm), 1)
    before = (ti < tj).astype(BF16)
    prior = carry_scr[...][:, 0:1] + jnp.dot(hot.astype(BF16), before, preferred_element_type=F32)
    row8 = lax.broadcasted_iota(jnp.int32, (8, tm), 0)
    row128 = lax.broadcasted_iota(jnp.int32, (LANES, tm), 0)
    idx8 = jnp.zeros((8, tm), jnp.int32)
    rank8 = jnp.zeros((8, tm), jnp.int32)
    w128 = jnp.zeros((LANES, tm), F32)
    for r in range(TOP_K):
        rank_r = jnp.sum(jnp.where(eid == idxs[r], prior, 0.0), axis=0, keepdims=True)
        idx8 = jnp.where(row8 == r, idxs[r], idx8)
        rank8 = jnp.where(row8 == r, rank_r.astype(jnp.int32), rank8)
        w128 = jnp.where(row128 == r, wts[r], w128)
    idx_ref[...] = idx8
    rank_ref[...] = rank8
    wrow_ref[...] = jnp.transpose(w128)
    carry = carry_scr[...] + jnp.sum(hot, axis=1, keepdims=True)
    carry_scr[...] = carry
    cnt_ref[...] = carry


def _merge(oa, ob, proj, x2, mod3, g_post_mix, g_pre_ffn, w_br_a, w_br_b, w_o, w_router, b_router, S):
    T, D = x2.shape
    E = N_EXPERTS
    tm = min(512, S)
    per_b = S // tm
    W = DN_HEADS * HEAD_DIM
    row = lambda i: (i, 0)
    const = lambda i: (0, 0)
    lane_t = lambda i: (0, i)
    return pl.pallas_call(
        _merge_kernel,
        grid=(T // tm,),
        in_specs=[pl.BlockSpec((tm, W), row),
                  pl.BlockSpec((tm, W), row),
                  pl.BlockSpec((tm, D), lambda i: (i, COL_GA * LANES // D)),
                  pl.BlockSpec((tm, D), lambda i: (i, COL_GB * LANES // D)),
                  pl.BlockSpec((tm, D), row),
                  pl.BlockSpec((1, 6, D), lambda i: (i // per_b, 0, 0)),
                  pl.BlockSpec((1, D), const),
                  pl.BlockSpec((1, D), const),
                  pl.BlockSpec((W, D), const),
                  pl.BlockSpec((W, D), const),
                  pl.BlockSpec((D, D), const),
                  pl.BlockSpec((E, D), const),
                  pl.BlockSpec((E, 1), const)],
        out_specs=[pl.BlockSpec((tm, D), row),
                   pl.BlockSpec((tm, D // 2), row),
                   pl.BlockSpec((8, tm), lane_t),
                   pl.BlockSpec((tm, LANES), row),
                   pl.BlockSpec((8, tm), lane_t),
                   pl.BlockSpec((E, LANES), const)],
        out_shape=[jax.ShapeDtypeStruct((T, D), F32),
                   jax.ShapeDtypeStruct((T, D // 2), jnp.uint32),
                   jax.ShapeDtypeStruct((8, T), jnp.int32),
                   jax.ShapeDtypeStruct((T, LANES), F32),
                   jax.ShapeDtypeStruct((8, T), jnp.int32),
                   jax.ShapeDtypeStruct((E, LANES), F32)],
        scratch_shapes=[pltpu.VMEM((E, LANES), F32)],
        compiler_params=_cparams(("arbitrary",)),
        name="merge",
    )(oa, ob, proj, proj, x2, mod3, g_post_mix.reshape(1, D), g_pre_ffn.reshape(1, D),
      w_br_a.astype(BF16), w_br_b.astype(BF16), w_o.astype(BF16),
      jnp.transpose(w_router).astype(F32), b_router.reshape(E, 1).astype(F32))


DISPATCH_TOKENS = 256


def _dispatch_kernel(dest_ref, src_ref, dst_ref, sem):
    n = dest_ref.shape[1]

    def issue(j, carry):
        for r in range(TOP_K):
            pltpu.make_async_copy(src_ref.at[pl.ds(j, 1)], dst_ref.at[pl.ds(dest_ref[r, j], 1)], sem).start(priority=r % 2)
        return carry

    lax.fori_loop(0, n, issue, 0, unroll=8)
    for r in range(TOP_K):
        pltpu.make_async_copy(src_ref, dst_ref.at[pl.ds(0, n)], sem).wait()


def _dispatch(dest, hp):
    T, Wd = hp.shape
    n = min(DISPATCH_TOKENS, T)
    return pl.pallas_call(
        _dispatch_kernel,
        grid=(T // n,),
        in_specs=[pl.BlockSpec((TOP_K, n), lambda i: (0, i), memory_space=pltpu.SMEM),
                  pl.BlockSpec((n, Wd), lambda i: (i, 0))],
        out_specs=pl.BlockSpec(memory_space=pl.ANY),
        out_shape=jax.ShapeDtypeStruct((T * TOP_K, Wd), hp.dtype),
        scratch_shapes=[pltpu.SemaphoreType.DMA(())],
        compiler_params=_cparams(("arbitrary",)),
        name="dispatch",
    )(dest, hp)


EXPERT_ROWS = 512
EXPERT_FEATURE_TILE = 1024


def _experts_kernel(blk_ref, exp_ref, lo_ref, hi_ref, new_ref, x_ref, wgu_ref, bgu_ref, wdn_ref, bdn_ref, o_ref,
                    wgu_bf, wdn_bf):
    k = pl.program_id(0)
    lo = lo_ref[k]
    hi = hi_ref[k]
    tm = x_ref.shape[0]
    F = wdn_ref.shape[0]

    @pl.when(new_ref[k] == 1)
    def _():
        wgu_bf[...] = wgu_ref[...].astype(BF16)
        wdn_bf[...] = wdn_ref[...].astype(BF16)

    @pl.when(hi > lo)
    def _():
        word = x_ref[...]
        x = jnp.concatenate([pltpu.bitcast(word << 16, F32).astype(BF16),
                             pltpu.bitcast(word & jnp.uint32(0xFFFF0000), F32).astype(BF16)], axis=1)
        y = None
        for c in range(0, F, EXPERT_FEATURE_TILE):
            ft = slice(c, c + EXPERT_FEATURE_TILE)
            ut = slice(F + c, F + c + EXPERT_FEATURE_TILE)
            gate = jnp.dot(x, wgu_bf[:, ft], preferred_element_type=F32) + bgu_ref[:, ft]
            up = jnp.dot(x, wgu_bf[:, ut], preferred_element_type=F32) + bgu_ref[:, ut]
            gate = jnp.minimum(gate, SWIGLU_LIMIT)
            up = jnp.clip(up, -SWIGLU_LIMIT, SWIGLU_LIMIT)
            act = (up + 1.0) * gate * _sigmoid(SWIGLU_ALPHA * gate)
            part = jnp.dot(act.astype(BF16), wdn_bf[ft, :], preferred_element_type=F32)
            y = part + bdn_ref[...] if y is None else y + part
        rows = lax.broadcasted_iota(jnp.int32, (tm, 1), 0)
        keep = (rows >= lo) & (rows < hi)

        @pl.when(lo == 0)
        def _():
            o_ref[...] = jnp.where(keep, y, 0.0)

        @pl.when(lo > 0)
        def _():
            o_ref[...] = jnp.where(keep, y, o_ref[...])


def _experts(xs, item_blk, item_exp, item_lo, item_hi, item_new, w_gu, b_gu, w_down, b_down):
    A, half = xs.shape
    E, D, F2 = w_gu.shape
    F = F2 // 2
    tm = EXPERT_ROWS
    n_items = item_blk.shape[0]
    grid_spec = pltpu.PrefetchScalarGridSpec(
        num_scalar_prefetch=5,
        grid=(n_items,),
        in_specs=[pl.BlockSpec((tm, half), lambda k, blk, ex, lo, hi, nw: (blk[k], 0)),
                  pl.BlockSpec((None, D, F2), lambda k, blk, ex, lo, hi, nw: (ex[k], 0, 0)),
                  pl.BlockSpec((None, 1, F2), lambda k, blk, ex, lo, hi, nw: (ex[k], 0, 0)),
                  pl.BlockSpec((None, F, D), lambda k, blk, ex, lo, hi, nw: (ex[k], 0, 0)),
                  pl.BlockSpec((None, 1, D), lambda k, blk, ex, lo, hi, nw: (ex[k], 0, 0))],
        out_specs=pl.BlockSpec((tm, D), lambda k, blk, ex, lo, hi, nw: (blk[k], 0)),
        scratch_shapes=[pltpu.VMEM((D, F2), BF16), pltpu.VMEM((F, D), BF16)],
    )
    return pl.pallas_call(
        _experts_kernel,
        grid_spec=grid_spec,
        out_shape=jax.ShapeDtypeStruct((A, D), F32),
        compiler_params=_cparams(("arbitrary",)),
        name="experts",
    )(item_blk, item_exp, item_lo, item_hi, item_new, xs, w_gu, b_gu.reshape(E, 1, F2).astype(F32),
      w_down, b_down.reshape(E, 1, D).astype(F32))


def _work_items(counts, A):
    E = N_EXPERTS
    tm = EXPERT_ROWS
    n_items = A // tm + E - 1
    end = jnp.cumsum(counts)
    start = end - counts
    first_blk = start // tm
    last_blk = jnp.maximum(end - 1, 0) // tm
    n_e = jnp.where(counts > 0, last_blk - first_blk + 1, 0)
    off_end = jnp.cumsum(n_e)
    off = off_end - n_e
    total = off_end[-1]
    k = jnp.arange(n_items, dtype=jnp.int32)
    kk = jnp.minimum(k, total - 1)
    e = jnp.searchsorted(off_end, kk, side='right').astype(jnp.int32)
    blk = (first_blk[e] + (kk - off[e])).astype(jnp.int32)
    lo = jnp.maximum(start[e], blk * tm) - blk * tm
    hi = jnp.minimum(end[e], (blk + 1) * tm) - blk * tm
    valid = k < total
    lo = jnp.where(valid, lo, 0).astype(jnp.int32)
    hi = jnp.where(valid, hi, 0).astype(jnp.int32)
    new = jnp.concatenate([jnp.ones((1,), jnp.int32), (e[1:] != e[:-1]).astype(jnp.int32)])
    return blk, e, lo, hi, new


COMBINE_TOKENS = 256


def _combine_kernel(dest_ref, y_ref, wrow_ref, x1_ref, mod_ref, gpost_ref, o_ref, stage, sem):
    n = x1_ref.shape[0]

    def issue(j, carry):
        for r in range(TOP_K):
            pltpu.make_async_copy(y_ref.at[pl.ds(dest_ref[r, j], 1)], stage.at[r, pl.ds(j, 1)], sem).start(priority=r % 2)
        return carry

    lax.fori_loop(0, n, issue, 0, unroll=8)
    for r in range(TOP_K):
        pltpu.make_async_copy(y_ref.at[pl.ds(0, n)], stage.at[r], sem).wait()

    wrow = wrow_ref[...]
    moe = wrow[:, 0:1] * stage[0]
    for r in range(1, TOP_K):
        moe = moe + wrow[:, r:r + 1] * stage[r]
    o_ref[...] = x1_ref[...] + mod_ref[0, 5:6, :] * (_rms(moe) * gpost_ref[...])


def _combine(dest, y, wrow, x1, mod3, g_post_ffn, S):
    T, D = x1.shape
    n = min(COMBINE_TOKENS, S)
    per_b = S // n
    return pl.pallas_call(
        _combine_kernel,
        grid=(T // n,),
        in_specs=[pl.BlockSpec((TOP_K, n), lambda i: (0, i), memory_space=pltpu.SMEM),
                  pl.BlockSpec(memory_space=pl.ANY),
                  pl.BlockSpec((n, LANES), lambda i: (i, 0)),
                  pl.BlockSpec((n, D), lambda i: (i, 0)),
                  pl.BlockSpec((1, 6, D), lambda i: (i // per_b, 0, 0)),
                  pl.BlockSpec((1, D), lambda i: (0, 0))],
        out_specs=pl.BlockSpec((n, D), lambda i: (i, 0)),
        out_shape=jax.ShapeDtypeStruct((T, D), F32),
        scratch_shapes=[pltpu.VMEM((TOP_K, n, D), F32), pltpu.SemaphoreType.DMA(())],
        compiler_params=_cparams(("arbitrary",)),
        name="combine",
    )(dest, y, wrow, x1, mod3, g_post_ffn.reshape(1, D))


def _regroup_w_in(w_in):
    D = w_in.shape[0]
    dw = DN_HEADS * HEAD_DIM
    mw = MB_HEADS * HEAD_DIM
    cuts = np.cumsum([dw, dw, dw, dw, DN_HEADS, DN_HEADS, mw, mw, mw, D, D])[:-1]
    dq, dk, dv, dz, db, da, mq, mk, mv, ga, gb = jnp.split(w_in, [int(c) for c in cuts], axis=1)
    small = jnp.concatenate([db, da, jnp.zeros((D, LANES - 2 * DN_HEADS), w_in.dtype)], axis=1)
    return jnp.concatenate([ga, gb, dq, dk, dv, dz, mq, mk, mv, small], axis=1).astype(BF16)


def kernel(x, c, w_ada, b_ada, g_pre_mix, g_post_mix, g_pre_ffn, g_post_ffn, w_in, conv_w, a_log, dt_bias,
           g_dn_out, w_br_a, w_br_b, w_o, w_router, b_router, w_gu, b_gu, w_down, b_down):
    B, S, D = x.shape
    l = 0
    mod = _adaln(c, w_ada[l], b_ada[l]).reshape(B, 6, D)
    proj = _inproj(x, mod, g_pre_mix[l], _regroup_w_in(w_in[l]))
    qr, kr, vb, kmean = _mobaprep(proj, B, S)
    ob = _moba(qr, kr, vb, kmean, B, S)
    oa = _deltanet(proj, conv_w[l], a_log[l], dt_bias[l], g_dn_out[l], B, S)
    x1, hp, idx8, wrow, rank8, cnt = _merge(oa, ob, proj, x.reshape(B * S, D), mod, g_post_mix[l], g_pre_ffn[l],
                                            w_br_a[l], w_br_b[l], w_o[l], w_router[l], b_router[l], S)
    out = _moe(x1, hp, idx8, wrow, rank8, cnt, mod, g_post_ffn[l], w_gu[l], b_gu[l], w_down[l], b_down[l], S)
    return out.reshape(B, S, D)


def _moe(x1, hp, idx8, wrow, rank8, cnt, mod, g_post_ffn, w_gu, b_gu, w_down, b_down, S):
    T = x1.shape[0]
    counts = cnt[:, 0].astype(jnp.int32)
    start = jnp.cumsum(counts) - counts
    hot = idx8[:TOP_K, :, None] == jnp.arange(N_EXPERTS, dtype=jnp.int32)
    dest = rank8[:TOP_K] + jnp.sum(jnp.where(hot, start, 0), axis=-1)
    xs = _dispatch(dest, hp)
    blk, e, lo, hi, new = _work_items(counts, T * TOP_K)
    y = _experts(xs, blk, e, lo, hi, new, w_gu, b_gu, w_down, b_down)
    return _combine(dest, y, wrow, x1, mod, g_post_ffn, S)
```

```python
import functools

import jax
import jax.numpy as jnp
import numpy as np
from jax import lax
from jax.experimental import pallas as pl
from jax.experimental.pallas import tpu as pltpu

F32 = jnp.float32
BF16 = jnp.bfloat16
HIGHEST = lax.Precision.HIGHEST

HEAD_DIM = 128
DN_HEADS = 4
DN_CONV = 4
DN_CHUNK = 64
MB_HEADS = 4
MB_BLOCK = 256
MB_TOPK = 3
ROPE_THETA = 500000.0
ROPE_DIM = HEAD_DIM // 4
N_EXPERTS = 32
TOP_K = 4
SWIGLU_LIMIT = 7.0
SWIGLU_ALPHA = 1.702
NORM_EPS = 1e-6
LANES = 128
NEG_BIG = -1e30

_W = DN_HEADS * HEAD_DIM // LANES
COL_GA = 0
COL_GB = 8
COL_DQ = 16
COL_DK = 20
COL_DV = 24
COL_DZ = 28
COL_MQ = 32
COL_MK = 36
COL_MV = 40
COL_SMALL = 44
N_PROJ = 45 * LANES

VMEM_LIMIT = 56 * 1024 * 1024


def _cparams(sem):
    return pltpu.CompilerParams(dimension_semantics=sem, vmem_limit_bytes=VMEM_LIMIT)


def _silu(v):
    return v * (1.0 / (1.0 + jnp.exp(-v)))


def _sigmoid(v):
    return 1.0 / (1.0 + jnp.exp(-v))


def _adaln_kernel(c_ref, w_ref, b_ref, o_ref):
    a = _silu(c_ref[...])
    o_ref[...] = jnp.dot(a, w_ref[...], preferred_element_type=F32, precision=HIGHEST) + b_ref[...]


def _adaln(c, w_ada, b_ada):
    B, D = c.shape
    N = w_ada.shape[1]
    tn = D
    return pl.pallas_call(
        _adaln_kernel,
        grid=(N // tn,),
        in_specs=[pl.BlockSpec((B, D), lambda j: (0, 0)),
                  pl.BlockSpec((D, tn), lambda j: (0, j)),
                  pl.BlockSpec((1, tn), lambda j: (0, j))],
        out_specs=pl.BlockSpec((B, tn), lambda j: (0, j)),
        out_shape=jax.ShapeDtypeStruct((B, N), F32),
        compiler_params=_cparams(("arbitrary",)),
        name="adaln",
    )(c, w_ada, b_ada.reshape(1, N))


def _inproj_kernel(x_ref, mod_ref, g_ref, w_ref, o_ref, h_scr):
    @pl.when(pl.program_id(2) == 0)
    def _():
        x = x_ref[0]
        y = x * lax.rsqrt(jnp.mean(x * x, axis=-1, keepdims=True) + NORM_EPS) * g_ref[...]
        h = y * (1.0 + mod_ref[0, 1:2, :]) + mod_ref[0, 0:1, :]
        h_scr[...] = h.astype(BF16)

    o_ref[...] = jnp.dot(h_scr[...], w_ref[...], preferred_element_type=F32)


def _inproj(x, mod3, g_pre, w_all):
    B, S, D = x.shape
    tm = min(1024, S)
    tn = N_PROJ // 3
    nrow = S // tm
    return pl.pallas_call(
        _inproj_kernel,
        grid=(B, nrow, N_PROJ // tn),
        in_specs=[pl.BlockSpec((1, tm, D), lambda b, i, j: (b, i, 0)),
                  pl.BlockSpec((1, 6, D), lambda b, i, j: (b, 0, 0)),
                  pl.BlockSpec((1, D), lambda b, i, j: (0, 0)),
                  pl.BlockSpec((D, tn), lambda b, i, j: (0, j))],
        out_specs=pl.BlockSpec((tm, tn), lambda b, i, j: (b * nrow + i, j)),
        out_shape=jax.ShapeDtypeStruct((B * S, N_PROJ), F32),
        scratch_shapes=[pltpu.VMEM((tm, D), BF16)],
        compiler_params=_cparams(("arbitrary", "arbitrary", "arbitrary")),
        name="inproj",
    )(x, mod3, g_pre.reshape(1, D), w_all)


def _rope(v, cosf, sinf, lane):
    rot = jnp.where(lane < ROPE_DIM // 2, pltpu.roll(v, LANES - ROPE_DIM // 2, 1), pltpu.roll(v, ROPE_DIM // 2, 1))
    return v * cosf + rot * sinf


def _mobaprep_kernel(q_ref, k_ref, v_ref, cos_ref, sin_ref, qo_ref, ko_ref, vo_ref, km_ref):
    cosf = cos_ref[...]
    sinf = sin_ref[...]
    lane = lax.broadcasted_iota(jnp.int32, cosf.shape, 1)
    for h in range(MB_HEADS):
        sl = slice(h * HEAD_DIM, (h + 1) * HEAD_DIM)
        qo_ref[:, sl] = (_rope(q_ref[:, sl], cosf, sinf, lane) * (HEAD_DIM ** -0.5)).astype(BF16)
        kr = _rope(k_ref[:, sl], cosf, sinf, lane)
        ko_ref[:, sl] = kr.astype(BF16)
        km_ref[0, :, sl] = jnp.mean(kr, axis=0, keepdims=True)
    vo_ref[...] = v_ref[...].astype(BF16)


def _rope_tables(S):
    half = ROPE_DIM // 2
    inv_freq = ROPE_THETA ** (-jnp.arange(half, dtype=F32) / half)
    ang = jnp.arange(S, dtype=F32)[:, None] * inv_freq[None, :]
    cos, sin = jnp.cos(ang), jnp.sin(ang)
    rest = HEAD_DIM - ROPE_DIM
    cosf = jnp.concatenate([cos, cos, jnp.ones((S, rest), F32)], axis=1)
    sinf = jnp.concatenate([-sin, sin, jnp.zeros((S, rest), F32)], axis=1)
    return cosf, sinf


def _mobaprep(proj, B, S):
    T = B * S
    tb = MB_BLOCK
    nb = S // tb
    W = MB_HEADS * HEAD_DIM
    cosf, sinf = _rope_tables(S)
    col = lambda c: (lambda i: (i, c))
    return pl.pallas_call(
        _mobaprep_kernel,
        grid=(T // tb,),
        in_specs=[pl.BlockSpec((tb, W), col(COL_MQ // _W)),
                  pl.BlockSpec((tb, W), col(COL_MK // _W)),
                  pl.BlockSpec((tb, W), col(COL_MV // _W)),
                  pl.BlockSpec((tb, LANES), lambda i: (i % nb, 0)),
                  pl.BlockSpec((tb, LANES), lambda i: (i % nb, 0))],
        out_specs=[pl.BlockSpec((tb, W), lambda i: (i, 0)),
                   pl.BlockSpec((tb, W), lambda i: (i, 0)),
                   pl.BlockSpec((tb, W), lambda i: (i, 0)),
                   pl.BlockSpec((1, 1, W), lambda i: (i, 0, 0))],
        out_shape=[jax.ShapeDtypeStruct((T, W), BF16),
                   jax.ShapeDtypeStruct((T, W), BF16),
                   jax.ShapeDtypeStruct((T, W), BF16),
                   jax.ShapeDtypeStruct((T // tb, 1, W), F32)],
        compiler_params=_cparams(("arbitrary",)),
        name="mobaprep",
    )(proj, proj, proj, cosf, sinf)


MB_HPS = 4


def _moba_kernel(q_ref, k_ref, v_ref, km_ref, o_ref, s_scr, sd_scr, mx_scr, l_scr, acc_scr):
    qi = pl.program_id(1)
    tb = MB_BLOCK
    nb = km_ref.shape[0]
    nt = tb // LANES
    D = HEAD_DIM
    heads = range(MB_HPS)
    hs = [slice(h * D, (h + 1) * D) for h in heads]
    qs = [q_ref[:, hs[h]] for h in heads]

    sts = [lax.dot_general(km_ref[:, hs[h]], qs[h].astype(F32), (((1,), (1,)), ((), ())),
                           preferred_element_type=F32, precision=HIGHEST) for h in heads]
    blk = lax.broadcasted_iota(jnp.int32, (nb, tb), 0)
    rowid = lax.broadcasted_iota(jnp.int32, (LANES, tb), 0)
    sel_ts = []
    for h in heads:
        st = jnp.where(blk < qi, sts[h], -jnp.inf)
        sel_t = jnp.full((LANES, tb), -1.0, F32)
        for r in range(MB_TOPK):
            m = jnp.max(st, axis=0, keepdims=True)
            idx = jnp.min(jnp.where(st == m, blk, nb), axis=0, keepdims=True)
            sel_t = jnp.where(rowid == r, jnp.where(r < qi, idx, -1).astype(F32), sel_t)
            st = jnp.where(blk == idx, -jnp.inf, st)
        sel_ts.append(sel_t)
    sels = [jnp.transpose(t) for t in sel_ts]
    sel_rep = [[jnp.broadcast_to(sels[h][:, r:r + 1], (tb, LANES)) for r in range(MB_TOPK)] for h in heads]

    def logits(h, start, width):
        kslab = k_ref[pl.ds(pl.multiple_of(start, tb), width), hs[h]]
        return lax.dot_general(qs[h], kslab, (((1,), (1,)), ((), ())), preferred_element_type=F32)

    r_i = lax.broadcasted_iota(jnp.int32, (tb, LANES), 0)
    c_i = lax.broadcasted_iota(jnp.int32, (tb, LANES), 1)
    sds = [logits(h, qi * tb, tb) for h in heads]
    for h in heads:
        mx = jnp.full((tb, LANES), NEG_BIG, F32)
        for t in range(nt):
            piece = jnp.where(c_i + t * LANES <= r_i, sds[h][:, t * LANES:(t + 1) * LANES], NEG_BIG)
            sd_scr[h, :, t * LANES:(t + 1) * LANES] = piece
            mx = jnp.maximum(mx, piece)
        mx_scr[h] = mx

    n_pairs = (qi + 1) // 2

    def pass1(i, carry):
        kb0 = 2 * i
        s2s = [logits(h, kb0 * tb, 2 * tb) for h in heads]
        for h in heads:
            mx = mx_scr[h]
            for half in range(2):
                kbf = (kb0 + half).astype(F32)
                hit = (sel_rep[h][0] == kbf) | (sel_rep[h][1] == kbf) | (sel_rep[h][2] == kbf)
                for t in range(nt):
                    c0 = (half * nt + t) * LANES
                    piece = jnp.where(hit, s2s[h][:, c0:c0 + LANES], NEG_BIG)
                    s_scr[h, i, :, c0:c0 + LANES] = piece
                    mx = jnp.maximum(mx, piece)
            mx_scr[h] = mx
        return carry

    lax.fori_loop(0, n_pairs, pass1, 0)
    m_reps = [jnp.broadcast_to(jnp.max(mx_scr[h], axis=-1, keepdims=True), (tb, LANES)) for h in heads]

    def probs(load, width, m_rep):
        ps, lsum = [], jnp.zeros((tb, LANES), F32)
        for t in range(width // LANES):
            p = jnp.exp(load(t) - m_rep)
            lsum = lsum + p
            ps.append(p.astype(BF16))
        return jnp.concatenate(ps, axis=1), lsum

    pls = [probs(lambda t, h=h: sd_scr[h, :, t * LANES:(t + 1) * LANES], tb, m_reps[h]) for h in heads]
    for h in heads:
        l_scr[h] = pls[h][1]
        acc_scr[h] = jnp.dot(pls[h][0], v_ref[pl.ds(pl.multiple_of(qi * tb, tb), tb), hs[h]],
                             preferred_element_type=F32)

    def pass2(i, carry):
        pls = [probs(lambda t, h=h: s_scr[h, i, :, t * LANES:(t + 1) * LANES], 2 * tb, m_reps[h]) for h in heads]
        for h in heads:
            vslab = v_ref[pl.ds(pl.multiple_of(2 * i * tb, tb), 2 * tb), hs[h]]
            l_scr[h] = l_scr[h] + pls[h][1]
            acc_scr[h] = acc_scr[h] + jnp.dot(pls[h][0], vslab, preferred_element_type=F32)
        return carry

    lax.fori_loop(0, n_pairs, pass2, 0)
    for h in heads:
        o_ref[:, hs[h]] = (acc_scr[h] / jnp.sum(l_scr[h], axis=-1, keepdims=True)).astype(o_ref.dtype)


def _moba(qr, kr, vb, kmean, B, S):
    tb = MB_BLOCK
    nb = S // tb
    T = B * S
    km = kmean.reshape(B, nb, MB_HEADS * HEAD_DIM)
    HW = MB_HPS * HEAD_DIM
    ng = MB_HEADS // MB_HPS
    return pl.pallas_call(
        _moba_kernel,
        grid=(B * ng, nb),
        in_specs=[pl.BlockSpec((tb, HW), lambda g, i: ((g // ng) * nb + i, g % ng)),
                  pl.BlockSpec((S, HW), lambda g, i: (g // ng, g % ng)),
                  pl.BlockSpec((S, HW), lambda g, i: (g // ng, g % ng)),
                  pl.BlockSpec((None, nb, HW), lambda g, i: (g // ng, 0, g % ng))],
        out_specs=pl.BlockSpec((tb, HW), lambda g, i: ((g // ng) * nb + i, g % ng)),
        out_shape=jax.ShapeDtypeStruct((T, MB_HEADS * HEAD_DIM), BF16),
        scratch_shapes=[pltpu.VMEM((MB_HPS, (nb + 1) // 2, tb, 2 * tb), F32),
                        pltpu.VMEM((MB_HPS, tb, tb), F32),
                        pltpu.VMEM((MB_HPS, tb, LANES), F32),
                        pltpu.VMEM((MB_HPS, tb, LANES), F32),
                        pltpu.VMEM((MB_HPS, tb, HEAD_DIM), F32)],
        compiler_params=_cparams(("arbitrary", "arbitrary")),
        name="moba",
    )(qr, kr, vb, km)


DN_TILE_CHUNKS = 8


def _softplus(v):
    return jnp.maximum(v, 0.0) + jnp.log1p(jnp.exp(-jnp.abs(v)))


DN_GROUP = 4
DN_HPS = 4


def _split(a):
    hi = a.astype(BF16)
    return hi, (a - hi.astype(F32)).astype(BF16)


def _dot3(ah, al, bh, bl):
    lhs = jnp.concatenate([ah, ah, al], axis=1)
    rhs = jnp.concatenate([bh, bl, bh], axis=0)
    return jnp.dot(lhs, rhs, preferred_element_type=F32)


def _dot_bf(a, b):
    return jnp.dot(a.astype(BF16), b.astype(BF16), preferred_element_type=F32)


def _deltanet_kernel(q_ref, k_ref, v_ref, z_ref, sm_ref, wq_ref, wk_ref, wv_ref, alog_ref, dtb_ref, gout_ref,
                     o_ref, xp_scr, state_scr):
    i = pl.program_id(2)
    C = DN_CHUNK
    TR = q_ref.shape[0]
    HALO = 8

    @pl.when(i == 0)
    def _():
        xp_scr[...] = jnp.zeros(xp_scr.shape, F32)
        state_scr[...] = jnp.zeros(state_scr.shape, F32)

    def conv_silu(slot, x_ref, w_ref):
        xp_scr[slot, 0:HALO, :] = xp_scr[slot, TR:TR + HALO, :]
        xp_scr[slot, HALO:HALO + TR, :] = x_ref[...]
        acc = w_ref[DN_CONV - 1:DN_CONV, :] * xp_scr[slot, HALO:HALO + TR, :]
        for j in range(1, DN_CONV):
            acc = acc + w_ref[DN_CONV - 1 - j:DN_CONV - j, :] * xp_scr[slot, HALO - j:HALO - j + TR, :]
        return _silu(acc)

    q_all = conv_silu(0, q_ref, wq_ref)
    k_all = conv_silu(1, k_ref, wk_ref)
    v_all = conv_silu(2, v_ref, wv_ref)

    sm = sm_ref[...]
    lane = lax.broadcasted_iota(jnp.int32, sm.shape, 1)
    row = lax.broadcasted_iota(jnp.int32, sm.shape, 0)
    beta_all = _sigmoid(sm)
    g_all = -jnp.exp(alog_ref[...]) * _softplus(sm + dtb_ref[...])
    pos = row % C
    gc_all = g_all
    shift = 1
    while shift < C:
        gc_all = gc_all + jnp.where(pos >= shift, pltpu.roll(gc_all, shift, 0), 0.0)
        shift *= 2
    gc_t = jnp.transpose(gc_all)
    row_t = lax.broadcasted_iota(jnp.int32, gc_t.shape, 0)

    G = DN_GROUP * C
    ri = lax.broadcasted_iota(jnp.int32, (G, G), 0)
    ci = lax.broadcasted_iota(jnp.int32, (G, G), 1)
    same = (ri // C) == (ci // C)
    tril = same & (ci <= ri)
    strict = same & (ci < ri)
    eye = (ci == ri).astype(F32)
    D = HEAD_DIM
    gout = gout_ref[...]

    heads = []
    for hh in range(DN_HPS):
        h = pl.program_id(1) * DN_HPS + hh
        sl = slice(hh * D, (hh + 1) * D)
        q, k, v = q_all[:, sl], k_all[:, sl], v_all[:, sl]
        q = q * lax.rsqrt(jnp.sum(q * q, axis=-1, keepdims=True) + 1e-6) * (D ** -0.5)
        k = k * lax.rsqrt(jnp.sum(k * k, axis=-1, keepdims=True) + 1e-6)
        beta = jnp.sum(jnp.where(lane == h, beta_all, 0.0), axis=1, keepdims=True)
        gc = jnp.sum(jnp.where(lane == DN_HEADS + h, gc_all, 0.0), axis=1, keepdims=True)
        gc_row = jnp.sum(jnp.where(row_t == DN_HEADS + h, gc_t, 0.0), axis=0, keepdims=True)
        heads.append((q, k, v, beta, gc, gc_row, jnp.transpose(k)))

    steps = [[None] * (TR // C) for _ in range(DN_HPS)]
    preps = []
    for hh in range(DN_HPS):
        for g in range(TR // G):
            preps.append(_deltanet_group(heads[hh], g, tril, strict, eye, steps[hh]))
    _round_robin(preps)

    outs = [[] for _ in range(DN_HPS)]
    _round_robin([_deltanet_chain(hh, steps[hh], state_scr, outs[hh]) for hh in range(DN_HPS)])
    for hh in range(DN_HPS):
        sl = slice(hh * D, (hh + 1) * D)
        o = jnp.concatenate(outs[hh], axis=0)
        y = o * lax.rsqrt(jnp.mean(o * o, axis=-1, keepdims=True) + NORM_EPS) * gout
        o_ref[:, sl] = (y * _silu(z_ref[:, sl])).astype(o_ref.dtype)


def _round_robin(gens):
    active = list(gens)
    while active:
        still = []
        for gen in active:
            try:
                next(gen)
                still.append(gen)
            except StopIteration:
                pass
        active = still


def _deltanet_group(head, g, tril, strict, eye, steps_out):
    C = DN_CHUNK
    D = HEAD_DIM
    G = DN_GROUP * C
    q, k, v, beta, gc, gc_row, k_t = head
    r0 = g * G
    qg, kg, vg = q[r0:r0 + G], k[r0:r0 + G], v[r0:r0 + G]
    bg = beta[r0:r0 + G]
    gcg = gc[r0:r0 + G]
    gcr = gc_row[:, r0:r0 + G]
    ktg = k_t[:, r0:r0 + G]
    decay = jnp.where(tril, jnp.exp(jnp.where(tril, gcg - gcr, 0.0)), 0.0)
    kb = kg * bg
    aq = _dot_bf(jnp.concatenate([kb, qg], axis=0), ktg)
    yield
    m_neg = jnp.where(strict, -(aq[:G] * decay), 0.0)
    qk = (aq[G:] * decay).astype(BF16)
    t_inv = eye + m_neg
    ph, pl_ = _split(m_neg)
    for _ in range(5):
        th, tl = _split(t_inv)
        ph, pl_ = _split(_dot3(ph, pl_, ph, pl_))
        yield
        t_inv = t_inv + _dot3(th, tl, ph, pl_)
        yield
    egc = jnp.exp(gcg)
    th, tl = _split(t_inv)
    rh, rl = _split(jnp.concatenate([kb * egc, vg * bg], axis=1))
    wu = _dot3(th, tl, rh, rl).astype(BF16)
    yield
    qr = jnp.dot(qk, wu, preferred_element_type=F32)
    qp = qg * egc - qr[:, :D]
    r_all = qr[:, D:]
    yield
    for c in range(DN_GROUP):
        c0 = c * C
        g_last = gcg[c0 + C - 1:c0 + C, :]
        kt_tail = ktg[:, c0:c0 + C] * jnp.exp(g_last - gcr[:, c0:c0 + C])
        gh = jnp.dot(kt_tail.astype(BF16), wu[c0:c0 + C, :], preferred_element_type=F32)
        lhs = jnp.concatenate([gh[:, :D], qp[c0:c0 + C]], axis=0).astype(BF16)
        steps_out[g * DN_GROUP + c] = (lhs, gh[:, D:], r_all[c0:c0 + C], jnp.exp(g_last))
        yield


def _deltanet_chain(hh, steps, state_scr, outs):
    D = HEAD_DIM
    state = state_scr[hh]
    for lhs, h_add, r_add, dec in steps:
        res = jnp.dot(lhs, state.astype(BF16), preferred_element_type=F32)
        outs.append(res[D:] + r_add)
        state = state * dec - res[:D] + h_add
        yield
    state_scr[hh] = state


def _deltanet(proj, conv_w, a_log, dt_bias, g_dn_out, B, S):
    T = B * S
    TR = min(DN_TILE_CHUNKS * DN_CHUNK, S)
    nt = S // TR
    pad = jnp.zeros((DN_HEADS,), F32)
    rest = jnp.zeros((LANES - 2 * DN_HEADS,), F32)
    alog_lane = jnp.concatenate([pad, a_log.astype(F32), rest]).reshape(1, LANES)
    dtb_lane = jnp.concatenate([pad, dt_bias.astype(F32), rest]).reshape(1, LANES)
    HW = DN_HPS * HEAD_DIM
    per = HW // LANES
    rows = lambda c0: (lambda b, h, i: (b * nt + i, c0 // per + h))
    wcol = lambda c0: (lambda b, h, i: (0, c0 // per + h))
    const = lambda b, h, i: (0, 0)
    return pl.pallas_call(
        _deltanet_kernel,
        grid=(B, DN_HEADS // DN_HPS, nt),
        in_specs=[pl.BlockSpec((TR, HW), rows(COL_DQ)),
                  pl.BlockSpec((TR, HW), rows(COL_DK)),
                  pl.BlockSpec((TR, HW), rows(COL_DV)),
                  pl.BlockSpec((TR, HW), rows(COL_DZ)),
                  pl.BlockSpec((TR, LANES), lambda b, h, i: (b * nt + i, COL_SMALL)),
                  pl.BlockSpec((DN_CONV, HW), wcol(0)),
                  pl.BlockSpec((DN_CONV, HW), wcol(DN_HEADS)),
                  pl.BlockSpec((DN_CONV, HW), wcol(2 * DN_HEADS)),
                  pl.BlockSpec((1, LANES), const),
                  pl.BlockSpec((1, LANES), const),
                  pl.BlockSpec((1, HEAD_DIM), const)],
        out_specs=pl.BlockSpec((TR, HW), lambda b, h, i: (b * nt + i, h)),
        out_shape=jax.ShapeDtypeStruct((T, DN_HEADS * HEAD_DIM), BF16),
        scratch_shapes=[pltpu.VMEM((3, TR + 8, HW), F32), pltpu.VMEM((DN_HPS, HEAD_DIM, HEAD_DIM), F32)],
        compiler_params=_cparams(("arbitrary", "arbitrary", "arbitrary")),
        name="deltanet",
    )(proj, proj, proj, proj, proj, conv_w, conv_w, conv_w, alog_lane, dtb_lane, g_dn_out.reshape(1, HEAD_DIM))


def _rms(v):
    return v * lax.rsqrt(jnp.mean(v * v, axis=-1, keepdims=True) + NORM_EPS)


def _merge_kernel(oa_ref, ob_ref, ga_ref, gb_ref, x_ref, mod_ref, gpost_ref, gpre_ref, wa_ref, wb_ref, wo_ref,
                  wrt_ref, br_ref, x1_ref, hp_ref, idx_ref, wrow_ref, rank_ref, cnt_ref, carry_scr):
    E = N_EXPERTS
    tm = x_ref.shape[0]
    half = x_ref.shape[1] // 2

    @pl.when(pl.program_id(0) == 0)
    def _():
        carry_scr[...] = jnp.zeros(carry_scr.shape, F32)

    ya = jnp.dot(oa_ref[...], wa_ref[...], preferred_element_type=F32)
    yb = jnp.dot(ob_ref[...], wb_ref[...], preferred_element_type=F32)
    merged = _sigmoid(ga_ref[...]) * ya + _sigmoid(gb_ref[...]) * yb
    mix = jnp.dot(merged.astype(BF16), wo_ref[...], preferred_element_type=F32)
    x1 = x_ref[...] + mod_ref[0, 2:3, :] * (_rms(mix) * gpost_ref[...])
    x1_ref[...] = x1
    h2 = (_rms(x1) * gpre_ref[...]) * (1.0 + mod_ref[0, 4:5, :]) + mod_ref[0, 3:4, :]

    lo_bits = pltpu.bitcast(h2[:, :half].astype(BF16).astype(F32), jnp.uint32) >> 16
    hi_bits = pltpu.bitcast(h2[:, half:].astype(BF16).astype(F32), jnp.uint32) & jnp.uint32(0xFFFF0000)
    hp_ref[...] = hi_bits | lo_bits

    lt = lax.dot_general(wrt_ref[...], h2, (((1,), (1,)), ((), ())), preferred_element_type=F32,
                         precision=HIGHEST) + br_ref[...]
    eid = lax.broadcasted_iota(jnp.int32, (E, tm), 0)
    vals, idxs = [], []
    for _ in range(TOP_K):
        m = jnp.max(lt, axis=0, keepdims=True)
        idx = jnp.min(jnp.where(lt == m, eid, E), axis=0, keepdims=True)
        vals.append(m)
        idxs.append(idx)
        lt = jnp.where(eid == idx, -jnp.inf, lt)
    exps = [jnp.exp(v - vals[0]) for v in vals]
    den = exps[0] + exps[1] + exps[2] + exps[3]
    wts = [e / den for e in exps]

    hot = jnp.zeros((E, tm), F32)
    for idx in idxs:
        hot = hot + (eid == idx).astype(F32)
    ti = lax.broadcasted_iota(jnp.int32, (tm, tm), 0)
    tj = lax.broadcasted_iota(jnp.int32, (tm, tm), 1)
    before = (ti < tj).astype(BF16)
    prior = carry_scr[...][:, 0:1] + jnp.dot(hot.astype(BF16), before, preferred_element_type=F32)
    row8 = lax.broadcasted_iota(jnp.int32, (8, tm), 0)
    row128 = lax.broadcasted_iota(jnp.int32, (LANES, tm), 0)
    idx8 = jnp.zeros((8, tm), jnp.int32)
    rank8 = jnp.zeros((8, tm), jnp.int32)
    w128 = jnp.zeros((LANES, tm), F32)
    for r in range(TOP_K):
        rank_r = jnp.sum(jnp.where(eid == idxs[r], prior, 0.0), axis=0, keepdims=True)
        idx8 = jnp.where(row8 == r, idxs[r], idx8)
        rank8 = jnp.where(row8 == r, rank_r.astype(jnp.int32), rank8)
        w128 = jnp.where(row128 == r, wts[r], w128)
    idx_ref[...] = idx8
    rank_ref[...] = rank8
    wrow_ref[...] = jnp.transpose(w128)
    carry = carry_scr[...] + jnp.sum(hot, axis=1, keepdims=True)
    carry_scr[...] = carry
    cnt_ref[...] = carry


def _merge(oa, ob, proj, x2, mod3, g_post_mix, g_pre_ffn, w_br_a, w_br_b, w_o, w_router, b_router, S):
    T, D = x2.shape
    E = N_EXPERTS
    tm = min(512, S)
    per_b = S // tm
    W = DN_HEADS * HEAD_DIM
    row = lambda i: (i, 0)
    const = lambda i: (0, 0)
    lane_t = lambda i: (0, i)
    return pl.pallas_call(
        _merge_kernel,
        grid=(T // tm,),
        in_specs=[pl.BlockSpec((tm, W), row),
                  pl.BlockSpec((tm, W), row),
                  pl.BlockSpec((tm, D), lambda i: (i, COL_GA * LANES // D)),
                  pl.BlockSpec((tm, D), lambda i: (i, COL_GB * LANES // D)),
                  pl.BlockSpec((tm, D), row),
                  pl.BlockSpec((1, 6, D), lambda i: (i // per_b, 0, 0)),
                  pl.BlockSpec((1, D), const),
                  pl.BlockSpec((1, D), const),
                  pl.BlockSpec((W, D), const),
                  pl.BlockSpec((W, D), const),
                  pl.BlockSpec((D, D), const),
                  pl.BlockSpec((E, D), const),
                  pl.BlockSpec((E, 1), const)],
        out_specs=[pl.BlockSpec((tm, D), row),
                   pl.BlockSpec((tm, D // 2), row),
                   pl.BlockSpec((8, tm), lane_t),
                   pl.BlockSpec((tm, LANES), row),
                   pl.BlockSpec((8, tm), lane_t),
                   pl.BlockSpec((E, LANES), const)],
        out_shape=[jax.ShapeDtypeStruct((T, D), F32),
                   jax.ShapeDtypeStruct((T, D // 2), jnp.uint32),
                   jax.ShapeDtypeStruct((8, T), jnp.int32),
                   jax.ShapeDtypeStruct((T, LANES), F32),
                   jax.ShapeDtypeStruct((8, T), jnp.int32),
                   jax.ShapeDtypeStruct((E, LANES), F32)],
        scratch_shapes=[pltpu.VMEM((E, LANES), F32)],
        compiler_params=_cparams(("arbitrary",)),
        name="merge",
    )(oa, ob, proj, proj, x2, mod3, g_post_mix.reshape(1, D), g_pre_ffn.reshape(1, D),
      w_br_a.astype(BF16), w_br_b.astype(BF16), w_o.astype(BF16),
      jnp.transpose(w_router).astype(F32), b_router.reshape(E, 1).astype(F32))


DISPATCH_TOKENS = 256


def _dispatch_kernel(dest_ref, src_ref, dst_ref, sem):
    n = dest_ref.shape[1]

    def issue(j, carry):
        for r in range(TOP_K):
            pltpu.make_async_copy(src_ref.at[pl.ds(j, 1)], dst_ref.at[pl.ds(dest_ref[r, j], 1)], sem).start(priority=r % 2)
        return carry

    lax.fori_loop(0, n, issue, 0, unroll=8)
    for r in range(TOP_K):
        pltpu.make_async_copy(src_ref, dst_ref.at[pl.ds(0, n)], sem).wait()


def _dispatch(dest, hp):
    T, Wd = hp.shape
    n = min(DISPATCH_TOKENS, T)
    return pl.pallas_call(
        _dispatch_kernel,
        grid=(T // n,),
        in_specs=[pl.BlockSpec((TOP_K, n), lambda i: (0, i), memory_space=pltpu.SMEM),
                  pl.BlockSpec((n, Wd), lambda i: (i, 0))],
        out_specs=pl.BlockSpec(memory_space=pl.ANY),
        out_shape=jax.ShapeDtypeStruct((T * TOP_K, Wd), hp.dtype),
        scratch_shapes=[pltpu.SemaphoreType.DMA(())],
        compiler_params=_cparams(("arbitrary",)),
        name="dispatch",
    )(dest, hp)


EXPERT_ROWS = 512
EXPERT_FEATURE_TILE = 1024


def _experts_kernel(blk_ref, exp_ref, lo_ref, hi_ref, new_ref, x_ref, wgu_ref, bgu_ref, wdn_ref, bdn_ref, o_ref,
                    wgu_bf, wdn_bf):
    k = pl.program_id(0)
    lo = lo_ref[k]
    hi = hi_ref[k]
    tm = x_ref.shape[0]
    F = wdn_ref.shape[0]

    @pl.when(new_ref[k] == 1)
    def _():
        wgu_bf[...] = wgu_ref[...].astype(BF16)
        wdn_bf[...] = wdn_ref[...].astype(BF16)

    @pl.when(hi > lo)
    def _():
        word = x_ref[...]
        x = jnp.concatenate([pltpu.bitcast(word << 16, F32).astype(BF16),
                             pltpu.bitcast(word & jnp.uint32(0xFFFF0000), F32).astype(BF16)], axis=1)
        y = None
        for c in range(0, F, EXPERT_FEATURE_TILE):
            ft = slice(c, c + EXPERT_FEATURE_TILE)
            ut = slice(F + c, F + c + EXPERT_FEATURE_TILE)
            gate = jnp.dot(x, wgu_bf[:, ft], preferred_element_type=F32) + bgu_ref[:, ft]
            up = jnp.dot(x, wgu_bf[:, ut], preferred_element_type=F32) + bgu_ref[:, ut]
            gate = jnp.minimum(gate, SWIGLU_LIMIT)
            up = jnp.clip(up, -SWIGLU_LIMIT, SWIGLU_LIMIT)
            act = (up + 1.0) * gate * _sigmoid(SWIGLU_ALPHA * gate)
            part = jnp.dot(act.astype(BF16), wdn_bf[ft, :], preferred_element_type=F32)
            y = part + bdn_ref[...] if y is None else y + part
        rows = lax.broadcasted_iota(jnp.int32, (tm, 1), 0)
        keep = (rows >= lo) & (rows < hi)

        @pl.when(lo == 0)
        def _():
            o_ref[...] = jnp.where(keep, y, 0.0)

        @pl.when(lo > 0)
        def _():
            o_ref[...] = jnp.where(keep, y, o_ref[...])


def _experts(xs, item_blk, item_exp, item_lo, item_hi, item_new, w_gu, b_gu, w_down, b_down):
    A, half = xs.shape
    E, D, F2 = w_gu.shape
    F = F2 // 2
    tm = EXPERT_ROWS
    n_items = item_blk.shape[0]
    grid_spec = pltpu.PrefetchScalarGridSpec(
        num_scalar_prefetch=5,
        grid=(n_items,),
        in_specs=[pl.BlockSpec((tm, half), lambda k, blk, ex, lo, hi, nw: (blk[k], 0)),
                  pl.BlockSpec((None, D, F2), lambda k, blk, ex, lo, hi, nw: (ex[k], 0, 0)),
                  pl.BlockSpec((None, 1, F2), lambda k, blk, ex, lo, hi, nw: (ex[k], 0, 0)),
                  pl.BlockSpec((None, F, D), lambda k, blk, ex, lo, hi, nw: (ex[k], 0, 0)),
                  pl.BlockSpec((None, 1, D), lambda k, blk, ex, lo, hi, nw: (ex[k], 0, 0))],
        out_specs=pl.BlockSpec((tm, D), lambda k, blk, ex, lo, hi, nw: (blk[k], 0)),
        scratch_shapes=[pltpu.VMEM((D, F2), BF16), pltpu.VMEM((F, D), BF16)],
    )
    return pl.pallas_call(
        _experts_kernel,
        grid_spec=grid_spec,
        out_shape=jax.ShapeDtypeStruct((A, D), F32),
        compiler_params=_cparams(("arbitrary",)),
        name="experts",
    )(item_blk, item_exp, item_lo, item_hi, item_new, xs, w_gu, b_gu.reshape(E, 1, F2).astype(F32),
      w_down, b_down.reshape(E, 1, D).astype(F32))


def _work_items(counts, A):
    E = N_EXPERTS
    tm = EXPERT_ROWS
    n_items = A // tm + E - 1
    end = jnp.cumsum(counts)
    start = end - counts
    first_blk = start // tm
    last_blk = jnp.maximum(end - 1, 0) // tm
    n_e = jnp.where(counts > 0, last_blk - first_blk + 1, 0)
    off_end = jnp.cumsum(n_e)
    off = off_end - n_e
    total = off_end[-1]
    k = jnp.arange(n_items, dtype=jnp.int32)
    kk = jnp.minimum(k, total - 1)
    e = jnp.searchsorted(off_end, kk, side='right').astype(jnp.int32)
    blk = (first_blk[e] + (kk - off[e])).astype(jnp.int32)
    lo = jnp.maximum(start[e], blk * tm) - blk * tm
    hi = jnp.minimum(end[e], (blk + 1) * tm) - blk * tm
    valid = k < total
    lo = jnp.where(valid, lo, 0).astype(jnp.int32)
    hi = jnp.where(valid, hi, 0).astype(jnp.int32)
    new = jnp.concatenate([jnp.ones((1,), jnp.int32), (e[1:] != e[:-1]).astype(jnp.int32)])
    return blk, e, lo, hi, new


COMBINE_TOKENS = 256


def _combine_kernel(dest_ref, y_ref, wrow_ref, x1_ref, mod_ref, gpost_ref, o_ref, stage, sem):
    n = x1_ref.shape[0]

    def issue(j, carry):
        for r in range(TOP_K):
            pltpu.make_async_copy(y_ref.at[pl.ds(dest_ref[r, j], 1)], stage.at[r, pl.ds(j, 1)], sem).start(priority=r % 2)
        return carry

    lax.fori_loop(0, n, issue, 0, unroll=8)
    for r in range(TOP_K):
        pltpu.make_async_copy(y_ref.at[pl.ds(0, n)], stage.at[r], sem).wait()

    wrow = wrow_ref[...]
    moe = wrow[:, 0:1] * stage[0]
    for r in range(1, TOP_K):
        moe = moe + wrow[:, r:r + 1] * stage[r]
    o_ref[...] = x1_ref[...] + mod_ref[0, 5:6, :] * (_rms(moe) * gpost_ref[...])


def _combine(dest, y, wrow, x1, mod3, g_post_ffn, S):
    T, D = x1.shape
    n = min(COMBINE_TOKENS, S)
    per_b = S // n
    return pl.pallas_call(
        _combine_kernel,
        grid=(T // n,),
        in_specs=[pl.BlockSpec((TOP_K, n), lambda i: (0, i), memory_space=pltpu.SMEM),
                  pl.BlockSpec(memory_space=pl.ANY),
                  pl.BlockSpec((n, LANES), lambda i: (i, 0)),
                  pl.BlockSpec((n, D), lambda i: (i, 0)),
                  pl.BlockSpec((1, 6, D), lambda i: (i // per_b, 0, 0)),
                  pl.BlockSpec((1, D), lambda i: (0, 0))],
        out_specs=pl.BlockSpec((n, D), lambda i: (i, 0)),
        out_shape=jax.ShapeDtypeStruct((T, D), F32),
        scratch_shapes=[pltpu.VMEM((TOP_K, n, D), F32), pltpu.SemaphoreType.DMA(())],
        compiler_params=_cparams(("arbitrary",)),
        name="combine",
    )(dest, y, wrow, x1, mod3, g_post_ffn.reshape(1, D))


def _regroup_w_in(w_in):
    D = w_in.shape[0]
    dw = DN_HEADS * HEAD_DIM
    mw = MB_HEADS * HEAD_DIM
    cuts = np.cumsum([dw, dw, dw, dw, DN_HEADS, DN_HEADS, mw, mw, mw, D, D])[:-1]
    dq, dk, dv, dz, db, da, mq, mk, mv, ga, gb = jnp.split(w_in, [int(c) for c in cuts], axis=1)
    small = jnp.concatenate([db, da, jnp.zeros((D, LANES - 2 * DN_HEADS), w_in.dtype)], axis=1)
    return jnp.concatenate([ga, gb, dq, dk, dv, dz, mq, mk, mv, small], axis=1).astype(BF16)


def kernel(x, c, w_ada, b_ada, g_pre_mix, g_post_mix, g_pre_ffn, g_post_ffn, w_in, conv_w, a_log, dt_bias,
           g_dn_out, w_br_a, w_br_b, w_o, w_router, b_router, w_gu, b_gu, w_down, b_down):
    B, S, D = x.shape
    l = 0
    mod = _adaln(c, w_ada[l], b_ada[l]).reshape(B, 6, D)
    proj = _inproj(x, mod, g_pre_mix[l], _regroup_w_in(w_in[l]))
    qr, kr, vb, kmean = _mobaprep(proj, B, S)
    ob = _moba(qr, kr, vb, kmean, B, S)
    oa = _deltanet(proj, conv_w[l], a_log[l], dt_bias[l], g_dn_out[l], B, S)
    x1, hp, idx8, wrow, rank8, cnt = _merge(oa, ob, proj, x.reshape(B * S, D), mod, g_post_mix[l], g_pre_ffn[l],
                                            w_br_a[l], w_br_b[l], w_o[l], w_router[l], b_router[l], S)
    out = _moe(x1, hp, idx8, wrow, rank8, cnt, mod, g_post_ffn[l], w_gu[l], b_gu[l], w_down[l], b_down[l], S)
    return out.reshape(B, S, D)


def _moe(x1, hp, idx8, wrow, rank8, cnt, mod, g_post_ffn, w_gu, b_gu, w_down, b_down, S):
    T = x1.shape[0]
    counts = cnt[:, 0].astype(jnp.int32)
    start = jnp.cumsum(counts) - counts
    hot = idx8[:TOP_K, :, None] == jnp.arange(N_EXPERTS, dtype=jnp.int32)
    dest = rank8[:TOP_K] + jnp.sum(jnp.where(hot, start, 0), axis=-1)
    xs = _dispatch(dest, hp)
    blk, e, lo, hi, new = _work_items(counts, T * TOP_K)
    y = _experts(xs, blk, e, lo, hi, new, w_gu, b_gu, w_down, b_down)
    return _combine(dest, y, wrow, x1, mod, g_post_ffn, S)
```

```python
import functools

import jax
import jax.numpy as jnp
import numpy as np
from jax import lax
from jax.experimental import pallas as pl
from jax.experimental.pallas import tpu as pltpu

F32 = jnp.float32
BF16 = jnp.bfloat16
HIGHEST = lax.Precision.HIGHEST

HEAD_DIM = 128
DN_HEADS = 4
DN_CONV = 4
DN_CHUNK = 64
MB_HEADS = 4
MB_BLOCK = 256
MB_TOPK = 3
ROPE_THETA = 500000.0
ROPE_DIM = HEAD_DIM // 4
N_EXPERTS = 32
TOP_K = 4
SWIGLU_LIMIT = 7.0
SWIGLU_ALPHA = 1.702
NORM_EPS = 1e-6
LANES = 128
NEG_BIG = -1e30

_W = DN_HEADS * HEAD_DIM // LANES
COL_GA = 0
COL_GB = 8
COL_DQ = 16
COL_DK = 20
COL_DV = 24
COL_DZ = 28
COL_MQ = 32
COL_MK = 36
COL_MV = 40
COL_SMALL = 44
N_PROJ = 45 * LANES

VMEM_LIMIT = 56 * 1024 * 1024


def _cparams(sem):
    return pltpu.CompilerParams(dimension_semantics=sem, vmem_limit_bytes=VMEM_LIMIT)


def _silu(v):
    return v * (1.0 / (1.0 + jnp.exp(-v)))


def _sigmoid(v):
    return 1.0 / (1.0 + jnp.exp(-v))


def _adaln_kernel(c_ref, w_ref, b_ref, o_ref):
    a = _silu(c_ref[...])
    o_ref[...] = jnp.dot(a, w_ref[...], preferred_element_type=F32, precision=HIGHEST) + b_ref[...]


def _adaln(c, w_ada, b_ada):
    B, D = c.shape
    N = w_ada.shape[1]
    tn = D
    return pl.pallas_call(
        _adaln_kernel,
        grid=(N // tn,),
        in_specs=[pl.BlockSpec((B, D), lambda j: (0, 0)),
                  pl.BlockSpec((D, tn), lambda j: (0, j)),
                  pl.BlockSpec((1, tn), lambda j: (0, j))],
        out_specs=pl.BlockSpec((B, tn), lambda j: (0, j)),
        out_shape=jax.ShapeDtypeStruct((B, N), F32),
        compiler_params=_cparams(("arbitrary",)),
        name="adaln",
    )(c, w_ada, b_ada.reshape(1, N))


def _inproj_kernel(x_ref, mod_ref, g_ref, w_ref, o_ref, h_scr):
    @pl.when(pl.program_id(2) == 0)
    def _():
        x = x_ref[0]
        y = x * lax.rsqrt(jnp.mean(x * x, axis=-1, keepdims=True) + NORM_EPS) * g_ref[...]
        h = y * (1.0 + mod_ref[0, 1:2, :]) + mod_ref[0, 0:1, :]
        h_scr[...] = h.astype(BF16)

    o_ref[...] = jnp.dot(h_scr[...], w_ref[...], preferred_element_type=F32)


def _inproj(x, mod3, g_pre, w_all):
    B, S, D = x.shape
    tm = min(1024, S)
    tn = N_PROJ // 3
    nrow = S // tm
    return pl.pallas_call(
        _inproj_kernel,
        grid=(B, nrow, N_PROJ // tn),
        in_specs=[pl.BlockSpec((1, tm, D), lambda b, i, j: (b, i, 0)),
                  pl.BlockSpec((1, 6, D), lambda b, i, j: (b, 0, 0)),
                  pl.BlockSpec((1, D), lambda b, i, j: (0, 0)),
                  pl.BlockSpec((D, tn), lambda b, i, j: (0, j))],
        out_specs=pl.BlockSpec((tm, tn), lambda b, i, j: (b * nrow + i, j)),
        out_shape=jax.ShapeDtypeStruct((B * S, N_PROJ), F32),
        scratch_shapes=[pltpu.VMEM((tm, D), BF16)],
        compiler_params=_cparams(("arbitrary", "arbitrary", "arbitrary")),
        name="inproj",
    )(x, mod3, g_pre.reshape(1, D), w_all)


def _rope(v, cosf, sinf, lane):
    rot = jnp.where(lane < ROPE_DIM // 2, pltpu.roll(v, LANES - ROPE_DIM // 2, 1), pltpu.roll(v, ROPE_DIM // 2, 1))
    return v * cosf + rot * sinf


def _mobaprep_kernel(q_ref, k_ref, v_ref, cos_ref, sin_ref, qo_ref, ko_ref, vo_ref, km_ref):
    cosf = cos_ref[...]
    sinf = sin_ref[...]
    lane = lax.broadcasted_iota(jnp.int32, cosf.shape, 1)
    for h in range(MB_HEADS):
        sl = slice(h * HEAD_DIM, (h + 1) * HEAD_DIM)
        qo_ref[:, sl] = (_rope(q_ref[:, sl], cosf, sinf, lane) * (HEAD_DIM ** -0.5)).astype(BF16)
        kr = _rope(k_ref[:, sl], cosf, sinf, lane)
        ko_ref[:, sl] = kr.astype(BF16)
        km_ref[0, :, sl] = jnp.mean(kr, axis=0, keepdims=True)
    vo_ref[...] = v_ref[...].astype(BF16)


def _rope_tables(S):
    half = ROPE_DIM // 2
    inv_freq = ROPE_THETA ** (-jnp.arange(half, dtype=F32) / half)
    ang = jnp.arange(S, dtype=F32)[:, None] * inv_freq[None, :]
    cos, sin = jnp.cos(ang), jnp.sin(ang)
    rest = HEAD_DIM - ROPE_DIM
    cosf = jnp.concatenate([cos, cos, jnp.ones((S, rest), F32)], axis=1)
    sinf = jnp.concatenate([-sin, sin, jnp.zeros((S, rest), F32)], axis=1)
    return cosf, sinf


def _mobaprep(proj, B, S):
    T = B * S
    tb = MB_BLOCK
    nb = S // tb
    W = MB_HEADS * HEAD_DIM
    cosf, sinf = _rope_tables(S)
    col = lambda c: (lambda i: (i, c))
    return pl.pallas_call(
        _mobaprep_kernel,
        grid=(T // tb,),
        in_specs=[pl.BlockSpec((tb, W), col(COL_MQ // _W)),
                  pl.BlockSpec((tb, W), col(COL_MK // _W)),
                  pl.BlockSpec((tb, W), col(COL_MV // _W)),
                  pl.BlockSpec((tb, LANES), lambda i: (i % nb, 0)),
                  pl.BlockSpec((tb, LANES), lambda i: (i % nb, 0))],
        out_specs=[pl.BlockSpec((tb, W), lambda i: (i, 0)),
                   pl.BlockSpec((tb, W), lambda i: (i, 0)),
                   pl.BlockSpec((tb, W), lambda i: (i, 0)),
                   pl.BlockSpec((1, 1, W), lambda i: (i, 0, 0))],
        out_shape=[jax.ShapeDtypeStruct((T, W), BF16),
                   jax.ShapeDtypeStruct((T, W), BF16),
                   jax.ShapeDtypeStruct((T, W), BF16),
                   jax.ShapeDtypeStruct((T // tb, 1, W), F32)],
        compiler_params=_cparams(("arbitrary",)),
        name="mobaprep",
    )(proj, proj, proj, cosf, sinf)


MB_HPS = 4


def _moba_kernel(q_ref, k_ref, v_ref, km_ref, o_ref, s_scr, sd_scr, mx_scr, l_scr, acc_scr):
    qi = pl.program_id(1)
    tb = MB_BLOCK
    nb = km_ref.shape[0]
    nt = tb // LANES
    D = HEAD_DIM
    heads = range(MB_HPS)
    hs = [slice(h * D, (h + 1) * D) for h in heads]
    qs = [q_ref[:, hs[h]] for h in heads]

    sts = [lax.dot_general(km_ref[:, hs[h]], qs[h].astype(F32), (((1,), (1,)), ((), ())),
                           preferred_element_type=F32, precision=HIGHEST) for h in heads]
    blk = lax.broadcasted_iota(jnp.int32, (nb, tb), 0)
    rowid = lax.broadcasted_iota(jnp.int32, (LANES, tb), 0)
    sel_ts = []
    for h in heads:
        st = jnp.where(blk < qi, sts[h], -jnp.inf)
        sel_t = jnp.full((LANES, tb), -1.0, F32)
        for r in range(MB_TOPK):
            m = jnp.max(st, axis=0, keepdims=True)
            idx = jnp.min(jnp.where(st == m, blk, nb), axis=0, keepdims=True)
            sel_t = jnp.where(rowid == r, jnp.where(r < qi, idx, -1).astype(F32), sel_t)
            st = jnp.where(blk == idx, -jnp.inf, st)
        sel_ts.append(sel_t)
    sels = [jnp.transpose(t) for t in sel_ts]
    sel_rep = [[jnp.broadcast_to(sels[h][:, r:r + 1], (tb, LANES)) for r in range(MB_TOPK)] for h in heads]

    def logits(h, start, width):
        kslab = k_ref[pl.ds(pl.multiple_of(start, tb), width), hs[h]]
        return lax.dot_general(qs[h], kslab, (((1,), (1,)), ((), ())), preferred_element_type=F32)

    r_i = lax.broadcasted_iota(jnp.int32, (tb, LANES), 0)
    c_i = lax.broadcasted_iota(jnp.int32, (tb, LANES), 1)
    sds = [logits(h, qi * tb, tb) for h in heads]
    for h in heads:
        mx = jnp.full((tb, LANES), NEG_BIG, F32)
        for t in range(nt):
            piece = jnp.where(c_i + t * LANES <= r_i, sds[h][:, t * LANES:(t + 1) * LANES], NEG_BIG)
            sd_scr[h, :, t * LANES:(t + 1) * LANES] = piece
            mx = jnp.maximum(mx, piece)
        mx_scr[h] = mx

    n_pairs = (qi + 1) // 2

    def pass1(i, carry):
        kb0 = 2 * i
        s2s = [logits(h, kb0 * tb, 2 * tb) for h in heads]
        for h in heads:
            mx = mx_scr[h]
            for half in range(2):
                kbf = (kb0 + half).astype(F32)
                hit = (sel_rep[h][0] == kbf) | (sel_rep[h][1] == kbf) | (sel_rep[h][2] == kbf)
                for t in range(nt):
                    c0 = (half * nt + t) * LANES
                    piece = jnp.where(hit, s2s[h][:, c0:c0 + LANES], NEG_BIG)
                    s_scr[h, i, :, c0:c0 + LANES] = piece
                    mx = jnp.maximum(mx, piece)
            mx_scr[h] = mx
        return carry

    lax.fori_loop(0, n_pairs, pass1, 0)
    m_reps = [jnp.broadcast_to(jnp.max(mx_scr[h], axis=-1, keepdims=True), (tb, LANES)) for h in heads]

    def probs(load, width, m_rep):
        ps, lsum = [], jnp.zeros((tb, LANES), F32)
        for t in range(width // LANES):
            p = jnp.exp(load(t) - m_rep)
            lsum = lsum + p
            ps.append(p.astype(BF16))
        return jnp.concatenate(ps, axis=1), lsum

    pls = [probs(lambda t, h=h: sd_scr[h, :, t * LANES:(t + 1) * LANES], tb, m_reps[h]) for h in heads]
    for h in heads:
        l_scr[h] = pls[h][1]
        acc_scr[h] = jnp.dot(pls[h][0], v_ref[pl.ds(pl.multiple_of(qi * tb, tb), tb), hs[h]],
                             preferred_element_type=F32)

    def pass2(i, carry):
        pls = [probs(lambda t, h=h: s_scr[h, i, :, t * LANES:(t + 1) * LANES], 2 * tb, m_reps[h]) for h in heads]
        for h in heads:
            vslab = v_ref[pl.ds(pl.multiple_of(2 * i * tb, tb), 2 * tb), hs[h]]
            l_scr[h] = l_scr[h] + pls[h][1]
            acc_scr[h] = acc_scr[h] + jnp.dot(pls[h][0], vslab, preferred_element_type=F32)
        return carry

    lax.fori_loop(0, n_pairs, pass2, 0)
    for h in heads:
        o_ref[:, hs[h]] = (acc_scr[h] / jnp.sum(l_scr[h], axis=-1, keepdims=True)).astype(o_ref.dtype)


def _moba(qr, kr, vb, kmean, B, S):
    tb = MB_BLOCK
    nb = S // tb
    T = B * S
    km = kmean.reshape(B, nb, MB_HEADS * HEAD_DIM)
    HW = MB_HPS * HEAD_DIM
    ng = MB_HEADS // MB_HPS
    return pl.pallas_call(
        _moba_kernel,
        grid=(B * ng, nb),
        in_specs=[pl.BlockSpec((tb, HW), lambda g, i: ((g // ng) * nb + i, g % ng)),
                  pl.BlockSpec((S, HW), lambda g, i: (g // ng, g % ng)),
                  pl.BlockSpec((S, HW), lambda g, i: (g // ng, g % ng)),
                  pl.BlockSpec((None, nb, HW), lambda g, i: (g // ng, 0, g % ng))],
        out_specs=pl.BlockSpec((tb, HW), lambda g, i: ((g // ng) * nb + i, g % ng)),
        out_shape=jax.ShapeDtypeStruct((T, MB_HEADS * HEAD_DIM), BF16),
        scratch_shapes=[pltpu.VMEM((MB_HPS, (nb + 1) // 2, tb, 2 * tb), F32),
                        pltpu.VMEM((MB_HPS, tb, tb), F32),
                        pltpu.VMEM((MB_HPS, tb, LANES), F32),
                        pltpu.VMEM((MB_HPS, tb, LANES), F32),
                        pltpu.VMEM((MB_HPS, tb, HEAD_DIM), F32)],
        compiler_params=_cparams(("arbitrary", "arbitrary")),
        name="moba",
    )(qr, kr, vb, km)


DN_TILE_CHUNKS = 8


def _softplus(v):
    return jnp.maximum(v, 0.0) + jnp.log1p(jnp.exp(-jnp.abs(v)))


DN_GROUP = 4
DN_HPS = 4


def _split(a):
    hi = a.astype(BF16)
    return hi, (a - hi.astype(F32)).astype(BF16)


def _dot3(ah, al, bh, bl):
    lhs = jnp.concatenate([ah, ah, al], axis=1)
    rhs = jnp.concatenate([bh, bl, bh], axis=0)
    return jnp.dot(lhs, rhs, preferred_element_type=F32)


def _dot_bf(a, b):
    return jnp.dot(a.astype(BF16), b.astype(BF16), preferred_element_type=F32)


def _deltanet_kernel(q_ref, k_ref, v_ref, z_ref, sm_ref, wq_ref, wk_ref, wv_ref, alog_ref, dtb_ref, gout_ref,
                     o_ref, xp_scr, state_scr):
    i = pl.program_id(2)
    C = DN_CHUNK
    TR = q_ref.shape[0]
    HALO = 8

    @pl.when(i == 0)
    def _():
        xp_scr[...] = jnp.zeros(xp_scr.shape, F32)
        state_scr[...] = jnp.zeros(state_scr.shape, F32)

    def conv_silu(slot, x_ref, w_ref):
        xp_scr[slot, 0:HALO, :] = xp_scr[slot, TR:TR + HALO, :]
        xp_scr[slot, HALO:HALO + TR, :] = x_ref[...]
        acc = w_ref[DN_CONV - 1:DN_CONV, :] * xp_scr[slot, HALO:HALO + TR, :]
        for j in range(1, DN_CONV):
            acc = acc + w_ref[DN_CONV - 1 - j:DN_CONV - j, :] * xp_scr[slot, HALO - j:HALO - j + TR, :]
        return _silu(acc)

    q_all = conv_silu(0, q_ref, wq_ref)
    k_all = conv_silu(1, k_ref, wk_ref)
    v_all = conv_silu(2, v_ref, wv_ref)

    sm = sm_ref[...]
    lane = lax.broadcasted_iota(jnp.int32, sm.shape, 1)
    row = lax.broadcasted_iota(jnp.int32, sm.shape, 0)
    beta_all = _sigmoid(sm)
    g_all = -jnp.exp(alog_ref[...]) * _softplus(sm + dtb_ref[...])
    pos = row % C
    gc_all = g_all
    shift = 1
    while shift < C:
        gc_all = gc_all + jnp.where(pos >= shift, pltpu.roll(gc_all, shift, 0), 0.0)
        shift *= 2
    gc_t = jnp.transpose(gc_all)
    row_t = lax.broadcasted_iota(jnp.int32, gc_t.shape, 0)

    G = DN_GROUP * C
    ri = lax.broadcasted_iota(jnp.int32, (G, G), 0)
    ci = lax.broadcasted_iota(jnp.int32, (G, G), 1)
    same = (ri // C) == (ci // C)
    tril = same & (ci <= ri)
    strict = same & (ci < ri)
    eye = (ci == ri).astype(F32)
    D = HEAD_DIM
    gout = gout_ref[...]

    heads = []
    for hh in range(DN_HPS):
        h = pl.program_id(1) * DN_HPS + hh
        sl = slice(hh * D, (hh + 1) * D)
        q, k, v = q_all[:, sl], k_all[:, sl], v_all[:, sl]
        q = q * lax.rsqrt(jnp.sum(q * q, axis=-1, keepdims=True) + 1e-6) * (D ** -0.5)
        k = k * lax.rsqrt(jnp.sum(k * k, axis=-1, keepdims=True) + 1e-6)
        beta = jnp.sum(jnp.where(lane == h, beta_all, 0.0), axis=1, keepdims=True)
        gc = jnp.sum(jnp.where(lane == DN_HEADS + h, gc_all, 0.0), axis=1, keepdims=True)
        gc_row = jnp.sum(jnp.where(row_t == DN_HEADS + h, gc_t, 0.0), axis=0, keepdims=True)
        heads.append((q, k, v, beta, gc, gc_row, jnp.transpose(k)))

    steps = [[None] * (TR // C) for _ in range(DN_HPS)]
    preps = []
    for hh in range(DN_HPS):
        for g in range(TR // G):
            preps.append(_deltanet_group(heads[hh], g, tril, strict, eye, steps[hh]))
    _round_robin(preps)

    outs = [[] for _ in range(DN_HPS)]
    _round_robin([_deltanet_chain(hh, steps[hh], state_scr, outs[hh]) for hh in range(DN_HPS)])
    for hh in range(DN_HPS):
        sl = slice(hh * D, (hh + 1) * D)
        o = jnp.concatenate(outs[hh], axis=0)
        y = o * lax.rsqrt(jnp.mean(o * o, axis=-1, keepdims=True) + NORM_EPS) * gout
        o_ref[:, sl] = (y * _silu(z_ref[:, sl])).astype(o_ref.dtype)


def _round_robin(gens):
    active = list(gens)
    while active:
        still = []
        for gen in active:
            try:
                next(gen)
                still.append(gen)
            except StopIteration:
                pass
        active = still


def _deltanet_group(head, g, tril, strict, eye, steps_out):
    C = DN_CHUNK
    D = HEAD_DIM
    G = DN_GROUP * C
    q, k, v, beta, gc, gc_row, k_t = head
    r0 = g * G
    qg, kg, vg = q[r0:r0 + G], k[r0:r0 + G], v[r0:r0 + G]
    bg = beta[r0:r0 + G]
    gcg = gc[r0:r0 + G]
    gcr = gc_row[:, r0:r0 + G]
    ktg = k_t[:, r0:r0 + G]
    decay = jnp.where(tril, jnp.exp(jnp.where(tril, gcg - gcr, 0.0)), 0.0)
    kb = kg * bg
    aq = _dot_bf(jnp.concatenate([kb, qg], axis=0), ktg)
    yield
    m_neg = jnp.where(strict, -(aq[:G] * decay), 0.0)
    qk = (aq[G:] * decay).astype(BF16)
    t_inv = eye + m_neg
    ph, pl_ = _split(m_neg)
    for _ in range(5):
        th, tl = _split(t_inv)
        ph, pl_ = _split(_dot3(ph, pl_, ph, pl_))
        yield
        t_inv = t_inv + _dot3(th, tl, ph, pl_)
        yield
    egc = jnp.exp(gcg)
    th, tl = _split(t_inv)
    rh, rl = _split(jnp.concatenate([kb * egc, vg * bg], axis=1))
    wu = _dot3(th, tl, rh, rl).astype(BF16)
    yield
    qr = jnp.dot(qk, wu, preferred_element_type=F32)
    qp = qg * egc - qr[:, :D]
    r_all = qr[:, D:]
    yield
    for c in range(DN_GROUP):
        c0 = c * C
        g_last = gcg[c0 + C - 1:c0 + C, :]
        kt_tail = ktg[:, c0:c0 + C] * jnp.exp(g_last - gcr[:, c0:c0 + C])
        gh = jnp.dot(kt_tail.astype(BF16), wu[c0:c0 + C, :], preferred_element_type=F32)
        lhs = jnp.concatenate([gh[:, :D], qp[c0:c0 + C]], axis=0).astype(BF16)
        steps_out[g * DN_GROUP + c] = (lhs, gh[:, D:], r_all[c0:c0 + C], jnp.exp(g_last))
        yield


def _deltanet_chain(hh, steps, state_scr, outs):
    D = HEAD_DIM
    state = state_scr[hh]
    for lhs, h_add, r_add, dec in steps:
        res = jnp.dot(lhs, state.astype(BF16), preferred_element_type=F32)
        outs.append(res[D:] + r_add)
        state = state * dec - res[:D] + h_add
        yield
    state_scr[hh] = state


def _deltanet(proj, conv_w, a_log, dt_bias, g_dn_out, B, S):
    T = B * S
    TR = min(DN_TILE_CHUNKS * DN_CHUNK, S)
    nt = S // TR
    pad = jnp.zeros((DN_HEADS,), F32)
    rest = jnp.zeros((LANES - 2 * DN_HEADS,), F32)
    alog_lane = jnp.concatenate([pad, a_log.astype(F32), rest]).reshape(1, LANES)
    dtb_lane = jnp.concatenate([pad, dt_bias.astype(F32), rest]).reshape(1, LANES)
    HW = DN_HPS * HEAD_DIM
    per = HW // LANES
    rows = lambda c0: (lambda b, h, i: (b * nt + i, c0 // per + h))
    wcol = lambda c0: (lambda b, h, i: (0, c0 // per + h))
    const = lambda b, h, i: (0, 0)
    return pl.pallas_call(
        _deltanet_kernel,
        grid=(B, DN_HEADS // DN_HPS, nt),
        in_specs=[pl.BlockSpec((TR, HW), rows(COL_DQ)),
                  pl.BlockSpec((TR, HW), rows(COL_DK)),
                  pl.BlockSpec((TR, HW), rows(COL_DV)),
                  pl.BlockSpec((TR, HW), rows(COL_DZ)),
                  pl.BlockSpec((TR, LANES), lambda b, h, i: (b * nt + i, COL_SMALL)),
                  pl.BlockSpec((DN_CONV, HW), wcol(0)),
                  pl.BlockSpec((DN_CONV, HW), wcol(DN_HEADS)),
                  pl.BlockSpec((DN_CONV, HW), wcol(2 * DN_HEADS)),
                  pl.BlockSpec((1, LANES), const),
                  pl.BlockSpec((1, LANES), const),
                  pl.BlockSpec((1, HEAD_DIM), const)],
        out_specs=pl.BlockSpec((TR, HW), lambda b, h, i: (b * nt + i, h)),
        out_shape=jax.ShapeDtypeStruct((T, DN_HEADS * HEAD_DIM), BF16),
        scratch_shapes=[pltpu.VMEM((3, TR + 8, HW), F32), pltpu.VMEM((DN_HPS, HEAD_DIM, HEAD_DIM), F32)],
        compiler_params=_cparams(("arbitrary", "arbitrary", "arbitrary")),
        name="deltanet",
    )(proj, proj, proj, proj, proj, conv_w, conv_w, conv_w, alog_lane, dtb_lane, g_dn_out.reshape(1, HEAD_DIM))


def _rms(v):
    return v * lax.rsqrt(jnp.mean(v * v, axis=-1, keepdims=True) + NORM_EPS)


def _merge_kernel(oa_ref, ob_ref, ga_ref, gb_ref, x_ref, mod_ref, gpost_ref, gpre_ref, wa_ref, wb_ref, wo_ref,
                  wrt_ref, br_ref, x1_ref, hp_ref, idx_ref, wrow_ref, rank_ref, cnt_ref, carry_scr):
    E = N_EXPERTS
    tm = x_ref.shape[0]
    half = x_ref.shape[1] // 2

    @pl.when(pl.program_id(0) == 0)
    def _():
        carry_scr[...] = jnp.zeros(carry_scr.shape, F32)

    ya = jnp.dot(oa_ref[...], wa_ref[...], preferred_element_type=F32)
    yb = jnp.dot(ob_ref[...], wb_ref[...], preferred_element_type=F32)
    merged = _sigmoid(ga_ref[...]) * ya + _sigmoid(gb_ref[...]) * yb
    mix = jnp.dot(merged.astype(BF16), wo_ref[...], preferred_element_type=F32)
    x1 = x_ref[...] + mod_ref[0, 2:3, :] * (_rms(mix) * gpost_ref[...])
    x1_ref[...] = x1
    h2 = (_rms(x1) * gpre_ref[...]) * (1.0 + mod_ref[0, 4:5, :]) + mod_ref[0, 3:4, :]

    lo_bits = pltpu.bitcast(h2[:, :half].astype(BF16).astype(F32), jnp.uint32) >> 16
    hi_bits = pltpu.bitcast(h2[:, half:].astype(BF16).astype(F32), jnp.uint32) & jnp.uint32(0xFFFF0000)
    hp_ref[...] = hi_bits | lo_bits

    lt = lax.dot_general(wrt_ref[...], h2, (((1,), (1,)), ((), ())), preferred_element_type=F32,
                         precision=HIGHEST) + br_ref[...]
    eid = lax.broadcasted_iota(jnp.int32, (E, tm), 0)
    vals, idxs = [], []
    for _ in range(TOP_K):
        m = jnp.max(lt, axis=0, keepdims=True)
        idx = jnp.min(jnp.where(lt == m, eid, E), axis=0, keepdims=True)
        vals.append(m)
        idxs.append(idx)
        lt = jnp.where(eid == idx, -jnp.inf, lt)
    exps = [jnp.exp(v - vals[0]) for v in vals]
    den = exps[0] + exps[1] + exps[2] + exps[3]
    wts = [e / den for e in exps]

    hot = jnp.zeros((E, tm), F32)
    for idx in idxs:
        hot = hot + (eid == idx).astype(F32)
    ti = lax.broadcasted_iota(jnp.int32, (tm, tm), 0)
    tj = lax.broadcasted_iota(jnp.int32, (tm, tm), 1)
    before = (ti < tj).astype(BF16)
    prior = carry_scr[...][:, 0:1] + jnp.dot(hot.astype(BF16), before, preferred_element_type=F32)
    row8 = lax.broadcasted_iota(jnp.int32, (8, tm), 0)
    row128 = lax.broadcasted_iota(jnp.int32, (LANES, tm), 0)
    idx8 = jnp.zeros((8, tm), jnp.int32)
    rank8 = jnp.zeros((8, tm), jnp.int32)
    w128 = jnp.zeros((LANES, tm), F32)
    for r in range(TOP_K):
        rank_r = jnp.sum(jnp.where(eid == idxs[r], prior, 0.0), axis=0, keepdims=True)
        idx8 = jnp.where(row8 == r, idxs[r], idx8)
        rank8 = jnp.where(row8 == r, rank_r.astype(jnp.int32), rank8)
        w128 = jnp.where(row128 == r, wts[r], w128)
    idx_ref[...] = idx8
    rank_ref[...] = rank8
    wrow_ref[...] = jnp.transpose(w128)
    carry = carry_scr[...] + jnp.sum(hot, axis=1, keepdims=True)
    carry_scr[...] = carry
    cnt_ref[...] = carry


def _merge(oa, ob, proj, x2, mod3, g_post_mix, g_pre_ffn, w_br_a, w_br_b, w_o, w_router, b_router, S):
    T, D = x2.shape
    E = N_EXPERTS
    tm = min(512, S)
    per_b = S // tm
    W = DN_HEADS * HEAD_DIM
    row = lambda i: (i, 0)
    const = lambda i: (0, 0)
    lane_t = lambda i: (0, i)
    return pl.pallas_call(
        _merge_kernel,
        grid=(T // tm,),
        in_specs=[pl.BlockSpec((tm, W), row),
                  pl.BlockSpec((tm, W), row),
                  pl.BlockSpec((tm, D), lambda i: (i, COL_GA * LANES // D)),
                  pl.BlockSpec((tm, D), lambda i: (i, COL_GB * LANES // D)),
                  pl.BlockSpec((tm, D), row),
                  pl.BlockSpec((1, 6, D), lambda i: (i // per_b, 0, 0)),
                  pl.BlockSpec((1, D), const),
                  pl.BlockSpec((1, D), const),
                  pl.BlockSpec((W, D), const),
                  pl.BlockSpec((W, D), const),
                  pl.BlockSpec((D, D), const),
                  pl.BlockSpec((E, D), const),
                  pl.BlockSpec((E, 1), const)],
        out_specs=[pl.BlockSpec((tm, D), row),
                   pl.BlockSpec((tm, D // 2), row),
                   pl.BlockSpec((8, tm), lane_t),
                   pl.BlockSpec((tm, LANES), row),
                   pl.BlockSpec((8, tm), lane_t),
                   pl.BlockSpec((E, LANES), const)],
        out_shape=[jax.ShapeDtypeStruct((T, D), F32),
                   jax.ShapeDtypeStruct((T, D // 2), jnp.uint32),
                   jax.ShapeDtypeStruct((8, T), jnp.int32),
                   jax.ShapeDtypeStruct((T, LANES), F32),
                   jax.ShapeDtypeStruct((8, T), jnp.int32),
                   jax.ShapeDtypeStruct((E, LANES), F32)],
        scratch_shapes=[pltpu.VMEM((E, LANES), F32)],
        compiler_params=_cparams(("arbitrary",)),
        name="merge",
    )(oa, ob, proj, proj, x2, mod3, g_post_mix.reshape(1, D), g_pre_ffn.reshape(1, D),
      w_br_a.astype(BF16), w_br_b.astype(BF16), w_o.astype(BF16),
      jnp.transpose(w_router).astype(F32), b_router.reshape(E, 1).astype(F32))


DISPATCH_TOKENS = 256


def _dispatch_kernel(dest_ref, src_ref, dst_ref, sem):
    n = dest_ref.shape[1]

    def issue(j, carry):
        for r in range(TOP_K):
            pltpu.make_async_copy(src_ref.at[pl.ds(j, 1)], dst_ref.at[pl.ds(dest_ref[r, j], 1)], sem).start()
        return carry

    lax.fori_loop(0, n, issue, 0, unroll=8)
    for r in range(TOP_K):
        pltpu.make_async_copy(src_ref, dst_ref.at[pl.ds(0, n)], sem).wait()


def _dispatch(dest, hp):
    T, Wd = hp.shape
    n = min(DISPATCH_TOKENS, T)
    return pl.pallas_call(
        _dispatch_kernel,
        grid=(T // n,),
        in_specs=[pl.BlockSpec((TOP_K, n), lambda i: (0, i), memory_space=pltpu.SMEM),
                  pl.BlockSpec((n, Wd), lambda i: (i, 0))],
        out_specs=pl.BlockSpec(memory_space=pl.ANY),
        out_shape=jax.ShapeDtypeStruct((T * TOP_K, Wd), hp.dtype),
        scratch_shapes=[pltpu.SemaphoreType.DMA(())],
        compiler_params=_cparams(("arbitrary",)),
        name="dispatch",
    )(dest, hp)


EXPERT_ROWS = 512
EXPERT_FEATURE_TILE = 1024


def _experts_kernel(blk_ref, exp_ref, lo_ref, hi_ref, new_ref, x_ref, wgu_ref, bgu_ref, wdn_ref, bdn_ref, o_ref,
                    wgu_bf, wdn_bf):
    k = pl.program_id(0)
    lo = lo_ref[k]
    hi = hi_ref[k]
    tm = x_ref.shape[0]
    F = wdn_ref.shape[0]

    @pl.when(new_ref[k] == 1)
    def _():
        wgu_bf[...] = wgu_ref[...].astype(BF16)
        wdn_bf[...] = wdn_ref[...].astype(BF16)

    @pl.when(hi > lo)
    def _():
        word = x_ref[...]
        x = jnp.concatenate([pltpu.bitcast(word << 16, F32).astype(BF16),
                             pltpu.bitcast(word & jnp.uint32(0xFFFF0000), F32).astype(BF16)], axis=1)
        y = None
        for c in range(0, F, EXPERT_FEATURE_TILE):
            ft = slice(c, c + EXPERT_FEATURE_TILE)
            ut = slice(F + c, F + c + EXPERT_FEATURE_TILE)
            gate = jnp.dot(x, wgu_bf[:, ft], preferred_element_type=F32) + bgu_ref[:, ft]
            up = jnp.dot(x, wgu_bf[:, ut], preferred_element_type=F32) + bgu_ref[:, ut]
            gate = jnp.minimum(gate, SWIGLU_LIMIT)
            up = jnp.clip(up, -SWIGLU_LIMIT, SWIGLU_LIMIT)
            act = (up + 1.0) * gate * _sigmoid(SWIGLU_ALPHA * gate)
            part = jnp.dot(act.astype(BF16), wdn_bf[ft, :], preferred_element_type=F32)
            y = part + bdn_ref[...] if y is None else y + part
        rows = lax.broadcasted_iota(jnp.int32, (tm, 1), 0)
        keep = (rows >= lo) & (rows < hi)

        @pl.when(lo == 0)
        def _():
            o_ref[:, 0, :] = jnp.where(keep, y, 0.0)

        @pl.when(lo > 0)
        def _():
            o_ref[:, 0, :] = jnp.where(keep, y, o_ref[:, 0, :])


def _experts(xs, item_blk, item_exp, item_lo, item_hi, item_new, w_gu, b_gu, w_down, b_down):
    A, half = xs.shape
    E, D, F2 = w_gu.shape
    F = F2 // 2
    tm = EXPERT_ROWS
    n_items = item_blk.shape[0]
    grid_spec = pltpu.PrefetchScalarGridSpec(
        num_scalar_prefetch=5,
        grid=(n_items,),
        in_specs=[pl.BlockSpec((tm, half), lambda k, blk, ex, lo, hi, nw: (blk[k], 0)),
                  pl.BlockSpec((None, D, F2), lambda k, blk, ex, lo, hi, nw: (ex[k], 0, 0)),
                  pl.BlockSpec((None, 1, F2), lambda k, blk, ex, lo, hi, nw: (ex[k], 0, 0)),
                  pl.BlockSpec((None, F, D), lambda k, blk, ex, lo, hi, nw: (ex[k], 0, 0)),
                  pl.BlockSpec((None, 1, D), lambda k, blk, ex, lo, hi, nw: (ex[k], 0, 0))],
        out_specs=pl.BlockSpec((tm, 1, D), lambda k, blk, ex, lo, hi, nw: (blk[k], 0, 0)),
        scratch_shapes=[pltpu.VMEM((D, F2), BF16), pltpu.VMEM((F, D), BF16)],
    )
    return pl.pallas_call(
        _experts_kernel,
        grid_spec=grid_spec,
        out_shape=jax.ShapeDtypeStruct((A, 1, D), F32),
        compiler_params=_cparams(("arbitrary",)),
        name="experts",
    )(item_blk, item_exp, item_lo, item_hi, item_new, xs, w_gu, b_gu.reshape(E, 1, F2).astype(F32),
      w_down, b_down.reshape(E, 1, D).astype(F32))


def _work_items(counts, A):
    E = N_EXPERTS
    tm = EXPERT_ROWS
    n_items = A // tm + E - 1
    end = jnp.cumsum(counts)
    start = end - counts
    first_blk = start // tm
    last_blk = jnp.maximum(end - 1, 0) // tm
    n_e = jnp.where(counts > 0, last_blk - first_blk + 1, 0)
    off_end = jnp.cumsum(n_e)
    off = off_end - n_e
    total = off_end[-1]
    k = jnp.arange(n_items, dtype=jnp.int32)
    kk = jnp.minimum(k, total - 1)
    e = jnp.searchsorted(off_end, kk, side='right').astype(jnp.int32)
    blk = (first_blk[e] + (kk - off[e])).astype(jnp.int32)
    lo = jnp.maximum(start[e], blk * tm) - blk * tm
    hi = jnp.minimum(end[e], (blk + 1) * tm) - blk * tm
    valid = k < total
    lo = jnp.where(valid, lo, 0).astype(jnp.int32)
    hi = jnp.where(valid, hi, 0).astype(jnp.int32)
    new = jnp.concatenate([jnp.ones((1,), jnp.int32), (e[1:] != e[:-1]).astype(jnp.int32)])
    return blk, e, lo, hi, new


COMBINE_TOKENS = 256


def _combine_kernel(dest_ref, nxt_ref, y_ref, wrow_ref, x1_ref, mod_ref, gpost_ref, o_ref, stage, moe_scr, sem):
    i = pl.program_id(0)
    last = pl.num_programs(0) - 1
    n = x1_ref.shape[0]
    SUB = 8

    def gather(idx_ref, j, s):
        for r in range(TOP_K):
            pltpu.make_async_copy(y_ref.at[idx_ref[r, j]], stage.at[s, r, j], sem.at[s]).start()

    @pl.when(i == 0)
    def _():
        def prime(j, carry):
            gather(dest_ref, j, 0)
            return carry
        lax.fori_loop(0, n, prime, 0, unroll=SUB)

    def step(slot):
        for r in range(TOP_K):
            pltpu.make_async_copy(y_ref.at[pl.ds(0, n)], stage.at[slot, r], sem.at[slot]).wait()

        def reduce_group(g):
            rows = pl.ds(pl.multiple_of(g * SUB, SUB), SUB)
            w = wrow_ref[rows, :]
            acc = w[:, 0:1] * stage[slot, 0, rows, 0, :]
            for r in range(1, TOP_K):
                acc = acc + w[:, r:r + 1] * stage[slot, r, rows, 0, :]
            moe_scr[rows, :] = acc

        @pl.when(i < last)
        def _():
            def body(g, carry):
                for t in range(SUB):
                    gather(nxt_ref, g * SUB + t, 1 - slot)
                reduce_group(g)
                return carry
            lax.fori_loop(0, n // SUB, body, 0)

        @pl.when(i == last)
        def _():
            def body(g, carry):
                reduce_group(g)
                return carry
            lax.fori_loop(0, n // SUB, body, 0)

    for parity in range(2):
        pl.when(i % 2 == parity)(functools.partial(step, parity))

    o_ref[...] = x1_ref[...] + mod_ref[0, 5:6, :] * (_rms(moe_scr[...]) * gpost_ref[...])


def _combine(dest, y, wrow, x1, mod3, g_post_ffn, S):
    T, D = x1.shape
    n = min(COMBINE_TOKENS, S)
    per_b = S // n
    steps = T // n
    return pl.pallas_call(
        _combine_kernel,
        grid=(steps,),
        in_specs=[pl.BlockSpec((TOP_K, n), lambda i: (0, i), memory_space=pltpu.SMEM),
                  pl.BlockSpec((TOP_K, n), lambda i: (0, jnp.minimum(i + 1, steps - 1)), memory_space=pltpu.SMEM),
                  pl.BlockSpec(memory_space=pl.ANY),
                  pl.BlockSpec((n, LANES), lambda i: (i, 0)),
                  pl.BlockSpec((n, D), lambda i: (i, 0)),
                  pl.BlockSpec((1, 6, D), lambda i: (i // per_b, 0, 0)),
                  pl.BlockSpec((1, D), lambda i: (0, 0))],
        out_specs=pl.BlockSpec((n, D), lambda i: (i, 0)),
        out_shape=jax.ShapeDtypeStruct((T, D), F32),
        scratch_shapes=[pltpu.VMEM((2, TOP_K, n, 1, D), F32), pltpu.VMEM((n, D), F32),
                        pltpu.SemaphoreType.DMA((2,))],
        compiler_params=_cparams(("arbitrary",)),
        name="combine",
    )(dest, dest, y, wrow, x1, mod3, g_post_ffn.reshape(1, D))


def _regroup_w_in(w_in):
    D = w_in.shape[0]
    dw = DN_HEADS * HEAD_DIM
    mw = MB_HEADS * HEAD_DIM
    cuts = np.cumsum([dw, dw, dw, dw, DN_HEADS, DN_HEADS, mw, mw, mw, D, D])[:-1]
    dq, dk, dv, dz, db, da, mq, mk, mv, ga, gb = jnp.split(w_in, [int(c) for c in cuts], axis=1)
    small = jnp.concatenate([db, da, jnp.zeros((D, LANES - 2 * DN_HEADS), w_in.dtype)], axis=1)
    return jnp.concatenate([ga, gb, dq, dk, dv, dz, mq, mk, mv, small], axis=1).astype(BF16)


def kernel(x, c, w_ada, b_ada, g_pre_mix, g_post_mix, g_pre_ffn, g_post_ffn, w_in, conv_w, a_log, dt_bias,
           g_dn_out, w_br_a, w_br_b, w_o, w_router, b_router, w_gu, b_gu, w_down, b_down):
    B, S, D = x.shape
    l = 0
    mod = _adaln(c, w_ada[l], b_ada[l]).reshape(B, 6, D)
    proj = _inproj(x, mod, g_pre_mix[l], _regroup_w_in(w_in[l]))
    qr, kr, vb, kmean = _mobaprep(proj, B, S)
    ob = _moba(qr, kr, vb, kmean, B, S)
    oa = _deltanet(proj, conv_w[l], a_log[l], dt_bias[l], g_dn_out[l], B, S)
    x1, hp, idx8, wrow, rank8, cnt = _merge(oa, ob, proj, x.reshape(B * S, D), mod, g_post_mix[l], g_pre_ffn[l],
                                            w_br_a[l], w_br_b[l], w_o[l], w_router[l], b_router[l], S)
    out = _moe(x1, hp, idx8, wrow, rank8, cnt, mod, g_post_ffn[l], w_gu[l], b_gu[l], w_down[l], b_down[l], S)
    return out.reshape(B, S, D)


def _moe(x1, hp, idx8, wrow, rank8, cnt, mod, g_post_ffn, w_gu, b_gu, w_down, b_down, S):
    T = x1.shape[0]
    counts = cnt[:, 0].astype(jnp.int32)
    start = jnp.cumsum(counts) - counts
    hot = idx8[:TOP_K, :, None] == jnp.arange(N_EXPERTS, dtype=jnp.int32)
    dest = rank8[:TOP_K] + jnp.sum(jnp.where(hot, start, 0), axis=-1)
    xs = _dispatch(dest, hp)
    blk, e, lo, hi, new = _work_items(counts, T * TOP_K)
    y = _experts(xs, blk, e, lo, hi, new, w_gu, b_gu, w_down, b_down)
    return _combine(dest, y, wrow, x1, mod, g_post_ffn, S)
```

```python
import functools

import jax
import jax.numpy as jnp
import numpy as np
from jax import lax
from jax.experimental import pallas as pl
from jax.experimental.pallas import tpu as pltpu

F32 = jnp.float32
BF16 = jnp.bfloat16
HIGHEST = lax.Precision.HIGHEST

HEAD_DIM = 128
DN_HEADS = 4
DN_CONV = 4
DN_CHUNK = 64
MB_HEADS = 4
MB_BLOCK = 256
MB_TOPK = 3
ROPE_THETA = 500000.0
ROPE_DIM = HEAD_DIM // 4
N_EXPERTS = 32
TOP_K = 4
SWIGLU_LIMIT = 7.0
SWIGLU_ALPHA = 1.702
NORM_EPS = 1e-6
LANES = 128
NEG_BIG = -1e30

_W = DN_HEADS * HEAD_DIM // LANES
COL_GA = 0
COL_GB = 8
COL_DQ = 16
COL_DK = 20
COL_DV = 24
COL_DZ = 28
COL_MQ = 32
COL_MK = 36
COL_MV = 40
COL_SMALL = 44
N_PROJ = 45 * LANES

VMEM_LIMIT = 56 * 1024 * 1024


def _cparams(sem):
    return pltpu.CompilerParams(dimension_semantics=sem, vmem_limit_bytes=VMEM_LIMIT)


def _sigmoid(v):
    return 0.5 * jnp.tanh(0.5 * v) + 0.5


def _silu(v):
    return v * _sigmoid(v)


def _adaln_kernel(c_ref, w_ref, b_ref, o_ref):
    a = _silu(c_ref[...])
    o_ref[...] = jnp.dot(a, w_ref[...], preferred_element_type=F32, precision=HIGHEST) + b_ref[...]


def _adaln(c, w_ada, b_ada):
    B, D = c.shape
    N = w_ada.shape[1]
    tn = D
    return pl.pallas_call(
        _adaln_kernel,
        grid=(N // tn,),
        in_specs=[pl.BlockSpec((B, D), lambda j: (0, 0)),
                  pl.BlockSpec((D, tn), lambda j: (0, j)),
                  pl.BlockSpec((1, tn), lambda j: (0, j))],
        out_specs=pl.BlockSpec((B, tn), lambda j: (0, j)),
        out_shape=jax.ShapeDtypeStruct((B, N), F32),
        compiler_params=_cparams(("arbitrary",)),
        name="adaln",
    )(c, w_ada, b_ada.reshape(1, N))


def _inproj_kernel(x_ref, mod_ref, g_ref, w_ref, o_ref, h_scr):
    @pl.when(pl.program_id(2) == 0)
    def _():
        x = x_ref[0]
        y = x * lax.rsqrt(jnp.mean(x * x, axis=-1, keepdims=True) + NORM_EPS) * g_ref[...]
        h = y * (1.0 + mod_ref[0, 1:2, :]) + mod_ref[0, 0:1, :]
        h_scr[...] = h.astype(BF16)

    o_ref[...] = jnp.dot(h_scr[...], w_ref[...], preferred_element_type=F32)


def _inproj(x, mod3, g_pre, w_all):
    B, S, D = x.shape
    tm = min(1024, S)
    tn = N_PROJ // 3
    nrow = S // tm
    return pl.pallas_call(
        _inproj_kernel,
        grid=(B, nrow, N_PROJ // tn),
        in_specs=[pl.BlockSpec((1, tm, D), lambda b, i, j: (b, i, 0)),
                  pl.BlockSpec((1, 6, D), lambda b, i, j: (b, 0, 0)),
                  pl.BlockSpec((1, D), lambda b, i, j: (0, 0)),
                  pl.BlockSpec((D, tn), lambda b, i, j: (0, j))],
        out_specs=pl.BlockSpec((tm, tn), lambda b, i, j: (b * nrow + i, j)),
        out_shape=jax.ShapeDtypeStruct((B * S, N_PROJ), F32),
        scratch_shapes=[pltpu.VMEM((tm, D), BF16)],
        compiler_params=_cparams(("arbitrary", "arbitrary", "arbitrary")),
        name="inproj",
    )(x, mod3, g_pre.reshape(1, D), w_all)


def _rope(v, cosf, sinf, lane):
    rot = jnp.where(lane < ROPE_DIM // 2, pltpu.roll(v, LANES - ROPE_DIM // 2, 1), pltpu.roll(v, ROPE_DIM // 2, 1))
    return v * cosf + rot * sinf


def _mobaprep_kernel(q_ref, k_ref, v_ref, cos_ref, sin_ref, qo_ref, ko_ref, vo_ref, km_ref):
    cosf = cos_ref[...]
    sinf = sin_ref[...]
    lane = lax.broadcasted_iota(jnp.int32, cosf.shape, 1)
    for h in range(MB_HEADS):
        sl = slice(h * HEAD_DIM, (h + 1) * HEAD_DIM)
        qo_ref[:, sl] = (_rope(q_ref[:, sl], cosf, sinf, lane) * (HEAD_DIM ** -0.5)).astype(BF16)
        kr = _rope(k_ref[:, sl], cosf, sinf, lane)
        ko_ref[:, sl] = kr.astype(BF16)
        for blk in range(km_ref.shape[0]):
            km_ref[blk, :, sl] = jnp.mean(kr[blk * MB_BLOCK:(blk + 1) * MB_BLOCK], axis=0, keepdims=True)
    vo_ref[...] = v_ref[...].astype(BF16)


def _rope_tables(S):
    half = ROPE_DIM // 2
    inv_freq = ROPE_THETA ** (-jnp.arange(half, dtype=F32) / half)
    ang = jnp.arange(S, dtype=F32)[:, None] * inv_freq[None, :]
    cos, sin = jnp.cos(ang), jnp.sin(ang)
    rest = HEAD_DIM - ROPE_DIM
    cosf = jnp.concatenate([cos, cos, jnp.ones((S, rest), F32)], axis=1)
    sinf = jnp.concatenate([-sin, sin, jnp.zeros((S, rest), F32)], axis=1)
    return cosf, sinf


def _mobaprep(proj, B, S):
    T = B * S
    tb = min(4 * MB_BLOCK, S)
    nb = S // tb
    W = MB_HEADS * HEAD_DIM
    cosf, sinf = _rope_tables(S)
    col = lambda c: (lambda i: (i, c))
    return pl.pallas_call(
        _mobaprep_kernel,
        grid=(T // tb,),
        in_specs=[pl.BlockSpec((tb, W), col(COL_MQ // _W)),
                  pl.BlockSpec((tb, W), col(COL_MK // _W)),
                  pl.BlockSpec((tb, W), col(COL_MV // _W)),
                  pl.BlockSpec((tb, LANES), lambda i: (i % nb, 0)),
                  pl.BlockSpec((tb, LANES), lambda i: (i % nb, 0))],
        out_specs=[pl.BlockSpec((tb, W), lambda i: (i, 0)),
                   pl.BlockSpec((tb, W), lambda i: (i, 0)),
                   pl.BlockSpec((tb, W), lambda i: (i, 0)),
                   pl.BlockSpec((tb // MB_BLOCK, 1, W), lambda i: (i, 0, 0))],
        out_shape=[jax.ShapeDtypeStruct((T, W), BF16),
                   jax.ShapeDtypeStruct((T, W), BF16),
                   jax.ShapeDtypeStruct((T, W), BF16),
                   jax.ShapeDtypeStruct((T // MB_BLOCK, 1, W), F32)],
        compiler_params=_cparams(("arbitrary",)),
        name="mobaprep",
    )(proj, proj, proj, cosf, sinf)


MB_HPS = 4


def _moba_kernel(q_ref, k_ref, v_ref, km_ref, o_ref, s_scr, sd_scr, mx_scr, l_scr, acc_scr):
    qi = pl.program_id(1)
    tb = MB_BLOCK
    nb = km_ref.shape[0]
    nt = tb // LANES
    D = HEAD_DIM
    heads = range(MB_HPS)
    hs = [slice(h * D, (h + 1) * D) for h in heads]
    qs = [q_ref[:, hs[h]] for h in heads]

    sts = [lax.dot_general(km_ref[:, hs[h]], qs[h].astype(F32), (((1,), (1,)), ((), ())),
                           preferred_element_type=F32, precision=HIGHEST) for h in heads]
    blk = lax.broadcasted_iota(jnp.int32, (nb, tb), 0)
    rowid = lax.broadcasted_iota(jnp.int32, (LANES, tb), 0)
    sel_ts = []
    for h in heads:
        st = jnp.where(blk < qi, sts[h], -jnp.inf)
        sel_t = jnp.full((LANES, tb), -1.0, F32)
        for r in range(MB_TOPK):
            m = jnp.max(st, axis=0, keepdims=True)
            idx = jnp.min(jnp.where(st == m, blk, nb), axis=0, keepdims=True)
            sel_t = jnp.where(rowid == r, jnp.where(r < qi, idx, -1).astype(F32), sel_t)
            st = jnp.where(blk == idx, -jnp.inf, st)
        sel_ts.append(sel_t)
    sels = [jnp.transpose(t) for t in sel_ts]
    sel_rep = [[jnp.broadcast_to(sels[h][:, r:r + 1], (tb, LANES)) for r in range(MB_TOPK)] for h in heads]

    def logits(h, start, width):
        kslab = k_ref[pl.ds(pl.multiple_of(start, tb), width), hs[h]]
        return lax.dot_general(qs[h], kslab, (((1,), (1,)), ((), ())), preferred_element_type=F32)

    r_i = lax.broadcasted_iota(jnp.int32, (tb, LANES), 0)
    c_i = lax.broadcasted_iota(jnp.int32, (tb, LANES), 1)
    sds = [logits(h, qi * tb, tb) for h in heads]
    for h in heads:
        mx = jnp.full((tb, LANES), NEG_BIG, F32)
        for t in range(nt):
            piece = jnp.where(c_i + t * LANES <= r_i, sds[h][:, t * LANES:(t + 1) * LANES], NEG_BIG)
            sd_scr[h, :, t * LANES:(t + 1) * LANES] = piece
            mx = jnp.maximum(mx, piece)
        mx_scr[h] = mx

    n_pairs = (qi + 1) // 2

    def pass1(i, carry):
        kb0 = 2 * i
        s2s = [logits(h, kb0 * tb, 2 * tb) for h in heads]
        for h in heads:
            mx = mx_scr[h]
            for half in range(2):
                kbf = (kb0 + half).astype(F32)
                hit = (sel_rep[h][0] == kbf) | (sel_rep[h][1] == kbf) | (sel_rep[h][2] == kbf)
                for t in range(nt):
                    c0 = (half * nt + t) * LANES
                    piece = jnp.where(hit, s2s[h][:, c0:c0 + LANES], NEG_BIG)
                    s_scr[h, i, :, c0:c0 + LANES] = piece
                    mx = jnp.maximum(mx, piece)
            mx_scr[h] = mx
        return carry

    lax.fori_loop(0, n_pairs, pass1, 0)
    m_reps = [jnp.broadcast_to(jnp.max(mx_scr[h], axis=-1, keepdims=True), (tb, LANES)) for h in heads]

    def probs(load, width, m_rep):
        ps, lsum = [], jnp.zeros((tb, LANES), F32)
        for t in range(width // LANES):
            p = jnp.exp(load(t) - m_rep)
            lsum = lsum + p
            ps.append(p.astype(BF16))
        return jnp.concatenate(ps, axis=1), lsum

    pls = [probs(lambda t, h=h: sd_scr[h, :, t * LANES:(t + 1) * LANES], tb, m_reps[h]) for h in heads]
    for h in heads:
        l_scr[h] = pls[h][1]
        acc_scr[h] = jnp.dot(pls[h][0], v_ref[pl.ds(pl.multiple_of(qi * tb, tb), tb), hs[h]],
                             preferred_element_type=F32)

    def pass2(i, carry):
        pls = [probs(lambda t, h=h: s_scr[h, i, :, t * LANES:(t + 1) * LANES], 2 * tb, m_reps[h]) for h in heads]
        for h in heads:
            vslab = v_ref[pl.ds(pl.multiple_of(2 * i * tb, tb), 2 * tb), hs[h]]
            l_scr[h] = l_scr[h] + pls[h][1]
            acc_scr[h] = acc_scr[h] + jnp.dot(pls[h][0], vslab, preferred_element_type=F32)
        return carry

    lax.fori_loop(0, n_pairs, pass2, 0)
    for h in heads:
        o_ref[:, hs[h]] = (acc_scr[h] / jnp.sum(l_scr[h], axis=-1, keepdims=True)).astype(o_ref.dtype)


def _moba(qr, kr, vb, kmean, B, S):
    tb = MB_BLOCK
    nb = S // tb
    T = B * S
    km = kmean.reshape(B, nb, MB_HEADS * HEAD_DIM)
    HW = MB_HPS * HEAD_DIM
    ng = MB_HEADS // MB_HPS
    return pl.pallas_call(
        _moba_kernel,
        grid=(B * ng, nb),
        in_specs=[pl.BlockSpec((tb, HW), lambda g, i: ((g // ng) * nb + i, g % ng)),
                  pl.BlockSpec((S, HW), lambda g, i: (g // ng, g % ng)),
                  pl.BlockSpec((S, HW), lambda g, i: (g // ng, g % ng)),
                  pl.BlockSpec((None, nb, HW), lambda g, i: (g // ng, 0, g % ng))],
        out_specs=pl.BlockSpec((tb, HW), lambda g, i: ((g // ng) * nb + i, g % ng)),
        out_shape=jax.ShapeDtypeStruct((T, MB_HEADS * HEAD_DIM), BF16),
        scratch_shapes=[pltpu.VMEM((MB_HPS, (nb + 1) // 2, tb, 2 * tb), F32),
                        pltpu.VMEM((MB_HPS, tb, tb), F32),
                        pltpu.VMEM((MB_HPS, tb, LANES), F32),
                        pltpu.VMEM((MB_HPS, tb, LANES), F32),
                        pltpu.VMEM((MB_HPS, tb, HEAD_DIM), F32)],
        compiler_params=_cparams(("arbitrary", "arbitrary")),
        name="moba",
    )(qr, kr, vb, km)


DN_TILE_CHUNKS = 8


def _softplus(v):
    return jnp.maximum(v, 0.0) + jnp.log1p(jnp.exp(-jnp.abs(v)))


DN_GROUP = 2
DN_HPS = 4


def _split(a):
    hi = pltpu.bitcast(pltpu.bitcast(a, jnp.uint32) & jnp.uint32(0xFFFF0000), F32)
    return hi.astype(BF16), (a - hi).astype(BF16)


def _dot3(ah, al, bh, bl):
    lhs = jnp.concatenate([ah, ah, al], axis=1)
    rhs = jnp.concatenate([bh, bl, bh], axis=0)
    return jnp.dot(lhs, rhs, preferred_element_type=F32)


def _dot_bf(a, b):
    return jnp.dot(a.astype(BF16), b.astype(BF16), preferred_element_type=F32)


def _deltanet_kernel(q_ref, k_ref, v_ref, z_ref, sm_ref, wq_ref, wk_ref, wv_ref, alog_ref, dtb_ref, gout_ref,
                     o_ref, xp_scr, state_scr):
    i = pl.program_id(2)
    C = DN_CHUNK
    TR = q_ref.shape[0]
    HALO = 8

    @pl.when(i == 0)
    def _():
        xp_scr[...] = jnp.zeros(xp_scr.shape, F32)
        state_scr[...] = jnp.zeros(state_scr.shape, F32)

    def conv_silu(slot, x_ref, w_ref):
        xp_scr[slot, 0:HALO, :] = xp_scr[slot, TR:TR + HALO, :]
        xp_scr[slot, HALO:HALO + TR, :] = x_ref[...]
        acc = w_ref[DN_CONV - 1:DN_CONV, :] * xp_scr[slot, HALO:HALO + TR, :]
        for j in range(1, DN_CONV):
            acc = acc + w_ref[DN_CONV - 1 - j:DN_CONV - j, :] * xp_scr[slot, HALO - j:HALO - j + TR, :]
        return _silu(acc)

    q_all = conv_silu(0, q_ref, wq_ref)
    k_all = conv_silu(1, k_ref, wk_ref)
    v_all = conv_silu(2, v_ref, wv_ref)

    sm = sm_ref[...]
    lane = lax.broadcasted_iota(jnp.int32, sm.shape, 1)
    row = lax.broadcasted_iota(jnp.int32, sm.shape, 0)
    beta_all = _sigmoid(sm)
    g_all = -jnp.exp(alog_ref[...]) * _softplus(sm + dtb_ref[...])
    pos = row % C
    gc_all = g_all
    shift = 1
    while shift < C:
        gc_all = gc_all + jnp.where(pos >= shift, pltpu.roll(gc_all, shift, 0), 0.0)
        shift *= 2
    gc_t = jnp.transpose(gc_all)
    row_t = lax.broadcasted_iota(jnp.int32, gc_t.shape, 0)

    G = DN_GROUP * C
    ri = lax.broadcasted_iota(jnp.int32, (G, G), 0)
    ci = lax.broadcasted_iota(jnp.int32, (G, G), 1)
    same = (ri // C) == (ci // C)
    tril = same & (ci <= ri)
    strict = same & (ci < ri)
    eye = (ci == ri).astype(F32)
    D = HEAD_DIM
    gout = gout_ref[...]

    heads = []
    for hh in range(DN_HPS):
        h = pl.program_id(1) * DN_HPS + hh
        sl = slice(hh * D, (hh + 1) * D)
        q, k, v = q_all[:, sl], k_all[:, sl], v_all[:, sl]
        q = q * lax.rsqrt(jnp.sum(q * q, axis=-1, keepdims=True) + 1e-6) * (D ** -0.5)
        k = k * lax.rsqrt(jnp.sum(k * k, axis=-1, keepdims=True) + 1e-6)
        beta = jnp.sum(jnp.where(lane == h, beta_all, 0.0), axis=1, keepdims=True)
        gc = jnp.sum(jnp.where(lane == DN_HEADS + h, gc_all, 0.0), axis=1, keepdims=True)
        gc_row = jnp.sum(jnp.where(row_t == DN_HEADS + h, gc_t, 0.0), axis=0, keepdims=True)
        heads.append((q, k, v, beta, gc, gc_row, jnp.transpose(k)))

    steps = [[None] * (TR // C) for _ in range(DN_HPS)]
    preps = []
    for hh in range(DN_HPS):
        for g in range(TR // G):
            preps.append(_deltanet_group(heads[hh], g, tril, strict, eye, steps[hh]))
    _round_robin(preps)

    outs = [[] for _ in range(DN_HPS)]
    _round_robin([_deltanet_chain(hh, steps[hh], state_scr, outs[hh]) for hh in range(DN_HPS)])
    for hh in range(DN_HPS):
        sl = slice(hh * D, (hh + 1) * D)
        o = jnp.concatenate(outs[hh], axis=0)
        y = o * lax.rsqrt(jnp.mean(o * o, axis=-1, keepdims=True) + NORM_EPS) * gout
        o_ref[:, sl] = (y * _silu(z_ref[:, sl])).astype(o_ref.dtype)


def _round_robin(gens):
    active = list(gens)
    while active:
        still = []
        for gen in active:
            try:
                next(gen)
                still.append(gen)
            except StopIteration:
                pass
        active = still


def _deltanet_group(head, g, tril, strict, eye, steps_out):
    C = DN_CHUNK
    D = HEAD_DIM
    G = DN_GROUP * C
    q, k, v, beta, gc, gc_row, k_t = head
    r0 = g * G
    qg, kg, vg = q[r0:r0 + G], k[r0:r0 + G], v[r0:r0 + G]
    bg = beta[r0:r0 + G]
    gcg = gc[r0:r0 + G]
    gcr = gc_row[:, r0:r0 + G]
    ktg = k_t[:, r0:r0 + G]
    decay = jnp.where(tril, jnp.exp(jnp.where(tril, gcg - gcr, 0.0)), 0.0)
    kb = kg * bg
    aq = _dot_bf(jnp.concatenate([kb, qg], axis=0), ktg)
    yield
    m_neg = jnp.where(strict, -(aq[:G] * decay), 0.0)
    qk = (aq[G:] * decay).astype(BF16)
    t_inv = eye + m_neg
    ph, pl_ = _split(m_neg)
    for _ in range(5):
        th, tl = _split(t_inv)
        ph, pl_ = _split(_dot3(ph, pl_, ph, pl_))
        yield
        t_inv = t_inv + _dot3(th, tl, ph, pl_)
        yield
    egc = jnp.exp(gcg)
    th, tl = _split(t_inv)
    rh, rl = _split(jnp.concatenate([kb * egc, vg * bg], axis=1))
    wu = _dot3(th, tl, rh, rl).astype(BF16)
    yield
    qr = jnp.dot(qk, wu, preferred_element_type=F32)
    qp = qg * egc - qr[:, :D]
    r_all = qr[:, D:]
    yield
    for c in range(DN_GROUP):
        c0 = c * C
        g_last = gcg[c0 + C - 1:c0 + C, :]
        kt_tail = ktg[:, c0:c0 + C] * jnp.exp(g_last - gcr[:, c0:c0 + C])
        gh = jnp.dot(kt_tail.astype(BF16), wu[c0:c0 + C, :], preferred_element_type=F32)
        lhs = jnp.concatenate([gh[:, :D], qp[c0:c0 + C]], axis=0).astype(BF16)
        steps_out[g * DN_GROUP + c] = (lhs, gh[:, D:], r_all[c0:c0 + C], jnp.exp(g_last))
        yield


def _deltanet_chain(hh, steps, state_scr, outs):
    D = HEAD_DIM
    state = state_scr[hh]
    for lhs, h_add, r_add, dec in steps:
        res = jnp.dot(lhs, state.astype(BF16), preferred_element_type=F32)
        outs.append(res[D:] + r_add)
        state = state * dec - res[:D] + h_add
        yield
    state_scr[hh] = state


def _deltanet(proj, conv_w, a_log, dt_bias, g_dn_out, B, S):
    T = B * S
    TR = min(DN_TILE_CHUNKS * DN_CHUNK, S)
    nt = S // TR
    pad = jnp.zeros((DN_HEADS,), F32)
    rest = jnp.zeros((LANES - 2 * DN_HEADS,), F32)
    alog_lane = jnp.concatenate([pad, a_log.astype(F32), rest]).reshape(1, LANES)
    dtb_lane = jnp.concatenate([pad, dt_bias.astype(F32), rest]).reshape(1, LANES)
    HW = DN_HPS * HEAD_DIM
    per = HW // LANES
    rows = lambda c0: (lambda b, h, i: (b * nt + i, c0 // per + h))
    wcol = lambda c0: (lambda b, h, i: (0, c0 // per + h))
    const = lambda b, h, i: (0, 0)
    return pl.pallas_call(
        _deltanet_kernel,
        grid=(B, DN_HEADS // DN_HPS, nt),
        in_specs=[pl.BlockSpec((TR, HW), rows(COL_DQ)),
                  pl.BlockSpec((TR, HW), rows(COL_DK)),
                  pl.BlockSpec((TR, HW), rows(COL_DV)),
                  pl.BlockSpec((TR, HW), rows(COL_DZ)),
                  pl.BlockSpec((TR, LANES), lambda b, h, i: (b * nt + i, COL_SMALL)),
                  pl.BlockSpec((DN_CONV, HW), wcol(0)),
                  pl.BlockSpec((DN_CONV, HW), wcol(DN_HEADS)),
                  pl.BlockSpec((DN_CONV, HW), wcol(2 * DN_HEADS)),
                  pl.BlockSpec((1, LANES), const),
                  pl.BlockSpec((1, LANES), const),
                  pl.BlockSpec((1, HEAD_DIM), const)],
        out_specs=pl.BlockSpec((TR, HW), lambda b, h, i: (b * nt + i, h)),
        out_shape=jax.ShapeDtypeStruct((T, DN_HEADS * HEAD_DIM), BF16),
        scratch_shapes=[pltpu.VMEM((3, TR + 8, HW), F32), pltpu.VMEM((DN_HPS, HEAD_DIM, HEAD_DIM), F32)],
        compiler_params=_cparams(("arbitrary", "arbitrary", "arbitrary")),
        name="deltanet",
    )(proj, proj, proj, proj, proj, conv_w, conv_w, conv_w, alog_lane, dtb_lane, g_dn_out.reshape(1, HEAD_DIM))


def _rms(v):
    return v * lax.rsqrt(jnp.mean(v * v, axis=-1, keepdims=True) + NORM_EPS)


def _merge_kernel(oa_ref, ob_ref, ga_ref, gb_ref, x_ref, mod_ref, gpost_ref, gpre_ref, wa_ref, wb_ref, wo_ref,
                  wrt_ref, br_ref, x1_ref, hp_ref, idx_ref, wrow_ref, rank_ref, cnt_ref, carry_scr):
    E = N_EXPERTS
    tm = x_ref.shape[0]
    half = x_ref.shape[1] // 2

    @pl.when(pl.program_id(0) == 0)
    def _():
        carry_scr[...] = jnp.zeros(carry_scr.shape, F32)

    ya = jnp.dot(oa_ref[...], wa_ref[...], preferred_element_type=F32)
    yb = jnp.dot(ob_ref[...], wb_ref[...], preferred_element_type=F32)
    merged = _sigmoid(ga_ref[...]) * ya + _sigmoid(gb_ref[...]) * yb
    mix = jnp.dot(merged.astype(BF16), wo_ref[...], preferred_element_type=F32)
    x1 = x_ref[...] + mod_ref[0, 2:3, :] * (_rms(mix) * gpost_ref[...])
    x1_ref[...] = x1
    h2 = (_rms(x1) * gpre_ref[...]) * (1.0 + mod_ref[0, 4:5, :]) + mod_ref[0, 3:4, :]

    lo_bits = pltpu.bitcast(h2[:, :half].astype(BF16).astype(F32), jnp.uint32) >> 16
    hi_bits = pltpu.bitcast(h2[:, half:].astype(BF16).astype(F32), jnp.uint32) & jnp.uint32(0xFFFF0000)
    hp_ref[...] = hi_bits | lo_bits

    lt = lax.dot_general(wrt_ref[...], h2, (((1,), (1,)), ((), ())), preferred_element_type=F32,
                         precision=HIGHEST) + br_ref[...]
    eid = lax.broadcasted_iota(jnp.int32, (E, tm), 0)
    vals, idxs = [], []
    for _ in range(TOP_K):
        m = jnp.max(lt, axis=0, keepdims=True)
        idx = jnp.min(jnp.where(lt == m, eid, E), axis=0, keepdims=True)
        vals.append(m)
        idxs.append(idx)
        lt = jnp.where(eid == idx, -jnp.inf, lt)
    exps = [jnp.exp(v - vals[0]) for v in vals]
    den = exps[0] + exps[1] + exps[2] + exps[3]
    wts = [e / den for e in exps]

    hot = jnp.zeros((E, tm), F32)
    for idx in idxs:
        hot = hot + (eid == idx).astype(F32)
    ti = lax.broadcasted_iota(jnp.int32, (tm, tm), 0)
    tj = lax.broadcasted_iota(jnp.int32, (tm, tm), 1)
    before = (ti < tj).astype(BF16)
    prior = carry_scr[...][:, 0:1] + jnp.dot(hot.astype(BF16), before, preferred_element_type=F32)
    row8 = lax.broadcasted_iota(jnp.int32, (8, tm), 0)
    row128 = lax.broadcasted_iota(jnp.int32, (LANES, tm), 0)
    idx8 = jnp.zeros((8, tm), jnp.int32)
    rank8 = jnp.zeros((8, tm), jnp.int32)
    w128 = jnp.zeros((LANES, tm), F32)
    for r in range(TOP_K):
        rank_r = jnp.sum(jnp.where(eid == idxs[r], prior, 0.0), axis=0, keepdims=True)
        idx8 = jnp.where(row8 == r, idxs[r], idx8)
        rank8 = jnp.where(row8 == r, rank_r.astype(jnp.int32), rank8)
        w128 = jnp.where(row128 == r, wts[r], w128)
    idx_ref[...] = idx8
    rank_ref[...] = rank8
    wrow_ref[...] = jnp.transpose(w128)
    carry = carry_scr[...] + jnp.sum(hot, axis=1, keepdims=True)
    carry_scr[...] = carry
    cnt_ref[...] = carry


def _merge(oa, ob, proj, x2, mod3, g_post_mix, g_pre_ffn, w_br_a, w_br_b, w_o, w_router, b_router, S):
    T, D = x2.shape
    E = N_EXPERTS
    tm = min(512, S)
    per_b = S // tm
    W = DN_HEADS * HEAD_DIM
    row = lambda i: (i, 0)
    const = lambda i: (0, 0)
    lane_t = lambda i: (0, i)
    return pl.pallas_call(
        _merge_kernel,
        grid=(T // tm,),
        in_specs=[pl.BlockSpec((tm, W), row),
                  pl.BlockSpec((tm, W), row),
                  pl.BlockSpec((tm, D), lambda i: (i, COL_GA * LANES // D)),
                  pl.BlockSpec((tm, D), lambda i: (i, COL_GB * LANES // D)),
                  pl.BlockSpec((tm, D), row),
                  pl.BlockSpec((1, 6, D), lambda i: (i // per_b, 0, 0)),
                  pl.BlockSpec((1, D), const),
                  pl.BlockSpec((1, D), const),
                  pl.BlockSpec((W, D), const),
                  pl.BlockSpec((W, D), const),
                  pl.BlockSpec((D, D), const),
                  pl.BlockSpec((E, D), const),
                  pl.BlockSpec((E, 1), const)],
        out_specs=[pl.BlockSpec((tm, D), row),
                   pl.BlockSpec((tm, D // 2), row),
                   pl.BlockSpec((8, tm), lane_t),
                   pl.BlockSpec((tm, LANES), row),
                   pl.BlockSpec((8, tm), lane_t),
                   pl.BlockSpec((E, LANES), const)],
        out_shape=[jax.ShapeDtypeStruct((T, D), F32),
                   jax.ShapeDtypeStruct((T, D // 2), jnp.uint32),
                   jax.ShapeDtypeStruct((8, T), jnp.int32),
                   jax.ShapeDtypeStruct((T, LANES), F32),
                   jax.ShapeDtypeStruct((8, T), jnp.int32),
                   jax.ShapeDtypeStruct((E, LANES), F32)],
        scratch_shapes=[pltpu.VMEM((E, LANES), F32)],
        compiler_params=_cparams(("arbitrary",)),
        name="merge",
    )(oa, ob, proj, proj, x2, mod3, g_post_mix.reshape(1, D), g_pre_ffn.reshape(1, D),
      w_br_a.astype(BF16), w_br_b.astype(BF16), w_o.astype(BF16),
      jnp.transpose(w_router).astype(F32), b_router.reshape(E, 1).astype(F32))


DISPATCH_TOKENS = 256


def _dispatch_kernel(dest_ref, src_ref, dst_ref, sem):
    n = dest_ref.shape[1]

    def issue(j, carry):
        for r in range(TOP_K):
            pltpu.make_async_copy(src_ref.at[pl.ds(j, 1)], dst_ref.at[pl.ds(dest_ref[r, j], 1)], sem).start()
        return carry

    lax.fori_loop(0, n, issue, 0, unroll=8)
    for r in range(TOP_K):
        pltpu.make_async_copy(src_ref, dst_ref.at[pl.ds(0, n)], sem).wait()


def _dispatch(dest, hp):
    T, Wd = hp.shape
    n = min(DISPATCH_TOKENS, T)
    return pl.pallas_call(
        _dispatch_kernel,
        grid=(T // n,),
        in_specs=[pl.BlockSpec((TOP_K, n), lambda i: (0, i), memory_space=pltpu.SMEM),
                  pl.BlockSpec((n, Wd), lambda i: (i, 0))],
        out_specs=pl.BlockSpec(memory_space=pl.ANY),
        out_shape=jax.ShapeDtypeStruct((T * TOP_K, Wd), hp.dtype),
        scratch_shapes=[pltpu.SemaphoreType.DMA(())],
        compiler_params=_cparams(("arbitrary",)),
        name="dispatch",
    )(dest, hp)


EXPERT_ROWS = 512
EXPERT_FEATURE_TILE = 1024


def _experts_kernel(blk_ref, exp_ref, lo_ref, hi_ref, new_ref, x_ref, wgu_ref, bgu_ref, wdn_ref, bdn_ref, o_ref,
                    wgu_bf, wdn_bf):
    k = pl.program_id(0)
    lo = lo_ref[k]
    hi = hi_ref[k]
    tm = x_ref.shape[0]
    F = wdn_ref.shape[0]

    @pl.when(new_ref[k] == 1)
    def _():
        wgu_bf[...] = wgu_ref[...].astype(BF16)
        wdn_bf[...] = wdn_ref[...].astype(BF16)

    @pl.when(hi > lo)
    def _():
        word = x_ref[...]
        x = jnp.concatenate([pltpu.bitcast(word << 16, F32).astype(BF16),
                             pltpu.bitcast(word & jnp.uint32(0xFFFF0000), F32).astype(BF16)], axis=1)
        y = None
        for c in range(0, F, EXPERT_FEATURE_TILE):
            ft = slice(c, c + EXPERT_FEATURE_TILE)
            ut = slice(F + c, F + c + EXPERT_FEATURE_TILE)
            gate = jnp.dot(x, wgu_bf[:, ft], preferred_element_type=F32) + bgu_ref[:, ft]
            up = jnp.dot(x, wgu_bf[:, ut], preferred_element_type=F32) + bgu_ref[:, ut]
            gate = jnp.minimum(gate, SWIGLU_LIMIT)
            up = jnp.clip(up, -SWIGLU_LIMIT, SWIGLU_LIMIT)
            act = (up + 1.0) * gate * _sigmoid(SWIGLU_ALPHA * gate)
            part = jnp.dot(act.astype(BF16), wdn_bf[ft, :], preferred_element_type=F32)
            y = part + bdn_ref[...] if y is None else y + part
        rows = lax.broadcasted_iota(jnp.int32, (tm, 1), 0)
        keep = (rows >= lo) & (rows < hi)

        @pl.when(lo == 0)
        def _():
            o_ref[:, 0, :] = jnp.where(keep, y, 0.0)

        @pl.when(lo > 0)
        def _():
            o_ref[:, 0, :] = jnp.where(keep, y, o_ref[:, 0, :])


def _experts(xs, item_blk, item_exp, item_lo, item_hi, item_new, w_gu, b_gu, w_down, b_down):
    A, half = xs.shape
    E, D, F2 = w_gu.shape
    F = F2 // 2
    tm = EXPERT_ROWS
    n_items = item_blk.shape[0]
    grid_spec = pltpu.PrefetchScalarGridSpec(
        num_scalar_prefetch=5,
        grid=(n_items,),
        in_specs=[pl.BlockSpec((tm, half), lambda k, blk, ex, lo, hi, nw: (blk[k], 0)),
                  pl.BlockSpec((None, D, F2), lambda k, blk, ex, lo, hi, nw: (ex[k], 0, 0)),
                  pl.BlockSpec((None, 1, F2), lambda k, blk, ex, lo, hi, nw: (ex[k], 0, 0)),
                  pl.BlockSpec((None, F, D), lambda k, blk, ex, lo, hi, nw: (ex[k], 0, 0)),
                  pl.BlockSpec((None, 1, D), lambda k, blk, ex, lo, hi, nw: (ex[k], 0, 0))],
        out_specs=pl.BlockSpec((tm, 1, D), lambda k, blk, ex, lo, hi, nw: (blk[k], 0, 0)),
        scratch_shapes=[pltpu.VMEM((D, F2), BF16), pltpu.VMEM((F, D), BF16)],
    )
    return pl.pallas_call(
        _experts_kernel,
        grid_spec=grid_spec,
        out_shape=jax.ShapeDtypeStruct((A, 1, D), F32),
        compiler_params=_cparams(("arbitrary",)),
        name="experts",
    )(item_blk, item_exp, item_lo, item_hi, item_new, xs, w_gu, b_gu.reshape(E, 1, F2).astype(F32),
      w_down, b_down.reshape(E, 1, D).astype(F32))


def _work_items(counts, A):
    E = N_EXPERTS
    tm = EXPERT_ROWS
    n_items = A // tm + E - 1
    end = jnp.cumsum(counts)
    start = end - counts
    first_blk = start // tm
    last_blk = jnp.maximum(end - 1, 0) // tm
    n_e = jnp.where(counts > 0, last_blk - first_blk + 1, 0)
    off_end = jnp.cumsum(n_e)
    off = off_end - n_e
    total = off_end[-1]
    k = jnp.arange(n_items, dtype=jnp.int32)
    kk = jnp.minimum(k, total - 1)
    e = jnp.searchsorted(off_end, kk, side='right').astype(jnp.int32)
    blk = (first_blk[e] + (kk - off[e])).astype(jnp.int32)
    lo = jnp.maximum(start[e], blk * tm) - blk * tm
    hi = jnp.minimum(end[e], (blk + 1) * tm) - blk * tm
    valid = k < total
    lo = jnp.where(valid, lo, 0).astype(jnp.int32)
    hi = jnp.where(valid, hi, 0).astype(jnp.int32)
    new = jnp.concatenate([jnp.ones((1,), jnp.int32), (e[1:] != e[:-1]).astype(jnp.int32)])
    return blk, e, lo, hi, new


COMBINE_TOKENS = 256


def _combine_kernel(dest_ref, nxt_ref, y_ref, wrow_ref, x1_ref, mod_ref, gpost_ref, o_ref, stage, moe_scr, sem):
    i = pl.program_id(0)
    last = pl.num_programs(0) - 1
    n = x1_ref.shape[0]
    SUB = 8

    def gather(idx_ref, j, s):
        for r in range(TOP_K):
            pltpu.make_async_copy(y_ref.at[idx_ref[r, j]], stage.at[s, r, j], sem.at[s]).start()

    @pl.when(i == 0)
    def _():
        def prime(j, carry):
            gather(dest_ref, j, 0)
            return carry
        lax.fori_loop(0, n, prime, 0, unroll=SUB)

    def step(slot):
        for r in range(TOP_K):
            pltpu.make_async_copy(y_ref.at[pl.ds(0, n)], stage.at[slot, r], sem.at[slot]).wait()

        def reduce_group(g):
            rows = pl.ds(pl.multiple_of(g * SUB, SUB), SUB)
            w = wrow_ref[rows, :]
            acc = w[:, 0:1] * stage[slot, 0, rows, 0, :]
            for r in range(1, TOP_K):
                acc = acc + w[:, r:r + 1] * stage[slot, r, rows, 0, :]
            moe_scr[rows, :] = acc

        @pl.when(i < last)
        def _():
            def body(g, carry):
                for t in range(SUB):
                    gather(nxt_ref, g * SUB + t, 1 - slot)
                reduce_group(g)
                return carry
            lax.fori_loop(0, n // SUB, body, 0)

        @pl.when(i == last)
        def _():
            def body(g, carry):
                reduce_group(g)
                return carry
            lax.fori_loop(0, n // SUB, body, 0)

    for parity in range(2):
        pl.when(i % 2 == parity)(functools.partial(step, parity))

    o_ref[...] = x1_ref[...] + mod_ref[0, 5:6, :] * (_rms(moe_scr[...]) * gpost_ref[...])


def _combine(dest, y, wrow, x1, mod3, g_post_ffn, S):
    T, D = x1.shape
    n = min(COMBINE_TOKENS, S)
    per_b = S // n
    steps = T // n
    return pl.pallas_call(
        _combine_kernel,
        grid=(steps,),
        in_specs=[pl.BlockSpec((TOP_K, n), lambda i: (0, i), memory_space=pltpu.SMEM),
                  pl.BlockSpec((TOP_K, n), lambda i: (0, jnp.minimum(i + 1, steps - 1)), memory_space=pltpu.SMEM),
                  pl.BlockSpec(memory_space=pl.ANY),
                  pl.BlockSpec((n, LANES), lambda i: (i, 0)),
                  pl.BlockSpec((n, D), lambda i: (i, 0)),
                  pl.BlockSpec((1, 6, D), lambda i: (i // per_b, 0, 0)),
                  pl.BlockSpec((1, D), lambda i: (0, 0))],
        out_specs=pl.BlockSpec((n, D), lambda i: (i, 0)),
        out_shape=jax.ShapeDtypeStruct((T, D), F32),
        scratch_shapes=[pltpu.VMEM((2, TOP_K, n, 1, D), F32), pltpu.VMEM((n, D), F32),
                        pltpu.SemaphoreType.DMA((2,))],
        compiler_params=_cparams(("arbitrary",)),
        name="combine",
    )(dest, dest, y, wrow, x1, mod3, g_post_ffn.reshape(1, D))


def _regroup_w_in(w_in):
    D = w_in.shape[0]
    dw = DN_HEADS * HEAD_DIM
    mw = MB_HEADS * HEAD_DIM
    cuts = np.cumsum([dw, dw, dw, dw, DN_HEADS, DN_HEADS, mw, mw, mw, D, D])[:-1]
    dq, dk, dv, dz, db, da, mq, mk, mv, ga, gb = jnp.split(w_in, [int(c) for c in cuts], axis=1)
    small = jnp.concatenate([db, da, jnp.zeros((D, LANES - 2 * DN_HEADS), w_in.dtype)], axis=1)
    return jnp.concatenate([ga, gb, dq, dk, dv, dz, mq, mk, mv, small], axis=1).astype(BF16)


def kernel(x, c, w_ada, b_ada, g_pre_mix, g_post_mix, g_pre_ffn, g_post_ffn, w_in, conv_w, a_log, dt_bias,
           g_dn_out, w_br_a, w_br_b, w_o, w_router, b_router, w_gu, b_gu, w_down, b_down):
    B, S, D = x.shape
    l = 0
    mod = _adaln(c, w_ada[l], b_ada[l]).reshape(B, 6, D)
    proj = _inproj(x, mod, g_pre_mix[l], _regroup_w_in(w_in[l]))
    qr, kr, vb, kmean = _mobaprep(proj, B, S)
    ob = _moba(qr, kr, vb, kmean, B, S)
    oa = _deltanet(proj, conv_w[l], a_log[l], dt_bias[l], g_dn_out[l], B, S)
    x1, hp, idx8, wrow, rank8, cnt = _merge(oa, ob, proj, x.reshape(B * S, D), mod, g_post_mix[l], g_pre_ffn[l],
                                            w_br_a[l], w_br_b[l], w_o[l], w_router[l], b_router[l], S)
    out = _moe(x1, hp, idx8, wrow, rank8, cnt, mod, g_post_ffn[l], w_gu[l], b_gu[l], w_down[l], b_down[l], S)
    return out.reshape(B, S, D)


def _moe(x1, hp, idx8, wrow, rank8, cnt, mod, g_post_ffn, w_gu, b_gu, w_down, b_down, S):
    T = x1.shape[0]
    counts = cnt[:, 0].astype(jnp.int32)
    start = jnp.cumsum(counts) - counts
    hot = idx8[:TOP_K, :, None] == jnp.arange(N_EXPERTS, dtype=jnp.int32)
    dest = rank8[:TOP_K] + jnp.sum(jnp.where(hot, start, 0), axis=-1)
    xs = _dispatch(dest, hp)
    blk, e, lo, hi, new = _work_items(counts, T * TOP_K)
    y = _experts(xs, blk, e, lo, hi, new, w_gu, b_gu, w_down, b_down)
    return _combine(dest, y, wrow, x1, mod, g_post_ffn, S)
```

```python
import functools

import jax
import jax.numpy as jnp
import numpy as np
from jax import lax
from jax.experimental import pallas as pl
from jax.experimental.pallas import tpu as pltpu

F32 = jnp.float32
BF16 = jnp.bfloat16
HIGHEST = lax.Precision.HIGHEST

HEAD_DIM = 128
DN_HEADS = 4
DN_CONV = 4
DN_CHUNK = 64
MB_HEADS = 4
MB_BLOCK = 256
MB_TOPK = 3
ROPE_THETA = 500000.0
ROPE_DIM = HEAD_DIM // 4
N_EXPERTS = 32
TOP_K = 4
SWIGLU_LIMIT = 7.0
SWIGLU_ALPHA = 1.702
NORM_EPS = 1e-6
LANES = 128
NEG_BIG = -1e30
MB_Q_SCALE = HEAD_DIM ** -0.5 * float(np.log2(np.e))

_W = DN_HEADS * HEAD_DIM // LANES
COL_GA = 0
COL_GB = 8
COL_DQ = 16
COL_DK = 20
COL_DV = 24
COL_DZ = 28
COL_MQ = 32
COL_MK = 36
COL_MV = 40
COL_SMALL = 44
N_PROJ = 45 * LANES

VMEM_LIMIT = 56 * 1024 * 1024


def _cparams(sem):
    return pltpu.CompilerParams(dimension_semantics=sem, vmem_limit_bytes=VMEM_LIMIT)


def _sigmoid(v):
    return 0.5 * jnp.tanh(0.5 * v) + 0.5


def _silu(v):
    return v * _sigmoid(v)


def _adaln_kernel(c_ref, w_ref, b_ref, o_ref):
    a = _silu(c_ref[...])
    o_ref[...] = jnp.dot(a, w_ref[...], preferred_element_type=F32, precision=HIGHEST) + b_ref[...]


def _adaln(c, w_ada, b_ada):
    B, D = c.shape
    N = w_ada.shape[1]
    tn = D
    return pl.pallas_call(
        _adaln_kernel,
        grid=(N // tn,),
        in_specs=[pl.BlockSpec((B, D), lambda j: (0, 0)),
                  pl.BlockSpec((D, tn), lambda j: (0, j)),
                  pl.BlockSpec((1, tn), lambda j: (0, j))],
        out_specs=pl.BlockSpec((B, tn), lambda j: (0, j)),
        out_shape=jax.ShapeDtypeStruct((B, N), F32),
        compiler_params=_cparams(("arbitrary",)),
        name="adaln",
    )(c, w_ada, b_ada.reshape(1, N))


def _inproj_kernel(x_ref, mod_ref, g_ref, w_ref, o_ref, h_scr):
    @pl.when(pl.program_id(2) == 0)
    def _():
        x = x_ref[0]
        y = x * lax.rsqrt(jnp.mean(x * x, axis=-1, keepdims=True) + NORM_EPS) * g_ref[...]
        h = y * (1.0 + mod_ref[0, 1:2, :]) + mod_ref[0, 0:1, :]
        h_scr[...] = h.astype(BF16)

    o_ref[...] = jnp.dot(h_scr[...], w_ref[...], preferred_element_type=F32)


def _inproj(x, mod3, g_pre, w_all):
    B, S, D = x.shape
    tm = min(1024, S)
    tn = N_PROJ // 3
    nrow = S // tm
    return pl.pallas_call(
        _inproj_kernel,
        grid=(B, nrow, N_PROJ // tn),
        in_specs=[pl.BlockSpec((1, tm, D), lambda b, i, j: (b, i, 0)),
                  pl.BlockSpec((1, 6, D), lambda b, i, j: (b, 0, 0)),
                  pl.BlockSpec((1, D), lambda b, i, j: (0, 0)),
                  pl.BlockSpec((D, tn), lambda b, i, j: (0, j))],
        out_specs=pl.BlockSpec((tm, tn), lambda b, i, j: (b * nrow + i, j)),
        out_shape=jax.ShapeDtypeStruct((B * S, N_PROJ), F32),
        scratch_shapes=[pltpu.VMEM((tm, D), BF16)],
        compiler_params=_cparams(("arbitrary", "arbitrary", "arbitrary")),
        name="inproj",
    )(x, mod3, g_pre.reshape(1, D), w_all)


def _rope(v, cosf, sinf, lane):
    rot = jnp.where(lane < ROPE_DIM // 2, pltpu.roll(v, LANES - ROPE_DIM // 2, 1), pltpu.roll(v, ROPE_DIM // 2, 1))
    return v * cosf + rot * sinf


def _mobaprep_kernel(q_ref, k_ref, v_ref, cos_ref, sin_ref, qo_ref, ko_ref, vo_ref, km_ref):
    cosf = cos_ref[...]
    sinf = sin_ref[...]
    lane = lax.broadcasted_iota(jnp.int32, cosf.shape, 1)
    for h in range(MB_HEADS):
        sl = slice(h * HEAD_DIM, (h + 1) * HEAD_DIM)
        qo_ref[:, sl] = (_rope(q_ref[:, sl], cosf, sinf, lane) * MB_Q_SCALE).astype(BF16)
        kr = _rope(k_ref[:, sl], cosf, sinf, lane)
        ko_ref[:, sl] = kr.astype(BF16)
        for blk in range(km_ref.shape[0]):
            km_ref[blk, :, sl] = jnp.mean(kr[blk * MB_BLOCK:(blk + 1) * MB_BLOCK], axis=0, keepdims=True)
    vo_ref[...] = v_ref[...].astype(BF16)


def _rope_tables(S):
    half = ROPE_DIM // 2
    inv_freq = ROPE_THETA ** (-jnp.arange(half, dtype=F32) / half)
    ang = jnp.arange(S, dtype=F32)[:, None] * inv_freq[None, :]
    cos, sin = jnp.cos(ang), jnp.sin(ang)
    rest = HEAD_DIM - ROPE_DIM
    cosf = jnp.concatenate([cos, cos, jnp.ones((S, rest), F32)], axis=1)
    sinf = jnp.concatenate([-sin, sin, jnp.zeros((S, rest), F32)], axis=1)
    return cosf, sinf


def _mobaprep(proj, B, S):
    T = B * S
    tb = min(4 * MB_BLOCK, S)
    nb = S // tb
    W = MB_HEADS * HEAD_DIM
    cosf, sinf = _rope_tables(S)
    col = lambda c: (lambda i: (i, c))
    return pl.pallas_call(
        _mobaprep_kernel,
        grid=(T // tb,),
        in_specs=[pl.BlockSpec((tb, W), col(COL_MQ // _W)),
                  pl.BlockSpec((tb, W), col(COL_MK // _W)),
                  pl.BlockSpec((tb, W), col(COL_MV // _W)),
                  pl.BlockSpec((tb, LANES), lambda i: (i % nb, 0)),
                  pl.BlockSpec((tb, LANES), lambda i: (i % nb, 0))],
        out_specs=[pl.BlockSpec((tb, W), lambda i: (i, 0)),
                   pl.BlockSpec((tb, W), lambda i: (i, 0)),
                   pl.BlockSpec((tb, W), lambda i: (i, 0)),
                   pl.BlockSpec((tb // MB_BLOCK, 1, W), lambda i: (i, 0, 0))],
        out_shape=[jax.ShapeDtypeStruct((T, W), BF16),
                   jax.ShapeDtypeStruct((T, W), BF16),
                   jax.ShapeDtypeStruct((T, W), BF16),
                   jax.ShapeDtypeStruct((T // MB_BLOCK, 1, W), F32)],
        compiler_params=_cparams(("arbitrary",)),
        name="mobaprep",
    )(proj, proj, proj, cosf, sinf)


MB_HPS = 4


def _moba_kernel(q_ref, k_ref, v_ref, km_ref, o_ref, s_scr, sd_scr, mx_scr, l_scr, acc_scr):
    qi = pl.program_id(1)
    tb = MB_BLOCK
    nb = km_ref.shape[0]
    nt = tb // LANES
    D = HEAD_DIM
    heads = range(MB_HPS)
    hs = [slice(h * D, (h + 1) * D) for h in heads]
    qs = [q_ref[:, hs[h]] for h in heads]

    sts = [lax.dot_general(km_ref[:, hs[h]], qs[h].astype(F32), (((1,), (1,)), ((), ())),
                           preferred_element_type=F32, precision=HIGHEST) for h in heads]
    blk = lax.broadcasted_iota(jnp.int32, (nb, tb), 0)
    rowid = lax.broadcasted_iota(jnp.int32, (LANES, tb), 0)
    sel_ts = []
    for h in heads:
        st = jnp.where(blk < qi, sts[h], -jnp.inf)
        sel_t = jnp.full((LANES, tb), -1.0, F32)
        for r in range(MB_TOPK):
            m = jnp.max(st, axis=0, keepdims=True)
            idx = jnp.min(jnp.where(st == m, blk, nb), axis=0, keepdims=True)
            sel_t = jnp.where(rowid == r, jnp.where(r < qi, idx, -1).astype(F32), sel_t)
            st = jnp.where(blk == idx, -jnp.inf, st)
        sel_ts.append(sel_t)
    sels = [jnp.transpose(t) for t in sel_ts]
    sel_rep = [[jnp.broadcast_to(sels[h][:, r:r + 1], (tb, LANES)) for r in range(MB_TOPK)] for h in heads]

    def logits(h, start, width):
        kslab = k_ref[pl.ds(pl.multiple_of(start, tb), width), hs[h]]
        return lax.dot_general(qs[h], kslab, (((1,), (1,)), ((), ())), preferred_element_type=F32)

    r_i = lax.broadcasted_iota(jnp.int32, (tb, LANES), 0)
    c_i = lax.broadcasted_iota(jnp.int32, (tb, LANES), 1)
    sds = [logits(h, qi * tb, tb) for h in heads]
    for h in heads:
        mx = jnp.full((tb, LANES), NEG_BIG, F32)
        for t in range(nt):
            piece = jnp.where(c_i + t * LANES <= r_i, sds[h][:, t * LANES:(t + 1) * LANES], NEG_BIG)
            sd_scr[h, :, t * LANES:(t + 1) * LANES] = piece
            mx = jnp.maximum(mx, piece)
        mx_scr[h] = mx

    n_pairs = (qi + 1) // 2

    def pass1(i, carry):
        kb0 = 2 * i
        s2s = [logits(h, kb0 * tb, 2 * tb) for h in heads]
        for h in heads:
            mx = mx_scr[h]
            for half in range(2):
                kbf = (kb0 + half).astype(F32)
                hit = (sel_rep[h][0] == kbf) | (sel_rep[h][1] == kbf) | (sel_rep[h][2] == kbf)
                for t in range(nt):
                    c0 = (half * nt + t) * LANES
                    piece = jnp.where(hit, s2s[h][:, c0:c0 + LANES], NEG_BIG)
                    s_scr[h, i, :, c0:c0 + LANES] = piece
                    mx = jnp.maximum(mx, piece)
            mx_scr[h] = mx
        return carry

    lax.fori_loop(0, n_pairs, pass1, 0)
    m_reps = [jnp.broadcast_to(jnp.max(mx_scr[h], axis=-1, keepdims=True), (tb, LANES)) for h in heads]

    def probs(load, width, m_rep):
        ps, lsum = [], jnp.zeros((tb, LANES), F32)
        for t in range(width // LANES):
            p = jnp.exp2(load(t) - m_rep)
            lsum = lsum + p
            ps.append(p.astype(BF16))
        return jnp.concatenate(ps, axis=1), lsum

    pls = [probs(lambda t, h=h: sd_scr[h, :, t * LANES:(t + 1) * LANES], tb, m_reps[h]) for h in heads]
    for h in heads:
        l_scr[h] = pls[h][1]
        acc_scr[h] = jnp.dot(pls[h][0], v_ref[pl.ds(pl.multiple_of(qi * tb, tb), tb), hs[h]],
                             preferred_element_type=F32)

    def pass2(i, carry):
        pls = [probs(lambda t, h=h: s_scr[h, i, :, t * LANES:(t + 1) * LANES], 2 * tb, m_reps[h]) for h in heads]
        for h in heads:
            vslab = v_ref[pl.ds(pl.multiple_of(2 * i * tb, tb), 2 * tb), hs[h]]
            l_scr[h] = l_scr[h] + pls[h][1]
            acc_scr[h] = acc_scr[h] + jnp.dot(pls[h][0], vslab, preferred_element_type=F32)
        return carry

    lax.fori_loop(0, n_pairs, pass2, 0)
    for h in heads:
        o_ref[:, hs[h]] = (acc_scr[h] / jnp.sum(l_scr[h], axis=-1, keepdims=True)).astype(o_ref.dtype)


def _moba(qr, kr, vb, kmean, B, S):
    tb = MB_BLOCK
    nb = S // tb
    T = B * S
    km = kmean.reshape(B, nb, MB_HEADS * HEAD_DIM)
    HW = MB_HPS * HEAD_DIM
    ng = MB_HEADS // MB_HPS
    return pl.pallas_call(
        _moba_kernel,
        grid=(B * ng, nb),
        in_specs=[pl.BlockSpec((tb, HW), lambda g, i: ((g // ng) * nb + i, g % ng)),
                  pl.BlockSpec((S, HW), lambda g, i: (g // ng, g % ng)),
                  pl.BlockSpec((S, HW), lambda g, i: (g // ng, g % ng)),
                  pl.BlockSpec((None, nb, HW), lambda g, i: (g // ng, 0, g % ng))],
        out_specs=pl.BlockSpec((tb, HW), lambda g, i: ((g // ng) * nb + i, g % ng)),
        out_shape=jax.ShapeDtypeStruct((T, MB_HEADS * HEAD_DIM), BF16),
        scratch_shapes=[pltpu.VMEM((MB_HPS, (nb + 1) // 2, tb, 2 * tb), F32),
                        pltpu.VMEM((MB_HPS, tb, tb), F32),
                        pltpu.VMEM((MB_HPS, tb, LANES), F32),
                        pltpu.VMEM((MB_HPS, tb, LANES), F32),
                        pltpu.VMEM((MB_HPS, tb, HEAD_DIM), F32)],
        compiler_params=_cparams(("arbitrary", "arbitrary")),
        name="moba",
    )(qr, kr, vb, km)


DN_TILE_CHUNKS = 8


def _softplus(v):
    return jnp.maximum(v, 0.0) + jnp.log1p(jnp.exp(-jnp.abs(v)))


DN_GROUP = 2
DN_HPS = 4


def _split(a):
    hi = pltpu.bitcast(pltpu.bitcast(a, jnp.uint32) & jnp.uint32(0xFFFF0000), F32)
    return hi.astype(BF16), (a - hi).astype(BF16)


def _dot3(ah, al, bh, bl):
    lhs = jnp.concatenate([ah, ah, al], axis=1)
    rhs = jnp.concatenate([bh, bl, bh], axis=0)
    return jnp.dot(lhs, rhs, preferred_element_type=F32)


def _dot_bf(a, b):
    return jnp.dot(a.astype(BF16), b.astype(BF16), preferred_element_type=F32)


def _deltanet_kernel(q_ref, k_ref, v_ref, z_ref, sm_ref, wq_ref, wk_ref, wv_ref, alog_ref, dtb_ref, gout_ref,
                     o_ref, xp_scr, state_scr):
    i = pl.program_id(2)
    C = DN_CHUNK
    TR = q_ref.shape[0]
    HALO = 8

    @pl.when(i == 0)
    def _():
        xp_scr[...] = jnp.zeros(xp_scr.shape, F32)
        state_scr[...] = jnp.zeros(state_scr.shape, F32)

    def conv_silu(slot, x_ref, w_ref):
        xp_scr[slot, 0:HALO, :] = xp_scr[slot, TR:TR + HALO, :]
        xp_scr[slot, HALO:HALO + TR, :] = x_ref[...]
        acc = w_ref[DN_CONV - 1:DN_CONV, :] * xp_scr[slot, HALO:HALO + TR, :]
        for j in range(1, DN_CONV):
            acc = acc + w_ref[DN_CONV - 1 - j:DN_CONV - j, :] * xp_scr[slot, HALO - j:HALO - j + TR, :]
        return _silu(acc)

    q_all = conv_silu(0, q_ref, wq_ref)
    k_all = conv_silu(1, k_ref, wk_ref)
    v_all = conv_silu(2, v_ref, wv_ref)

    sm = sm_ref[...]
    lane = lax.broadcasted_iota(jnp.int32, sm.shape, 1)
    row = lax.broadcasted_iota(jnp.int32, sm.shape, 0)
    beta_all = _sigmoid(sm)
    g_all = -jnp.exp(alog_ref[...]) * _softplus(sm + dtb_ref[...])
    pos = row % C
    gc_all = g_all
    shift = 1
    while shift < C:
        gc_all = gc_all + jnp.where(pos >= shift, pltpu.roll(gc_all, shift, 0), 0.0)
        shift *= 2
    gc_t = jnp.transpose(gc_all)
    row_t = lax.broadcasted_iota(jnp.int32, gc_t.shape, 0)

    G = DN_GROUP * C
    ri = lax.broadcasted_iota(jnp.int32, (G, G), 0)
    ci = lax.broadcasted_iota(jnp.int32, (G, G), 1)
    same = (ri // C) == (ci // C)
    tril = same & (ci <= ri)
    strict = same & (ci < ri)
    eye = (ci == ri).astype(F32)
    D = HEAD_DIM
    gout = gout_ref[...]

    heads = []
    for hh in range(DN_HPS):
        h = pl.program_id(1) * DN_HPS + hh
        sl = slice(hh * D, (hh + 1) * D)
        q, k, v = q_all[:, sl], k_all[:, sl], v_all[:, sl]
        q = q * lax.rsqrt(jnp.sum(q * q, axis=-1, keepdims=True) + 1e-6) * (D ** -0.5)
        k = k * lax.rsqrt(jnp.sum(k * k, axis=-1, keepdims=True) + 1e-6)
        beta = jnp.sum(jnp.where(lane == h, beta_all, 0.0), axis=1, keepdims=True)
        gc = jnp.sum(jnp.where(lane == DN_HEADS + h, gc_all, 0.0), axis=1, keepdims=True)
        gc_row = jnp.sum(jnp.where(row_t == DN_HEADS + h, gc_t, 0.0), axis=0, keepdims=True)
        heads.append((q, k, v, beta, gc, gc_row, jnp.transpose(k)))

    steps = [[None] * (TR // C) for _ in range(DN_HPS)]
    preps = []
    for hh in range(DN_HPS):
        for g in range(TR // G):
            preps.append(_deltanet_group(heads[hh], g, tril, strict, eye, steps[hh]))
    _round_robin(preps)

    outs = [[] for _ in range(DN_HPS)]
    _round_robin([_deltanet_chain(hh, steps[hh], state_scr, outs[hh]) for hh in range(DN_HPS)])
    for hh in range(DN_HPS):
        sl = slice(hh * D, (hh + 1) * D)
        o = jnp.concatenate(outs[hh], axis=0)
        y = o * lax.rsqrt(jnp.mean(o * o, axis=-1, keepdims=True) + NORM_EPS) * gout
        o_ref[:, sl] = (y * _silu(z_ref[:, sl])).astype(o_ref.dtype)


def _round_robin(gens):
    active = list(gens)
    while active:
        still = []
        for gen in active:
            try:
                next(gen)
                still.append(gen)
            except StopIteration:
                pass
        active = still


def _deltanet_group(head, g, tril, strict, eye, steps_out):
    C = DN_CHUNK
    D = HEAD_DIM
    G = DN_GROUP * C
    q, k, v, beta, gc, gc_row, k_t = head
    r0 = g * G
    qg, kg, vg = q[r0:r0 + G], k[r0:r0 + G], v[r0:r0 + G]
    bg = beta[r0:r0 + G]
    gcg = gc[r0:r0 + G]
    gcr = gc_row[:, r0:r0 + G]
    ktg = k_t[:, r0:r0 + G]
    decay = jnp.where(tril, jnp.exp(jnp.where(tril, gcg - gcr, 0.0)), 0.0)
    kb = kg * bg
    aq = _dot_bf(jnp.concatenate([kb, qg], axis=0), ktg)
    yield
    m_neg = jnp.where(strict, -(aq[:G] * decay), 0.0)
    qk = (aq[G:] * decay).astype(BF16)
    t_inv = eye + m_neg
    ph, pl_ = _split(m_neg)
    for _ in range(5):
        th, tl = _split(t_inv)
        ph, pl_ = _split(_dot3(ph, pl_, ph, pl_))
        yield
        t_inv = t_inv + _dot3(th, tl, ph, pl_)
        yield
    egc = jnp.exp(gcg)
    th, tl = _split(t_inv)
    rh, rl = _split(jnp.concatenate([kb * egc, vg * bg], axis=1))
    wu = _dot3(th, tl, rh, rl).astype(BF16)
    yield
    qr = jnp.dot(qk, wu, preferred_element_type=F32)
    qp = qg * egc - qr[:, :D]
    r_all = qr[:, D:]
    yield
    for c in range(DN_GROUP):
        c0 = c * C
        g_last = gcg[c0 + C - 1:c0 + C, :]
        kt_tail = ktg[:, c0:c0 + C] * jnp.exp(g_last - gcr[:, c0:c0 + C])
        gh = jnp.dot(kt_tail.astype(BF16), wu[c0:c0 + C, :], preferred_element_type=F32)
        lhs = jnp.concatenate([gh[:, :D], qp[c0:c0 + C]], axis=0).astype(BF16)
        steps_out[g * DN_GROUP + c] = (lhs, gh[:, D:], r_all[c0:c0 + C], jnp.exp(g_last))
        yield


def _deltanet_chain(hh, steps, state_scr, outs):
    D = HEAD_DIM
    state = state_scr[hh]
    for lhs, h_add, r_add, dec in steps:
        res = jnp.dot(lhs, state.astype(BF16), preferred_element_type=F32)
        outs.append(res[D:] + r_add)
        state = state * dec - res[:D] + h_add
        yield
    state_scr[hh] = state


def _deltanet(proj, conv_w, a_log, dt_bias, g_dn_out, B, S):
    T = B * S
    TR = min(DN_TILE_CHUNKS * DN_CHUNK, S)
    nt = S // TR
    pad = jnp.zeros((DN_HEADS,), F32)
    rest = jnp.zeros((LANES - 2 * DN_HEADS,), F32)
    alog_lane = jnp.concatenate([pad, a_log.astype(F32), rest]).reshape(1, LANES)
    dtb_lane = jnp.concatenate([pad, dt_bias.astype(F32), rest]).reshape(1, LANES)
    HW = DN_HPS * HEAD_DIM
    per = HW // LANES
    rows = lambda c0: (lambda b, h, i: (b * nt + i, c0 // per + h))
    wcol = lambda c0: (lambda b, h, i: (0, c0 // per + h))
    const = lambda b, h, i: (0, 0)
    return pl.pallas_call(
        _deltanet_kernel,
        grid=(B, DN_HEADS // DN_HPS, nt),
        in_specs=[pl.BlockSpec((TR, HW), rows(COL_DQ)),
                  pl.BlockSpec((TR, HW), rows(COL_DK)),
                  pl.BlockSpec((TR, HW), rows(COL_DV)),
                  pl.BlockSpec((TR, HW), rows(COL_DZ)),
                  pl.BlockSpec((TR, LANES), lambda b, h, i: (b * nt + i, COL_SMALL)),
                  pl.BlockSpec((DN_CONV, HW), wcol(0)),
                  pl.BlockSpec((DN_CONV, HW), wcol(DN_HEADS)),
                  pl.BlockSpec((DN_CONV, HW), wcol(2 * DN_HEADS)),
                  pl.BlockSpec((1, LANES), const),
                  pl.BlockSpec((1, LANES), const),
                  pl.BlockSpec((1, HEAD_DIM), const)],
        out_specs=pl.BlockSpec((TR, HW), lambda b, h, i: (b * nt + i, h)),
        out_shape=jax.ShapeDtypeStruct((T, DN_HEADS * HEAD_DIM), BF16),
        scratch_shapes=[pltpu.VMEM((3, TR + 8, HW), F32), pltpu.VMEM((DN_HPS, HEAD_DIM, HEAD_DIM), F32)],
        compiler_params=_cparams(("arbitrary", "arbitrary", "arbitrary")),
        name="deltanet",
    )(proj, proj, proj, proj, proj, conv_w, conv_w, conv_w, alog_lane, dtb_lane, g_dn_out.reshape(1, HEAD_DIM))


def _rms(v):
    return v * lax.rsqrt(jnp.mean(v * v, axis=-1, keepdims=True) + NORM_EPS)


def _merge_kernel(oa_ref, ob_ref, ga_ref, gb_ref, x_ref, mod_ref, gpost_ref, gpre_ref, wa_ref, wb_ref, wo_ref,
                  wrt_ref, br_ref, x1_ref, hp_ref, idx_ref, wrow_ref, rank_ref, cnt_ref, carry_scr):
    E = N_EXPERTS
    tm = x_ref.shape[0]
    half = x_ref.shape[1] // 2

    @pl.when(pl.program_id(0) == 0)
    def _():
        carry_scr[...] = jnp.zeros(carry_scr.shape, F32)

    ya = jnp.dot(oa_ref[...], wa_ref[...], preferred_element_type=F32)
    yb = jnp.dot(ob_ref[...], wb_ref[...], preferred_element_type=F32)
    merged = _sigmoid(ga_ref[...]) * ya + _sigmoid(gb_ref[...]) * yb
    mix = jnp.dot(merged.astype(BF16), wo_ref[...], preferred_element_type=F32)
    x1 = x_ref[...] + mod_ref[0, 2:3, :] * (_rms(mix) * gpost_ref[...])
    x1_ref[...] = x1
    h2 = (_rms(x1) * gpre_ref[...]) * (1.0 + mod_ref[0, 4:5, :]) + mod_ref[0, 3:4, :]

    lo_bits = pltpu.bitcast(h2[:, :half].astype(BF16).astype(F32), jnp.uint32) >> 16
    hi_bits = pltpu.bitcast(h2[:, half:].astype(BF16).astype(F32), jnp.uint32) & jnp.uint32(0xFFFF0000)
    hp_ref[...] = hi_bits | lo_bits

    lt = lax.dot_general(wrt_ref[...], h2, (((1,), (1,)), ((), ())), preferred_element_type=F32,
                         precision=HIGHEST) + br_ref[...]
    eid = lax.broadcasted_iota(jnp.int32, (E, tm), 0)
    vals, idxs = [], []
    for _ in range(TOP_K):
        m = jnp.max(lt, axis=0, keepdims=True)
        idx = jnp.min(jnp.where(lt == m, eid, E), axis=0, keepdims=True)
        vals.append(m)
        idxs.append(idx)
        lt = jnp.where(eid == idx, -jnp.inf, lt)
    exps = [jnp.exp(v - vals[0]) for v in vals]
    den = exps[0] + exps[1] + exps[2] + exps[3]
    wts = [e / den for e in exps]

    hot = jnp.zeros((E, tm), F32)
    for idx in idxs:
        hot = hot + (eid == idx).astype(F32)
    ti = lax.broadcasted_iota(jnp.int32, (tm, tm), 0)
    tj = lax.broadcasted_iota(jnp.int32, (tm, tm), 1)
    before = (ti < tj).astype(BF16)
    prior = carry_scr[...][:, 0:1] + jnp.dot(hot.astype(BF16), before, preferred_element_type=F32)
    row8 = lax.broadcasted_iota(jnp.int32, (8, tm), 0)
    row128 = lax.broadcasted_iota(jnp.int32, (LANES, tm), 0)
    idx8 = jnp.zeros((8, tm), jnp.int32)
    rank8 = jnp.zeros((8, tm), jnp.int32)
    w128 = jnp.zeros((LANES, tm), F32)
    for r in range(TOP_K):
        rank_r = jnp.sum(jnp.where(eid == idxs[r], prior, 0.0), axis=0, keepdims=True)
        idx8 = jnp.where(row8 == r, idxs[r], idx8)
        rank8 = jnp.where(row8 == r, rank_r.astype(jnp.int32), rank8)
        w128 = jnp.where(row128 == r, wts[r], w128)
    idx_ref[...] = idx8
    rank_ref[...] = rank8
    wrow_ref[...] = jnp.transpose(w128)
    carry = carry_scr[...] + jnp.sum(hot, axis=1, keepdims=True)
    carry_scr[...] = carry
    cnt_ref[...] = carry


def _merge(oa, ob, proj, x2, mod3, g_post_mix, g_pre_ffn, w_br_a, w_br_b, w_o, w_router, b_router, S):
    T, D = x2.shape
    E = N_EXPERTS
    tm = min(512, S)
    per_b = S // tm
    W = DN_HEADS * HEAD_DIM
    row = lambda i: (i, 0)
    const = lambda i: (0, 0)
    lane_t = lambda i: (0, i)
    return pl.pallas_call(
        _merge_kernel,
        grid=(T // tm,),
        in_specs=[pl.BlockSpec((tm, W), row),
                  pl.BlockSpec((tm, W), row),
                  pl.BlockSpec((tm, D), lambda i: (i, COL_GA * LANES // D)),
                  pl.BlockSpec((tm, D), lambda i: (i, COL_GB * LANES // D)),
                  pl.BlockSpec((tm, D), row),
                  pl.BlockSpec((1, 6, D), lambda i: (i // per_b, 0, 0)),
                  pl.BlockSpec((1, D), const),
                  pl.BlockSpec((1, D), const),
                  pl.BlockSpec((W, D), const),
                  pl.BlockSpec((W, D), const),
                  pl.BlockSpec((D, D), const),
                  pl.BlockSpec((E, D), const),
                  pl.BlockSpec((E, 1), const)],
        out_specs=[pl.BlockSpec((tm, D), row),
                   pl.BlockSpec((tm, D // 2), row),
                   pl.BlockSpec((8, tm), lane_t),
                   pl.BlockSpec((tm, LANES), row),
                   pl.BlockSpec((8, tm), lane_t),
                   pl.BlockSpec((E, LANES), const)],
        out_shape=[jax.ShapeDtypeStruct((T, D), F32),
                   jax.ShapeDtypeStruct((T, D // 2), jnp.uint32),
                   jax.ShapeDtypeStruct((8, T), jnp.int32),
                   jax.ShapeDtypeStruct((T, LANES), F32),
                   jax.ShapeDtypeStruct((8, T), jnp.int32),
                   jax.ShapeDtypeStruct((E, LANES), F32)],
        scratch_shapes=[pltpu.VMEM((E, LANES), F32)],
        compiler_params=_cparams(("arbitrary",)),
        name="merge",
    )(oa, ob, proj, proj, x2, mod3, g_post_mix.reshape(1, D), g_pre_ffn.reshape(1, D),
      w_br_a.astype(BF16), w_br_b.astype(BF16), w_o.astype(BF16),
      jnp.transpose(w_router).astype(F32), b_router.reshape(E, 1).astype(F32))


DISPATCH_TOKENS = 256


def _dispatch_kernel(dest_ref, src_ref, dst_ref, sem):
    n = dest_ref.shape[1]

    def issue(j, carry):
        for r in range(TOP_K):
            pltpu.make_async_copy(src_ref.at[pl.ds(j, 1)], dst_ref.at[pl.ds(dest_ref[r, j], 1)], sem).start()
        return carry

    lax.fori_loop(0, n, issue, 0, unroll=8)
    for r in range(TOP_K):
        pltpu.make_async_copy(src_ref, dst_ref.at[pl.ds(0, n)], sem).wait()


def _dispatch(dest, hp):
    T, Wd = hp.shape
    n = min(DISPATCH_TOKENS, T)
    return pl.pallas_call(
        _dispatch_kernel,
        grid=(T // n,),
        in_specs=[pl.BlockSpec((TOP_K, n), lambda i: (0, i), memory_space=pltpu.SMEM),
                  pl.BlockSpec((n, Wd), lambda i: (i, 0))],
        out_specs=pl.BlockSpec(memory_space=pl.ANY),
        out_shape=jax.ShapeDtypeStruct((T * TOP_K, Wd), hp.dtype),
        scratch_shapes=[pltpu.SemaphoreType.DMA(())],
        compiler_params=_cparams(("arbitrary",)),
        name="dispatch",
    )(dest, hp)


EXPERT_ROWS = 512
EXPERT_FEATURE_TILE = 512


def _experts_kernel(blk_ref, exp_ref, lo_ref, hi_ref, new_ref, x_ref, wgu_ref, bgu_ref, wdn_ref, bdn_ref, o_ref,
                    wgu_bf, wdn_bf):
    k = pl.program_id(0)
    lo = lo_ref[k]
    hi = hi_ref[k]
    tm = x_ref.shape[0]
    F = wdn_ref.shape[0]

    @pl.when(new_ref[k] == 1)
    def _():
        wgu_bf[...] = wgu_ref[...].astype(BF16)
        wdn_bf[...] = wdn_ref[...].astype(BF16)

    def ffn():
        word = x_ref[...]
        x = jnp.concatenate([pltpu.bitcast(word << 16, F32).astype(BF16),
                             pltpu.bitcast(word & jnp.uint32(0xFFFF0000), F32).astype(BF16)], axis=1)

        def gate_up(c):
            ft = slice(c, c + EXPERT_FEATURE_TILE)
            ut = slice(F + c, F + c + EXPERT_FEATURE_TILE)
            return (jnp.dot(x, wgu_bf[:, ft], preferred_element_type=F32) + bgu_ref[:, ft],
                    jnp.dot(x, wgu_bf[:, ut], preferred_element_type=F32) + bgu_ref[:, ut])

        tiles = list(range(0, F, EXPERT_FEATURE_TILE))
        y = None
        nxt = gate_up(tiles[0])
        for t, c in enumerate(tiles):
            gate, up = nxt
            if t + 1 < len(tiles):
                nxt = gate_up(tiles[t + 1])
            gate = jnp.minimum(gate, SWIGLU_LIMIT)
            up = jnp.clip(up, -SWIGLU_LIMIT, SWIGLU_LIMIT)
            act = (up + 1.0) * gate * _sigmoid(SWIGLU_ALPHA * gate)
            part = jnp.dot(act.astype(BF16), wdn_bf[c:c + EXPERT_FEATURE_TILE, :], preferred_element_type=F32)
            y = part + bdn_ref[...] if y is None else y + part
        return y

    full = (lo == 0) & (hi == tm)

    @pl.when(full)
    def _():
        o_ref[:, 0, :] = ffn()

    @pl.when(jnp.logical_not(full) & (hi > lo))
    def _():
        y = ffn()
        rows = lax.broadcasted_iota(jnp.int32, (tm, 1), 0)
        keep = (rows >= lo) & (rows < hi)

        @pl.when(lo == 0)
        def _():
            o_ref[:, 0, :] = jnp.where(keep, y, 0.0)

        @pl.when(lo > 0)
        def _():
            o_ref[:, 0, :] = jnp.where(keep, y, o_ref[:, 0, :])


def _experts(xs, item_blk, item_exp, item_lo, item_hi, item_new, w_gu, b_gu, w_down, b_down):
    A, half = xs.shape
    E, D, F2 = w_gu.shape
    F = F2 // 2
    tm = EXPERT_ROWS
    n_items = item_blk.shape[0]
    grid_spec = pltpu.PrefetchScalarGridSpec(
        num_scalar_prefetch=5,
        grid=(n_items,),
        in_specs=[pl.BlockSpec((tm, half), lambda k, blk, ex, lo, hi, nw: (blk[k], 0)),
                  pl.BlockSpec((None, D, F2), lambda k, blk, ex, lo, hi, nw: (ex[k], 0, 0)),
                  pl.BlockSpec((None, 1, F2), lambda k, blk, ex, lo, hi, nw: (ex[k], 0, 0)),
                  pl.BlockSpec((None, F, D), lambda k, blk, ex, lo, hi, nw: (ex[k], 0, 0)),
                  pl.BlockSpec((None, 1, D), lambda k, blk, ex, lo, hi, nw: (ex[k], 0, 0))],
        out_specs=pl.BlockSpec((tm, 1, D), lambda k, blk, ex, lo, hi, nw: (blk[k], 0, 0)),
        scratch_shapes=[pltpu.VMEM((D, F2), BF16), pltpu.VMEM((F, D), BF16)],
    )
    return pl.pallas_call(
        _experts_kernel,
        grid_spec=grid_spec,
        out_shape=jax.ShapeDtypeStruct((A, 1, D), F32),
        compiler_params=_cparams(("arbitrary",)),
        name="experts",
    )(item_blk, item_exp, item_lo, item_hi, item_new, xs, w_gu, b_gu.reshape(E, 1, F2).astype(F32),
      w_down, b_down.reshape(E, 1, D).astype(F32))


def _work_items(counts, A):
    E = N_EXPERTS
    tm = EXPERT_ROWS
    n_items = A // tm + E - 1
    end = jnp.cumsum(counts)
    start = end - counts
    first_blk = start // tm
    last_blk = jnp.maximum(end - 1, 0) // tm
    n_e = jnp.where(counts > 0, last_blk - first_blk + 1, 0)
    off_end = jnp.cumsum(n_e)
    off = off_end - n_e
    total = off_end[-1]
    k = jnp.arange(n_items, dtype=jnp.int32)
    kk = jnp.minimum(k, total - 1)
    e = jnp.sum((off_end[None, :] <= kk[:, None]).astype(jnp.int32), axis=1)
    hot = e[:, None] == jnp.arange(E, dtype=jnp.int32)[None, :]
    pick = lambda table: jnp.sum(jnp.where(hot, table[None, :], 0), axis=1)
    blk = (pick(first_blk) + (kk - pick(off))).astype(jnp.int32)
    lo = jnp.maximum(pick(start), blk * tm) - blk * tm
    hi = jnp.minimum(pick(end), (blk + 1) * tm) - blk * tm
    valid = k < total
    lo = jnp.where(valid, lo, 0).astype(jnp.int32)
    hi = jnp.where(valid, hi, 0).astype(jnp.int32)
    new = jnp.concatenate([jnp.ones((1,), jnp.int32), (e[1:] != e[:-1]).astype(jnp.int32)])
    return blk, e, lo, hi, new


COMBINE_TOKENS = 256


def _combine_kernel(dest_ref, nxt_ref, y_ref, wrow_ref, x1_ref, mod_ref, gpost_ref, o_ref, stage, moe_scr, sem):
    i = pl.program_id(0)
    last = pl.num_programs(0) - 1
    n = x1_ref.shape[0]
    SUB = 8

    def gather(idx_ref, j, s):
        for r in range(TOP_K):
            pltpu.make_async_copy(y_ref.at[idx_ref[r, j]], stage.at[s, r, j], sem.at[s]).start()

    @pl.when(i == 0)
    def _():
        def prime(j, carry):
            gather(dest_ref, j, 0)
            return carry
        lax.fori_loop(0, n, prime, 0, unroll=SUB)

    def step(slot):
        for r in range(TOP_K):
            pltpu.make_async_copy(y_ref.at[pl.ds(0, n)], stage.at[slot, r], sem.at[slot]).wait()

        def reduce_group(g):
            rows = pl.ds(pl.multiple_of(g * SUB, SUB), SUB)
            w = wrow_ref[rows, :]
            acc = w[:, 0:1] * stage[slot, 0, rows, 0, :]
            for r in range(1, TOP_K):
                acc = acc + w[:, r:r + 1] * stage[slot, r, rows, 0, :]
            moe_scr[rows, :] = acc

        @pl.when(i < last)
        def _():
            def body(g, carry):
                for t in range(SUB):
                    gather(nxt_ref, g * SUB + t, 1 - slot)
                reduce_group(g)
                return carry
            lax.fori_loop(0, n // SUB, body, 0)

        @pl.when(i == last)
        def _():
            def body(g, carry):
                reduce_group(g)
                return carry
            lax.fori_loop(0, n // SUB, body, 0)

    for parity in range(2):
        pl.when(i % 2 == parity)(functools.partial(step, parity))

    o_ref[...] = x1_ref[...] + mod_ref[0, 5:6, :] * (_rms(moe_scr[...]) * gpost_ref[...])


def _combine(dest, y, wrow, x1, mod3, g_post_ffn, S):
    T, D = x1.shape
    n = min(COMBINE_TOKENS, S)
    per_b = S // n
    steps = T // n
    return pl.pallas_call(
        _combine_kernel,
        grid=(steps,),
        in_specs=[pl.BlockSpec((TOP_K, n), lambda i: (0, i), memory_space=pltpu.SMEM),
                  pl.BlockSpec((TOP_K, n), lambda i: (0, jnp.minimum(i + 1, steps - 1)), memory_space=pltpu.SMEM),
                  pl.BlockSpec(memory_space=pl.ANY),
                  pl.BlockSpec((n, LANES), lambda i: (i, 0)),
                  pl.BlockSpec((n, D), lambda i: (i, 0)),
                  pl.BlockSpec((1, 6, D), lambda i: (i // per_b, 0, 0)),
                  pl.BlockSpec((1, D), lambda i: (0, 0))],
        out_specs=pl.BlockSpec((n, D), lambda i: (i, 0)),
        out_shape=jax.ShapeDtypeStruct((T, D), F32),
        scratch_shapes=[pltpu.VMEM((2, TOP_K, n, 1, D), F32), pltpu.VMEM((n, D), F32),
                        pltpu.SemaphoreType.DMA((2,))],
        compiler_params=_cparams(("arbitrary",)),
        name="combine",
    )(dest, dest, y, wrow, x1, mod3, g_post_ffn.reshape(1, D))


def _regroup_w_in(w_in):
    D = w_in.shape[0]
    dw = DN_HEADS * HEAD_DIM
    mw = MB_HEADS * HEAD_DIM
    cuts = np.cumsum([dw, dw, dw, dw, DN_HEADS, DN_HEADS, mw, mw, mw, D, D])[:-1]
    dq, dk, dv, dz, db, da, mq, mk, mv, ga, gb = jnp.split(w_in, [int(c) for c in cuts], axis=1)
    small = jnp.concatenate([db, da, jnp.zeros((D, LANES - 2 * DN_HEADS), w_in.dtype)], axis=1)
    return jnp.concatenate([p.astype(BF16) for p in (ga, gb, dq, dk, dv, dz, mq, mk, mv, small)], axis=1)


def kernel(x, c, w_ada, b_ada, g_pre_mix, g_post_mix, g_pre_ffn, g_post_ffn, w_in, conv_w, a_log, dt_bias,
           g_dn_out, w_br_a, w_br_b, w_o, w_router, b_router, w_gu, b_gu, w_down, b_down):
    B, S, D = x.shape
    l = 0
    mod = _adaln(c, w_ada[l], b_ada[l]).reshape(B, 6, D)
    proj = _inproj(x, mod, g_pre_mix[l], _regroup_w_in(w_in[l]))
    qr, kr, vb, kmean = _mobaprep(proj, B, S)
    ob = _moba(qr, kr, vb, kmean, B, S)
    oa = _deltanet(proj, conv_w[l], a_log[l], dt_bias[l], g_dn_out[l], B, S)
    x1, hp, idx8, wrow, rank8, cnt = _merge(oa, ob, proj, x.reshape(B * S, D), mod, g_post_mix[l], g_pre_ffn[l],
                                            w_br_a[l], w_br_b[l], w_o[l], w_router[l], b_router[l], S)
    out = _moe(x1, hp, idx8, wrow, rank8, cnt, mod, g_post_ffn[l], w_gu[l], b_gu[l], w_down[l], b_down[l], S)
    return out.reshape(B, S, D)


def _moe(x1, hp, idx8, wrow, rank8, cnt, mod, g_post_ffn, w_gu, b_gu, w_down, b_down, S):
    T = x1.shape[0]
    counts = cnt[:, 0].astype(jnp.int32)
    start = jnp.cumsum(counts) - counts
    hot = idx8[:TOP_K, :, None] == jnp.arange(N_EXPERTS, dtype=jnp.int32)
    dest = rank8[:TOP_K] + jnp.sum(jnp.where(hot, start, 0), axis=-1)
    xs = _dispatch(dest, hp)
    blk, e, lo, hi, new = _work_items(counts, T * TOP_K)
    y = _experts(xs, blk, e, lo, hi, new, w_gu, b_gu, w_down, b_down)
    return _combine(dest, y, wrow, x1, mod, g_post_ffn, S)
```

```python
import functools

import jax
import jax.numpy as jnp
import numpy as np
from jax import lax
from jax.experimental import pallas as pl
from jax.experimental.pallas import tpu as pltpu

F32 = jnp.float32
BF16 = jnp.bfloat16
HIGHEST = lax.Precision.HIGHEST

HEAD_DIM = 128
DN_HEADS = 4
DN_CONV = 4
DN_CHUNK = 64
MB_HEADS = 4
MB_BLOCK = 256
MB_TOPK = 3
ROPE_THETA = 500000.0
ROPE_DIM = HEAD_DIM // 4
N_EXPERTS = 32
TOP_K = 4
SWIGLU_LIMIT = 7.0
SWIGLU_ALPHA = 1.702
NORM_EPS = 1e-6
LANES = 128
NEG_BIG = -1e30
MB_Q_SCALE = HEAD_DIM ** -0.5 * float(np.log2(np.e))

_W = DN_HEADS * HEAD_DIM // LANES
COL_GA = 0
COL_GB = 8
COL_DQ = 16
COL_DK = 20
COL_DV = 24
COL_DZ = 28
COL_MQ = 32
COL_MK = 36
COL_MV = 40
COL_SMALL = 44
N_PROJ = 45 * LANES

VMEM_LIMIT = 56 * 1024 * 1024


def _cparams(sem):
    return pltpu.CompilerParams(dimension_semantics=sem, vmem_limit_bytes=VMEM_LIMIT)


def _sigmoid(v):
    return 0.5 * jnp.tanh(0.5 * v) + 0.5


def _silu(v):
    return v * _sigmoid(v)


def _adaln_kernel(c_ref, w_ref, b_ref, o_ref):
    a = _silu(c_ref[...])
    o_ref[...] = jnp.dot(a, w_ref[...], preferred_element_type=F32, precision=HIGHEST) + b_ref[...]


def _adaln(c, w_ada, b_ada):
    B, D = c.shape
    N = w_ada.shape[1]
    tn = D
    return pl.pallas_call(
        _adaln_kernel,
        grid=(N // tn,),
        in_specs=[pl.BlockSpec((B, D), lambda j: (0, 0)),
                  pl.BlockSpec((D, tn), lambda j: (0, j)),
                  pl.BlockSpec((1, tn), lambda j: (0, j))],
        out_specs=pl.BlockSpec((B, tn), lambda j: (0, j)),
        out_shape=jax.ShapeDtypeStruct((B, N), F32),
        compiler_params=_cparams(("arbitrary",)),
        name="adaln",
    )(c, w_ada, b_ada.reshape(1, N))


def _inproj_kernel(x_ref, mod_ref, g_ref, w_ref, o_ref, h_scr):
    @pl.when(pl.program_id(2) == 0)
    def _():
        x = x_ref[0]
        y = x * lax.rsqrt(jnp.mean(x * x, axis=-1, keepdims=True) + NORM_EPS) * g_ref[...]
        h = y * (1.0 + mod_ref[0, 1:2, :]) + mod_ref[0, 0:1, :]
        h_scr[...] = h.astype(BF16)

    o_ref[...] = jnp.dot(h_scr[...], w_ref[...], preferred_element_type=F32)


def _inproj(x, mod3, g_pre, w_all):
    B, S, D = x.shape
    tm = min(1024, S)
    tn = N_PROJ // 3
    nrow = S // tm
    return pl.pallas_call(
        _inproj_kernel,
        grid=(B, nrow, N_PROJ // tn),
        in_specs=[pl.BlockSpec((1, tm, D), lambda b, i, j: (b, i, 0)),
                  pl.BlockSpec((1, 6, D), lambda b, i, j: (b, 0, 0)),
                  pl.BlockSpec((1, D), lambda b, i, j: (0, 0)),
                  pl.BlockSpec((D, tn), lambda b, i, j: (0, j))],
        out_specs=pl.BlockSpec((tm, tn), lambda b, i, j: (b * nrow + i, j)),
        out_shape=jax.ShapeDtypeStruct((B * S, N_PROJ), F32),
        scratch_shapes=[pltpu.VMEM((tm, D), BF16)],
        compiler_params=_cparams(("arbitrary", "arbitrary", "arbitrary")),
        name="inproj",
    )(x, mod3, g_pre.reshape(1, D), w_all)


def _rope(v, cosf, sinf, lane):
    rot = jnp.where(lane < ROPE_DIM // 2, pltpu.roll(v, LANES - ROPE_DIM // 2, 1), pltpu.roll(v, ROPE_DIM // 2, 1))
    return v * cosf + rot * sinf


def _mobaprep_kernel(q_ref, k_ref, v_ref, cos_ref, sin_ref, qo_ref, ko_ref, vo_ref, km_ref):
    cosf = cos_ref[...]
    sinf = sin_ref[...]
    lane = lax.broadcasted_iota(jnp.int32, cosf.shape, 1)
    for h in range(MB_HEADS):
        sl = slice(h * HEAD_DIM, (h + 1) * HEAD_DIM)
        qo_ref[:, sl] = (_rope(q_ref[:, sl], cosf, sinf, lane) * MB_Q_SCALE).astype(BF16)
        kr = _rope(k_ref[:, sl], cosf, sinf, lane)
        ko_ref[:, sl] = kr.astype(BF16)
        for blk in range(km_ref.shape[0]):
            km_ref[blk, :, sl] = jnp.mean(kr[blk * MB_BLOCK:(blk + 1) * MB_BLOCK], axis=0, keepdims=True)
    vo_ref[...] = v_ref[...].astype(BF16)


def _rope_tables(S):
    half = ROPE_DIM // 2
    inv_freq = ROPE_THETA ** (-jnp.arange(half, dtype=F32) / half)
    ang = jnp.arange(S, dtype=F32)[:, None] * inv_freq[None, :]
    cos, sin = jnp.cos(ang), jnp.sin(ang)
    rest = HEAD_DIM - ROPE_DIM
    cosf = jnp.concatenate([cos, cos, jnp.ones((S, rest), F32)], axis=1)
    sinf = jnp.concatenate([-sin, sin, jnp.zeros((S, rest), F32)], axis=1)
    return cosf, sinf


def _mobaprep(proj, B, S):
    T = B * S
    tb = min(4 * MB_BLOCK, S)
    nb = S // tb
    W = MB_HEADS * HEAD_DIM
    cosf, sinf = _rope_tables(S)
    col = lambda c: (lambda i: (i, c))
    return pl.pallas_call(
        _mobaprep_kernel,
        grid=(T // tb,),
        in_specs=[pl.BlockSpec((tb, W), col(COL_MQ // _W)),
                  pl.BlockSpec((tb, W), col(COL_MK // _W)),
                  pl.BlockSpec((tb, W), col(COL_MV // _W)),
                  pl.BlockSpec((tb, LANES), lambda i: (i % nb, 0)),
                  pl.BlockSpec((tb, LANES), lambda i: (i % nb, 0))],
        out_specs=[pl.BlockSpec((tb, W), lambda i: (i, 0)),
                   pl.BlockSpec((tb, W), lambda i: (i, 0)),
                   pl.BlockSpec((tb, W), lambda i: (i, 0)),
                   pl.BlockSpec((tb // MB_BLOCK, 1, W), lambda i: (i, 0, 0))],
        out_shape=[jax.ShapeDtypeStruct((T, W), BF16),
                   jax.ShapeDtypeStruct((T, W), BF16),
                   jax.ShapeDtypeStruct((T, W), BF16),
                   jax.ShapeDtypeStruct((T // MB_BLOCK, 1, W), F32)],
        compiler_params=_cparams(("arbitrary",)),
        name="mobaprep",
    )(proj, proj, proj, cosf, sinf)


MB_HPS = 4


def _moba_kernel(q_ref, k_ref, v_ref, km_ref, o_ref, s_scr, sd_scr, mx_scr, l_scr, acc_scr):
    qi = pl.program_id(1)
    tb = MB_BLOCK
    nb = km_ref.shape[0]
    nt = tb // LANES
    D = HEAD_DIM
    heads = range(MB_HPS)
    hs = [slice(h * D, (h + 1) * D) for h in heads]
    qs = [q_ref[:, hs[h]] for h in heads]

    sts = [lax.dot_general(km_ref[:, hs[h]], qs[h].astype(F32), (((1,), (1,)), ((), ())),
                           preferred_element_type=F32, precision=HIGHEST) for h in heads]
    blk = lax.broadcasted_iota(jnp.int32, (nb, tb), 0)
    rowid = lax.broadcasted_iota(jnp.int32, (LANES, tb), 0)
    sel_ts = []
    for h in heads:
        st = jnp.where(blk < qi, sts[h], -jnp.inf)
        sel_t = jnp.full((LANES, tb), -1.0, F32)
        for r in range(MB_TOPK):
            m = jnp.max(st, axis=0, keepdims=True)
            idx = jnp.min(jnp.where(st == m, blk, nb), axis=0, keepdims=True)
            sel_t = jnp.where(rowid == r, jnp.where(r < qi, idx, -1).astype(F32), sel_t)
            st = jnp.where(blk == idx, -jnp.inf, st)
        sel_ts.append(sel_t)
    sels = [jnp.transpose(t) for t in sel_ts]
    sel_rep = [[jnp.broadcast_to(sels[h][:, r:r + 1], (tb, LANES)) for r in range(MB_TOPK)] for h in heads]

    def logits(h, start, width):
        kslab = k_ref[pl.ds(pl.multiple_of(start, tb), width), hs[h]]
        return lax.dot_general(qs[h], kslab, (((1,), (1,)), ((), ())), preferred_element_type=F32)

    r_i = lax.broadcasted_iota(jnp.int32, (tb, LANES), 0)
    c_i = lax.broadcasted_iota(jnp.int32, (tb, LANES), 1)
    sds = [logits(h, qi * tb, tb) for h in heads]
    for h in heads:
        mx = jnp.full((tb, LANES), NEG_BIG, F32)
        for t in range(nt):
            piece = jnp.where(c_i + t * LANES <= r_i, sds[h][:, t * LANES:(t + 1) * LANES], NEG_BIG)
            sd_scr[h, :, t * LANES:(t + 1) * LANES] = piece
            mx = jnp.maximum(mx, piece)
        mx_scr[h] = mx

    n_pairs = (qi + 1) // 2

    def pass1(i, carry):
        kb0 = 2 * i
        s2s = [logits(h, kb0 * tb, 2 * tb) for h in heads]
        for h in heads:
            mx = mx_scr[h]
            for half in range(2):
                kbf = (kb0 + half).astype(F32)
                hit = (sel_rep[h][0] == kbf) | (sel_rep[h][1] == kbf) | (sel_rep[h][2] == kbf)
                for t in range(nt):
                    c0 = (half * nt + t) * LANES
                    piece = jnp.where(hit, s2s[h][:, c0:c0 + LANES], NEG_BIG)
                    s_scr[h, i, :, c0:c0 + LANES] = piece
                    mx = jnp.maximum(mx, piece)
            mx_scr[h] = mx
        return carry

    lax.fori_loop(0, n_pairs, pass1, 0)
    m_reps = [jnp.broadcast_to(jnp.max(mx_scr[h], axis=-1, keepdims=True), (tb, LANES)) for h in heads]

    def probs(load, width, m_rep):
        ps, lsum = [], jnp.zeros((tb, LANES), F32)
        for t in range(width // LANES):
            p = jnp.exp2(load(t) - m_rep)
            lsum = lsum + p
            ps.append(p.astype(BF16))
        return jnp.concatenate(ps, axis=1), lsum

    pls = [probs(lambda t, h=h: sd_scr[h, :, t * LANES:(t + 1) * LANES], tb, m_reps[h]) for h in heads]
    for h in heads:
        l_scr[h] = pls[h][1]
        acc_scr[h] = jnp.dot(pls[h][0], v_ref[pl.ds(pl.multiple_of(qi * tb, tb), tb), hs[h]],
                             preferred_element_type=F32)

    def pass2(i, carry):
        pls = [probs(lambda t, h=h: s_scr[h, i, :, t * LANES:(t + 1) * LANES], 2 * tb, m_reps[h]) for h in heads]
        for h in heads:
            vslab = v_ref[pl.ds(pl.multiple_of(2 * i * tb, tb), 2 * tb), hs[h]]
            l_scr[h] = l_scr[h] + pls[h][1]
            acc_scr[h] = acc_scr[h] + jnp.dot(pls[h][0], vslab, preferred_element_type=F32)
        return carry

    lax.fori_loop(0, n_pairs, pass2, 0)
    for h in heads:
        o_ref[:, hs[h]] = (acc_scr[h] / jnp.sum(l_scr[h], axis=-1, keepdims=True)).astype(o_ref.dtype)


def _moba(qr, kr, vb, kmean, B, S):
    tb = MB_BLOCK
    nb = S // tb
    T = B * S
    km = kmean.reshape(B, nb, MB_HEADS * HEAD_DIM)
    HW = MB_HPS * HEAD_DIM
    ng = MB_HEADS // MB_HPS
    return pl.pallas_call(
        _moba_kernel,
        grid=(B * ng, nb),
        in_specs=[pl.BlockSpec((tb, HW), lambda g, i: ((g // ng) * nb + i, g % ng)),
                  pl.BlockSpec((S, HW), lambda g, i: (g // ng, g % ng)),
                  pl.BlockSpec((S, HW), lambda g, i: (g // ng, g % ng)),
                  pl.BlockSpec((None, nb, HW), lambda g, i: (g // ng, 0, g % ng))],
        out_specs=pl.BlockSpec((tb, HW), lambda g, i: ((g // ng) * nb + i, g % ng)),
        out_shape=jax.ShapeDtypeStruct((T, MB_HEADS * HEAD_DIM), BF16),
        scratch_shapes=[pltpu.VMEM((MB_HPS, (nb + 1) // 2, tb, 2 * tb), F32),
                        pltpu.VMEM((MB_HPS, tb, tb), F32),
                        pltpu.VMEM((MB_HPS, tb, LANES), F32),
                        pltpu.VMEM((MB_HPS, tb, LANES), F32),
                        pltpu.VMEM((MB_HPS, tb, HEAD_DIM), F32)],
        compiler_params=_cparams(("arbitrary", "arbitrary")),
        name="moba",
    )(qr, kr, vb, km)


DN_TILE_CHUNKS = 8


def _softplus(v):
    return jnp.maximum(v, 0.0) + jnp.log1p(jnp.exp(-jnp.abs(v)))


DN_GROUP = 2
DN_HPS = 4


def _split(a):
    hi = pltpu.bitcast(pltpu.bitcast(a, jnp.uint32) & jnp.uint32(0xFFFF0000), F32)
    return hi.astype(BF16), (a - hi).astype(BF16)


def _dot3(ah, al, bh, bl):
    lhs = jnp.concatenate([ah, ah, al], axis=1)
    rhs = jnp.concatenate([bh, bl, bh], axis=0)
    return jnp.dot(lhs, rhs, preferred_element_type=F32)


def _dot_bf(a, b):
    return jnp.dot(a.astype(BF16), b.astype(BF16), preferred_element_type=F32)


def _deltanet_kernel(q_ref, k_ref, v_ref, z_ref, sm_ref, wq_ref, wk_ref, wv_ref, alog_ref, dtb_ref, gout_ref,
                     o_ref, xp_scr, state_scr):
    i = pl.program_id(2)
    C = DN_CHUNK
    TR = q_ref.shape[0]
    HALO = 8

    @pl.when(i == 0)
    def _():
        xp_scr[...] = jnp.zeros(xp_scr.shape, F32)
        state_scr[...] = jnp.zeros(state_scr.shape, F32)

    def conv_silu(slot, x_ref, w_ref):
        xp_scr[slot, 0:HALO, :] = xp_scr[slot, TR:TR + HALO, :]
        xp_scr[slot, HALO:HALO + TR, :] = x_ref[...]
        acc = w_ref[DN_CONV - 1:DN_CONV, :] * xp_scr[slot, HALO:HALO + TR, :]
        for j in range(1, DN_CONV):
            acc = acc + w_ref[DN_CONV - 1 - j:DN_CONV - j, :] * xp_scr[slot, HALO - j:HALO - j + TR, :]
        return _silu(acc)

    q_all = conv_silu(0, q_ref, wq_ref)
    k_all = conv_silu(1, k_ref, wk_ref)
    v_all = conv_silu(2, v_ref, wv_ref)

    sm = sm_ref[...]
    lane = lax.broadcasted_iota(jnp.int32, sm.shape, 1)
    row = lax.broadcasted_iota(jnp.int32, sm.shape, 0)
    beta_all = _sigmoid(sm)
    g_all = -jnp.exp(alog_ref[...]) * _softplus(sm + dtb_ref[...])
    pos = row % C
    gc_all = g_all
    shift = 1
    while shift < C:
        gc_all = gc_all + jnp.where(pos >= shift, pltpu.roll(gc_all, shift, 0), 0.0)
        shift *= 2
    gc_t = jnp.transpose(gc_all)
    row_t = lax.broadcasted_iota(jnp.int32, gc_t.shape, 0)

    G = DN_GROUP * C
    ri = lax.broadcasted_iota(jnp.int32, (G, G), 0)
    ci = lax.broadcasted_iota(jnp.int32, (G, G), 1)
    same = (ri // C) == (ci // C)
    tril = same & (ci <= ri)
    strict = same & (ci < ri)
    eye = (ci == ri).astype(F32)
    D = HEAD_DIM
    gout = gout_ref[...]

    heads = []
    for hh in range(DN_HPS):
        h = pl.program_id(1) * DN_HPS + hh
        sl = slice(hh * D, (hh + 1) * D)
        q, k, v = q_all[:, sl], k_all[:, sl], v_all[:, sl]
        q = q * lax.rsqrt(jnp.sum(q * q, axis=-1, keepdims=True) + 1e-6) * (D ** -0.5)
        k = k * lax.rsqrt(jnp.sum(k * k, axis=-1, keepdims=True) + 1e-6)
        beta = jnp.sum(jnp.where(lane == h, beta_all, 0.0), axis=1, keepdims=True)
        gc = jnp.sum(jnp.where(lane == DN_HEADS + h, gc_all, 0.0), axis=1, keepdims=True)
        gc_row = jnp.sum(jnp.where(row_t == DN_HEADS + h, gc_t, 0.0), axis=0, keepdims=True)
        heads.append((q, k, v, beta, gc, gc_row, jnp.transpose(k)))

    steps = [[None] * (TR // C) for _ in range(DN_HPS)]
    preps = []
    for hh in range(DN_HPS):
        for g in range(TR // G):
            preps.append(_deltanet_group(heads[hh], g, tril, strict, eye, steps[hh]))
    _round_robin(preps)

    outs = [[] for _ in range(DN_HPS)]
    _round_robin([_deltanet_chain(hh, steps[hh], state_scr, outs[hh]) for hh in range(DN_HPS)])
    for hh in range(DN_HPS):
        sl = slice(hh * D, (hh + 1) * D)
        o = jnp.concatenate(outs[hh], axis=0)
        y = o * lax.rsqrt(jnp.mean(o * o, axis=-1, keepdims=True) + NORM_EPS) * gout
        o_ref[:, sl] = (y * _silu(z_ref[:, sl])).astype(o_ref.dtype)


def _round_robin(gens):
    active = list(gens)
    while active:
        still = []
        for gen in active:
            try:
                next(gen)
                still.append(gen)
            except StopIteration:
                pass
        active = still


def _deltanet_group(head, g, tril, strict, eye, steps_out):
    C = DN_CHUNK
    D = HEAD_DIM
    G = DN_GROUP * C
    q, k, v, beta, gc, gc_row, k_t = head
    r0 = g * G
    qg, kg, vg = q[r0:r0 + G], k[r0:r0 + G], v[r0:r0 + G]
    bg = beta[r0:r0 + G]
    gcg = gc[r0:r0 + G]
    gcr = gc_row[:, r0:r0 + G]
    ktg = k_t[:, r0:r0 + G]
    decay = jnp.where(tril, jnp.exp(jnp.where(tril, gcg - gcr, 0.0)), 0.0)
    kb = kg * bg
    aq = _dot_bf(jnp.concatenate([kb, qg], axis=0), ktg)
    yield
    m_neg = jnp.where(strict, -(aq[:G] * decay), 0.0)
    qk = (aq[G:] * decay).astype(BF16)
    t_inv = eye + m_neg
    ph, pl_ = _split(m_neg)
    for _ in range(5):
        th, tl = _split(t_inv)
        ph, pl_ = _split(_dot3(ph, pl_, ph, pl_))
        yield
        t_inv = t_inv + _dot3(th, tl, ph, pl_)
        yield
    egc = jnp.exp(gcg)
    th, tl = _split(t_inv)
    rh, rl = _split(jnp.concatenate([kb * egc, vg * bg], axis=1))
    wu = _dot3(th, tl, rh, rl).astype(BF16)
    yield
    qr = jnp.dot(qk, wu, preferred_element_type=F32)
    qp = qg * egc - qr[:, :D]
    r_all = qr[:, D:]
    yield
    for c in range(DN_GROUP):
        c0 = c * C
        g_last = gcg[c0 + C - 1:c0 + C, :]
        kt_tail = ktg[:, c0:c0 + C] * jnp.exp(g_last - gcr[:, c0:c0 + C])
        gh = jnp.dot(kt_tail.astype(BF16), wu[c0:c0 + C, :], preferred_element_type=F32)
        lhs = jnp.concatenate([gh[:, :D], qp[c0:c0 + C]], axis=0).astype(BF16)
        steps_out[g * DN_GROUP + c] = (lhs, gh[:, D:], r_all[c0:c0 + C], jnp.exp(g_last))
        yield


def _deltanet_chain(hh, steps, state_scr, outs):
    D = HEAD_DIM
    state = state_scr[hh]
    for lhs, h_add, r_add, dec in steps:
        res = jnp.dot(lhs, state.astype(BF16), preferred_element_type=F32)
        outs.append(res[D:] + r_add)
        state = state * dec - res[:D] + h_add
        yield
    state_scr[hh] = state


def _deltanet(proj, conv_w, a_log, dt_bias, g_dn_out, B, S):
    T = B * S
    TR = min(DN_TILE_CHUNKS * DN_CHUNK, S)
    nt = S // TR
    pad = jnp.zeros((DN_HEADS,), F32)
    rest = jnp.zeros((LANES - 2 * DN_HEADS,), F32)
    alog_lane = jnp.concatenate([pad, a_log.astype(F32), rest]).reshape(1, LANES)
    dtb_lane = jnp.concatenate([pad, dt_bias.astype(F32), rest]).reshape(1, LANES)
    HW = DN_HPS * HEAD_DIM
    per = HW // LANES
    rows = lambda c0: (lambda b, h, i: (b * nt + i, c0 // per + h))
    wcol = lambda c0: (lambda b, h, i: (0, c0 // per + h))
    const = lambda b, h, i: (0, 0)
    return pl.pallas_call(
        _deltanet_kernel,
        grid=(B, DN_HEADS // DN_HPS, nt),
        in_specs=[pl.BlockSpec((TR, HW), rows(COL_DQ)),
                  pl.BlockSpec((TR, HW), rows(COL_DK)),
                  pl.BlockSpec((TR, HW), rows(COL_DV)),
                  pl.BlockSpec((TR, HW), rows(COL_DZ)),
                  pl.BlockSpec((TR, LANES), lambda b, h, i: (b * nt + i, COL_SMALL)),
                  pl.BlockSpec((DN_CONV, HW), wcol(0)),
                  pl.BlockSpec((DN_CONV, HW), wcol(DN_HEADS)),
                  pl.BlockSpec((DN_CONV, HW), wcol(2 * DN_HEADS)),
                  pl.BlockSpec((1, LANES), const),
                  pl.BlockSpec((1, LANES), const),
                  pl.BlockSpec((1, HEAD_DIM), const)],
        out_specs=pl.BlockSpec((TR, HW), lambda b, h, i: (b * nt + i, h)),
        out_shape=jax.ShapeDtypeStruct((T, DN_HEADS * HEAD_DIM), BF16),
        scratch_shapes=[pltpu.VMEM((3, TR + 8, HW), F32), pltpu.VMEM((DN_HPS, HEAD_DIM, HEAD_DIM), F32)],
        compiler_params=_cparams(("arbitrary", "arbitrary", "arbitrary")),
        name="deltanet",
    )(proj, proj, proj, proj, proj, conv_w, conv_w, conv_w, alog_lane, dtb_lane, g_dn_out.reshape(1, HEAD_DIM))


def _rms(v):
    return v * lax.rsqrt(jnp.mean(v * v, axis=-1, keepdims=True) + NORM_EPS)


def _merge_kernel(oa_ref, ob_ref, ga_ref, gb_ref, x_ref, mod_ref, gpost_ref, gpre_ref, wa_ref, wb_ref, wo_ref,
                  wrt_ref, br_ref, x1_ref, hp_ref, idx_ref, wrow_ref, rank_ref, cnt_ref, carry_scr):
    E = N_EXPERTS
    tm = x_ref.shape[0]
    half = x_ref.shape[1] // 2

    @pl.when(pl.program_id(0) == 0)
    def _():
        carry_scr[...] = jnp.zeros(carry_scr.shape, F32)

    ya = jnp.dot(oa_ref[...], wa_ref[...], preferred_element_type=F32)
    yb = jnp.dot(ob_ref[...], wb_ref[...], preferred_element_type=F32)
    merged = _sigmoid(ga_ref[...]) * ya + _sigmoid(gb_ref[...]) * yb
    mix = jnp.dot(merged.astype(BF16), wo_ref[...], preferred_element_type=F32)
    x1 = x_ref[...] + mod_ref[0, 2:3, :] * (_rms(mix) * gpost_ref[...])
    x1_ref[...] = x1
    h2 = (_rms(x1) * gpre_ref[...]) * (1.0 + mod_ref[0, 4:5, :]) + mod_ref[0, 3:4, :]

    lo_bits = pltpu.bitcast(h2[:, :half].astype(BF16).astype(F32), jnp.uint32) >> 16
    hi_bits = pltpu.bitcast(h2[:, half:].astype(BF16).astype(F32), jnp.uint32) & jnp.uint32(0xFFFF0000)
    hp_ref[...] = hi_bits | lo_bits

    lt = lax.dot_general(wrt_ref[...], h2, (((1,), (1,)), ((), ())), preferred_element_type=F32,
                         precision=HIGHEST) + br_ref[...]
    eid = lax.broadcasted_iota(jnp.int32, (E, tm), 0)
    vals, idxs = [], []
    for _ in range(TOP_K):
        m = jnp.max(lt, axis=0, keepdims=True)
        idx = jnp.min(jnp.where(lt == m, eid, E), axis=0, keepdims=True)
        vals.append(m)
        idxs.append(idx)
        lt = jnp.where(eid == idx, -jnp.inf, lt)
    exps = [jnp.exp(v - vals[0]) for v in vals]
    den = exps[0] + exps[1] + exps[2] + exps[3]
    wts = [e / den for e in exps]

    hot = jnp.zeros((E, tm), F32)
    for idx in idxs:
        hot = hot + (eid == idx).astype(F32)
    ti = lax.broadcasted_iota(jnp.int32, (tm, tm), 0)
    tj = lax.broadcasted_iota(jnp.int32, (tm, tm), 1)
    before = (ti < tj).astype(BF16)
    prior = carry_scr[...][:, 0:1] + jnp.dot(hot.astype(BF16), before, preferred_element_type=F32)
    row8 = lax.broadcasted_iota(jnp.int32, (8, tm), 0)
    row128 = lax.broadcasted_iota(jnp.int32, (LANES, tm), 0)
    idx8 = jnp.zeros((8, tm), jnp.int32)
    rank8 = jnp.zeros((8, tm), jnp.int32)
    w128 = jnp.zeros((LANES, tm), F32)
    for r in range(TOP_K):
        rank_r = jnp.sum(jnp.where(eid == idxs[r], prior, 0.0), axis=0, keepdims=True)
        idx8 = jnp.where(row8 == r, idxs[r], idx8)
        rank8 = jnp.where(row8 == r, rank_r.astype(jnp.int32), rank8)
        w128 = jnp.where(row128 == r, wts[r], w128)
    idx_ref[...] = idx8
    rank_ref[...] = rank8
    wrow_ref[...] = jnp.transpose(w128)
    carry = carry_scr[...] + jnp.sum(hot, axis=1, keepdims=True)
    carry_scr[...] = carry
    cnt_ref[...] = carry


def _merge(oa, ob, proj, x2, mod3, g_post_mix, g_pre_ffn, w_br_a, w_br_b, w_o, w_router, b_router, S):
    T, D = x2.shape
    E = N_EXPERTS
    tm = min(512, S)
    per_b = S // tm
    W = DN_HEADS * HEAD_DIM
    row = lambda i: (i, 0)
    const = lambda i: (0, 0)
    lane_t = lambda i: (0, i)
    return pl.pallas_call(
        _merge_kernel,
        grid=(T // tm,),
        in_specs=[pl.BlockSpec((tm, W), row),
                  pl.BlockSpec((tm, W), row),
                  pl.BlockSpec((tm, D), lambda i: (i, COL_GA * LANES // D)),
                  pl.BlockSpec((tm, D), lambda i: (i, COL_GB * LANES // D)),
                  pl.BlockSpec((tm, D), row),
                  pl.BlockSpec((1, 6, D), lambda i: (i // per_b, 0, 0)),
                  pl.BlockSpec((1, D), const),
                  pl.BlockSpec((1, D), const),
                  pl.BlockSpec((W, D), const),
                  pl.BlockSpec((W, D), const),
                  pl.BlockSpec((D, D), const),
                  pl.BlockSpec((E, D), const),
                  pl.BlockSpec((E, 1), const)],
        out_specs=[pl.BlockSpec((tm, D), row),
                   pl.BlockSpec((tm, D // 2), row),
                   pl.BlockSpec((8, tm), lane_t),
                   pl.BlockSpec((tm, LANES), row),
                   pl.BlockSpec((8, tm), lane_t),
                   pl.BlockSpec((E, LANES), const)],
        out_shape=[jax.ShapeDtypeStruct((T, D), F32),
                   jax.ShapeDtypeStruct((T, D // 2), jnp.uint32),
                   jax.ShapeDtypeStruct((8, T), jnp.int32),
                   jax.ShapeDtypeStruct((T, LANES), F32),
                   jax.ShapeDtypeStruct((8, T), jnp.int32),
                   jax.ShapeDtypeStruct((E, LANES), F32)],
        scratch_shapes=[pltpu.VMEM((E, LANES), F32)],
        compiler_params=_cparams(("arbitrary",)),
        name="merge",
    )(oa, ob, proj, proj, x2, mod3, g_post_mix.reshape(1, D), g_pre_ffn.reshape(1, D),
      w_br_a.astype(BF16), w_br_b.astype(BF16), w_o.astype(BF16),
      jnp.transpose(w_router).astype(F32), b_router.reshape(E, 1).astype(F32))


DISPATCH_TOKENS = 256


def _dispatch_kernel(fill_ref, dest_ref, src_ref, dst_ref, zero_scr, sem, zsem):
    n = dest_ref.shape[1]
    tm = zero_scr.shape[0]

    @pl.when(pl.program_id(0) == 0)
    def _():
        zero_scr[...] = jnp.zeros(zero_scr.shape, zero_scr.dtype)

        def fill_copy(k):
            return pltpu.make_async_copy(zero_scr, dst_ref.at[pl.ds(pl.multiple_of(k * tm, tm), tm)], zsem)

        def start(k, carry):
            pl.when(fill_ref[k] == 1)(lambda: fill_copy(k).start())
            return carry

        def wait(k, carry):
            pl.when(fill_ref[k] == 1)(lambda: fill_copy(k).wait())
            return carry

        lax.fori_loop(0, fill_ref.shape[0], start, 0)
        lax.fori_loop(0, fill_ref.shape[0], wait, 0)

    def issue(j, carry):
        for r in range(TOP_K):
            pltpu.make_async_copy(src_ref.at[pl.ds(j, 1)], dst_ref.at[pl.ds(dest_ref[r, j], 1)], sem).start()
        return carry

    lax.fori_loop(0, n, issue, 0, unroll=8)
    for r in range(TOP_K):
        pltpu.make_async_copy(src_ref, dst_ref.at[pl.ds(0, n)], sem).wait()


def _dispatch(dest, blk_fill, hp):
    T, Wd = hp.shape
    n = min(DISPATCH_TOKENS, T)
    tm = EXPERT_ROWS
    grid_spec = pltpu.PrefetchScalarGridSpec(
        num_scalar_prefetch=1,
        grid=(T // n,),
        in_specs=[pl.BlockSpec((TOP_K, n), lambda i, fill: (0, i), memory_space=pltpu.SMEM),
                  pl.BlockSpec((n, Wd), lambda i, fill: (i, 0))],
        out_specs=pl.BlockSpec(memory_space=pl.ANY),
        scratch_shapes=[pltpu.VMEM((tm, Wd), hp.dtype), pltpu.SemaphoreType.DMA(()), pltpu.SemaphoreType.DMA(())],
    )
    return pl.pallas_call(
        _dispatch_kernel,
        grid_spec=grid_spec,
        out_shape=jax.ShapeDtypeStruct((_expert_blocks(T * TOP_K) * tm, Wd), hp.dtype),
        compiler_params=_cparams(("arbitrary",)),
        name="dispatch",
    )(blk_fill, dest, hp)


EXPERT_ROWS = 512
EXPERT_FEATURE_TILE = 512


def _experts_kernel(exp_ref, valid_ref, new_ref, x_ref, wgu_ref, bgu_ref, wdn_ref, bdn_ref, o_ref, wgu_bf, wdn_bf):
    k = pl.program_id(0)
    F = wdn_ref.shape[0]

    @pl.when(new_ref[k] == 1)
    def _():
        wgu_bf[...] = wgu_ref[...].astype(BF16)
        wdn_bf[...] = wdn_ref[...].astype(BF16)

    def ffn():
        word = x_ref[...]
        x = jnp.concatenate([pltpu.bitcast(word << 16, F32).astype(BF16),
                             pltpu.bitcast(word & jnp.uint32(0xFFFF0000), F32).astype(BF16)], axis=1)

        def gate_up(c):
            ft = slice(c, c + EXPERT_FEATURE_TILE)
            ut = slice(F + c, F + c + EXPERT_FEATURE_TILE)
            return (jnp.dot(x, wgu_bf[:, ft], preferred_element_type=F32) + bgu_ref[:, ft],
                    jnp.dot(x, wgu_bf[:, ut], preferred_element_type=F32) + bgu_ref[:, ut])

        tiles = list(range(0, F, EXPERT_FEATURE_TILE))
        y = None
        nxt = gate_up(tiles[0])
        for t, c in enumerate(tiles):
            gate, up = nxt
            if t + 1 < len(tiles):
                nxt = gate_up(tiles[t + 1])
            gate = jnp.minimum(gate, SWIGLU_LIMIT)
            up = jnp.clip(up, -SWIGLU_LIMIT, SWIGLU_LIMIT)
            act = (up + 1.0) * gate * _sigmoid(SWIGLU_ALPHA * gate)
            part = jnp.dot(act.astype(BF16), wdn_bf[c:c + EXPERT_FEATURE_TILE, :], preferred_element_type=F32)
            y = part + bdn_ref[...] if y is None else y + part
        return y

    @pl.when(valid_ref[k] == 1)
    def _():
        o_ref[:, 0, :] = ffn()

    @pl.when(valid_ref[k] == 0)
    def _():
        o_ref[...] = jnp.zeros(o_ref.shape, o_ref.dtype)


def _experts(xs, blk_exp, blk_valid, blk_new, w_gu, b_gu, w_down, b_down):
    R, half = xs.shape
    E, D, F2 = w_gu.shape
    F = F2 // 2
    tm = EXPERT_ROWS
    grid_spec = pltpu.PrefetchScalarGridSpec(
        num_scalar_prefetch=3,
        grid=(R // tm,),
        in_specs=[pl.BlockSpec((tm, half), lambda k, ex, va, nw: (k, 0)),
                  pl.BlockSpec((None, D, F2), lambda k, ex, va, nw: (ex[k], 0, 0)),
                  pl.BlockSpec((None, 1, F2), lambda k, ex, va, nw: (ex[k], 0, 0)),
                  pl.BlockSpec((None, F, D), lambda k, ex, va, nw: (ex[k], 0, 0)),
                  pl.BlockSpec((None, 1, D), lambda k, ex, va, nw: (ex[k], 0, 0))],
        out_specs=pl.BlockSpec((tm, 1, D), lambda k, ex, va, nw: (k, 0, 0)),
        scratch_shapes=[pltpu.VMEM((D, F2), BF16), pltpu.VMEM((F, D), BF16)],
    )
    return pl.pallas_call(
        _experts_kernel,
        grid_spec=grid_spec,
        out_shape=jax.ShapeDtypeStruct((R, 1, D), F32),
        compiler_params=_cparams(("arbitrary",)),
        name="experts",
    )(blk_exp, blk_valid, blk_new, xs, w_gu, b_gu.reshape(E, 1, F2).astype(F32),
      w_down, b_down.reshape(E, 1, D).astype(F32))


def _expert_blocks(A):
    return A // EXPERT_ROWS + N_EXPERTS


def _block_tables(counts, A):
    E = N_EXPERTS
    tm = EXPERT_ROWS
    n_blk = _expert_blocks(A)
    nblk_e = (counts + tm - 1) // tm
    blk_end = jnp.cumsum(nblk_e)
    blk_start = blk_end - nblk_e
    used = blk_end[-1]
    k = jnp.arange(n_blk, dtype=jnp.int32)
    kk = jnp.minimum(k, used - 1)
    e = jnp.sum((blk_end[None, :] <= kk[:, None]).astype(jnp.int32), axis=1)
    hot = e[:, None] == jnp.arange(E, dtype=jnp.int32)[None, :]
    pick = lambda table: jnp.sum(jnp.where(hot, table[None, :], 0), axis=1)
    valid = k < used
    new = valid & (k == pick(blk_start))
    fill = jnp.logical_not(valid) | (k == pick(blk_end) - 1)
    i32 = lambda v: v.astype(jnp.int32)
    return blk_start * tm, i32(e), i32(valid), i32(new), i32(fill)


COMBINE_TOKENS = 256


def _combine_kernel(dest_ref, nxt_ref, y_ref, wrow_ref, x1_ref, mod_ref, gpost_ref, o_ref, stage, moe_scr, sem):
    i = pl.program_id(0)
    last = pl.num_programs(0) - 1
    n = x1_ref.shape[0]
    SUB = 8

    def gather(idx_ref, j, s):
        for r in range(TOP_K):
            pltpu.make_async_copy(y_ref.at[idx_ref[r, j]], stage.at[s, r, j], sem.at[s]).start(priority=r % 2)

    @pl.when(i == 0)
    def _():
        def prime(j, carry):
            gather(dest_ref, j, 0)
            return carry
        lax.fori_loop(0, n, prime, 0, unroll=SUB)

    def step(slot):
        for r in range(TOP_K):
            pltpu.make_async_copy(y_ref.at[pl.ds(0, n)], stage.at[slot, r], sem.at[slot]).wait()

        def reduce_group(g):
            rows = pl.ds(pl.multiple_of(g * SUB, SUB), SUB)
            w = wrow_ref[rows, :]
            acc = w[:, 0:1] * stage[slot, 0, rows, 0, :]
            for r in range(1, TOP_K):
                acc = acc + w[:, r:r + 1] * stage[slot, r, rows, 0, :]
            moe_scr[rows, :] = acc

        @pl.when(i < last)
        def _():
            def body(g, carry):
                for t in range(SUB):
                    gather(nxt_ref, g * SUB + t, 1 - slot)
                reduce_group(g)
                return carry
            lax.fori_loop(0, n // SUB, body, 0)

        @pl.when(i == last)
        def _():
            def body(g, carry):
                reduce_group(g)
                return carry
            lax.fori_loop(0, n // SUB, body, 0)

    for parity in range(2):
        pl.when(i % 2 == parity)(functools.partial(step, parity))

    o_ref[...] = x1_ref[...] + mod_ref[0, 5:6, :] * (_rms(moe_scr[...]) * gpost_ref[...])


def _combine(dest, y, wrow, x1, mod3, g_post_ffn, S):
    T, D = x1.shape
    n = min(COMBINE_TOKENS, S)
    per_b = S // n
    steps = T // n
    return pl.pallas_call(
        _combine_kernel,
        grid=(steps,),
        in_specs=[pl.BlockSpec((TOP_K, n), lambda i: (0, i), memory_space=pltpu.SMEM),
                  pl.BlockSpec((TOP_K, n), lambda i: (0, jnp.minimum(i + 1, steps - 1)), memory_space=pltpu.SMEM),
                  pl.BlockSpec(memory_space=pl.ANY),
                  pl.BlockSpec((n, LANES), lambda i: (i, 0)),
                  pl.BlockSpec((n, D), lambda i: (i, 0)),
                  pl.BlockSpec((1, 6, D), lambda i: (i // per_b, 0, 0)),
                  pl.BlockSpec((1, D), lambda i: (0, 0))],
        out_specs=pl.BlockSpec((n, D), lambda i: (i, 0)),
        out_shape=jax.ShapeDtypeStruct((T, D), F32),
        scratch_shapes=[pltpu.VMEM((2, TOP_K, n, 1, D), F32), pltpu.VMEM((n, D), F32),
                        pltpu.SemaphoreType.DMA((2,))],
        compiler_params=_cparams(("arbitrary",)),
        name="combine",
    )(dest, dest, y, wrow, x1, mod3, g_post_ffn.reshape(1, D))


def _regroup_w_in(w_in):
    D = w_in.shape[0]
    dw = DN_HEADS * HEAD_DIM
    mw = MB_HEADS * HEAD_DIM
    cuts = np.cumsum([dw, dw, dw, dw, DN_HEADS, DN_HEADS, mw, mw, mw, D, D])[:-1]
    dq, dk, dv, dz, db, da, mq, mk, mv, ga, gb = jnp.split(w_in, [int(c) for c in cuts], axis=1)
    small = jnp.concatenate([db, da, jnp.zeros((D, LANES - 2 * DN_HEADS), w_in.dtype)], axis=1)
    return jnp.concatenate([p.astype(BF16) for p in (ga, gb, dq, dk, dv, dz, mq, mk, mv, small)], axis=1)


def kernel(x, c, w_ada, b_ada, g_pre_mix, g_post_mix, g_pre_ffn, g_post_ffn, w_in, conv_w, a_log, dt_bias,
           g_dn_out, w_br_a, w_br_b, w_o, w_router, b_router, w_gu, b_gu, w_down, b_down):
    B, S, D = x.shape
    l = 0
    mod = _adaln(c, w_ada[l], b_ada[l]).reshape(B, 6, D)
    proj = _inproj(x, mod, g_pre_mix[l], _regroup_w_in(w_in[l]))
    qr, kr, vb, kmean = _mobaprep(proj, B, S)
    ob = _moba(qr, kr, vb, kmean, B, S)
    oa = _deltanet(proj, conv_w[l], a_log[l], dt_bias[l], g_dn_out[l], B, S)
    x1, hp, idx8, wrow, rank8, cnt = _merge(oa, ob, proj, x.reshape(B * S, D), mod, g_post_mix[l], g_pre_ffn[l],
                                            w_br_a[l], w_br_b[l], w_o[l], w_router[l], b_router[l], S)
    out = _moe(x1, hp, idx8, wrow, rank8, cnt, mod, g_post_ffn[l], w_gu[l], b_gu[l], w_down[l], b_down[l], S)
    return out.reshape(B, S, D)


def _moe(x1, hp, idx8, wrow, rank8, cnt, mod, g_post_ffn, w_gu, b_gu, w_down, b_down, S):
    T = x1.shape[0]
    counts = cnt[:, 0].astype(jnp.int32)
    start, blk_exp, blk_valid, blk_new, blk_fill = _block_tables(counts, T * TOP_K)
    hot = idx8[:TOP_K, :, None] == jnp.arange(N_EXPERTS, dtype=jnp.int32)
    dest = rank8[:TOP_K] + jnp.sum(jnp.where(hot, start, 0), axis=-1)
    xs = _dispatch(dest, blk_fill, hp)
    y = _experts(xs, blk_exp, blk_valid, blk_new, w_gu, b_gu, w_down, b_down)
    return _combine(dest, y, wrow, x1, mod, g_post_ffn, S)
```

```python
import functools

import jax
import jax.numpy as jnp
import numpy as np
from jax import lax
from jax.experimental import pallas as pl
from jax.experimental.pallas import tpu as pltpu

F32 = jnp.float32
BF16 = jnp.bfloat16
HIGHEST = lax.Precision.HIGHEST

HEAD_DIM = 128
DN_HEADS = 4
DN_CONV = 4
DN_CHUNK = 64
MB_HEADS = 4
MB_BLOCK = 256
MB_TOPK = 3
ROPE_THETA = 500000.0
ROPE_DIM = HEAD_DIM // 4
N_EXPERTS = 32
TOP_K = 4
SWIGLU_LIMIT = 7.0
SWIGLU_ALPHA = 1.702
NORM_EPS = 1e-6
LANES = 128
NEG_BIG = -1e30
MB_Q_SCALE = HEAD_DIM ** -0.5 * float(np.log2(np.e))

_W = DN_HEADS * HEAD_DIM // LANES
COL_GA = 0
COL_GB = 8
COL_DQ = 16
COL_DK = 20
COL_DV = 24
COL_DZ = 28
COL_MQ = 32
COL_MK = 36
COL_MV = 40
COL_SMALL = 44
N_PROJ = 45 * LANES

VMEM_LIMIT = 56 * 1024 * 1024


def _cparams(sem):
    return pltpu.CompilerParams(dimension_semantics=sem, vmem_limit_bytes=VMEM_LIMIT)


def _sigmoid(v):
    return 0.5 * jnp.tanh(0.5 * v) + 0.5


def _silu(v):
    return v * _sigmoid(v)


def _adaln_kernel(c_ref, w_ref, b_ref, o_ref):
    a = _silu(c_ref[...])
    o_ref[...] = jnp.dot(a, w_ref[...], preferred_element_type=F32, precision=HIGHEST) + b_ref[...]


def _adaln(c, w_ada, b_ada):
    B, D = c.shape
    N = w_ada.shape[1]
    tn = D
    return pl.pallas_call(
        _adaln_kernel,
        grid=(N // tn,),
        in_specs=[pl.BlockSpec((B, D), lambda j: (0, 0)),
                  pl.BlockSpec((D, tn), lambda j: (0, j)),
                  pl.BlockSpec((1, tn), lambda j: (0, j))],
        out_specs=pl.BlockSpec((B, tn), lambda j: (0, j)),
        out_shape=jax.ShapeDtypeStruct((B, N), F32),
        compiler_params=_cparams(("arbitrary",)),
        name="adaln",
    )(c, w_ada, b_ada.reshape(1, N))


def _inproj_kernel(x_ref, mod_ref, g_ref, w_ref, o_ref, h_scr):
    @pl.when(pl.program_id(2) == 0)
    def _():
        x = x_ref[0]
        y = x * lax.rsqrt(jnp.mean(x * x, axis=-1, keepdims=True) + NORM_EPS) * g_ref[...]
        h = y * (1.0 + mod_ref[0, 1:2, :]) + mod_ref[0, 0:1, :]
        h_scr[...] = h.astype(BF16)

    o_ref[...] = jnp.dot(h_scr[...], w_ref[...], preferred_element_type=F32)


def _inproj(x, mod3, g_pre, w_all):
    B, S, D = x.shape
    tm = min(1024, S)
    tn = N_PROJ // 3
    nrow = S // tm
    return pl.pallas_call(
        _inproj_kernel,
        grid=(B, nrow, N_PROJ // tn),
        in_specs=[pl.BlockSpec((1, tm, D), lambda b, i, j: (b, i, 0)),
                  pl.BlockSpec((1, 6, D), lambda b, i, j: (b, 0, 0)),
                  pl.BlockSpec((1, D), lambda b, i, j: (0, 0)),
                  pl.BlockSpec((D, tn), lambda b, i, j: (0, j))],
        out_specs=pl.BlockSpec((tm, tn), lambda b, i, j: (b * nrow + i, j)),
        out_shape=jax.ShapeDtypeStruct((B * S, N_PROJ), F32),
        scratch_shapes=[pltpu.VMEM((tm, D), BF16)],
        compiler_params=_cparams(("arbitrary", "arbitrary", "arbitrary")),
        name="inproj",
    )(x, mod3, g_pre.reshape(1, D), w_all)


def _rope(v, cosf, sinf, lane):
    rot = jnp.where(lane < ROPE_DIM // 2, pltpu.roll(v, LANES - ROPE_DIM // 2, 1), pltpu.roll(v, ROPE_DIM // 2, 1))
    return v * cosf + rot * sinf


def _mobaprep_kernel(q_ref, k_ref, v_ref, cos_ref, sin_ref, qo_ref, ko_ref, vo_ref, km_ref):
    cosf = cos_ref[...]
    sinf = sin_ref[...]
    lane = lax.broadcasted_iota(jnp.int32, cosf.shape, 1)
    for h in range(MB_HEADS):
        sl = slice(h * HEAD_DIM, (h + 1) * HEAD_DIM)
        qo_ref[:, sl] = (_rope(q_ref[:, sl], cosf, sinf, lane) * MB_Q_SCALE).astype(BF16)
        kr = _rope(k_ref[:, sl], cosf, sinf, lane)
        ko_ref[:, sl] = kr.astype(BF16)
        for blk in range(km_ref.shape[0]):
            km_ref[blk, :, sl] = jnp.mean(kr[blk * MB_BLOCK:(blk + 1) * MB_BLOCK], axis=0, keepdims=True)
    vo_ref[...] = v_ref[...].astype(BF16)


def _rope_tables(S):
    half = ROPE_DIM // 2
    inv_freq = ROPE_THETA ** (-jnp.arange(half, dtype=F32) / half)
    ang = jnp.arange(S, dtype=F32)[:, None] * inv_freq[None, :]
    cos, sin = jnp.cos(ang), jnp.sin(ang)
    rest = HEAD_DIM - ROPE_DIM
    cosf = jnp.concatenate([cos, cos, jnp.ones((S, rest), F32)], axis=1)
    sinf = jnp.concatenate([-sin, sin, jnp.zeros((S, rest), F32)], axis=1)
    return cosf, sinf


def _mobaprep(proj, B, S):
    T = B * S
    tb = min(4 * MB_BLOCK, S)
    nb = S // tb
    W = MB_HEADS * HEAD_DIM
    cosf, sinf = _rope_tables(S)
    col = lambda c: (lambda i: (i, c))
    return pl.pallas_call(
        _mobaprep_kernel,
        grid=(T // tb,),
        in_specs=[pl.BlockSpec((tb, W), col(COL_MQ // _W)),
                  pl.BlockSpec((tb, W), col(COL_MK // _W)),
                  pl.BlockSpec((tb, W), col(COL_MV // _W)),
                  pl.BlockSpec((tb, LANES), lambda i: (i % nb, 0)),
                  pl.BlockSpec((tb, LANES), lambda i: (i % nb, 0))],
        out_specs=[pl.BlockSpec((tb, W), lambda i: (i, 0)),
                   pl.BlockSpec((tb, W), lambda i: (i, 0)),
                   pl.BlockSpec((tb, W), lambda i: (i, 0)),
                   pl.BlockSpec((tb // MB_BLOCK, 1, W), lambda i: (i, 0, 0))],
        out_shape=[jax.ShapeDtypeStruct((T, W), BF16),
                   jax.ShapeDtypeStruct((T, W), BF16),
                   jax.ShapeDtypeStruct((T, W), BF16),
                   jax.ShapeDtypeStruct((T // MB_BLOCK, 1, W), F32)],
        compiler_params=_cparams(("arbitrary",)),
        name="mobaprep",
    )(proj, proj, proj, cosf, sinf)


MB_HPS = 4


def _moba_kernel(q_ref, k_ref, v_ref, km_ref, o_ref, s_scr, sd_scr, mx_scr, l_scr, acc_scr):
    qi = pl.program_id(1)
    tb = MB_BLOCK
    nb = km_ref.shape[0]
    nt = tb // LANES
    D = HEAD_DIM
    heads = range(MB_HPS)
    hs = [slice(h * D, (h + 1) * D) for h in heads]
    qs = [q_ref[:, hs[h]] for h in heads]

    sts = [lax.dot_general(km_ref[:, hs[h]], qs[h].astype(F32), (((1,), (1,)), ((), ())),
                           preferred_element_type=F32, precision=HIGHEST) for h in heads]
    blk = lax.broadcasted_iota(jnp.int32, (nb, tb), 0)
    rowid = lax.broadcasted_iota(jnp.int32, (LANES, tb), 0)
    sel_ts = []
    for h in heads:
        st = jnp.where(blk < qi, sts[h], -jnp.inf)
        sel_t = jnp.full((LANES, tb), -1.0, F32)
        for r in range(MB_TOPK):
            m = jnp.max(st, axis=0, keepdims=True)
            idx = jnp.min(jnp.where(st == m, blk, nb), axis=0, keepdims=True)
            sel_t = jnp.where(rowid == r, jnp.where(r < qi, idx, -1).astype(F32), sel_t)
            st = jnp.where(blk == idx, -jnp.inf, st)
        sel_ts.append(sel_t)
    sels = [jnp.transpose(t) for t in sel_ts]
    sel_rep = [[jnp.broadcast_to(sels[h][:, r:r + 1], (tb, LANES)) for r in range(MB_TOPK)] for h in heads]

    def logits(h, start, width):
        kslab = k_ref[pl.ds(pl.multiple_of(start, tb), width), hs[h]]
        return lax.dot_general(qs[h], kslab, (((1,), (1,)), ((), ())), preferred_element_type=F32)

    r_i = lax.broadcasted_iota(jnp.int32, (tb, LANES), 0)
    c_i = lax.broadcasted_iota(jnp.int32, (tb, LANES), 1)
    sds = [logits(h, qi * tb, tb) for h in heads]
    for h in heads:
        mx = jnp.full((tb, LANES), NEG_BIG, F32)
        for t in range(nt):
            piece = jnp.where(c_i + t * LANES <= r_i, sds[h][:, t * LANES:(t + 1) * LANES], NEG_BIG)
            sd_scr[h, :, t * LANES:(t + 1) * LANES] = piece
            mx = jnp.maximum(mx, piece)
        mx_scr[h] = mx

    n_pairs = (qi + 1) // 2

    def pass1(i, carry):
        kb0 = 2 * i
        s2s = [logits(h, kb0 * tb, 2 * tb) for h in heads]
        for h in heads:
            mx = mx_scr[h]
            for half in range(2):
                kbf = (kb0 + half).astype(F32)
                hit = (sel_rep[h][0] == kbf) | (sel_rep[h][1] == kbf) | (sel_rep[h][2] == kbf)
                for t in range(nt):
                    c0 = (half * nt + t) * LANES
                    piece = jnp.where(hit, s2s[h][:, c0:c0 + LANES], NEG_BIG)
                    s_scr[h, i, :, c0:c0 + LANES] = piece
                    mx = jnp.maximum(mx, piece)
            mx_scr[h] = mx
        return carry

    lax.fori_loop(0, n_pairs, pass1, 0)
    m_reps = [jnp.broadcast_to(jnp.max(mx_scr[h], axis=-1, keepdims=True), (tb, LANES)) for h in heads]

    def probs(load, width, m_rep):
        ps, lsum = [], jnp.zeros((tb, LANES), F32)
        for t in range(width // LANES):
            p = jnp.exp2(load(t) - m_rep)
            lsum = lsum + p
            ps.append(p.astype(BF16))
        return jnp.concatenate(ps, axis=1), lsum

    pls = [probs(lambda t, h=h: sd_scr[h, :, t * LANES:(t + 1) * LANES], tb, m_reps[h]) for h in heads]
    for h in heads:
        l_scr[h] = pls[h][1]
        acc_scr[h] = jnp.dot(pls[h][0], v_ref[pl.ds(pl.multiple_of(qi * tb, tb), tb), hs[h]],
                             preferred_element_type=F32)

    def pass2(i, carry):
        pls = [probs(lambda t, h=h: s_scr[h, i, :, t * LANES:(t + 1) * LANES], 2 * tb, m_reps[h]) for h in heads]
        for h in heads:
            vslab = v_ref[pl.ds(pl.multiple_of(2 * i * tb, tb), 2 * tb), hs[h]]
            l_scr[h] = l_scr[h] + pls[h][1]
            acc_scr[h] = acc_scr[h] + jnp.dot(pls[h][0], vslab, preferred_element_type=F32)
        return carry

    lax.fori_loop(0, n_pairs, pass2, 0)
    for h in heads:
        o_ref[:, hs[h]] = (acc_scr[h] / jnp.sum(l_scr[h], axis=-1, keepdims=True)).astype(o_ref.dtype)


def _moba(qr, kr, vb, kmean, B, S):
    tb = MB_BLOCK
    nb = S // tb
    T = B * S
    km = kmean.reshape(B, nb, MB_HEADS * HEAD_DIM)
    HW = MB_HPS * HEAD_DIM
    ng = MB_HEADS // MB_HPS
    return pl.pallas_call(
        _moba_kernel,
        grid=(B * ng, nb),
        in_specs=[pl.BlockSpec((tb, HW), lambda g, i: ((g // ng) * nb + i, g % ng)),
                  pl.BlockSpec((S, HW), lambda g, i: (g // ng, g % ng)),
                  pl.BlockSpec((S, HW), lambda g, i: (g // ng, g % ng)),
                  pl.BlockSpec((None, nb, HW), lambda g, i: (g // ng, 0, g % ng))],
        out_specs=pl.BlockSpec((tb, HW), lambda g, i: ((g // ng) * nb + i, g % ng)),
        out_shape=jax.ShapeDtypeStruct((T, MB_HEADS * HEAD_DIM), BF16),
        scratch_shapes=[pltpu.VMEM((MB_HPS, (nb + 1) // 2, tb, 2 * tb), F32),
                        pltpu.VMEM((MB_HPS, tb, tb), F32),
                        pltpu.VMEM((MB_HPS, tb, LANES), F32),
                        pltpu.VMEM((MB_HPS, tb, LANES), F32),
                        pltpu.VMEM((MB_HPS, tb, HEAD_DIM), F32)],
        compiler_params=_cparams(("arbitrary", "arbitrary")),
        name="moba",
    )(qr, kr, vb, km)


DN_TILE_CHUNKS = 8


def _softplus(v):
    return jnp.maximum(v, 0.0) + jnp.log1p(jnp.exp(-jnp.abs(v)))


DN_GROUP = 2
DN_HPS = 4


def _split(a):
    hi = pltpu.bitcast(pltpu.bitcast(a, jnp.uint32) & jnp.uint32(0xFFFF0000), F32)
    return hi.astype(BF16), (a - hi).astype(BF16)


def _dot3(ah, al, bh, bl):
    lhs = jnp.concatenate([ah, ah, al], axis=1)
    rhs = jnp.concatenate([bh, bl, bh], axis=0)
    return jnp.dot(lhs, rhs, preferred_element_type=F32)


def _dot_bf(a, b):
    return jnp.dot(a.astype(BF16), b.astype(BF16), preferred_element_type=F32)


def _deltanet_kernel(q_ref, k_ref, v_ref, z_ref, sm_ref, wq_ref, wk_ref, wv_ref, alog_ref, dtb_ref, gout_ref,
                     o_ref, xp_scr, state_scr):
    i = pl.program_id(2)
    C = DN_CHUNK
    TR = q_ref.shape[0]
    HALO = 8

    @pl.when(i == 0)
    def _():
        xp_scr[...] = jnp.zeros(xp_scr.shape, F32)
        state_scr[...] = jnp.zeros(state_scr.shape, F32)

    def conv_silu(slot, x_ref, w_ref):
        xp_scr[slot, 0:HALO, :] = xp_scr[slot, TR:TR + HALO, :]
        xp_scr[slot, HALO:HALO + TR, :] = x_ref[...]
        acc = w_ref[DN_CONV - 1:DN_CONV, :] * xp_scr[slot, HALO:HALO + TR, :]
        for j in range(1, DN_CONV):
            acc = acc + w_ref[DN_CONV - 1 - j:DN_CONV - j, :] * xp_scr[slot, HALO - j:HALO - j + TR, :]
        return _silu(acc)

    q_all = conv_silu(0, q_ref, wq_ref)
    k_all = conv_silu(1, k_ref, wk_ref)
    v_all = conv_silu(2, v_ref, wv_ref)

    sm = sm_ref[...]
    lane = lax.broadcasted_iota(jnp.int32, sm.shape, 1)
    row = lax.broadcasted_iota(jnp.int32, sm.shape, 0)
    beta_all = _sigmoid(sm)
    g_all = -jnp.exp(alog_ref[...]) * _softplus(sm + dtb_ref[...])
    pos = row % C
    gc_all = g_all
    shift = 1
    while shift < C:
        gc_all = gc_all + jnp.where(pos >= shift, pltpu.roll(gc_all, shift, 0), 0.0)
        shift *= 2
    gc_t = jnp.transpose(gc_all)
    row_t = lax.broadcasted_iota(jnp.int32, gc_t.shape, 0)

    G = DN_GROUP * C
    ri = lax.broadcasted_iota(jnp.int32, (G, G), 0)
    ci = lax.broadcasted_iota(jnp.int32, (G, G), 1)
    same = (ri // C) == (ci // C)
    tril = same & (ci <= ri)
    strict = same & (ci < ri)
    eye = (ci == ri).astype(F32)
    D = HEAD_DIM
    gout = gout_ref[...]

    heads = []
    for hh in range(DN_HPS):
        h = pl.program_id(1) * DN_HPS + hh
        sl = slice(hh * D, (hh + 1) * D)
        q, k, v = q_all[:, sl], k_all[:, sl], v_all[:, sl]
        q = q * lax.rsqrt(jnp.sum(q * q, axis=-1, keepdims=True) + 1e-6) * (D ** -0.5)
        k = k * lax.rsqrt(jnp.sum(k * k, axis=-1, keepdims=True) + 1e-6)
        beta = jnp.sum(jnp.where(lane == h, beta_all, 0.0), axis=1, keepdims=True)
        gc = jnp.sum(jnp.where(lane == DN_HEADS + h, gc_all, 0.0), axis=1, keepdims=True)
        gc_row = jnp.sum(jnp.where(row_t == DN_HEADS + h, gc_t, 0.0), axis=0, keepdims=True)
        heads.append((q, k, v, beta, gc, gc_row, jnp.transpose(k)))

    steps = [[None] * (TR // C) for _ in range(DN_HPS)]
    preps = []
    for hh in range(DN_HPS):
        for g in range(TR // G):
            preps.append(_deltanet_group(heads[hh], g, tril, strict, eye, steps[hh]))
    _round_robin(preps)

    outs = [[] for _ in range(DN_HPS)]
    _round_robin([_deltanet_chain(hh, steps[hh], state_scr, outs[hh]) for hh in range(DN_HPS)])
    for hh in range(DN_HPS):
        sl = slice(hh * D, (hh + 1) * D)
        o = jnp.concatenate(outs[hh], axis=0)
        y = o * lax.rsqrt(jnp.mean(o * o, axis=-1, keepdims=True) + NORM_EPS) * gout
        o_ref[:, sl] = (y * _silu(z_ref[:, sl])).astype(o_ref.dtype)


def _round_robin(gens):
    active = list(gens)
    while active:
        still = []
        for gen in active:
            try:
                next(gen)
                still.append(gen)
            except StopIteration:
                pass
        active = still


def _deltanet_group(head, g, tril, strict, eye, steps_out):
    C = DN_CHUNK
    D = HEAD_DIM
    G = DN_GROUP * C
    q, k, v, beta, gc, gc_row, k_t = head
    r0 = g * G
    qg, kg, vg = q[r0:r0 + G], k[r0:r0 + G], v[r0:r0 + G]
    bg = beta[r0:r0 + G]
    gcg = gc[r0:r0 + G]
    gcr = gc_row[:, r0:r0 + G]
    ktg = k_t[:, r0:r0 + G]
    decay = jnp.where(tril, jnp.exp(jnp.where(tril, gcg - gcr, 0.0)), 0.0)
    kb = kg * bg
    aq = _dot_bf(jnp.concatenate([kb, qg], axis=0), ktg)
    yield
    m_neg = jnp.where(strict, -(aq[:G] * decay), 0.0)
    qk = (aq[G:] * decay).astype(BF16)
    t_inv = eye + m_neg
    ph, pl_ = _split(m_neg)
    for _ in range(5):
        th, tl = _split(t_inv)
        ph, pl_ = _split(_dot3(ph, pl_, ph, pl_))
        yield
        t_inv = t_inv + _dot3(th, tl, ph, pl_)
        yield
    egc = jnp.exp(gcg)
    th, tl = _split(t_inv)
    rh, rl = _split(jnp.concatenate([kb * egc, vg * bg], axis=1))
    wu = _dot3(th, tl, rh, rl).astype(BF16)
    yield
    qr = jnp.dot(qk, wu, preferred_element_type=F32)
    qp = qg * egc - qr[:, :D]
    r_all = qr[:, D:]
    yield
    for c in range(DN_GROUP):
        c0 = c * C
        g_last = gcg[c0 + C - 1:c0 + C, :]
        kt_tail = ktg[:, c0:c0 + C] * jnp.exp(g_last - gcr[:, c0:c0 + C])
        gh = jnp.dot(kt_tail.astype(BF16), wu[c0:c0 + C, :], preferred_element_type=F32)
        lhs = jnp.concatenate([gh[:, :D], qp[c0:c0 + C]], axis=0).astype(BF16)
        steps_out[g * DN_GROUP + c] = (lhs, gh[:, D:], r_all[c0:c0 + C], jnp.exp(g_last))
        yield


def _deltanet_chain(hh, steps, state_scr, outs):
    D = HEAD_DIM
    state = state_scr[hh]
    for lhs, h_add, r_add, dec in steps:
        res = jnp.dot(lhs, state.astype(BF16), preferred_element_type=F32)
        outs.append(res[D:] + r_add)
        state = state * dec - res[:D] + h_add
        yield
    state_scr[hh] = state


def _deltanet(proj, conv_w, a_log, dt_bias, g_dn_out, B, S):
    T = B * S
    TR = min(DN_TILE_CHUNKS * DN_CHUNK, S)
    nt = S // TR
    pad = jnp.zeros((DN_HEADS,), F32)
    rest = jnp.zeros((LANES - 2 * DN_HEADS,), F32)
    alog_lane = jnp.concatenate([pad, a_log.astype(F32), rest]).reshape(1, LANES)
    dtb_lane = jnp.concatenate([pad, dt_bias.astype(F32), rest]).reshape(1, LANES)
    HW = DN_HPS * HEAD_DIM
    per = HW // LANES
    rows = lambda c0: (lambda b, h, i: (b * nt + i, c0 // per + h))
    wcol = lambda c0: (lambda b, h, i: (0, c0 // per + h))
    const = lambda b, h, i: (0, 0)
    return pl.pallas_call(
        _deltanet_kernel,
        grid=(B, DN_HEADS // DN_HPS, nt),
        in_specs=[pl.BlockSpec((TR, HW), rows(COL_DQ)),
                  pl.BlockSpec((TR, HW), rows(COL_DK)),
                  pl.BlockSpec((TR, HW), rows(COL_DV)),
                  pl.BlockSpec((TR, HW), rows(COL_DZ)),
                  pl.BlockSpec((TR, LANES), lambda b, h, i: (b * nt + i, COL_SMALL)),
                  pl.BlockSpec((DN_CONV, HW), wcol(0)),
                  pl.BlockSpec((DN_CONV, HW), wcol(DN_HEADS)),
                  pl.BlockSpec((DN_CONV, HW), wcol(2 * DN_HEADS)),
                  pl.BlockSpec((1, LANES), const),
                  pl.BlockSpec((1, LANES), const),
                  pl.BlockSpec((1, HEAD_DIM), const)],
        out_specs=pl.BlockSpec((TR, HW), lambda b, h, i: (b * nt + i, h)),
        out_shape=jax.ShapeDtypeStruct((T, DN_HEADS * HEAD_DIM), BF16),
        scratch_shapes=[pltpu.VMEM((3, TR + 8, HW), F32), pltpu.VMEM((DN_HPS, HEAD_DIM, HEAD_DIM), F32)],
        compiler_params=_cparams(("arbitrary", "arbitrary", "arbitrary")),
        name="deltanet",
    )(proj, proj, proj, proj, proj, conv_w, conv_w, conv_w, alog_lane, dtb_lane, g_dn_out.reshape(1, HEAD_DIM))


def _rms(v):
    return v * lax.rsqrt(jnp.mean(v * v, axis=-1, keepdims=True) + NORM_EPS)


def _merge_kernel(oa_ref, ob_ref, ga_ref, gb_ref, x_ref, mod_ref, gpost_ref, gpre_ref, wa_ref, wb_ref, wo_ref,
                  wrt_ref, br_ref, x1_ref, hp_ref, idx_ref, wrow_ref, rank_ref, cnt_ref, carry_scr):
    E = N_EXPERTS
    tm = x_ref.shape[0]
    half = x_ref.shape[1] // 2

    @pl.when(pl.program_id(0) == 0)
    def _():
        carry_scr[...] = jnp.zeros(carry_scr.shape, F32)

    ya = jnp.dot(oa_ref[...], wa_ref[...], preferred_element_type=F32)
    yb = jnp.dot(ob_ref[...], wb_ref[...], preferred_element_type=F32)
    merged = _sigmoid(ga_ref[...]) * ya + _sigmoid(gb_ref[...]) * yb
    mix = jnp.dot(merged.astype(BF16), wo_ref[...], preferred_element_type=F32)
    x1 = x_ref[...] + mod_ref[0, 2:3, :] * (_rms(mix) * gpost_ref[...])
    x1_ref[...] = x1
    h2 = (_rms(x1) * gpre_ref[...]) * (1.0 + mod_ref[0, 4:5, :]) + mod_ref[0, 3:4, :]

    lo_bits = pltpu.bitcast(h2[:, :half].astype(BF16).astype(F32), jnp.uint32) >> 16
    hi_bits = pltpu.bitcast(h2[:, half:].astype(BF16).astype(F32), jnp.uint32) & jnp.uint32(0xFFFF0000)
    hp_ref[...] = hi_bits | lo_bits

    lt = lax.dot_general(wrt_ref[...], h2, (((1,), (1,)), ((), ())), preferred_element_type=F32,
                         precision=HIGHEST) + br_ref[...]
    eid = lax.broadcasted_iota(jnp.int32, (E, tm), 0)
    vals, idxs = [], []
    for _ in range(TOP_K):
        m = jnp.max(lt, axis=0, keepdims=True)
        idx = jnp.min(jnp.where(lt == m, eid, E), axis=0, keepdims=True)
        vals.append(m)
        idxs.append(idx)
        lt = jnp.where(eid == idx, -jnp.inf, lt)
    exps = [jnp.exp(v - vals[0]) for v in vals]
    den = exps[0] + exps[1] + exps[2] + exps[3]
    wts = [e / den for e in exps]

    hot = jnp.zeros((E, tm), F32)
    for idx in idxs:
        hot = hot + (eid == idx).astype(F32)
    ti = lax.broadcasted_iota(jnp.int32, (tm, tm), 0)
    tj = lax.broadcasted_iota(jnp.int32, (tm, tm), 1)
    before = (ti < tj).astype(BF16)
    prior = carry_scr[...][:, 0:1] + jnp.dot(hot.astype(BF16), before, preferred_element_type=F32)
    row8 = lax.broadcasted_iota(jnp.int32, (8, tm), 0)
    row128 = lax.broadcasted_iota(jnp.int32, (LANES, tm), 0)
    idx8 = jnp.zeros((8, tm), jnp.int32)
    rank8 = jnp.zeros((8, tm), jnp.int32)
    w128 = jnp.zeros((LANES, tm), F32)
    for r in range(TOP_K):
        rank_r = jnp.sum(jnp.where(eid == idxs[r], prior, 0.0), axis=0, keepdims=True)
        idx8 = jnp.where(row8 == r, idxs[r], idx8)
        rank8 = jnp.where(row8 == r, rank_r.astype(jnp.int32), rank8)
        w128 = jnp.where(row128 == r, wts[r], w128)
    idx_ref[...] = idx8
    rank_ref[...] = rank8
    wrow_ref[...] = jnp.transpose(w128)
    carry = carry_scr[...] + jnp.sum(hot, axis=1, keepdims=True)
    carry_scr[...] = carry
    cnt_ref[...] = carry


def _merge(oa, ob, proj, x2, mod3, g_post_mix, g_pre_ffn, w_br_a, w_br_b, w_o, w_router, b_router, S):
    T, D = x2.shape
    E = N_EXPERTS
    tm = min(512, S)
    per_b = S // tm
    W = DN_HEADS * HEAD_DIM
    row = lambda i: (i, 0)
    const = lambda i: (0, 0)
    lane_t = lambda i: (0, i)
    return pl.pallas_call(
        _merge_kernel,
        grid=(T // tm,),
        in_specs=[pl.BlockSpec((tm, W), row),
                  pl.BlockSpec((tm, W), row),
                  pl.BlockSpec((tm, D), lambda i: (i, COL_GA * LANES // D)),
                  pl.BlockSpec((tm, D), lambda i: (i, COL_GB * LANES // D)),
                  pl.BlockSpec((tm, D), row),
                  pl.BlockSpec((1, 6, D), lambda i: (i // per_b, 0, 0)),
                  pl.BlockSpec((1, D), const),
                  pl.BlockSpec((1, D), const),
                  pl.BlockSpec((W, D), const),
                  pl.BlockSpec((W, D), const),
                  pl.BlockSpec((D, D), const),
                  pl.BlockSpec((E, D), const),
                  pl.BlockSpec((E, 1), const)],
        out_specs=[pl.BlockSpec((tm, D), row),
                   pl.BlockSpec((tm, D // 2), row),
                   pl.BlockSpec((8, tm), lane_t),
                   pl.BlockSpec((tm, LANES), row),
                   pl.BlockSpec((8, tm), lane_t),
                   pl.BlockSpec((E, LANES), const)],
        out_shape=[jax.ShapeDtypeStruct((T, D), F32),
                   jax.ShapeDtypeStruct((T, D // 2), jnp.uint32),
                   jax.ShapeDtypeStruct((8, T), jnp.int32),
                   jax.ShapeDtypeStruct((T, LANES), F32),
                   jax.ShapeDtypeStruct((8, T), jnp.int32),
                   jax.ShapeDtypeStruct((E, LANES), F32)],
        scratch_shapes=[pltpu.VMEM((E, LANES), F32)],
        compiler_params=_cparams(("arbitrary",)),
        name="merge",
    )(oa, ob, proj, proj, x2, mod3, g_post_mix.reshape(1, D), g_pre_ffn.reshape(1, D),
      w_br_a.astype(BF16), w_br_b.astype(BF16), w_o.astype(BF16),
      jnp.transpose(w_router).astype(F32), b_router.reshape(E, 1).astype(F32))


DISPATCH_TOKENS = 512


def _dispatch_kernel(fill_ref, dest_ref, src_ref, dst_ref, zero_scr, sem, zsem):
    n = dest_ref.shape[1]
    tm = zero_scr.shape[0]

    @pl.when(pl.program_id(0) == 0)
    def _():
        zero_scr[...] = jnp.zeros(zero_scr.shape, zero_scr.dtype)

        def fill_copy(k):
            return pltpu.make_async_copy(zero_scr, dst_ref.at[pl.ds(pl.multiple_of(k * tm, tm), tm)], zsem)

        def start(k, carry):
            pl.when(fill_ref[k] == 1)(lambda: fill_copy(k).start())
            return carry

        def wait(k, carry):
            pl.when(fill_ref[k] == 1)(lambda: fill_copy(k).wait())
            return carry

        lax.fori_loop(0, fill_ref.shape[0], start, 0)
        lax.fori_loop(0, fill_ref.shape[0], wait, 0)

    def issue(j, carry):
        for r in range(TOP_K):
            pltpu.make_async_copy(src_ref.at[pl.ds(j, 1)], dst_ref.at[pl.ds(dest_ref[r, j], 1)], sem).start()
        return carry

    lax.fori_loop(0, n, issue, 0, unroll=8)
    for r in range(TOP_K):
        pltpu.make_async_copy(src_ref, dst_ref.at[pl.ds(0, n)], sem).wait()


def _dispatch(dest, blk_fill, hp):
    T, Wd = hp.shape
    n = min(DISPATCH_TOKENS, T)
    tm = EXPERT_ROWS
    grid_spec = pltpu.PrefetchScalarGridSpec(
        num_scalar_prefetch=1,
        grid=(T // n,),
        in_specs=[pl.BlockSpec((TOP_K, n), lambda i, fill: (0, i), memory_space=pltpu.SMEM),
                  pl.BlockSpec((n, Wd), lambda i, fill: (i, 0))],
        out_specs=pl.BlockSpec(memory_space=pl.ANY),
        scratch_shapes=[pltpu.VMEM((tm, Wd), hp.dtype), pltpu.SemaphoreType.DMA(()), pltpu.SemaphoreType.DMA(())],
    )
    return pl.pallas_call(
        _dispatch_kernel,
        grid_spec=grid_spec,
        out_shape=jax.ShapeDtypeStruct((_expert_blocks(T * TOP_K) * tm, Wd), hp.dtype),
        compiler_params=_cparams(("arbitrary",)),
        name="dispatch",
    )(blk_fill, dest, hp)


EXPERT_ROWS = 512
EXPERT_FEATURE_TILE = 512


def _experts_kernel(exp_ref, valid_ref, new_ref, x_ref, wgu_ref, bgu_ref, wdn_ref, bdn_ref, o_ref, wgu_bf, wdn_bf):
    k = pl.program_id(0)
    F = wdn_ref.shape[0]

    @pl.when(new_ref[k] == 1)
    def _():
        wgu_bf[...] = wgu_ref[...].astype(BF16)
        wdn_bf[...] = wdn_ref[...].astype(BF16)

    def ffn():
        word = x_ref[...]
        x = jnp.concatenate([pltpu.bitcast(word << 16, F32).astype(BF16),
                             pltpu.bitcast(word & jnp.uint32(0xFFFF0000), F32).astype(BF16)], axis=1)

        def gate_up(c):
            ft = slice(c, c + EXPERT_FEATURE_TILE)
            ut = slice(F + c, F + c + EXPERT_FEATURE_TILE)
            return (jnp.dot(x, wgu_bf[:, ft], preferred_element_type=F32) + bgu_ref[:, ft],
                    jnp.dot(x, wgu_bf[:, ut], preferred_element_type=F32) + bgu_ref[:, ut])

        tiles = list(range(0, F, EXPERT_FEATURE_TILE))
        y = None
        nxt = gate_up(tiles[0])
        for t, c in enumerate(tiles):
            gate, up = nxt
            if t + 1 < len(tiles):
                nxt = gate_up(tiles[t + 1])
            gate = jnp.minimum(gate, SWIGLU_LIMIT)
            up = jnp.clip(up, -SWIGLU_LIMIT, SWIGLU_LIMIT)
            act = (up + 1.0) * gate * _sigmoid(SWIGLU_ALPHA * gate)
            part = jnp.dot(act.astype(BF16), wdn_bf[c:c + EXPERT_FEATURE_TILE, :], preferred_element_type=F32)
            y = part + bdn_ref[...] if y is None else y + part
        return y

    @pl.when(valid_ref[k] == 1)
    def _():
        o_ref[:, 0, :] = ffn()

    @pl.when(valid_ref[k] == 0)
    def _():
        o_ref[...] = jnp.zeros(o_ref.shape, o_ref.dtype)


def _experts(xs, blk_exp, blk_valid, blk_new, w_gu, b_gu, w_down, b_down):
    R, half = xs.shape
    E, D, F2 = w_gu.shape
    F = F2 // 2
    tm = EXPERT_ROWS
    grid_spec = pltpu.PrefetchScalarGridSpec(
        num_scalar_prefetch=3,
        grid=(R // tm,),
        in_specs=[pl.BlockSpec((tm, half), lambda k, ex, va, nw: (k, 0)),
                  pl.BlockSpec((None, D, F2), lambda k, ex, va, nw: (ex[k], 0, 0)),
                  pl.BlockSpec((None, 1, F2), lambda k, ex, va, nw: (ex[k], 0, 0)),
                  pl.BlockSpec((None, F, D), lambda k, ex, va, nw: (ex[k], 0, 0)),
                  pl.BlockSpec((None, 1, D), lambda k, ex, va, nw: (ex[k], 0, 0))],
        out_specs=pl.BlockSpec((tm, 1, D), lambda k, ex, va, nw: (k, 0, 0)),
        scratch_shapes=[pltpu.VMEM((D, F2), BF16), pltpu.VMEM((F, D), BF16)],
    )
    return pl.pallas_call(
        _experts_kernel,
        grid_spec=grid_spec,
        out_shape=jax.ShapeDtypeStruct((R, 1, D), F32),
        compiler_params=_cparams(("arbitrary",)),
        name="experts",
    )(blk_exp, blk_valid, blk_new, xs, w_gu, b_gu.reshape(E, 1, F2).astype(F32),
      w_down, b_down.reshape(E, 1, D).astype(F32))


def _expert_blocks(A):
    return A // EXPERT_ROWS + N_EXPERTS


def _block_tables(counts, A):
    E = N_EXPERTS
    tm = EXPERT_ROWS
    n_blk = _expert_blocks(A)
    nblk_e = (counts + tm - 1) // tm
    blk_end = jnp.cumsum(nblk_e)
    blk_start = blk_end - nblk_e
    used = blk_end[-1]
    k = jnp.arange(n_blk, dtype=jnp.int32)
    kk = jnp.minimum(k, used - 1)
    e = jnp.sum((blk_end[None, :] <= kk[:, None]).astype(jnp.int32), axis=1)
    hot = e[:, None] == jnp.arange(E, dtype=jnp.int32)[None, :]
    pick = lambda table: jnp.sum(jnp.where(hot, table[None, :], 0), axis=1)
    valid = k < used
    new = valid & (k == pick(blk_start))
    fill = jnp.logical_not(valid) | (k == pick(blk_end) - 1)
    i32 = lambda v: v.astype(jnp.int32)
    return blk_start * tm, i32(e), i32(valid), i32(new), i32(fill)


COMBINE_TOKENS = 512


def _combine_kernel(dest_ref, nxt_ref, y_ref, wrow_ref, x1_ref, mod_ref, gpost_ref, o_ref, stage, moe_scr, sem):
    i = pl.program_id(0)
    last = pl.num_programs(0) - 1
    n = x1_ref.shape[0]
    SUB = 8

    def gather(idx_ref, j, s):
        for r in range(TOP_K):
            pltpu.make_async_copy(y_ref.at[idx_ref[r, j]], stage.at[s, r, j], sem.at[s]).start(priority=r % 2)

    @pl.when(i == 0)
    def _():
        def prime(j, carry):
            gather(dest_ref, j, 0)
            return carry
        lax.fori_loop(0, n, prime, 0, unroll=SUB)

    def step(slot):
        for r in range(TOP_K):
            pltpu.make_async_copy(y_ref.at[pl.ds(0, n)], stage.at[slot, r], sem.at[slot]).wait()

        def reduce_group(g):
            rows = pl.ds(pl.multiple_of(g * SUB, SUB), SUB)
            w = wrow_ref[rows, :]
            acc = w[:, 0:1] * stage[slot, 0, rows, 0, :]
            for r in range(1, TOP_K):
                acc = acc + w[:, r:r + 1] * stage[slot, r, rows, 0, :]
            moe_scr[rows, :] = acc

        @pl.when(i < last)
        def _():
            def body(g, carry):
                for t in range(SUB):
                    gather(nxt_ref, g * SUB + t, 1 - slot)
                reduce_group(g)
                return carry
            lax.fori_loop(0, n // SUB, body, 0)

        @pl.when(i == last)
        def _():
            def body(g, carry):
                reduce_group(g)
                return carry
            lax.fori_loop(0, n // SUB, body, 0)

    for parity in range(2):
        pl.when(i % 2 == parity)(functools.partial(step, parity))

    o_ref[...] = x1_ref[...] + mod_ref[0, 5:6, :] * (_rms(moe_scr[...]) * gpost_ref[...])


def _combine(dest, y, wrow, x1, mod3, g_post_ffn, S):
    T, D = x1.shape
    n = min(COMBINE_TOKENS, S)
    per_b = S // n
    steps = T // n
    return pl.pallas_call(
        _combine_kernel,
        grid=(steps,),
        in_specs=[pl.BlockSpec((TOP_K, n), lambda i: (0, i), memory_space=pltpu.SMEM),
                  pl.BlockSpec((TOP_K, n), lambda i: (0, jnp.minimum(i + 1, steps - 1)), memory_space=pltpu.SMEM),
                  pl.BlockSpec(memory_space=pl.ANY),
                  pl.BlockSpec((n, LANES), lambda i: (i, 0)),
                  pl.BlockSpec((n, D), lambda i: (i, 0)),
                  pl.BlockSpec((1, 6, D), lambda i: (i // per_b, 0, 0)),
                  pl.BlockSpec((1, D), lambda i: (0, 0))],
        out_specs=pl.BlockSpec((n, D), lambda i: (i, 0)),
        out_shape=jax.ShapeDtypeStruct((T, D), F32),
        scratch_shapes=[pltpu.VMEM((2, TOP_K, n, 1, D), F32), pltpu.VMEM((n, D), F32),
                        pltpu.SemaphoreType.DMA((2,))],
        compiler_params=_cparams(("arbitrary",)),
        name="combine",
    )(dest, dest, y, wrow, x1, mod3, g_post_ffn.reshape(1, D))


def _regroup_w_in(w_in):
    D = w_in.shape[0]
    dw = DN_HEADS * HEAD_DIM
    mw = MB_HEADS * HEAD_DIM
    cuts = np.cumsum([dw, dw, dw, dw, DN_HEADS, DN_HEADS, mw, mw, mw, D, D])[:-1]
    dq, dk, dv, dz, db, da, mq, mk, mv, ga, gb = jnp.split(w_in, [int(c) for c in cuts], axis=1)
    small = jnp.concatenate([db, da, jnp.zeros((D, LANES - 2 * DN_HEADS), w_in.dtype)], axis=1)
    return jnp.concatenate([p.astype(BF16) for p in (ga, gb, dq, dk, dv, dz, mq, mk, mv, small)], axis=1)


def kernel(x, c, w_ada, b_ada, g_pre_mix, g_post_mix, g_pre_ffn, g_post_ffn, w_in, conv_w, a_log, dt_bias,
           g_dn_out, w_br_a, w_br_b, w_o, w_router, b_router, w_gu, b_gu, w_down, b_down):
    B, S, D = x.shape
    l = 0
    mod = _adaln(c, w_ada[l], b_ada[l]).reshape(B, 6, D)
    proj = _inproj(x, mod, g_pre_mix[l], _regroup_w_in(w_in[l]))
    qr, kr, vb, kmean = _mobaprep(proj, B, S)
    ob = _moba(qr, kr, vb, kmean, B, S)
    oa = _deltanet(proj, conv_w[l], a_log[l], dt_bias[l], g_dn_out[l], B, S)
    x1, hp, idx8, wrow, rank8, cnt = _merge(oa, ob, proj, x.reshape(B * S, D), mod, g_post_mix[l], g_pre_ffn[l],
                                            w_br_a[l], w_br_b[l], w_o[l], w_router[l], b_router[l], S)
    out = _moe(x1, hp, idx8, wrow, rank8, cnt, mod, g_post_ffn[l], w_gu[l], b_gu[l], w_down[l], b_down[l], S)
    return out.reshape(B, S, D)


def _moe(x1, hp, idx8, wrow, rank8, cnt, mod, g_post_ffn, w_gu, b_gu, w_down, b_down, S):
    T = x1.shape[0]
    counts = cnt[:, 0].astype(jnp.int32)
    start, blk_exp, blk_valid, blk_new, blk_fill = _block_tables(counts, T * TOP_K)
    hot = idx8[:TOP_K, :, None] == jnp.arange(N_EXPERTS, dtype=jnp.int32)
    dest = rank8[:TOP_K] + jnp.sum(jnp.where(hot, start, 0), axis=-1)
    xs = _dispatch(dest, blk_fill, hp)
    y = _experts(xs, blk_exp, blk_valid, blk_new, w_gu, b_gu, w_down, b_down)
    return _combine(dest, y, wrow, x1, mod, g_post_ffn, S)
```

```python
import functools

import jax
import jax.numpy as jnp
import numpy as np
from jax import lax
from jax.experimental import pallas as pl
from jax.experimental.pallas import tpu as pltpu

F32 = jnp.float32
BF16 = jnp.bfloat16
HIGHEST = lax.Precision.HIGHEST

HEAD_DIM = 128
DN_HEADS = 4
DN_CONV = 4
DN_CHUNK = 64
MB_HEADS = 4
MB_BLOCK = 256
MB_TOPK = 3
ROPE_THETA = 500000.0
ROPE_DIM = HEAD_DIM // 4
N_EXPERTS = 32
TOP_K = 4
SWIGLU_LIMIT = 7.0
SWIGLU_ALPHA = 1.702
NORM_EPS = 1e-6
LANES = 128
NEG_BIG = -1e30
MB_Q_SCALE = HEAD_DIM ** -0.5 * float(np.log2(np.e))

_W = DN_HEADS * HEAD_DIM // LANES
COL_GA = 0
COL_GB = 8
COL_DQ = 16
COL_DK = 20
COL_DV = 24
COL_DZ = 28
COL_MQ = 32
COL_MK = 36
COL_MV = 40
N_PROJ = 44 * LANES

VMEM_LIMIT = 56 * 1024 * 1024


def _cparams(sem):
    return pltpu.CompilerParams(dimension_semantics=sem, vmem_limit_bytes=VMEM_LIMIT)


def _sigmoid(v):
    return 0.5 * jnp.tanh(0.5 * v) + 0.5


def _silu(v):
    return v * _sigmoid(v)


def _adaln_kernel(c_ref, w_ref, b_ref, o_ref):
    a = _silu(c_ref[...])
    o_ref[...] = jnp.dot(a, w_ref[...], preferred_element_type=F32, precision=HIGHEST) + b_ref[...]


def _adaln(c, w_ada, b_ada):
    B, D = c.shape
    N = w_ada.shape[1]
    tn = D
    return pl.pallas_call(
        _adaln_kernel,
        grid=(N // tn,),
        in_specs=[pl.BlockSpec((B, D), lambda j: (0, 0)),
                  pl.BlockSpec((D, tn), lambda j: (0, j)),
                  pl.BlockSpec((1, tn), lambda j: (0, j))],
        out_specs=pl.BlockSpec((B, tn), lambda j: (0, j)),
        out_shape=jax.ShapeDtypeStruct((B, N), F32),
        compiler_params=_cparams(("arbitrary",)),
        name="adaln",
    )(c, w_ada, b_ada.reshape(1, N))


def _inproj_kernel(x_ref, mod_ref, g_ref, w_ref, ws_ref, o_ref, os_ref, h_scr):
    @pl.when(pl.program_id(2) == 0)
    def _():
        x = x_ref[0]
        y = x * lax.rsqrt(jnp.mean(x * x, axis=-1, keepdims=True) + NORM_EPS) * g_ref[...]
        h = y * (1.0 + mod_ref[0, 1:2, :]) + mod_ref[0, 0:1, :]
        h_scr[...] = h.astype(BF16)
        os_ref[...] = jnp.dot(h_scr[...], ws_ref[...], preferred_element_type=F32)

    o_ref[...] = jnp.dot(h_scr[...], w_ref[...], preferred_element_type=F32)


def _inproj(x, mod3, g_pre, w_main, w_small):
    B, S, D = x.shape
    tm = min(1024, S)
    tn = N_PROJ // 2
    nrow = S // tm
    return pl.pallas_call(
        _inproj_kernel,
        grid=(B, nrow, N_PROJ // tn),
        in_specs=[pl.BlockSpec((1, tm, D), lambda b, i, j: (b, i, 0)),
                  pl.BlockSpec((1, 6, D), lambda b, i, j: (b, 0, 0)),
                  pl.BlockSpec((1, D), lambda b, i, j: (0, 0)),
                  pl.BlockSpec((D, tn), lambda b, i, j: (0, j)),
                  pl.BlockSpec((D, LANES), lambda b, i, j: (0, 0))],
        out_specs=[pl.BlockSpec((tm, tn), lambda b, i, j: (b * nrow + i, j)),
                   pl.BlockSpec((tm, LANES), lambda b, i, j: (b * nrow + i, 0))],
        out_shape=[jax.ShapeDtypeStruct((B * S, N_PROJ), F32),
                   jax.ShapeDtypeStruct((B * S, LANES), F32)],
        scratch_shapes=[pltpu.VMEM((tm, D), BF16)],
        compiler_params=_cparams(("arbitrary", "arbitrary", "arbitrary")),
        name="inproj",
    )(x, mod3, g_pre.reshape(1, D), w_main, w_small)


def _rope(v, cosf, sinf, lane):
    rot = jnp.where(lane < ROPE_DIM // 2, pltpu.roll(v, LANES - ROPE_DIM // 2, 1), pltpu.roll(v, ROPE_DIM // 2, 1))
    return v * cosf + rot * sinf


def _mobaprep_kernel(q_ref, k_ref, v_ref, cos_ref, sin_ref, qo_ref, ko_ref, vo_ref, km_ref):
    cosf = cos_ref[...]
    sinf = sin_ref[...]
    lane = lax.broadcasted_iota(jnp.int32, cosf.shape, 1)
    for h in range(MB_HEADS):
        sl = slice(h * HEAD_DIM, (h + 1) * HEAD_DIM)
        qo_ref[:, sl] = (_rope(q_ref[:, sl], cosf, sinf, lane) * MB_Q_SCALE).astype(BF16)
        kr = _rope(k_ref[:, sl], cosf, sinf, lane)
        ko_ref[:, sl] = kr.astype(BF16)
        for blk in range(km_ref.shape[0]):
            km_ref[blk, :, sl] = jnp.mean(kr[blk * MB_BLOCK:(blk + 1) * MB_BLOCK], axis=0, keepdims=True)
    vo_ref[...] = v_ref[...].astype(BF16)


def _rope_tables(S):
    half = ROPE_DIM // 2
    inv_freq = ROPE_THETA ** (-jnp.arange(half, dtype=F32) / half)
    ang = jnp.arange(S, dtype=F32)[:, None] * inv_freq[None, :]
    cos, sin = jnp.cos(ang), jnp.sin(ang)
    rest = HEAD_DIM - ROPE_DIM
    cosf = jnp.concatenate([cos, cos, jnp.ones((S, rest), F32)], axis=1)
    sinf = jnp.concatenate([-sin, sin, jnp.zeros((S, rest), F32)], axis=1)
    return cosf, sinf


def _mobaprep(proj, B, S):
    T = B * S
    tb = min(4 * MB_BLOCK, S)
    nb = S // tb
    W = MB_HEADS * HEAD_DIM
    cosf, sinf = _rope_tables(S)
    col = lambda c: (lambda i: (i, c))
    return pl.pallas_call(
        _mobaprep_kernel,
        grid=(T // tb,),
        in_specs=[pl.BlockSpec((tb, W), col(COL_MQ // _W)),
                  pl.BlockSpec((tb, W), col(COL_MK // _W)),
                  pl.BlockSpec((tb, W), col(COL_MV // _W)),
                  pl.BlockSpec((tb, LANES), lambda i: (i % nb, 0)),
                  pl.BlockSpec((tb, LANES), lambda i: (i % nb, 0))],
        out_specs=[pl.BlockSpec((tb, W), lambda i: (i, 0)),
                   pl.BlockSpec((tb, W), lambda i: (i, 0)),
                   pl.BlockSpec((tb, W), lambda i: (i, 0)),
                   pl.BlockSpec((tb // MB_BLOCK, 1, W), lambda i: (i, 0, 0))],
        out_shape=[jax.ShapeDtypeStruct((T, W), BF16),
                   jax.ShapeDtypeStruct((T, W), BF16),
                   jax.ShapeDtypeStruct((T, W), BF16),
                   jax.ShapeDtypeStruct((T // MB_BLOCK, 1, W), F32)],
        compiler_params=_cparams(("arbitrary",)),
        name="mobaprep",
    )(proj, proj, proj, cosf, sinf)


MB_HPS = 4


def _moba_kernel(q_ref, k_ref, v_ref, km_ref, o_ref, s_scr, sd_scr, mx_scr, l_scr, acc_scr):
    qi = pl.program_id(1)
    tb = MB_BLOCK
    nb = km_ref.shape[0]
    nt = tb // LANES
    D = HEAD_DIM
    heads = range(MB_HPS)
    hs = [slice(h * D, (h + 1) * D) for h in heads]
    qs = [q_ref[:, hs[h]] for h in heads]

    sts = [lax.dot_general(km_ref[:, hs[h]], qs[h].astype(F32), (((1,), (1,)), ((), ())),
                           preferred_element_type=F32, precision=HIGHEST) for h in heads]
    blk = lax.broadcasted_iota(jnp.int32, (nb, tb), 0)
    rowid = lax.broadcasted_iota(jnp.int32, (LANES, tb), 0)
    sel_ts = []
    for h in heads:
        st = jnp.where(blk < qi, sts[h], -jnp.inf)
        sel_t = jnp.full((LANES, tb), -1.0, F32)
        for r in range(MB_TOPK):
            m = jnp.max(st, axis=0, keepdims=True)
            idx = jnp.min(jnp.where(st == m, blk, nb), axis=0, keepdims=True)
            sel_t = jnp.where(rowid == r, jnp.where(r < qi, idx, -1).astype(F32), sel_t)
            st = jnp.where(blk == idx, -jnp.inf, st)
        sel_ts.append(sel_t)
    sels = [jnp.transpose(t) for t in sel_ts]
    sel_rep = [[jnp.broadcast_to(sels[h][:, r:r + 1], (tb, LANES)) for r in range(MB_TOPK)] for h in heads]

    def logits(h, start, width):
        kslab = k_ref[pl.ds(pl.multiple_of(start, tb), width), hs[h]]
        return lax.dot_general(qs[h], kslab, (((1,), (1,)), ((), ())), preferred_element_type=F32)

    r_i = lax.broadcasted_iota(jnp.int32, (tb, LANES), 0)
    c_i = lax.broadcasted_iota(jnp.int32, (tb, LANES), 1)
    sds = [logits(h, qi * tb, tb) for h in heads]
    for h in heads:
        mx = jnp.full((tb, LANES), NEG_BIG, F32)
        for t in range(nt):
            piece = jnp.where(c_i + t * LANES <= r_i, sds[h][:, t * LANES:(t + 1) * LANES], NEG_BIG)
            sd_scr[h, :, t * LANES:(t + 1) * LANES] = piece
            mx = jnp.maximum(mx, piece)
        mx_scr[h] = mx

    n_pairs = (qi + 1) // 2

    def pass1(i, carry):
        kb0 = 2 * i
        s2s = [logits(h, kb0 * tb, 2 * tb) for h in heads]
        for h in heads:
            mx = mx_scr[h]
            for half in range(2):
                kbf = (kb0 + half).astype(F32)
                hit = (sel_rep[h][0] == kbf) | (sel_rep[h][1] == kbf) | (sel_rep[h][2] == kbf)
                for t in range(nt):
                    c0 = (half * nt + t) * LANES
                    piece = jnp.where(hit, s2s[h][:, c0:c0 + LANES], NEG_BIG)
                    s_scr[h, i, :, c0:c0 + LANES] = piece
                    mx = jnp.maximum(mx, piece)
            mx_scr[h] = mx
        return carry

    lax.fori_loop(0, n_pairs, pass1, 0)
    m_reps = [jnp.broadcast_to(jnp.max(mx_scr[h], axis=-1, keepdims=True), (tb, LANES)) for h in heads]

    def probs(load, width, m_rep):
        ps, lsum = [], jnp.zeros((tb, LANES), F32)
        for t in range(width // LANES):
            p = jnp.exp2(load(t) - m_rep)
            lsum = lsum + p
            ps.append(p.astype(BF16))
        return jnp.concatenate(ps, axis=1), lsum

    pls = [probs(lambda t, h=h: sd_scr[h, :, t * LANES:(t + 1) * LANES], tb, m_reps[h]) for h in heads]
    for h in heads:
        l_scr[h] = pls[h][1]
        acc_scr[h] = jnp.dot(pls[h][0], v_ref[pl.ds(pl.multiple_of(qi * tb, tb), tb), hs[h]],
                             preferred_element_type=F32)

    def pass2(i, carry):
        pls = [probs(lambda t, h=h: s_scr[h, i, :, t * LANES:(t + 1) * LANES], 2 * tb, m_reps[h]) for h in heads]
        for h in heads:
            vslab = v_ref[pl.ds(pl.multiple_of(2 * i * tb, tb), 2 * tb), hs[h]]
            l_scr[h] = l_scr[h] + pls[h][1]
            acc_scr[h] = acc_scr[h] + jnp.dot(pls[h][0], vslab, preferred_element_type=F32)
        return carry

    lax.fori_loop(0, n_pairs, pass2, 0)
    for h in heads:
        o_ref[:, hs[h]] = (acc_scr[h] / jnp.sum(l_scr[h], axis=-1, keepdims=True)).astype(o_ref.dtype)


def _moba(qr, kr, vb, kmean, B, S):
    tb = MB_BLOCK
    nb = S // tb
    T = B * S
    km = kmean.reshape(B, nb, MB_HEADS * HEAD_DIM)
    HW = MB_HPS * HEAD_DIM
    ng = MB_HEADS // MB_HPS
    return pl.pallas_call(
        _moba_kernel,
        grid=(B * ng, nb),
        in_specs=[pl.BlockSpec((tb, HW), lambda g, i: ((g // ng) * nb + i, g % ng)),
                  pl.BlockSpec((S, HW), lambda g, i: (g // ng, g % ng)),
                  pl.BlockSpec((S, HW), lambda g, i: (g // ng, g % ng)),
                  pl.BlockSpec((None, nb, HW), lambda g, i: (g // ng, 0, g % ng))],
        out_specs=pl.BlockSpec((tb, HW), lambda g, i: ((g // ng) * nb + i, g % ng)),
        out_shape=jax.ShapeDtypeStruct((T, MB_HEADS * HEAD_DIM), BF16),
        scratch_shapes=[pltpu.VMEM((MB_HPS, (nb + 1) // 2, tb, 2 * tb), F32),
                        pltpu.VMEM((MB_HPS, tb, tb), F32),
                        pltpu.VMEM((MB_HPS, tb, LANES), F32),
                        pltpu.VMEM((MB_HPS, tb, LANES), F32),
                        pltpu.VMEM((MB_HPS, tb, HEAD_DIM), F32)],
        compiler_params=_cparams(("arbitrary", "arbitrary")),
        name="moba",
    )(qr, kr, vb, km)


DN_TILE_CHUNKS = 8


def _softplus(v):
    return jnp.maximum(v, 0.0) + jnp.log1p(jnp.exp(-jnp.abs(v)))


DN_GROUP = 2
DN_HPS = 4


def _split(a):
    hi = pltpu.bitcast(pltpu.bitcast(a, jnp.uint32) & jnp.uint32(0xFFFF0000), F32)
    return hi.astype(BF16), (a - hi).astype(BF16)


def _dot3(ah, al, bh, bl):
    lhs = jnp.concatenate([ah, ah, al], axis=1)
    rhs = jnp.concatenate([bh, bl, bh], axis=0)
    return jnp.dot(lhs, rhs, preferred_element_type=F32)


def _dot_bf(a, b):
    return jnp.dot(a.astype(BF16), b.astype(BF16), preferred_element_type=F32)


def _deltanet_kernel(q_ref, k_ref, v_ref, z_ref, sm_ref, wq_ref, wk_ref, wv_ref, alog_ref, dtb_ref, gout_ref,
                     o_ref, xp_scr, state_scr):
    i = pl.program_id(2)
    C = DN_CHUNK
    TR = q_ref.shape[0]
    HALO = 8

    @pl.when(i == 0)
    def _():
        xp_scr[...] = jnp.zeros(xp_scr.shape, F32)
        state_scr[...] = jnp.zeros(state_scr.shape, F32)

    def conv_silu(slot, x_ref, w_ref):
        xp_scr[slot, 0:HALO, :] = xp_scr[slot, TR:TR + HALO, :]
        xp_scr[slot, HALO:HALO + TR, :] = x_ref[...]
        acc = w_ref[DN_CONV - 1:DN_CONV, :] * xp_scr[slot, HALO:HALO + TR, :]
        for j in range(1, DN_CONV):
            acc = acc + w_ref[DN_CONV - 1 - j:DN_CONV - j, :] * xp_scr[slot, HALO - j:HALO - j + TR, :]
        return _silu(acc)

    q_all = conv_silu(0, q_ref, wq_ref)
    k_all = conv_silu(1, k_ref, wk_ref)
    v_all = conv_silu(2, v_ref, wv_ref)

    sm = sm_ref[...]
    lane = lax.broadcasted_iota(jnp.int32, sm.shape, 1)
    row = lax.broadcasted_iota(jnp.int32, sm.shape, 0)
    beta_all = _sigmoid(sm)
    g_all = -jnp.exp(alog_ref[...]) * _softplus(sm + dtb_ref[...])
    pos = row % C
    gc_all = g_all
    shift = 1
    while shift < C:
        gc_all = gc_all + jnp.where(pos >= shift, pltpu.roll(gc_all, shift, 0), 0.0)
        shift *= 2
    gc_t = jnp.transpose(gc_all)
    row_t = lax.broadcasted_iota(jnp.int32, gc_t.shape, 0)

    G = DN_GROUP * C
    ri = lax.broadcasted_iota(jnp.int32, (G, G), 0)
    ci = lax.broadcasted_iota(jnp.int32, (G, G), 1)
    same = (ri // C) == (ci // C)
    tril = same & (ci <= ri)
    strict = same & (ci < ri)
    eye = (ci == ri).astype(F32)
    D = HEAD_DIM
    gout = gout_ref[...]

    heads = []
    for hh in range(DN_HPS):
        h = pl.program_id(1) * DN_HPS + hh
        sl = slice(hh * D, (hh + 1) * D)
        q, k, v = q_all[:, sl], k_all[:, sl], v_all[:, sl]
        q = q * lax.rsqrt(jnp.sum(q * q, axis=-1, keepdims=True) + 1e-6) * (D ** -0.5)
        k = k * lax.rsqrt(jnp.sum(k * k, axis=-1, keepdims=True) + 1e-6)
        beta = jnp.sum(jnp.where(lane == h, beta_all, 0.0), axis=1, keepdims=True)
        gc = jnp.sum(jnp.where(lane == DN_HEADS + h, gc_all, 0.0), axis=1, keepdims=True)
        gc_row = jnp.sum(jnp.where(row_t == DN_HEADS + h, gc_t, 0.0), axis=0, keepdims=True)
        heads.append((q, k, v, beta, gc, gc_row, jnp.transpose(k)))

    steps = [[None] * (TR // C) for _ in range(DN_HPS)]
    preps = []
    for hh in range(DN_HPS):
        for g in range(TR // G):
            preps.append(_deltanet_group(heads[hh], g, tril, strict, eye, steps[hh]))
    _round_robin(preps)

    outs = [[] for _ in range(DN_HPS)]
    _round_robin([_deltanet_chain(hh, steps[hh], state_scr, outs[hh]) for hh in range(DN_HPS)])
    for hh in range(DN_HPS):
        sl = slice(hh * D, (hh + 1) * D)
        o = jnp.concatenate(outs[hh], axis=0)
        y = o * lax.rsqrt(jnp.mean(o * o, axis=-1, keepdims=True) + NORM_EPS) * gout
        o_ref[:, sl] = (y * _silu(z_ref[:, sl])).astype(o_ref.dtype)


def _round_robin(gens):
    active = list(gens)
    while active:
        still = []
        for gen in active:
            try:
                next(gen)
                still.append(gen)
            except StopIteration:
                pass
        active = still


def _deltanet_group(head, g, tril, strict, eye, steps_out):
    C = DN_CHUNK
    D = HEAD_DIM
    G = DN_GROUP * C
    q, k, v, beta, gc, gc_row, k_t = head
    r0 = g * G
    qg, kg, vg = q[r0:r0 + G], k[r0:r0 + G], v[r0:r0 + G]
    bg = beta[r0:r0 + G]
    gcg = gc[r0:r0 + G]
    gcr = gc_row[:, r0:r0 + G]
    ktg = k_t[:, r0:r0 + G]
    decay = jnp.where(tril, jnp.exp(jnp.where(tril, gcg - gcr, 0.0)), 0.0)
    kb = kg * bg
    aq = _dot_bf(jnp.concatenate([kb, qg], axis=0), ktg)
    yield
    m_neg = jnp.where(strict, -(aq[:G] * decay), 0.0)
    qk = (aq[G:] * decay).astype(BF16)
    t_inv = eye + m_neg
    ph, pl_ = _split(m_neg)
    for _ in range(5):
        th, tl = _split(t_inv)
        ph, pl_ = _split(_dot3(ph, pl_, ph, pl_))
        yield
        t_inv = t_inv + _dot3(th, tl, ph, pl_)
        yield
    egc = jnp.exp(gcg)
    th, tl = _split(t_inv)
    rh, rl = _split(jnp.concatenate([kb * egc, vg * bg], axis=1))
    wu = _dot3(th, tl, rh, rl).astype(BF16)
    yield
    qr = jnp.dot(qk, wu, preferred_element_type=F32)
    qp = qg * egc - qr[:, :D]
    r_all = qr[:, D:]
    yield
    for c in range(DN_GROUP):
        c0 = c * C
        g_last = gcg[c0 + C - 1:c0 + C, :]
        kt_tail = ktg[:, c0:c0 + C] * jnp.exp(g_last - gcr[:, c0:c0 + C])
        gh = jnp.dot(kt_tail.astype(BF16), wu[c0:c0 + C, :], preferred_element_type=F32)
        lhs = jnp.concatenate([gh[:, :D], qp[c0:c0 + C]], axis=0).astype(BF16)
        steps_out[g * DN_GROUP + c] = (lhs, gh[:, D:], r_all[c0:c0 + C], jnp.exp(g_last))
        yield


def _deltanet_chain(hh, steps, state_scr, outs):
    D = HEAD_DIM
    state = state_scr[hh]
    for lhs, h_add, r_add, dec in steps:
        res = jnp.dot(lhs, state.astype(BF16), preferred_element_type=F32)
        outs.append(res[D:] + r_add)
        state = state * dec - res[:D] + h_add
        yield
    state_scr[hh] = state


def _deltanet(proj, small, conv_w, a_log, dt_bias, g_dn_out, B, S):
    T = B * S
    TR = min(DN_TILE_CHUNKS * DN_CHUNK, S)
    nt = S // TR
    pad = jnp.zeros((DN_HEADS,), F32)
    rest = jnp.zeros((LANES - 2 * DN_HEADS,), F32)
    alog_lane = jnp.concatenate([pad, a_log.astype(F32), rest]).reshape(1, LANES)
    dtb_lane = jnp.concatenate([pad, dt_bias.astype(F32), rest]).reshape(1, LANES)
    HW = DN_HPS * HEAD_DIM
    per = HW // LANES
    rows = lambda c0: (lambda b, h, i: (b * nt + i, c0 // per + h))
    wcol = lambda c0: (lambda b, h, i: (0, c0 // per + h))
    const = lambda b, h, i: (0, 0)
    return pl.pallas_call(
        _deltanet_kernel,
        grid=(B, DN_HEADS // DN_HPS, nt),
        in_specs=[pl.BlockSpec((TR, HW), rows(COL_DQ)),
                  pl.BlockSpec((TR, HW), rows(COL_DK)),
                  pl.BlockSpec((TR, HW), rows(COL_DV)),
                  pl.BlockSpec((TR, HW), rows(COL_DZ)),
                  pl.BlockSpec((TR, LANES), lambda b, h, i: (b * nt + i, 0)),
                  pl.BlockSpec((DN_CONV, HW), wcol(0)),
                  pl.BlockSpec((DN_CONV, HW), wcol(DN_HEADS)),
                  pl.BlockSpec((DN_CONV, HW), wcol(2 * DN_HEADS)),
                  pl.BlockSpec((1, LANES), const),
                  pl.BlockSpec((1, LANES), const),
                  pl.BlockSpec((1, HEAD_DIM), const)],
        out_specs=pl.BlockSpec((TR, HW), lambda b, h, i: (b * nt + i, h)),
        out_shape=jax.ShapeDtypeStruct((T, DN_HEADS * HEAD_DIM), BF16),
        scratch_shapes=[pltpu.VMEM((3, TR + 8, HW), F32), pltpu.VMEM((DN_HPS, HEAD_DIM, HEAD_DIM), F32)],
        compiler_params=_cparams(("arbitrary", "arbitrary", "arbitrary")),
        name="deltanet",
    )(proj, proj, proj, proj, small, conv_w, conv_w, conv_w, alog_lane, dtb_lane, g_dn_out.reshape(1, HEAD_DIM))


def _rms(v):
    return v * lax.rsqrt(jnp.mean(v * v, axis=-1, keepdims=True) + NORM_EPS)


def _merge_kernel(oa_ref, ob_ref, ga_ref, gb_ref, x_ref, mod_ref, gpost_ref, gpre_ref, wa_ref, wb_ref, wo_ref,
                  wrt_ref, br_ref, x1_ref, hp_ref, idx_ref, wrow_ref, rank_ref, cnt_ref, carry_scr):
    E = N_EXPERTS
    tm = x_ref.shape[0]
    half = x_ref.shape[1] // 2

    @pl.when(pl.program_id(0) == 0)
    def _():
        carry_scr[...] = jnp.zeros(carry_scr.shape, F32)

    ya = jnp.dot(oa_ref[...], wa_ref[...], preferred_element_type=F32)
    yb = jnp.dot(ob_ref[...], wb_ref[...], preferred_element_type=F32)
    merged = _sigmoid(ga_ref[...]) * ya + _sigmoid(gb_ref[...]) * yb
    mix = jnp.dot(merged.astype(BF16), wo_ref[...], preferred_element_type=F32)
    x1 = x_ref[...] + mod_ref[0, 2:3, :] * (_rms(mix) * gpost_ref[...])
    x1_ref[...] = x1
    h2 = (_rms(x1) * gpre_ref[...]) * (1.0 + mod_ref[0, 4:5, :]) + mod_ref[0, 3:4, :]

    lo_bits = pltpu.bitcast(h2[:, :half].astype(BF16).astype(F32), jnp.uint32) >> 16
    hi_bits = pltpu.bitcast(h2[:, half:].astype(BF16).astype(F32), jnp.uint32) & jnp.uint32(0xFFFF0000)
    hp_ref[...] = hi_bits | lo_bits

    lt = lax.dot_general(wrt_ref[...], h2, (((1,), (1,)), ((), ())), preferred_element_type=F32,
                         precision=HIGHEST) + br_ref[...]
    eid = lax.broadcasted_iota(jnp.int32, (E, tm), 0)
    vals, idxs = [], []
    for _ in range(TOP_K):
        m = jnp.max(lt, axis=0, keepdims=True)
        idx = jnp.min(jnp.where(lt == m, eid, E), axis=0, keepdims=True)
        vals.append(m)
        idxs.append(idx)
        lt = jnp.where(eid == idx, -jnp.inf, lt)
    exps = [jnp.exp(v - vals[0]) for v in vals]
    den = exps[0] + exps[1] + exps[2] + exps[3]
    wts = [e / den for e in exps]

    hot = jnp.zeros((E, tm), F32)
    for idx in idxs:
        hot = hot + (eid == idx).astype(F32)
    ti = lax.broadcasted_iota(jnp.int32, (tm, tm), 0)
    tj = lax.broadcasted_iota(jnp.int32, (tm, tm), 1)
    before = (ti < tj).astype(BF16)
    prior = carry_scr[...][:, 0:1] + jnp.dot(hot.astype(BF16), before, preferred_element_type=F32)
    row8 = lax.broadcasted_iota(jnp.int32, (8, tm), 0)
    row128 = lax.broadcasted_iota(jnp.int32, (LANES, tm), 0)
    idx8 = jnp.zeros((8, tm), jnp.int32)
    rank8 = jnp.zeros((8, tm), jnp.int32)
    w128 = jnp.zeros((LANES, tm), F32)
    for r in range(TOP_K):
        rank_r = jnp.sum(jnp.where(eid == idxs[r], prior, 0.0), axis=0, keepdims=True)
        idx8 = jnp.where(row8 == r, idxs[r], idx8)
        rank8 = jnp.where(row8 == r, rank_r.astype(jnp.int32), rank8)
        w128 = jnp.where(row128 == r, wts[r], w128)
    idx_ref[...] = idx8
    rank_ref[...] = rank8
    wrow_ref[...] = jnp.transpose(w128)
    carry = carry_scr[...] + jnp.sum(hot, axis=1, keepdims=True)
    carry_scr[...] = carry
    cnt_ref[...] = carry


def _merge(oa, ob, proj, x2, mod3, g_post_mix, g_pre_ffn, w_br_a, w_br_b, w_o, w_router, b_router, S):
    T, D = x2.shape
    E = N_EXPERTS
    tm = min(512, S)
    per_b = S // tm
    W = DN_HEADS * HEAD_DIM
    row = lambda i: (i, 0)
    const = lambda i: (0, 0)
    lane_t = lambda i: (0, i)
    return pl.pallas_call(
        _merge_kernel,
        grid=(T // tm,),
        in_specs=[pl.BlockSpec((tm, W), row),
                  pl.BlockSpec((tm, W), row),
                  pl.BlockSpec((tm, D), lambda i: (i, COL_GA * LANES // D)),
                  pl.BlockSpec((tm, D), lambda i: (i, COL_GB * LANES // D)),
                  pl.BlockSpec((tm, D), row),
                  pl.BlockSpec((1, 6, D), lambda i: (i // per_b, 0, 0)),
                  pl.BlockSpec((1, D), const),
                  pl.BlockSpec((1, D), const),
                  pl.BlockSpec((W, D), const),
                  pl.BlockSpec((W, D), const),
                  pl.BlockSpec((D, D), const),
                  pl.BlockSpec((E, D), const),
                  pl.BlockSpec((E, 1), const)],
        out_specs=[pl.BlockSpec((tm, D), row),
                   pl.BlockSpec((tm, D // 2), row),
                   pl.BlockSpec((8, tm), lane_t),
                   pl.BlockSpec((tm, LANES), row),
                   pl.BlockSpec((8, tm), lane_t),
                   pl.BlockSpec((E, LANES), const)],
        out_shape=[jax.ShapeDtypeStruct((T, D), F32),
                   jax.ShapeDtypeStruct((T, D // 2), jnp.uint32),
                   jax.ShapeDtypeStruct((8, T), jnp.int32),
                   jax.ShapeDtypeStruct((T, LANES), F32),
                   jax.ShapeDtypeStruct((8, T), jnp.int32),
                   jax.ShapeDtypeStruct((E, LANES), F32)],
        scratch_shapes=[pltpu.VMEM((E, LANES), F32)],
        compiler_params=_cparams(("arbitrary",)),
        name="merge",
    )(oa, ob, proj, proj, x2, mod3, g_post_mix.reshape(1, D), g_pre_ffn.reshape(1, D),
      w_br_a.astype(BF16), w_br_b.astype(BF16), w_o.astype(BF16),
      jnp.transpose(w_router).astype(F32), b_router.reshape(E, 1).astype(F32))


DISPATCH_TOKENS = 512


def _dispatch_kernel(fill_ref, dest_ref, src_ref, dst_ref, zero_scr, sem, zsem):
    n = dest_ref.shape[1]
    tm = zero_scr.shape[0]

    @pl.when(pl.program_id(0) == 0)
    def _():
        zero_scr[...] = jnp.zeros(zero_scr.shape, zero_scr.dtype)

        def fill_copy(k):
            return pltpu.make_async_copy(zero_scr, dst_ref.at[pl.ds(pl.multiple_of(k * tm, tm), tm)], zsem)

        def start(k, carry):
            pl.when(fill_ref[k] == 1)(lambda: fill_copy(k).start())
            return carry

        def wait(k, carry):
            pl.when(fill_ref[k] == 1)(lambda: fill_copy(k).wait())
            return carry

        lax.fori_loop(0, fill_ref.shape[0], start, 0)
        lax.fori_loop(0, fill_ref.shape[0], wait, 0)

    def issue(j, carry):
        for r in range(TOP_K):
            pltpu.make_async_copy(src_ref.at[pl.ds(j, 1)], dst_ref.at[pl.ds(dest_ref[r, j], 1)], sem).start()
        return carry

    lax.fori_loop(0, n, issue, 0, unroll=8)
    for r in range(TOP_K):
        pltpu.make_async_copy(src_ref, dst_ref.at[pl.ds(0, n)], sem).wait()


def _dispatch(dest, blk_fill, hp):
    T, Wd = hp.shape
    n = min(DISPATCH_TOKENS, T)
    tm = EXPERT_ROWS
    grid_spec = pltpu.PrefetchScalarGridSpec(
        num_scalar_prefetch=1,
        grid=(T // n,),
        in_specs=[pl.BlockSpec((TOP_K, n), lambda i, fill: (0, i), memory_space=pltpu.SMEM),
                  pl.BlockSpec((n, Wd), lambda i, fill: (i, 0))],
        out_specs=pl.BlockSpec(memory_space=pl.ANY),
        scratch_shapes=[pltpu.VMEM((tm, Wd), hp.dtype), pltpu.SemaphoreType.DMA(()), pltpu.SemaphoreType.DMA(())],
    )
    return pl.pallas_call(
        _dispatch_kernel,
        grid_spec=grid_spec,
        out_shape=jax.ShapeDtypeStruct((_expert_blocks(T * TOP_K) * tm, Wd), hp.dtype),
        compiler_params=_cparams(("arbitrary",)),
        name="dispatch",
    )(blk_fill, dest, hp)


EXPERT_ROWS = 512
EXPERT_FEATURE_TILE = 512


def _experts_kernel(exp_ref, valid_ref, new_ref, x_ref, wgu_ref, bgu_ref, wdn_ref, bdn_ref, o_ref, wgu_bf, wdn_bf):
    k = pl.program_id(0)
    F = wdn_ref.shape[0]

    @pl.when(new_ref[k] == 1)
    def _():
        wgu_bf[...] = wgu_ref[...].astype(BF16)
        wdn_bf[...] = wdn_ref[...].astype(BF16)

    def ffn():
        word = x_ref[...]
        x = jnp.concatenate([pltpu.bitcast(word << 16, F32).astype(BF16),
                             pltpu.bitcast(word & jnp.uint32(0xFFFF0000), F32).astype(BF16)], axis=1)

        def gate_up(c):
            ft = slice(c, c + EXPERT_FEATURE_TILE)
            ut = slice(F + c, F + c + EXPERT_FEATURE_TILE)
            return (jnp.dot(x, wgu_bf[:, ft], preferred_element_type=F32) + bgu_ref[:, ft],
                    jnp.dot(x, wgu_bf[:, ut], preferred_element_type=F32) + bgu_ref[:, ut])

        tiles = list(range(0, F, EXPERT_FEATURE_TILE))
        y = None
        nxt = gate_up(tiles[0])
        for t, c in enumerate(tiles):
            gate, up = nxt
            if t + 1 < len(tiles):
                nxt = gate_up(tiles[t + 1])
            gate = jnp.minimum(gate, SWIGLU_LIMIT)
            up = jnp.clip(up, -SWIGLU_LIMIT, SWIGLU_LIMIT)
            act = (up + 1.0) * gate * _sigmoid(SWIGLU_ALPHA * gate)
            part = jnp.dot(act.astype(BF16), wdn_bf[c:c + EXPERT_FEATURE_TILE, :], preferred_element_type=F32)
            y = part + bdn_ref[...] if y is None else y + part
        return y

    @pl.when(valid_ref[k] == 1)
    def _():
        o_ref[:, 0, :] = ffn()

    @pl.when(valid_ref[k] == 0)
    def _():
        o_ref[...] = jnp.zeros(o_ref.shape, o_ref.dtype)


def _experts(xs, blk_exp, blk_valid, blk_new, w_gu, b_gu, w_down, b_down):
    R, half = xs.shape
    E, D, F2 = w_gu.shape
    F = F2 // 2
    tm = EXPERT_ROWS
    grid_spec = pltpu.PrefetchScalarGridSpec(
        num_scalar_prefetch=3,
        grid=(R // tm,),
        in_specs=[pl.BlockSpec((tm, half), lambda k, ex, va, nw: (k, 0)),
                  pl.BlockSpec((None, D, F2), lambda k, ex, va, nw: (ex[k], 0, 0)),
                  pl.BlockSpec((None, 1, F2), lambda k, ex, va, nw: (ex[k], 0, 0)),
                  pl.BlockSpec((None, F, D), lambda k, ex, va, nw: (ex[k], 0, 0)),
                  pl.BlockSpec((None, 1, D), lambda k, ex, va, nw: (ex[k], 0, 0))],
        out_specs=pl.BlockSpec((tm, 1, D), lambda k, ex, va, nw: (k, 0, 0)),
        scratch_shapes=[pltpu.VMEM((D, F2), BF16), pltpu.VMEM((F, D), BF16)],
    )
    return pl.pallas_call(
        _experts_kernel,
        grid_spec=grid_spec,
        out_shape=jax.ShapeDtypeStruct((R, 1, D), F32),
        compiler_params=_cparams(("arbitrary",)),
        name="experts",
    )(blk_exp, blk_valid, blk_new, xs, w_gu, b_gu.reshape(E, 1, F2).astype(F32),
      w_down, b_down.reshape(E, 1, D).astype(F32))


def _expert_blocks(A):
    return A // EXPERT_ROWS + N_EXPERTS


def _block_tables(counts, A):
    E = N_EXPERTS
    tm = EXPERT_ROWS
    n_blk = _expert_blocks(A)
    nblk_e = (counts + tm - 1) // tm
    blk_end = jnp.cumsum(nblk_e)
    blk_start = blk_end - nblk_e
    used = blk_end[-1]
    k = jnp.arange(n_blk, dtype=jnp.int32)
    kk = jnp.minimum(k, used - 1)
    e = jnp.sum((blk_end[None, :] <= kk[:, None]).astype(jnp.int32), axis=1)
    hot = e[:, None] == jnp.arange(E, dtype=jnp.int32)[None, :]
    pick = lambda table: jnp.sum(jnp.where(hot, table[None, :], 0), axis=1)
    valid = k < used
    new = valid & (k == pick(blk_start))
    fill = jnp.logical_not(valid) | (k == pick(blk_end) - 1)
    i32 = lambda v: v.astype(jnp.int32)
    return blk_start * tm, i32(e), i32(valid), i32(new), i32(fill)


COMBINE_TOKENS = 512


def _combine_kernel(dest_ref, nxt_ref, y_ref, wrow_ref, x1_ref, mod_ref, gpost_ref, o_ref, stage, moe_scr, sem):
    i = pl.program_id(0)
    last = pl.num_programs(0) - 1
    n = x1_ref.shape[0]
    SUB = 8

    def gather(idx_ref, j, s):
        for r in range(TOP_K):
            pltpu.make_async_copy(y_ref.at[idx_ref[r, j]], stage.at[s, r, j], sem.at[s]).start(priority=r % 2)

    @pl.when(i == 0)
    def _():
        def prime(j, carry):
            gather(dest_ref, j, 0)
            return carry
        lax.fori_loop(0, n, prime, 0, unroll=SUB)

    def step(slot):
        for r in range(TOP_K):
            pltpu.make_async_copy(y_ref.at[pl.ds(0, n)], stage.at[slot, r], sem.at[slot]).wait()

        def reduce_group(g):
            rows = pl.ds(pl.multiple_of(g * SUB, SUB), SUB)
            w = wrow_ref[rows, :]
            acc = w[:, 0:1] * stage[slot, 0, rows, 0, :]
            for r in range(1, TOP_K):
                acc = acc + w[:, r:r + 1] * stage[slot, r, rows, 0, :]
            moe_scr[rows, :] = acc

        @pl.when(i < last)
        def _():
            def body(g, carry):
                for t in range(SUB):
                    gather(nxt_ref, g * SUB + t, 1 - slot)
                reduce_group(g)
                return carry
            lax.fori_loop(0, n // SUB, body, 0)

        @pl.when(i == last)
        def _():
            def body(g, carry):
                reduce_group(g)
                return carry
            lax.fori_loop(0, n // SUB, body, 0)

    for parity in range(2):
        pl.when(i % 2 == parity)(functools.partial(step, parity))

    o_ref[...] = x1_ref[...] + mod_ref[0, 5:6, :] * (_rms(moe_scr[...]) * gpost_ref[...])


def _combine(dest, y, wrow, x1, mod3, g_post_ffn, S):
    T, D = x1.shape
    n = min(COMBINE_TOKENS, S)
    per_b = S // n
    steps = T // n
    return pl.pallas_call(
        _combine_kernel,
        grid=(steps,),
        in_specs=[pl.BlockSpec((TOP_K, n), lambda i: (0, i), memory_space=pltpu.SMEM),
                  pl.BlockSpec((TOP_K, n), lambda i: (0, jnp.minimum(i + 1, steps - 1)), memory_space=pltpu.SMEM),
                  pl.BlockSpec(memory_space=pl.ANY),
                  pl.BlockSpec((n, LANES), lambda i: (i, 0)),
                  pl.BlockSpec((n, D), lambda i: (i, 0)),
                  pl.BlockSpec((1, 6, D), lambda i: (i // per_b, 0, 0)),
                  pl.BlockSpec((1, D), lambda i: (0, 0))],
        out_specs=pl.BlockSpec((n, D), lambda i: (i, 0)),
        out_shape=jax.ShapeDtypeStruct((T, D), F32),
        scratch_shapes=[pltpu.VMEM((2, TOP_K, n, 1, D), F32), pltpu.VMEM((n, D), F32),
                        pltpu.SemaphoreType.DMA((2,))],
        compiler_params=_cparams(("arbitrary",)),
        name="combine",
    )(dest, dest, y, wrow, x1, mod3, g_post_ffn.reshape(1, D))


def _regroup_w_in(w_in):
    D = w_in.shape[0]
    dw = DN_HEADS * HEAD_DIM
    mw = MB_HEADS * HEAD_DIM
    cuts = np.cumsum([dw, dw, dw, dw, DN_HEADS, DN_HEADS, mw, mw, mw, D, D])[:-1]
    dq, dk, dv, dz, db, da, mq, mk, mv, ga, gb = jnp.split(w_in, [int(c) for c in cuts], axis=1)
    small = jnp.concatenate([db, da, jnp.zeros((D, LANES - 2 * DN_HEADS), w_in.dtype)], axis=1).astype(BF16)
    return jnp.concatenate([p.astype(BF16) for p in (ga, gb, dq, dk, dv, dz, mq, mk, mv)], axis=1), small


def kernel(x, c, w_ada, b_ada, g_pre_mix, g_post_mix, g_pre_ffn, g_post_ffn, w_in, conv_w, a_log, dt_bias,
           g_dn_out, w_br_a, w_br_b, w_o, w_router, b_router, w_gu, b_gu, w_down, b_down):
    B, S, D = x.shape
    l = 0
    mod = _adaln(c, w_ada[l], b_ada[l]).reshape(B, 6, D)
    proj, small = _inproj(x, mod, g_pre_mix[l], *_regroup_w_in(w_in[l]))
    qr, kr, vb, kmean = _mobaprep(proj, B, S)
    ob = _moba(qr, kr, vb, kmean, B, S)
    oa = _deltanet(proj, small, conv_w[l], a_log[l], dt_bias[l], g_dn_out[l], B, S)
    x1, hp, idx8, wrow, rank8, cnt = _merge(oa, ob, proj, x.reshape(B * S, D), mod, g_post_mix[l], g_pre_ffn[l],
                                            w_br_a[l], w_br_b[l], w_o[l], w_router[l], b_router[l], S)
    out = _moe(x1, hp, idx8, wrow, rank8, cnt, mod, g_post_ffn[l], w_gu[l], b_gu[l], w_down[l], b_down[l], S)
    return out.reshape(B, S, D)


def _moe(x1, hp, idx8, wrow, rank8, cnt, mod, g_post_ffn, w_gu, b_gu, w_down, b_down, S):
    T = x1.shape[0]
    counts = cnt[:, 0].astype(jnp.int32)
    start, blk_exp, blk_valid, blk_new, blk_fill = _block_tables(counts, T * TOP_K)
    hot = idx8[:TOP_K, :, None] == jnp.arange(N_EXPERTS, dtype=jnp.int32)
    dest = rank8[:TOP_K] + jnp.sum(jnp.where(hot, start, 0), axis=-1)
    xs = _dispatch(dest, blk_fill, hp)
    y = _experts(xs, blk_exp, blk_valid, blk_new, w_gu, b_gu, w_down, b_down)
    return _combine(dest, y, wrow, x1, mod, g_post_ffn, S)
```

```python
import functools

import jax
import jax.numpy as jnp
import numpy as np
from jax import lax
from jax.experimental import pallas as pl
from jax.experimental.pallas import tpu as pltpu

F32 = jnp.float32
BF16 = jnp.bfloat16
HIGHEST = lax.Precision.HIGHEST

HEAD_DIM = 128
DN_HEADS = 4
DN_CONV = 4
DN_CHUNK = 64
MB_HEADS = 4
MB_BLOCK = 256
MB_TOPK = 3
ROPE_THETA = 500000.0
ROPE_DIM = HEAD_DIM // 4
N_EXPERTS = 32
TOP_K = 4
SWIGLU_LIMIT = 7.0
SWIGLU_ALPHA = 1.702
NORM_EPS = 1e-6
LANES = 128
NEG_BIG = -1e30
MB_Q_SCALE = HEAD_DIM ** -0.5 * float(np.log2(np.e))

_W = DN_HEADS * HEAD_DIM // LANES
COL_GA = 0
COL_GB = 8
COL_DQ = 16
COL_DK = 20
COL_DV = 24
COL_DZ = 28
COL_MQ = 32
COL_MK = 36
COL_MV = 40
N_PROJ = 44 * LANES

VMEM_LIMIT = 56 * 1024 * 1024


def _cparams(sem):
    return pltpu.CompilerParams(dimension_semantics=sem, vmem_limit_bytes=VMEM_LIMIT)


def _sigmoid(v):
    return 0.5 * jnp.tanh(0.5 * v) + 0.5


def _silu(v):
    return v * _sigmoid(v)


def _adaln_kernel(c_ref, w_ref, b_ref, o_ref):
    a = _silu(c_ref[...])
    o_ref[...] = jnp.dot(a, w_ref[...], preferred_element_type=F32, precision=HIGHEST) + b_ref[...]


def _adaln(c, w_ada, b_ada):
    B, D = c.shape
    N = w_ada.shape[1]
    tn = D
    return pl.pallas_call(
        _adaln_kernel,
        grid=(N // tn,),
        in_specs=[pl.BlockSpec((B, D), lambda j: (0, 0)),
                  pl.BlockSpec((D, tn), lambda j: (0, j)),
                  pl.BlockSpec((1, tn), lambda j: (0, j))],
        out_specs=pl.BlockSpec((B, tn), lambda j: (0, j)),
        out_shape=jax.ShapeDtypeStruct((B, N), F32),
        compiler_params=_cparams(("arbitrary",)),
        name="adaln",
    )(c, w_ada, b_ada.reshape(1, N))


def _inproj_kernel(x_ref, mod_ref, g_ref, w_ref, ws_ref, o_ref, os_ref, h_scr):
    @pl.when(pl.program_id(2) == 0)
    def _():
        x = x_ref[0]
        y = x * lax.rsqrt(jnp.mean(x * x, axis=-1, keepdims=True) + NORM_EPS) * g_ref[...]
        h = y * (1.0 + mod_ref[0, 1:2, :]) + mod_ref[0, 0:1, :]
        h_scr[...] = h.astype(BF16)
        os_ref[...] = jnp.dot(h_scr[...], ws_ref[...], preferred_element_type=F32)

    o_ref[...] = jnp.dot(h_scr[...], w_ref[...], preferred_element_type=F32)


def _inproj(x, mod3, g_pre, w_main, w_small):
    B, S, D = x.shape
    tm = min(1024, S)
    tn = N_PROJ // 2
    nrow = S // tm
    return pl.pallas_call(
        _inproj_kernel,
        grid=(B, nrow, N_PROJ // tn),
        in_specs=[pl.BlockSpec((1, tm, D), lambda b, i, j: (b, i, 0)),
                  pl.BlockSpec((1, 6, D), lambda b, i, j: (b, 0, 0)),
                  pl.BlockSpec((1, D), lambda b, i, j: (0, 0)),
                  pl.BlockSpec((D, tn), lambda b, i, j: (0, j)),
                  pl.BlockSpec((D, LANES), lambda b, i, j: (0, 0))],
        out_specs=[pl.BlockSpec((tm, tn), lambda b, i, j: (b * nrow + i, j)),
                   pl.BlockSpec((tm, LANES), lambda b, i, j: (b * nrow + i, 0))],
        out_shape=[jax.ShapeDtypeStruct((B * S, N_PROJ), F32),
                   jax.ShapeDtypeStruct((B * S, LANES), F32)],
        scratch_shapes=[pltpu.VMEM((tm, D), BF16)],
        compiler_params=_cparams(("arbitrary", "arbitrary", "arbitrary")),
        name="inproj",
    )(x, mod3, g_pre.reshape(1, D), w_main, w_small)


def _rope(v, cosf, sinf, lane):
    rot = jnp.where(lane < ROPE_DIM // 2, pltpu.roll(v, LANES - ROPE_DIM // 2, 1), pltpu.roll(v, ROPE_DIM // 2, 1))
    return v * cosf + rot * sinf


def _mobaprep_kernel(q_ref, k_ref, v_ref, cos_ref, sin_ref, qo_ref, ko_ref, vo_ref, km_ref):
    cosf = cos_ref[...]
    sinf = sin_ref[...]
    lane = lax.broadcasted_iota(jnp.int32, cosf.shape, 1)
    for h in range(MB_HEADS):
        sl = slice(h * HEAD_DIM, (h + 1) * HEAD_DIM)
        qo_ref[:, sl] = (_rope(q_ref[:, sl], cosf, sinf, lane) * MB_Q_SCALE).astype(BF16)
        kr = _rope(k_ref[:, sl], cosf, sinf, lane)
        ko_ref[:, sl] = kr.astype(BF16)
        for blk in range(km_ref.shape[0]):
            km_ref[blk, :, sl] = jnp.mean(kr[blk * MB_BLOCK:(blk + 1) * MB_BLOCK], axis=0, keepdims=True)
    vo_ref[...] = v_ref[...].astype(BF16)


def _rope_tables(S):
    half = ROPE_DIM // 2
    inv_freq = ROPE_THETA ** (-jnp.arange(half, dtype=F32) / half)
    ang = jnp.arange(S, dtype=F32)[:, None] * inv_freq[None, :]
    cos, sin = jnp.cos(ang), jnp.sin(ang)
    rest = HEAD_DIM - ROPE_DIM
    cosf = jnp.concatenate([cos, cos, jnp.ones((S, rest), F32)], axis=1)
    sinf = jnp.concatenate([-sin, sin, jnp.zeros((S, rest), F32)], axis=1)
    return cosf, sinf


def _mobaprep(proj, B, S):
    T = B * S
    tb = min(4 * MB_BLOCK, S)
    nb = S // tb
    W = MB_HEADS * HEAD_DIM
    cosf, sinf = _rope_tables(S)
    col = lambda c: (lambda i: (i, c))
    return pl.pallas_call(
        _mobaprep_kernel,
        grid=(T // tb,),
        in_specs=[pl.BlockSpec((tb, W), col(COL_MQ // _W)),
                  pl.BlockSpec((tb, W), col(COL_MK // _W)),
                  pl.BlockSpec((tb, W), col(COL_MV // _W)),
                  pl.BlockSpec((tb, LANES), lambda i: (i % nb, 0)),
                  pl.BlockSpec((tb, LANES), lambda i: (i % nb, 0))],
        out_specs=[pl.BlockSpec((tb, W), lambda i: (i, 0)),
                   pl.BlockSpec((tb, W), lambda i: (i, 0)),
                   pl.BlockSpec((tb, W), lambda i: (i, 0)),
                   pl.BlockSpec((tb // MB_BLOCK, 1, W), lambda i: (i, 0, 0))],
        out_shape=[jax.ShapeDtypeStruct((T, W), BF16),
                   jax.ShapeDtypeStruct((T, W), BF16),
                   jax.ShapeDtypeStruct((T, W), BF16),
                   jax.ShapeDtypeStruct((T // MB_BLOCK, 1, W), F32)],
        compiler_params=_cparams(("arbitrary",)),
        name="mobaprep",
    )(proj, proj, proj, cosf, sinf)


MB_HPS = 4


def _moba_kernel(q_ref, k_ref, v_ref, km_ref, o_ref, s_scr, sd_scr, mx_scr, l_scr, acc_scr):
    qi = pl.program_id(1)
    tb = MB_BLOCK
    nb = km_ref.shape[0]
    nt = tb // LANES
    D = HEAD_DIM
    heads = range(MB_HPS)
    hs = [slice(h * D, (h + 1) * D) for h in heads]
    qs = [q_ref[:, hs[h]] for h in heads]

    sts = [lax.dot_general(km_ref[:, hs[h]], qs[h].astype(F32), (((1,), (1,)), ((), ())),
                           preferred_element_type=F32, precision=HIGHEST) for h in heads]
    blk = lax.broadcasted_iota(jnp.int32, (nb, tb), 0)
    rowid = lax.broadcasted_iota(jnp.int32, (LANES, tb), 0)
    sel_ts = []
    for h in heads:
        st = jnp.where(blk < qi, sts[h], -jnp.inf)
        sel_t = jnp.full((LANES, tb), -1.0, F32)
        for r in range(MB_TOPK):
            m = jnp.max(st, axis=0, keepdims=True)
            idx = jnp.min(jnp.where(st == m, blk, nb), axis=0, keepdims=True)
            sel_t = jnp.where(rowid == r, jnp.where(r < qi, idx, -1).astype(F32), sel_t)
            st = jnp.where(blk == idx, -jnp.inf, st)
        sel_ts.append(sel_t)
    sels = [jnp.transpose(t) for t in sel_ts]
    sel_rep = [[jnp.broadcast_to(sels[h][:, r:r + 1], (tb, LANES)) for r in range(MB_TOPK)] for h in heads]

    def logits(h, start, width):
        kslab = k_ref[pl.ds(pl.multiple_of(start, tb), width), hs[h]]
        return lax.dot_general(qs[h], kslab, (((1,), (1,)), ((), ())), preferred_element_type=F32)

    r_i = lax.broadcasted_iota(jnp.int32, (tb, LANES), 0)
    c_i = lax.broadcasted_iota(jnp.int32, (tb, LANES), 1)
    sds = [logits(h, qi * tb, tb) for h in heads]
    for h in heads:
        mx = jnp.full((tb, LANES), NEG_BIG, F32)
        for t in range(nt):
            piece = jnp.where(c_i + t * LANES <= r_i, sds[h][:, t * LANES:(t + 1) * LANES], NEG_BIG)
            sd_scr[h, :, t * LANES:(t + 1) * LANES] = piece
            mx = jnp.maximum(mx, piece)
        mx_scr[h] = mx

    n_pairs = (qi + 1) // 2

    def pass1(i, carry):
        kb0 = 2 * i
        s2s = [logits(h, kb0 * tb, 2 * tb) for h in heads]
        for h in heads:
            mx = mx_scr[h]
            for half in range(2):
                kbf = (kb0 + half).astype(F32)
                hit = (sel_rep[h][0] == kbf) | (sel_rep[h][1] == kbf) | (sel_rep[h][2] == kbf)
                for t in range(nt):
                    c0 = (half * nt + t) * LANES
                    piece = jnp.where(hit, s2s[h][:, c0:c0 + LANES], NEG_BIG)
                    s_scr[h, i, :, c0:c0 + LANES] = piece
                    mx = jnp.maximum(mx, piece)
            mx_scr[h] = mx
        return carry

    lax.fori_loop(0, n_pairs, pass1, 0)
    m_reps = [jnp.broadcast_to(jnp.max(mx_scr[h], axis=-1, keepdims=True), (tb, LANES)) for h in heads]

    def probs(load, width, m_rep):
        ps, lsum = [], jnp.zeros((tb, LANES), F32)
        for t in range(width // LANES):
            p = jnp.exp2(load(t) - m_rep)
            lsum = lsum + p
            ps.append(p.astype(BF16))
        return jnp.concatenate(ps, axis=1), lsum

    pls = [probs(lambda t, h=h: sd_scr[h, :, t * LANES:(t + 1) * LANES], tb, m_reps[h]) for h in heads]
    for h in heads:
        l_scr[h] = pls[h][1]
        acc_scr[h] = jnp.dot(pls[h][0], v_ref[pl.ds(pl.multiple_of(qi * tb, tb), tb), hs[h]],
                             preferred_element_type=F32)

    def pass2(i, carry):
        pls = [probs(lambda t, h=h: s_scr[h, i, :, t * LANES:(t + 1) * LANES], 2 * tb, m_reps[h]) for h in heads]
        for h in heads:
            vslab = v_ref[pl.ds(pl.multiple_of(2 * i * tb, tb), 2 * tb), hs[h]]
            l_scr[h] = l_scr[h] + pls[h][1]
            acc_scr[h] = acc_scr[h] + jnp.dot(pls[h][0], vslab, preferred_element_type=F32)
        return carry

    lax.fori_loop(0, n_pairs, pass2, 0)
    for h in heads:
        o_ref[:, hs[h]] = (acc_scr[h] / jnp.sum(l_scr[h], axis=-1, keepdims=True)).astype(o_ref.dtype)


def _moba(qr, kr, vb, kmean, B, S):
    tb = MB_BLOCK
    nb = S // tb
    T = B * S
    km = kmean.reshape(B, nb, MB_HEADS * HEAD_DIM)
    HW = MB_HPS * HEAD_DIM
    ng = MB_HEADS // MB_HPS
    return pl.pallas_call(
        _moba_kernel,
        grid=(B * ng, nb),
        in_specs=[pl.BlockSpec((tb, HW), lambda g, i: ((g // ng) * nb + i, g % ng)),
                  pl.BlockSpec((S, HW), lambda g, i: (g // ng, g % ng)),
                  pl.BlockSpec((S, HW), lambda g, i: (g // ng, g % ng)),
                  pl.BlockSpec((None, nb, HW), lambda g, i: (g // ng, 0, g % ng))],
        out_specs=pl.BlockSpec((tb, HW), lambda g, i: ((g // ng) * nb + i, g % ng)),
        out_shape=jax.ShapeDtypeStruct((T, MB_HEADS * HEAD_DIM), BF16),
        scratch_shapes=[pltpu.VMEM((MB_HPS, (nb + 1) // 2, tb, 2 * tb), F32),
                        pltpu.VMEM((MB_HPS, tb, tb), F32),
                        pltpu.VMEM((MB_HPS, tb, LANES), F32),
                        pltpu.VMEM((MB_HPS, tb, LANES), F32),
                        pltpu.VMEM((MB_HPS, tb, HEAD_DIM), F32)],
        compiler_params=_cparams(("arbitrary", "arbitrary")),
        name="moba",
    )(qr, kr, vb, km)


DN_TILE_CHUNKS = 8


def _softplus(v):
    return jnp.maximum(v, 0.0) + jnp.log1p(jnp.exp(-jnp.abs(v)))


DN_GROUP = 2
DN_HPS = 4


def _split(a):
    hi = pltpu.bitcast(pltpu.bitcast(a, jnp.uint32) & jnp.uint32(0xFFFF0000), F32)
    return hi.astype(BF16), (a - hi).astype(BF16)


def _dot3(ah, al, bh, bl):
    lhs = jnp.concatenate([ah, ah, al], axis=1)
    rhs = jnp.concatenate([bh, bl, bh], axis=0)
    return jnp.dot(lhs, rhs, preferred_element_type=F32)


def _dot_bf(a, b):
    return jnp.dot(a.astype(BF16), b.astype(BF16), preferred_element_type=F32)


def _deltanet_kernel(q_ref, k_ref, v_ref, z_ref, sm_ref, wq_ref, wk_ref, wv_ref, alog_ref, dtb_ref, gout_ref,
                     o_ref, xp_scr, state_scr):
    i = pl.program_id(2)
    C = DN_CHUNK
    TR = q_ref.shape[0]
    HALO = 8

    @pl.when(i == 0)
    def _():
        xp_scr[...] = jnp.zeros(xp_scr.shape, F32)
        state_scr[...] = jnp.zeros(state_scr.shape, F32)

    def conv_silu(slot, x_ref, w_ref, sl):
        xp_scr[slot, 0:HALO, sl] = xp_scr[slot, TR:TR + HALO, sl]
        xp_scr[slot, HALO:HALO + TR, sl] = x_ref[:, sl]
        acc = w_ref[DN_CONV - 1:DN_CONV, sl] * xp_scr[slot, HALO:HALO + TR, sl]
        for j in range(1, DN_CONV):
            acc = acc + w_ref[DN_CONV - 1 - j:DN_CONV - j, sl] * xp_scr[slot, HALO - j:HALO - j + TR, sl]
        return _silu(acc)

    sm = sm_ref[...]
    lane = lax.broadcasted_iota(jnp.int32, sm.shape, 1)
    row = lax.broadcasted_iota(jnp.int32, sm.shape, 0)
    beta_all = _sigmoid(sm)
    g_all = -jnp.exp(alog_ref[...]) * _softplus(sm + dtb_ref[...])
    pos = row % C
    gc_all = g_all
    shift = 1
    while shift < C:
        gc_all = gc_all + jnp.where(pos >= shift, pltpu.roll(gc_all, shift, 0), 0.0)
        shift *= 2
    gc_t = jnp.transpose(gc_all)
    row_t = lax.broadcasted_iota(jnp.int32, gc_t.shape, 0)

    G = DN_GROUP * C
    ri = lax.broadcasted_iota(jnp.int32, (G, G), 0)
    ci = lax.broadcasted_iota(jnp.int32, (G, G), 1)
    same = (ri // C) == (ci // C)
    tril = same & (ci <= ri)
    strict = same & (ci < ri)
    eye = (ci == ri).astype(F32)
    D = HEAD_DIM
    gout = gout_ref[...]

    def head_pipeline(hh):
        h = pl.program_id(1) * DN_HPS + hh
        sl = slice(hh * D, (hh + 1) * D)
        q = conv_silu(0, q_ref, wq_ref, sl)
        k = conv_silu(1, k_ref, wk_ref, sl)
        yield
        v = conv_silu(2, v_ref, wv_ref, sl)
        q = q * lax.rsqrt(jnp.sum(q * q, axis=-1, keepdims=True) + 1e-6) * (D ** -0.5)
        k = k * lax.rsqrt(jnp.sum(k * k, axis=-1, keepdims=True) + 1e-6)
        yield
        beta = jnp.sum(jnp.where(lane == h, beta_all, 0.0), axis=1, keepdims=True)
        gc = jnp.sum(jnp.where(lane == DN_HEADS + h, gc_all, 0.0), axis=1, keepdims=True)
        gc_row = jnp.sum(jnp.where(row_t == DN_HEADS + h, gc_t, 0.0), axis=0, keepdims=True)
        head = (q, k, v, beta, gc, gc_row, jnp.transpose(k))
        yield
        steps = [None] * (TR // C)
        yield from _round_robin([_deltanet_group(head, g, tril, strict, eye, steps) for g in range(TR // G)])
        outs = []
        yield from _deltanet_chain(hh, steps, state_scr, outs)
        o = jnp.concatenate(outs, axis=0)
        y = o * lax.rsqrt(jnp.mean(o * o, axis=-1, keepdims=True) + NORM_EPS) * gout
        o_ref[:, sl] = (y * _silu(z_ref[:, sl])).astype(o_ref.dtype)

    for _ in _round_robin([head_pipeline(hh) for hh in range(DN_HPS)]):
        pass


def _round_robin(gens):
    active = list(gens)
    while active:
        still = []
        for gen in active:
            try:
                next(gen)
                still.append(gen)
            except StopIteration:
                pass
        active = still
        yield


def _deltanet_group(head, g, tril, strict, eye, steps_out):
    C = DN_CHUNK
    D = HEAD_DIM
    G = DN_GROUP * C
    q, k, v, beta, gc, gc_row, k_t = head
    r0 = g * G
    qg, kg, vg = q[r0:r0 + G], k[r0:r0 + G], v[r0:r0 + G]
    bg = beta[r0:r0 + G]
    gcg = gc[r0:r0 + G]
    gcr = gc_row[:, r0:r0 + G]
    ktg = k_t[:, r0:r0 + G]
    decay = jnp.where(tril, jnp.exp(jnp.where(tril, gcg - gcr, 0.0)), 0.0)
    kb = kg * bg
    aq = _dot_bf(jnp.concatenate([kb, qg], axis=0), ktg)
    yield
    m_neg = jnp.where(strict, -(aq[:G] * decay), 0.0)
    qk = (aq[G:] * decay).astype(BF16)
    t_inv = eye + m_neg
    ph, pl_ = _split(m_neg)
    for _ in range(5):
        th, tl = _split(t_inv)
        ph, pl_ = _split(_dot3(ph, pl_, ph, pl_))
        yield
        t_inv = t_inv + _dot3(th, tl, ph, pl_)
        yield
    egc = jnp.exp(gcg)
    th, tl = _split(t_inv)
    rh, rl = _split(jnp.concatenate([kb * egc, vg * bg], axis=1))
    wu = _dot3(th, tl, rh, rl).astype(BF16)
    yield
    qr = jnp.dot(qk, wu, preferred_element_type=F32)
    qp = qg * egc - qr[:, :D]
    r_all = qr[:, D:]
    yield
    for c in range(DN_GROUP):
        c0 = c * C
        g_last = gcg[c0 + C - 1:c0 + C, :]
        kt_tail = ktg[:, c0:c0 + C] * jnp.exp(g_last - gcr[:, c0:c0 + C])
        gh = jnp.dot(kt_tail.astype(BF16), wu[c0:c0 + C, :], preferred_element_type=F32)
        lhs = jnp.concatenate([gh[:, :D], qp[c0:c0 + C]], axis=0).astype(BF16)
        steps_out[g * DN_GROUP + c] = (lhs, gh[:, D:], r_all[c0:c0 + C], jnp.exp(g_last))
        yield


def _deltanet_chain(hh, steps, state_scr, outs):
    D = HEAD_DIM
    state = state_scr[hh]
    for lhs, h_add, r_add, dec in steps:
        res = jnp.dot(lhs, state.astype(BF16), preferred_element_type=F32)
        outs.append(res[D:] + r_add)
        state = state * dec - res[:D] + h_add
        yield
    state_scr[hh] = state


def _deltanet(proj, small, conv_w, a_log, dt_bias, g_dn_out, B, S):
    T = B * S
    TR = min(DN_TILE_CHUNKS * DN_CHUNK, S)
    nt = S // TR
    pad = jnp.zeros((DN_HEADS,), F32)
    rest = jnp.zeros((LANES - 2 * DN_HEADS,), F32)
    alog_lane = jnp.concatenate([pad, a_log.astype(F32), rest]).reshape(1, LANES)
    dtb_lane = jnp.concatenate([pad, dt_bias.astype(F32), rest]).reshape(1, LANES)
    HW = DN_HPS * HEAD_DIM
    per = HW // LANES
    rows = lambda c0: (lambda b, h, i: (b * nt + i, c0 // per + h))
    wcol = lambda c0: (lambda b, h, i: (0, c0 // per + h))
    const = lambda b, h, i: (0, 0)
    return pl.pallas_call(
        _deltanet_kernel,
        grid=(B, DN_HEADS // DN_HPS, nt),
        in_specs=[pl.BlockSpec((TR, HW), rows(COL_DQ)),
                  pl.BlockSpec((TR, HW), rows(COL_DK)),
                  pl.BlockSpec((TR, HW), rows(COL_DV)),
                  pl.BlockSpec((TR, HW), rows(COL_DZ)),
                  pl.BlockSpec((TR, LANES), lambda b, h, i: (b * nt + i, 0)),
                  pl.BlockSpec((DN_CONV, HW), wcol(0)),
                  pl.BlockSpec((DN_CONV, HW), wcol(DN_HEADS)),
                  pl.BlockSpec((DN_CONV, HW), wcol(2 * DN_HEADS)),
                  pl.BlockSpec((1, LANES), const),
                  pl.BlockSpec((1, LANES), const),
                  pl.BlockSpec((1, HEAD_DIM), const)],
        out_specs=pl.BlockSpec((TR, HW), lambda b, h, i: (b * nt + i, h)),
        out_shape=jax.ShapeDtypeStruct((T, DN_HEADS * HEAD_DIM), BF16),
        scratch_shapes=[pltpu.VMEM((3, TR + 8, HW), F32), pltpu.VMEM((DN_HPS, HEAD_DIM, HEAD_DIM), F32)],
        compiler_params=_cparams(("arbitrary", "arbitrary", "arbitrary")),
        name="deltanet",
    )(proj, proj, proj, proj, small, conv_w, conv_w, conv_w, alog_lane, dtb_lane, g_dn_out.reshape(1, HEAD_DIM))


def _rms(v):
    return v * lax.rsqrt(jnp.mean(v * v, axis=-1, keepdims=True) + NORM_EPS)


def _merge_kernel(oa_ref, ob_ref, ga_ref, gb_ref, x_ref, mod_ref, gpost_ref, gpre_ref, wa_ref, wb_ref, wo_ref,
                  wrt_ref, br_ref, x1_ref, hp_ref, idx_ref, wrow_ref, rank_ref, cnt_ref, carry_scr):
    E = N_EXPERTS
    tm = x_ref.shape[0]
    half = x_ref.shape[1] // 2

    @pl.when(pl.program_id(0) == 0)
    def _():
        carry_scr[...] = jnp.zeros(carry_scr.shape, F32)

    ya = jnp.dot(oa_ref[...], wa_ref[...], preferred_element_type=F32)
    yb = jnp.dot(ob_ref[...], wb_ref[...], preferred_element_type=F32)
    merged = _sigmoid(ga_ref[...]) * ya + _sigmoid(gb_ref[...]) * yb
    mix = jnp.dot(merged.astype(BF16), wo_ref[...], preferred_element_type=F32)
    x1 = x_ref[...] + mod_ref[0, 2:3, :] * (_rms(mix) * gpost_ref[...])
    x1_ref[...] = x1
    h2 = (_rms(x1) * gpre_ref[...]) * (1.0 + mod_ref[0, 4:5, :]) + mod_ref[0, 3:4, :]

    lo_bits = pltpu.bitcast(h2[:, :half].astype(BF16).astype(F32), jnp.uint32) >> 16
    hi_bits = pltpu.bitcast(h2[:, half:].astype(BF16).astype(F32), jnp.uint32) & jnp.uint32(0xFFFF0000)
    hp_ref[...] = hi_bits | lo_bits

    lt = lax.dot_general(wrt_ref[...], h2, (((1,), (1,)), ((), ())), preferred_element_type=F32,
                         precision=HIGHEST) + br_ref[...]
    eid = lax.broadcasted_iota(jnp.int32, (E, tm), 0)
    vals, idxs = [], []
    for _ in range(TOP_K):
        m = jnp.max(lt, axis=0, keepdims=True)
        idx = jnp.min(jnp.where(lt == m, eid, E), axis=0, keepdims=True)
        vals.append(m)
        idxs.append(idx)
        lt = jnp.where(eid == idx, -jnp.inf, lt)
    exps = [jnp.exp(v - vals[0]) for v in vals]
    den = exps[0] + exps[1] + exps[2] + exps[3]
    wts = [e / den for e in exps]

    hot = jnp.zeros((E, tm), F32)
    for idx in idxs:
        hot = hot + (eid == idx).astype(F32)
    ti = lax.broadcasted_iota(jnp.int32, (tm, tm), 0)
    tj = lax.broadcasted_iota(jnp.int32, (tm, tm), 1)
    before = (ti < tj).astype(BF16)
    prior = carry_scr[...][:, 0:1] + jnp.dot(hot.astype(BF16), before, preferred_element_type=F32)
    row8 = lax.broadcasted_iota(jnp.int32, (8, tm), 0)
    row128 = lax.broadcasted_iota(jnp.int32, (LANES, tm), 0)
    idx8 = jnp.zeros((8, tm), jnp.int32)
    rank8 = jnp.zeros((8, tm), jnp.int32)
    w128 = jnp.zeros((LANES, tm), F32)
    for r in range(TOP_K):
        rank_r = jnp.sum(jnp.where(eid == idxs[r], prior, 0.0), axis=0, keepdims=True)
        idx8 = jnp.where(row8 == r, idxs[r], idx8)
        rank8 = jnp.where(row8 == r, rank_r.astype(jnp.int32), rank8)
        w128 = jnp.where(row128 == r, wts[r], w128)
    idx_ref[...] = idx8
    rank_ref[...] = rank8
    wrow_ref[...] = jnp.transpose(w128)
    carry = carry_scr[...] + jnp.sum(hot, axis=1, keepdims=True)
    carry_scr[...] = carry
    cnt_ref[...] = carry


def _merge(oa, ob, proj, x2, mod3, g_post_mix, g_pre_ffn, w_br_a, w_br_b, w_o, w_router, b_router, S):
    T, D = x2.shape
    E = N_EXPERTS
    tm = min(512, S)
    per_b = S // tm
    W = DN_HEADS * HEAD_DIM
    row = lambda i: (i, 0)
    const = lambda i: (0, 0)
    lane_t = lambda i: (0, i)
    return pl.pallas_call(
        _merge_kernel,
        grid=(T // tm,),
        in_specs=[pl.BlockSpec((tm, W), row),
                  pl.BlockSpec((tm, W), row),
                  pl.BlockSpec((tm, D), lambda i: (i, COL_GA * LANES // D)),
                  pl.BlockSpec((tm, D), lambda i: (i, COL_GB * LANES // D)),
                  pl.BlockSpec((tm, D), row),
                  pl.BlockSpec((1, 6, D), lambda i: (i // per_b, 0, 0)),
                  pl.BlockSpec((1, D), const),
                  pl.BlockSpec((1, D), const),
                  pl.BlockSpec((W, D), const),
                  pl.BlockSpec((W, D), const),
                  pl.BlockSpec((D, D), const),
                  pl.BlockSpec((E, D), const),
                  pl.BlockSpec((E, 1), const)],
        out_specs=[pl.BlockSpec((tm, D), row),
                   pl.BlockSpec((tm, D // 2), row),
                   pl.BlockSpec((8, tm), lane_t),
                   pl.BlockSpec((tm, LANES), row),
                   pl.BlockSpec((8, tm), lane_t),
                   pl.BlockSpec((E, LANES), const)],
        out_shape=[jax.ShapeDtypeStruct((T, D), F32),
                   jax.ShapeDtypeStruct((T, D // 2), jnp.uint32),
                   jax.ShapeDtypeStruct((8, T), jnp.int32),
                   jax.ShapeDtypeStruct((T, LANES), F32),
                   jax.ShapeDtypeStruct((8, T), jnp.int32),
                   jax.ShapeDtypeStruct((E, LANES), F32)],
        scratch_shapes=[pltpu.VMEM((E, LANES), F32)],
        compiler_params=_cparams(("arbitrary",)),
        name="merge",
    )(oa, ob, proj, proj, x2, mod3, g_post_mix.reshape(1, D), g_pre_ffn.reshape(1, D),
      w_br_a.astype(BF16), w_br_b.astype(BF16), w_o.astype(BF16),
      jnp.transpose(w_router).astype(F32), b_router.reshape(E, 1).astype(F32))


DISPATCH_TOKENS = 512


def _dispatch_kernel(fill_ref, dest_ref, src_ref, dst_ref, zero_scr, sem, zsem):
    n = dest_ref.shape[1]
    tm = zero_scr.shape[0]

    @pl.when(pl.program_id(0) == 0)
    def _():
        zero_scr[...] = jnp.zeros(zero_scr.shape, zero_scr.dtype)

        def fill_copy(k):
            return pltpu.make_async_copy(zero_scr, dst_ref.at[pl.ds(pl.multiple_of(k * tm, tm), tm)], zsem)

        def start(k, carry):
            pl.when(fill_ref[k] == 1)(lambda: fill_copy(k).start())
            return carry

        def wait(k, carry):
            pl.when(fill_ref[k] == 1)(lambda: fill_copy(k).wait())
            return carry

        lax.fori_loop(0, fill_ref.shape[0], start, 0)
        lax.fori_loop(0, fill_ref.shape[0], wait, 0)

    def issue(j, carry):
        for r in range(TOP_K):
            pltpu.make_async_copy(src_ref.at[pl.ds(j, 1)], dst_ref.at[pl.ds(dest_ref[r, j], 1)], sem).start()
        return carry

    lax.fori_loop(0, n, issue, 0, unroll=8)
    for r in range(TOP_K):
        pltpu.make_async_copy(src_ref, dst_ref.at[pl.ds(0, n)], sem).wait()


def _dispatch(dest, blk_fill, hp):
    T, Wd = hp.shape
    n = min(DISPATCH_TOKENS, T)
    tm = EXPERT_ROWS
    grid_spec = pltpu.PrefetchScalarGridSpec(
        num_scalar_prefetch=1,
        grid=(T // n,),
        in_specs=[pl.BlockSpec((TOP_K, n), lambda i, fill: (0, i), memory_space=pltpu.SMEM),
                  pl.BlockSpec((n, Wd), lambda i, fill: (i, 0))],
        out_specs=pl.BlockSpec(memory_space=pl.ANY),
        scratch_shapes=[pltpu.VMEM((tm, Wd), hp.dtype), pltpu.SemaphoreType.DMA(()), pltpu.SemaphoreType.DMA(())],
    )
    return pl.pallas_call(
        _dispatch_kernel,
        grid_spec=grid_spec,
        out_shape=jax.ShapeDtypeStruct((_expert_blocks(T * TOP_K) * tm, Wd), hp.dtype),
        compiler_params=_cparams(("arbitrary",)),
        name="dispatch",
    )(blk_fill, dest, hp)


EXPERT_ROWS = 512
EXPERT_FEATURE_TILE = 512


def _experts_kernel(exp_ref, valid_ref, new_ref, x_ref, wgu_ref, bgu_ref, wdn_ref, bdn_ref, o_ref, wgu_bf, wdn_bf):
    k = pl.program_id(0)
    F = wdn_ref.shape[0]

    @pl.when(new_ref[k] == 1)
    def _():
        wgu_bf[...] = wgu_ref[...].astype(BF16)
        wdn_bf[...] = wdn_ref[...].astype(BF16)

    def ffn():
        word = x_ref[...]
        x = jnp.concatenate([pltpu.bitcast(word << 16, F32).astype(BF16),
                             pltpu.bitcast(word & jnp.uint32(0xFFFF0000), F32).astype(BF16)], axis=1)

        def gate_up(c):
            ft = slice(c, c + EXPERT_FEATURE_TILE)
            ut = slice(F + c, F + c + EXPERT_FEATURE_TILE)
            return (jnp.dot(x, wgu_bf[:, ft], preferred_element_type=F32) + bgu_ref[:, ft],
                    jnp.dot(x, wgu_bf[:, ut], preferred_element_type=F32) + bgu_ref[:, ut])

        tiles = list(range(0, F, EXPERT_FEATURE_TILE))
        y = None
        nxt = gate_up(tiles[0])
        for t, c in enumerate(tiles):
            gate, up = nxt
            if t + 1 < len(tiles):
                nxt = gate_up(tiles[t + 1])
            gate = jnp.minimum(gate, SWIGLU_LIMIT)
            up = jnp.clip(up, -SWIGLU_LIMIT, SWIGLU_LIMIT)
            act = (up + 1.0) * gate * _sigmoid(SWIGLU_ALPHA * gate)
            part = jnp.dot(act.astype(BF16), wdn_bf[c:c + EXPERT_FEATURE_TILE, :], preferred_element_type=F32)
            y = part + bdn_ref[...] if y is None else y + part
        return y

    @pl.when(valid_ref[k] == 1)
    def _():
        o_ref[:, 0, :] = ffn()

    @pl.when(valid_ref[k] == 0)
    def _():
        o_ref[...] = jnp.zeros(o_ref.shape, o_ref.dtype)


def _experts(xs, blk_exp, blk_valid, blk_new, w_gu, b_gu, w_down, b_down):
    R, half = xs.shape
    E, D, F2 = w_gu.shape
    F = F2 // 2
    tm = EXPERT_ROWS
    grid_spec = pltpu.PrefetchScalarGridSpec(
        num_scalar_prefetch=3,
        grid=(R // tm,),
        in_specs=[pl.BlockSpec((tm, half), lambda k, ex, va, nw: (k, 0)),
                  pl.BlockSpec((None, D, F2), lambda k, ex, va, nw: (ex[k], 0, 0)),
                  pl.BlockSpec((None, 1, F2), lambda k, ex, va, nw: (ex[k], 0, 0)),
                  pl.BlockSpec((None, F, D), lambda k, ex, va, nw: (ex[k], 0, 0)),
                  pl.BlockSpec((None, 1, D), lambda k, ex, va, nw: (ex[k], 0, 0))],
        out_specs=pl.BlockSpec((tm, 1, D), lambda k, ex, va, nw: (k, 0, 0)),
        scratch_shapes=[pltpu.VMEM((D, F2), BF16), pltpu.VMEM((F, D), BF16)],
    )
    return pl.pallas_call(
        _experts_kernel,
        grid_spec=grid_spec,
        out_shape=jax.ShapeDtypeStruct((R, 1, D), F32),
        compiler_params=_cparams(("arbitrary",)),
        name="experts",
    )(blk_exp, blk_valid, blk_new, xs, w_gu, b_gu.reshape(E, 1, F2).astype(F32),
      w_down, b_down.reshape(E, 1, D).astype(F32))


def _expert_blocks(A):
    return A // EXPERT_ROWS + N_EXPERTS


def _block_tables(counts, A):
    E = N_EXPERTS
    tm = EXPERT_ROWS
    n_blk = _expert_blocks(A)
    nblk_e = (counts + tm - 1) // tm
    blk_end = jnp.cumsum(nblk_e)
    blk_start = blk_end - nblk_e
    used = blk_end[-1]
    k = jnp.arange(n_blk, dtype=jnp.int32)
    kk = jnp.minimum(k, used - 1)
    e = jnp.sum((blk_end[None, :] <= kk[:, None]).astype(jnp.int32), axis=1)
    hot = e[:, None] == jnp.arange(E, dtype=jnp.int32)[None, :]
    pick = lambda table: jnp.sum(jnp.where(hot, table[None, :], 0), axis=1)
    valid = k < used
    new = valid & (k == pick(blk_start))
    fill = jnp.logical_not(valid) | (k == pick(blk_end) - 1)
    i32 = lambda v: v.astype(jnp.int32)
    return blk_start * tm, i32(e), i32(valid), i32(new), i32(fill)


COMBINE_TOKENS = 512


def _combine_kernel(dest_ref, nxt_ref, y_ref, wrow_ref, x1_ref, mod_ref, gpost_ref, o_ref, stage, moe_scr, sem):
    i = pl.program_id(0)
    last = pl.num_programs(0) - 1
    n = x1_ref.shape[0]
    SUB = 8

    def gather(idx_ref, j, s):
        for r in range(TOP_K):
            pltpu.make_async_copy(y_ref.at[idx_ref[r, j]], stage.at[s, r, j], sem.at[s]).start(priority=r % 2)

    @pl.when(i == 0)
    def _():
        def prime(j, carry):
            gather(dest_ref, j, 0)
            return carry
        lax.fori_loop(0, n, prime, 0, unroll=SUB)

    def step(slot):
        for r in range(TOP_K):
            pltpu.make_async_copy(y_ref.at[pl.ds(0, n)], stage.at[slot, r], sem.at[slot]).wait()

        def reduce_group(g):
            rows = pl.ds(pl.multiple_of(g * SUB, SUB), SUB)
            w = wrow_ref[rows, :]
            acc = w[:, 0:1] * stage[slot, 0, rows, 0, :]
            for r in range(1, TOP_K):
                acc = acc + w[:, r:r + 1] * stage[slot, r, rows, 0, :]
            moe_scr[rows, :] = acc

        @pl.when(i < last)
        def _():
            def body(g, carry):
                for t in range(SUB):
                    gather(nxt_ref, g * SUB + t, 1 - slot)
                reduce_group(g)
                return carry
            lax.fori_loop(0, n // SUB, body, 0)

        @pl.when(i == last)
        def _():
            def body(g, carry):
                reduce_group(g)
                return carry
            lax.fori_loop(0, n // SUB, body, 0)

    for parity in range(2):
        pl.when(i % 2 == parity)(functools.partial(step, parity))

    o_ref[...] = x1_ref[...] + mod_ref[0, 5:6, :] * (_rms(moe_scr[...]) * gpost_ref[...])


def _combine(dest, y, wrow, x1, mod3, g_post_ffn, S):
    T, D = x1.shape
    n = min(COMBINE_TOKENS, S)
    per_b = S // n
    steps = T // n
    return pl.pallas_call(
        _combine_kernel,
        grid=(steps,),
        in_specs=[pl.BlockSpec((TOP_K, n), lambda i: (0, i), memory_space=pltpu.SMEM),
                  pl.BlockSpec((TOP_K, n), lambda i: (0, jnp.minimum(i + 1, steps - 1)), memory_space=pltpu.SMEM),
                  pl.BlockSpec(memory_space=pl.ANY),
                  pl.BlockSpec((n, LANES), lambda i: (i, 0)),
                  pl.BlockSpec((n, D), lambda i: (i, 0)),
                  pl.BlockSpec((1, 6, D), lambda i: (i // per_b, 0, 0)),
                  pl.BlockSpec((1, D), lambda i: (0, 0))],
        out_specs=pl.BlockSpec((n, D), lambda i: (i, 0)),
        out_shape=jax.ShapeDtypeStruct((T, D), F32),
        scratch_shapes=[pltpu.VMEM((2, TOP_K, n, 1, D), F32), pltpu.VMEM((n, D), F32),
                        pltpu.SemaphoreType.DMA((2,))],
        compiler_params=_cparams(("arbitrary",)),
        name="combine",
    )(dest, dest, y, wrow, x1, mod3, g_post_ffn.reshape(1, D))


def _regroup_w_in(w_in):
    D = w_in.shape[0]
    dw = DN_HEADS * HEAD_DIM
    mw = MB_HEADS * HEAD_DIM
    cuts = np.cumsum([dw, dw, dw, dw, DN_HEADS, DN_HEADS, mw, mw, mw, D, D])[:-1]
    dq, dk, dv, dz, db, da, mq, mk, mv, ga, gb = jnp.split(w_in, [int(c) for c in cuts], axis=1)
    small = jnp.concatenate([db, da, jnp.zeros((D, LANES - 2 * DN_HEADS), w_in.dtype)], axis=1).astype(BF16)
    return jnp.concatenate([p.astype(BF16) for p in (ga, gb, dq, dk, dv, dz, mq, mk, mv)], axis=1), small


def kernel(x, c, w_ada, b_ada, g_pre_mix, g_post_mix, g_pre_ffn, g_post_ffn, w_in, conv_w, a_log, dt_bias,
           g_dn_out, w_br_a, w_br_b, w_o, w_router, b_router, w_gu, b_gu, w_down, b_down):
    B, S, D = x.shape
    l = 0
    mod = _adaln(c, w_ada[l], b_ada[l]).reshape(B, 6, D)
    proj, small = _inproj(x, mod, g_pre_mix[l], *_regroup_w_in(w_in[l]))
    qr, kr, vb, kmean = _mobaprep(proj, B, S)
    ob = _moba(qr, kr, vb, kmean, B, S)
    oa = _deltanet(proj, small, conv_w[l], a_log[l], dt_bias[l], g_dn_out[l], B, S)
    x1, hp, idx8, wrow, rank8, cnt = _merge(oa, ob, proj, x.reshape(B * S, D), mod, g_post_mix[l], g_pre_ffn[l],
                                            w_br_a[l], w_br_b[l], w_o[l], w_router[l], b_router[l], S)
    out = _moe(x1, hp, idx8, wrow, rank8, cnt, mod, g_post_ffn[l], w_gu[l], b_gu[l], w_down[l], b_down[l], S)
    return out.reshape(B, S, D)


def _moe(x1, hp, idx8, wrow, rank8, cnt, mod, g_post_ffn, w_gu, b_gu, w_down, b_down, S):
    T = x1.shape[0]
    counts = cnt[:, 0].astype(jnp.int32)
    start, blk_exp, blk_valid, blk_new, blk_fill = _block_tables(counts, T * TOP_K)
    hot = idx8[:TOP_K, :, None] == jnp.arange(N_EXPERTS, dtype=jnp.int32)
    dest = rank8[:TOP_K] + jnp.sum(jnp.where(hot, start, 0), axis=-1)
    xs = _dispatch(dest, blk_fill, hp)
    y = _experts(xs, blk_exp, blk_valid, blk_new, w_gu, b_gu, w_down, b_down)
    return _combine(dest, y, wrow, x1, mod, g_post_ffn, S)
```

```python
import functools

import jax
import jax.numpy as jnp
import numpy as np
from jax import lax
from jax.experimental import pallas as pl
from jax.experimental.pallas import tpu as pltpu

F32 = jnp.float32
BF16 = jnp.bfloat16
HIGHEST = lax.Precision.HIGHEST

HEAD_DIM = 128
DN_HEADS = 4
DN_CONV = 4
DN_CHUNK = 64
MB_HEADS = 4
MB_BLOCK = 256
MB_TOPK = 3
ROPE_THETA = 500000.0
ROPE_DIM = HEAD_DIM // 4
N_EXPERTS = 32
TOP_K = 4
SWIGLU_LIMIT = 7.0
SWIGLU_ALPHA = 1.702
NORM_EPS = 1e-6
LANES = 128
NEG_BIG = -1e30
MB_Q_SCALE = HEAD_DIM ** -0.5 * float(np.log2(np.e))

_W = DN_HEADS * HEAD_DIM // LANES
COL_GA = 0
COL_GB = 8
COL_DQ = 16
COL_DK = 20
COL_DV = 24
COL_DZ = 28
COL_MQ = 32
COL_MK = 36
COL_MV = 40
N_PROJ = 44 * LANES

VMEM_LIMIT = 56 * 1024 * 1024


def _cparams(sem):
    return pltpu.CompilerParams(dimension_semantics=sem, vmem_limit_bytes=VMEM_LIMIT)


def _sigmoid(v):
    return 0.5 * jnp.tanh(0.5 * v) + 0.5


def _silu(v):
    return v * _sigmoid(v)


def _adaln_kernel(c_ref, w_ref, b_ref, o_ref):
    a = _silu(c_ref[...])
    o_ref[...] = jnp.dot(a, w_ref[...], preferred_element_type=F32, precision=HIGHEST) + b_ref[...]


def _adaln(c, w_ada, b_ada):
    B, D = c.shape
    N = w_ada.shape[1]
    tn = D
    return pl.pallas_call(
        _adaln_kernel,
        grid=(N // tn,),
        in_specs=[pl.BlockSpec((B, D), lambda j: (0, 0)),
                  pl.BlockSpec((D, tn), lambda j: (0, j)),
                  pl.BlockSpec((1, tn), lambda j: (0, j))],
        out_specs=pl.BlockSpec((B, tn), lambda j: (0, j)),
        out_shape=jax.ShapeDtypeStruct((B, N), F32),
        compiler_params=_cparams(("arbitrary",)),
        name="adaln",
    )(c, w_ada, b_ada.reshape(1, N))


def _inproj_kernel(x_ref, mod_ref, g_ref, w_ref, ws_ref, o_ref, os_ref, h_scr):
    @pl.when(pl.program_id(2) == 0)
    def _():
        x = x_ref[0]
        y = x * lax.rsqrt(jnp.mean(x * x, axis=-1, keepdims=True) + NORM_EPS) * g_ref[...]
        h = y * (1.0 + mod_ref[0, 1:2, :]) + mod_ref[0, 0:1, :]
        h_scr[...] = h.astype(BF16)
        os_ref[...] = jnp.dot(h_scr[...], ws_ref[...], preferred_element_type=F32)

    o_ref[...] = jnp.dot(h_scr[...], w_ref[...], preferred_element_type=F32)


def _inproj(x, mod3, g_pre, w_main, w_small):
    B, S, D = x.shape
    tm = min(1024, S)
    tn = N_PROJ // 2
    nrow = S // tm
    return pl.pallas_call(
        _inproj_kernel,
        grid=(B, nrow, N_PROJ // tn),
        in_specs=[pl.BlockSpec((1, tm, D), lambda b, i, j: (b, i, 0)),
                  pl.BlockSpec((1, 6, D), lambda b, i, j: (b, 0, 0)),
                  pl.BlockSpec((1, D), lambda b, i, j: (0, 0)),
                  pl.BlockSpec((D, tn), lambda b, i, j: (0, j)),
                  pl.BlockSpec((D, LANES), lambda b, i, j: (0, 0))],
        out_specs=[pl.BlockSpec((tm, tn), lambda b, i, j: (b * nrow + i, j)),
                   pl.BlockSpec((tm, LANES), lambda b, i, j: (b * nrow + i, 0))],
        out_shape=[jax.ShapeDtypeStruct((B * S, N_PROJ), F32),
                   jax.ShapeDtypeStruct((B * S, LANES), F32)],
        scratch_shapes=[pltpu.VMEM((tm, D), BF16)],
        compiler_params=_cparams(("arbitrary", "arbitrary", "arbitrary")),
        name="inproj",
    )(x, mod3, g_pre.reshape(1, D), w_main, w_small)


def _rope(v, cosf, sinf, lane):
    rot = jnp.where(lane < ROPE_DIM // 2, pltpu.roll(v, LANES - ROPE_DIM // 2, 1), pltpu.roll(v, ROPE_DIM // 2, 1))
    return v * cosf + rot * sinf


def _mobaprep_kernel(q_ref, k_ref, v_ref, cos_ref, sin_ref, qo_ref, ko_ref, vo_ref, km_ref):
    cosf = cos_ref[...]
    sinf = sin_ref[...]
    lane = lax.broadcasted_iota(jnp.int32, cosf.shape, 1)
    for h in range(MB_HEADS):
        sl = slice(h * HEAD_DIM, (h + 1) * HEAD_DIM)
        qo_ref[:, sl] = (_rope(q_ref[:, sl], cosf, sinf, lane) * MB_Q_SCALE).astype(BF16)
        kr = _rope(k_ref[:, sl], cosf, sinf, lane)
        ko_ref[:, sl] = kr.astype(BF16)
        for blk in range(km_ref.shape[0]):
            km_ref[blk, :, sl] = jnp.mean(kr[blk * MB_BLOCK:(blk + 1) * MB_BLOCK], axis=0, keepdims=True)
    vo_ref[...] = v_ref[...].astype(BF16)


def _rope_tables(S):
    half = ROPE_DIM // 2
    inv_freq = ROPE_THETA ** (-jnp.arange(half, dtype=F32) / half)
    ang = jnp.arange(S, dtype=F32)[:, None] * inv_freq[None, :]
    cos, sin = jnp.cos(ang), jnp.sin(ang)
    rest = HEAD_DIM - ROPE_DIM
    cosf = jnp.concatenate([cos, cos, jnp.ones((S, rest), F32)], axis=1)
    sinf = jnp.concatenate([-sin, sin, jnp.zeros((S, rest), F32)], axis=1)
    return cosf, sinf


def _mobaprep(proj, B, S):
    T = B * S
    tb = min(4 * MB_BLOCK, S)
    nb = S // tb
    W = MB_HEADS * HEAD_DIM
    cosf, sinf = _rope_tables(S)
    col = lambda c: (lambda i: (i, c))
    return pl.pallas_call(
        _mobaprep_kernel,
        grid=(T // tb,),
        in_specs=[pl.BlockSpec((tb, W), col(COL_MQ // _W)),
                  pl.BlockSpec((tb, W), col(COL_MK // _W)),
                  pl.BlockSpec((tb, W), col(COL_MV // _W)),
                  pl.BlockSpec((tb, LANES), lambda i: (i % nb, 0)),
                  pl.BlockSpec((tb, LANES), lambda i: (i % nb, 0))],
        out_specs=[pl.BlockSpec((tb, W), lambda i: (i, 0)),
                   pl.BlockSpec((tb, W), lambda i: (i, 0)),
                   pl.BlockSpec((tb, W), lambda i: (i, 0)),
                   pl.BlockSpec((tb // MB_BLOCK, 1, W), lambda i: (i, 0, 0))],
        out_shape=[jax.ShapeDtypeStruct((T, W), BF16),
                   jax.ShapeDtypeStruct((T, W), BF16),
                   jax.ShapeDtypeStruct((T, W), BF16),
                   jax.ShapeDtypeStruct((T // MB_BLOCK, 1, W), F32)],
        compiler_params=_cparams(("arbitrary",)),
        name="mobaprep",
    )(proj, proj, proj, cosf, sinf)


MB_HPS = 4


def _moba_kernel(q_ref, k_ref, v_ref, km_ref, o_ref, s_scr, sd_scr, mx_scr, l_scr, acc_scr):
    qi = pl.program_id(1)
    tb = MB_BLOCK
    nb = km_ref.shape[0]
    nt = tb // LANES
    D = HEAD_DIM
    heads = range(MB_HPS)
    hs = [slice(h * D, (h + 1) * D) for h in heads]
    qs = [q_ref[:, hs[h]] for h in heads]

    sts = [lax.dot_general(km_ref[:, hs[h]], qs[h].astype(F32), (((1,), (1,)), ((), ())),
                           preferred_element_type=F32, precision=HIGHEST) for h in heads]
    blk = lax.broadcasted_iota(jnp.int32, (nb, tb), 0)
    rowid = lax.broadcasted_iota(jnp.int32, (LANES, tb), 0)
    sel_ts = []
    for h in heads:
        st = jnp.where(blk < qi, sts[h], -jnp.inf)
        sel_t = jnp.full((LANES, tb), -1.0, F32)
        for r in range(MB_TOPK):
            m = jnp.max(st, axis=0, keepdims=True)
            idx = jnp.min(jnp.where(st == m, blk, nb), axis=0, keepdims=True)
            sel_t = jnp.where(rowid == r, jnp.where(r < qi, idx, -1).astype(F32), sel_t)
            st = jnp.where(blk == idx, -jnp.inf, st)
        sel_ts.append(sel_t)
    sels = [jnp.transpose(t) for t in sel_ts]
    sel_rep = [[jnp.broadcast_to(sels[h][:, r:r + 1], (tb, LANES)) for r in range(MB_TOPK)] for h in heads]

    def logits(h, start, width):
        kslab = k_ref[pl.ds(pl.multiple_of(start, tb), width), hs[h]]
        return lax.dot_general(qs[h], kslab, (((1,), (1,)), ((), ())), preferred_element_type=F32)

    r_i = lax.broadcasted_iota(jnp.int32, (tb, LANES), 0)
    c_i = lax.broadcasted_iota(jnp.int32, (tb, LANES), 1)
    sds = [logits(h, qi * tb, tb) for h in heads]
    for h in heads:
        mx = jnp.full((tb, LANES), NEG_BIG, F32)
        for t in range(nt):
            piece = jnp.where(c_i + t * LANES <= r_i, sds[h][:, t * LANES:(t + 1) * LANES], NEG_BIG)
            sd_scr[h, :, t * LANES:(t + 1) * LANES] = piece
            mx = jnp.maximum(mx, piece)
        mx_scr[h] = mx

    n_pairs = (qi + 1) // 2

    def pass1(pairs):
        s2s = [[logits(h, 2 * i * tb, 2 * tb) for h in heads] for i in pairs]
        for h in heads:
            mx = mx_scr[h]
            for n, i in enumerate(pairs):
                for half in range(2):
                    kbf = (2 * i + half).astype(F32)
                    hit = (sel_rep[h][0] == kbf) | (sel_rep[h][1] == kbf) | (sel_rep[h][2] == kbf)
                    for t in range(nt):
                        c0 = (half * nt + t) * LANES
                        piece = jnp.where(hit, s2s[n][h][:, c0:c0 + LANES], NEG_BIG)
                        s_scr[h, i, :, c0:c0 + LANES] = piece
                        mx = jnp.maximum(mx, piece)
            mx_scr[h] = mx

    def run_pairs(fn):
        def body(j, carry):
            fn([2 * j, 2 * j + 1])
            return carry
        lax.fori_loop(0, n_pairs // 2, body, 0)
        pl.when(n_pairs % 2 == 1)(lambda: fn([n_pairs - 1]))

    run_pairs(pass1)
    m_reps = [jnp.broadcast_to(jnp.max(mx_scr[h], axis=-1, keepdims=True), (tb, LANES)) for h in heads]

    def probs(load, width, m_rep):
        ps, lsum = [], jnp.zeros((tb, LANES), F32)
        for t in range(width // LANES):
            p = jnp.exp2(load(t) - m_rep)
            lsum = lsum + p
            ps.append(p.astype(BF16))
        return jnp.concatenate(ps, axis=1), lsum

    pls = [probs(lambda t, h=h: sd_scr[h, :, t * LANES:(t + 1) * LANES], tb, m_reps[h]) for h in heads]
    for h in heads:
        l_scr[h] = pls[h][1]
        acc_scr[h] = jnp.dot(pls[h][0], v_ref[pl.ds(pl.multiple_of(qi * tb, tb), tb), hs[h]],
                             preferred_element_type=F32)

    def pass2(pairs):
        pls = [[probs(lambda t, h=h, i=i: s_scr[h, i, :, t * LANES:(t + 1) * LANES], 2 * tb, m_reps[h])
                for h in heads] for i in pairs]
        for h in heads:
            lsum, acc = l_scr[h], acc_scr[h]
            for n, i in enumerate(pairs):
                vslab = v_ref[pl.ds(pl.multiple_of(2 * i * tb, tb), 2 * tb), hs[h]]
                lsum = lsum + pls[n][h][1]
                acc = acc + jnp.dot(pls[n][h][0], vslab, preferred_element_type=F32)
            l_scr[h], acc_scr[h] = lsum, acc

    run_pairs(pass2)
    for h in heads:
        o_ref[:, hs[h]] = (acc_scr[h] / jnp.sum(l_scr[h], axis=-1, keepdims=True)).astype(o_ref.dtype)


def _moba(qr, kr, vb, kmean, B, S):
    tb = MB_BLOCK
    nb = S // tb
    T = B * S
    km = kmean.reshape(B, nb, MB_HEADS * HEAD_DIM)
    HW = MB_HPS * HEAD_DIM
    ng = MB_HEADS // MB_HPS
    return pl.pallas_call(
        _moba_kernel,
        grid=(B * ng, nb),
        in_specs=[pl.BlockSpec((tb, HW), lambda g, i: ((g // ng) * nb + i, g % ng)),
                  pl.BlockSpec((S, HW), lambda g, i: (g // ng, g % ng)),
                  pl.BlockSpec((S, HW), lambda g, i: (g // ng, g % ng)),
                  pl.BlockSpec((None, nb, HW), lambda g, i: (g // ng, 0, g % ng))],
        out_specs=pl.BlockSpec((tb, HW), lambda g, i: ((g // ng) * nb + i, g % ng)),
        out_shape=jax.ShapeDtypeStruct((T, MB_HEADS * HEAD_DIM), BF16),
        scratch_shapes=[pltpu.VMEM((MB_HPS, (nb + 1) // 2, tb, 2 * tb), F32),
                        pltpu.VMEM((MB_HPS, tb, tb), F32),
                        pltpu.VMEM((MB_HPS, tb, LANES), F32),
                        pltpu.VMEM((MB_HPS, tb, LANES), F32),
                        pltpu.VMEM((MB_HPS, tb, HEAD_DIM), F32)],
        compiler_params=_cparams(("arbitrary", "arbitrary")),
        name="moba",
    )(qr, kr, vb, km)


DN_TILE_CHUNKS = 8


def _softplus(v):
    return jnp.maximum(v, 0.0) + jnp.log1p(jnp.exp(-jnp.abs(v)))


DN_GROUP = 2
DN_HPS = 4


def _split(a):
    hi = pltpu.bitcast(pltpu.bitcast(a, jnp.uint32) & jnp.uint32(0xFFFF0000), F32)
    return hi.astype(BF16), (a - hi).astype(BF16)


def _dot3(ah, al, bh, bl):
    lhs = jnp.concatenate([ah, ah, al], axis=1)
    rhs = jnp.concatenate([bh, bl, bh], axis=0)
    return jnp.dot(lhs, rhs, preferred_element_type=F32)


def _dot_bf(a, b):
    return jnp.dot(a.astype(BF16), b.astype(BF16), preferred_element_type=F32)


def _deltanet_kernel(q_ref, k_ref, v_ref, z_ref, sm_ref, wq_ref, wk_ref, wv_ref, alog_ref, dtb_ref, gout_ref,
                     o_ref, xp_scr, state_scr):
    i = pl.program_id(2)
    C = DN_CHUNK
    TR = q_ref.shape[0]
    HALO = 8

    @pl.when(i == 0)
    def _():
        xp_scr[...] = jnp.zeros(xp_scr.shape, F32)
        state_scr[...] = jnp.zeros(state_scr.shape, F32)

    def conv_silu(slot, x_ref, w_ref, sl):
        xp_scr[slot, 0:HALO, sl] = xp_scr[slot, TR:TR + HALO, sl]
        xp_scr[slot, HALO:HALO + TR, sl] = x_ref[:, sl]
        acc = w_ref[DN_CONV - 1:DN_CONV, sl] * xp_scr[slot, HALO:HALO + TR, sl]
        for j in range(1, DN_CONV):
            acc = acc + w_ref[DN_CONV - 1 - j:DN_CONV - j, sl] * xp_scr[slot, HALO - j:HALO - j + TR, sl]
        return _silu(acc)

    sm = sm_ref[...]
    lane = lax.broadcasted_iota(jnp.int32, sm.shape, 1)
    row = lax.broadcasted_iota(jnp.int32, sm.shape, 0)
    beta_all = _sigmoid(sm)
    g_all = -jnp.exp(alog_ref[...]) * _softplus(sm + dtb_ref[...])
    pos = row % C
    gc_all = g_all
    shift = 1
    while shift < C:
        gc_all = gc_all + jnp.where(pos >= shift, pltpu.roll(gc_all, shift, 0), 0.0)
        shift *= 2
    gc_t = jnp.transpose(gc_all)
    row_t = lax.broadcasted_iota(jnp.int32, gc_t.shape, 0)

    G = DN_GROUP * C
    ri = lax.broadcasted_iota(jnp.int32, (G, G), 0)
    ci = lax.broadcasted_iota(jnp.int32, (G, G), 1)
    same = (ri // C) == (ci // C)
    tril = same & (ci <= ri)
    strict = same & (ci < ri)
    eye = (ci == ri).astype(F32)
    D = HEAD_DIM
    gout = gout_ref[...]

    def head_pipeline(hh):
        h = pl.program_id(1) * DN_HPS + hh
        sl = slice(hh * D, (hh + 1) * D)
        q = conv_silu(0, q_ref, wq_ref, sl)
        k = conv_silu(1, k_ref, wk_ref, sl)
        yield
        v = conv_silu(2, v_ref, wv_ref, sl)
        q = q * lax.rsqrt(jnp.sum(q * q, axis=-1, keepdims=True) + 1e-6) * (D ** -0.5)
        k = k * lax.rsqrt(jnp.sum(k * k, axis=-1, keepdims=True) + 1e-6)
        yield
        beta = jnp.sum(jnp.where(lane == h, beta_all, 0.0), axis=1, keepdims=True)
        gc = jnp.sum(jnp.where(lane == DN_HEADS + h, gc_all, 0.0), axis=1, keepdims=True)
        gc_row = jnp.sum(jnp.where(row_t == DN_HEADS + h, gc_t, 0.0), axis=0, keepdims=True)
        head = (q, k, v, beta, gc, gc_row, jnp.transpose(k))
        yield
        steps = [None] * (TR // C)
        yield from _round_robin([_deltanet_group(head, g, tril, strict, eye, steps) for g in range(TR // G)])
        outs = []
        yield from _deltanet_chain(hh, steps, state_scr, outs)
        o = jnp.concatenate(outs, axis=0)
        y = o * lax.rsqrt(jnp.mean(o * o, axis=-1, keepdims=True) + NORM_EPS) * gout
        o_ref[:, sl] = (y * _silu(z_ref[:, sl])).astype(o_ref.dtype)

    for _ in _round_robin([head_pipeline(hh) for hh in range(DN_HPS)]):
        pass


def _round_robin(gens):
    active = list(gens)
    while active:
        still = []
        for gen in active:
            try:
                next(gen)
                still.append(gen)
            except StopIteration:
                pass
        active = still
        yield


def _deltanet_group(head, g, tril, strict, eye, steps_out):
    C = DN_CHUNK
    D = HEAD_DIM
    G = DN_GROUP * C
    q, k, v, beta, gc, gc_row, k_t = head
    r0 = g * G
    qg, kg, vg = q[r0:r0 + G], k[r0:r0 + G], v[r0:r0 + G]
    bg = beta[r0:r0 + G]
    gcg = gc[r0:r0 + G]
    gcr = gc_row[:, r0:r0 + G]
    ktg = k_t[:, r0:r0 + G]
    decay = jnp.where(tril, jnp.exp(jnp.where(tril, gcg - gcr, 0.0)), 0.0)
    kb = kg * bg
    aq = _dot_bf(jnp.concatenate([kb, qg], axis=0), ktg)
    yield
    m_neg = jnp.where(strict, -(aq[:G] * decay), 0.0)
    qk = (aq[G:] * decay).astype(BF16)
    t_inv = eye + m_neg
    ph, pl_ = _split(m_neg)
    for _ in range(5):
        th, tl = _split(t_inv)
        ph, pl_ = _split(_dot3(ph, pl_, ph, pl_))
        yield
        t_inv = t_inv + _dot3(th, tl, ph, pl_)
        yield
    egc = jnp.exp(gcg)
    th, tl = _split(t_inv)
    rh, rl = _split(jnp.concatenate([kb * egc, vg * bg], axis=1))
    wu = _dot3(th, tl, rh, rl).astype(BF16)
    yield
    qr = jnp.dot(qk, wu, preferred_element_type=F32)
    qp = qg * egc - qr[:, :D]
    r_all = qr[:, D:]
    yield
    for c in range(DN_GROUP):
        c0 = c * C
        g_last = gcg[c0 + C - 1:c0 + C, :]
        kt_tail = ktg[:, c0:c0 + C] * jnp.exp(g_last - gcr[:, c0:c0 + C])
        gh = jnp.dot(kt_tail.astype(BF16), wu[c0:c0 + C, :], preferred_element_type=F32)
        lhs = jnp.concatenate([gh[:, :D], qp[c0:c0 + C]], axis=0).astype(BF16)
        steps_out[g * DN_GROUP + c] = (lhs, gh[:, D:], r_all[c0:c0 + C], jnp.exp(g_last))
        yield


def _deltanet_chain(hh, steps, state_scr, outs):
    D = HEAD_DIM
    state = state_scr[hh]
    for lhs, h_add, r_add, dec in steps:
        res = jnp.dot(lhs, state.astype(BF16), preferred_element_type=F32)
        outs.append(res[D:] + r_add)
        state = state * dec - res[:D] + h_add
        yield
    state_scr[hh] = state


def _deltanet(proj, small, conv_w, a_log, dt_bias, g_dn_out, B, S):
    T = B * S
    TR = min(DN_TILE_CHUNKS * DN_CHUNK, S)
    nt = S // TR
    pad = jnp.zeros((DN_HEADS,), F32)
    rest = jnp.zeros((LANES - 2 * DN_HEADS,), F32)
    alog_lane = jnp.concatenate([pad, a_log.astype(F32), rest]).reshape(1, LANES)
    dtb_lane = jnp.concatenate([pad, dt_bias.astype(F32), rest]).reshape(1, LANES)
    HW = DN_HPS * HEAD_DIM
    per = HW // LANES
    rows = lambda c0: (lambda b, h, i: (b * nt + i, c0 // per + h))
    wcol = lambda c0: (lambda b, h, i: (0, c0 // per + h))
    const = lambda b, h, i: (0, 0)
    return pl.pallas_call(
        _deltanet_kernel,
        grid=(B, DN_HEADS // DN_HPS, nt),
        in_specs=[pl.BlockSpec((TR, HW), rows(COL_DQ)),
                  pl.BlockSpec((TR, HW), rows(COL_DK)),
                  pl.BlockSpec((TR, HW), rows(COL_DV)),
                  pl.BlockSpec((TR, HW), rows(COL_DZ)),
                  pl.BlockSpec((TR, LANES), lambda b, h, i: (b * nt + i, 0)),
                  pl.BlockSpec((DN_CONV, HW), wcol(0)),
                  pl.BlockSpec((DN_CONV, HW), wcol(DN_HEADS)),
                  pl.BlockSpec((DN_CONV, HW), wcol(2 * DN_HEADS)),
                  pl.BlockSpec((1, LANES), const),
                  pl.BlockSpec((1, LANES), const),
                  pl.BlockSpec((1, HEAD_DIM), const)],
        out_specs=pl.BlockSpec((TR, HW), lambda b, h, i: (b * nt + i, h)),
        out_shape=jax.ShapeDtypeStruct((T, DN_HEADS * HEAD_DIM), BF16),
        scratch_shapes=[pltpu.VMEM((3, TR + 8, HW), F32), pltpu.VMEM((DN_HPS, HEAD_DIM, HEAD_DIM), F32)],
        compiler_params=_cparams(("arbitrary", "arbitrary", "arbitrary")),
        name="deltanet",
    )(proj, proj, proj, proj, small, conv_w, conv_w, conv_w, alog_lane, dtb_lane, g_dn_out.reshape(1, HEAD_DIM))


def _rms(v):
    return v * lax.rsqrt(jnp.mean(v * v, axis=-1, keepdims=True) + NORM_EPS)


def _merge_kernel(oa_ref, ob_ref, ga_ref, gb_ref, x_ref, mod_ref, gpost_ref, gpre_ref, wa_ref, wb_ref, wo_ref,
                  wrt_ref, br_ref, x1_ref, hp_ref, idx_ref, wrow_ref, rank_ref, cnt_ref, carry_scr):
    E = N_EXPERTS
    tm = x_ref.shape[0]
    half = x_ref.shape[1] // 2

    @pl.when(pl.program_id(0) == 0)
    def _():
        carry_scr[...] = jnp.zeros(carry_scr.shape, F32)

    ya = jnp.dot(oa_ref[...], wa_ref[...], preferred_element_type=F32)
    yb = jnp.dot(ob_ref[...], wb_ref[...], preferred_element_type=F32)
    merged = _sigmoid(ga_ref[...]) * ya + _sigmoid(gb_ref[...]) * yb
    mix = jnp.dot(merged.astype(BF16), wo_ref[...], preferred_element_type=F32)
    x1 = x_ref[...] + mod_ref[0, 2:3, :] * (_rms(mix) * gpost_ref[...])
    x1_ref[...] = x1
    h2 = (_rms(x1) * gpre_ref[...]) * (1.0 + mod_ref[0, 4:5, :]) + mod_ref[0, 3:4, :]

    lo_bits = pltpu.bitcast(h2[:, :half].astype(BF16).astype(F32), jnp.uint32) >> 16
    hi_bits = pltpu.bitcast(h2[:, half:].astype(BF16).astype(F32), jnp.uint32) & jnp.uint32(0xFFFF0000)
    hp_ref[...] = hi_bits | lo_bits

    lt = lax.dot_general(wrt_ref[...], h2, (((1,), (1,)), ((), ())), preferred_element_type=F32,
                         precision=HIGHEST) + br_ref[...]
    eid = lax.broadcasted_iota(jnp.int32, (E, tm), 0)
    vals, idxs = [], []
    for _ in range(TOP_K):
        m = jnp.max(lt, axis=0, keepdims=True)
        idx = jnp.min(jnp.where(lt == m, eid, E), axis=0, keepdims=True)
        vals.append(m)
        idxs.append(idx)
        lt = jnp.where(eid == idx, -jnp.inf, lt)
    exps = [jnp.exp(v - vals[0]) for v in vals]
    den = exps[0] + exps[1] + exps[2] + exps[3]
    wts = [e / den for e in exps]

    hot = jnp.zeros((E, tm), F32)
    for idx in idxs:
        hot = hot + (eid == idx).astype(F32)
    ti = lax.broadcasted_iota(jnp.int32, (tm, tm), 0)
    tj = lax.broadcasted_iota(jnp.int32, (tm, tm), 1)
    before = (ti < tj).astype(BF16)
    prior = carry_scr[...][:, 0:1] + jnp.dot(hot.astype(BF16), before, preferred_element_type=F32)
    row8 = lax.broadcasted_iota(jnp.int32, (8, tm), 0)
    row128 = lax.broadcasted_iota(jnp.int32, (LANES, tm), 0)
    idx8 = jnp.zeros((8, tm), jnp.int32)
    rank8 = jnp.zeros((8, tm), jnp.int32)
    w128 = jnp.zeros((LANES, tm), F32)
    for r in range(TOP_K):
        rank_r = jnp.sum(jnp.where(eid == idxs[r], prior, 0.0), axis=0, keepdims=True)
        idx8 = jnp.where(row8 == r, idxs[r], idx8)
        rank8 = jnp.where(row8 == r, rank_r.astype(jnp.int32), rank8)
        w128 = jnp.where(row128 == r, wts[r], w128)
    idx_ref[...] = idx8
    rank_ref[...] = rank8
    wrow_ref[...] = jnp.transpose(w128)
    carry = carry_scr[...] + jnp.sum(hot, axis=1, keepdims=True)
    carry_scr[...] = carry
    cnt_ref[...] = carry


def _merge(oa, ob, proj, x2, mod3, g_post_mix, g_pre_ffn, w_br_a, w_br_b, w_o, w_router, b_router, S):
    T, D = x2.shape
    E = N_EXPERTS
    tm = min(512, S)
    per_b = S // tm
    W = DN_HEADS * HEAD_DIM
    row = lambda i: (i, 0)
    const = lambda i: (0, 0)
    lane_t = lambda i: (0, i)
    return pl.pallas_call(
        _merge_kernel,
        grid=(T // tm,),
        in_specs=[pl.BlockSpec((tm, W), row),
                  pl.BlockSpec((tm, W), row),
                  pl.BlockSpec((tm, D), lambda i: (i, COL_GA * LANES // D)),
                  pl.BlockSpec((tm, D), lambda i: (i, COL_GB * LANES // D)),
                  pl.BlockSpec((tm, D), row),
                  pl.BlockSpec((1, 6, D), lambda i: (i // per_b, 0, 0)),
                  pl.BlockSpec((1, D), const),
                  pl.BlockSpec((1, D), const),
                  pl.BlockSpec((W, D), const),
                  pl.BlockSpec((W, D), const),
                  pl.BlockSpec((D, D), const),
                  pl.BlockSpec((E, D), const),
                  pl.BlockSpec((E, 1), const)],
        out_specs=[pl.BlockSpec((tm, D), row),
                   pl.BlockSpec((tm, D // 2), row),
                   pl.BlockSpec((8, tm), lane_t),
                   pl.BlockSpec((tm, LANES), row),
                   pl.BlockSpec((8, tm), lane_t),
                   pl.BlockSpec((E, LANES), const)],
        out_shape=[jax.ShapeDtypeStruct((T, D), F32),
                   jax.ShapeDtypeStruct((T, D // 2), jnp.uint32),
                   jax.ShapeDtypeStruct((8, T), jnp.int32),
                   jax.ShapeDtypeStruct((T, LANES), F32),
                   jax.ShapeDtypeStruct((8, T), jnp.int32),
                   jax.ShapeDtypeStruct((E, LANES), F32)],
        scratch_shapes=[pltpu.VMEM((E, LANES), F32)],
        compiler_params=_cparams(("arbitrary",)),
        name="merge",
    )(oa, ob, proj, proj, x2, mod3, g_post_mix.reshape(1, D), g_pre_ffn.reshape(1, D),
      w_br_a.astype(BF16), w_br_b.astype(BF16), w_o.astype(BF16),
      jnp.transpose(w_router).astype(F32), b_router.reshape(E, 1).astype(F32))


DISPATCH_TOKENS = 512


def _dispatch_kernel(fill_ref, dest_ref, src_ref, dst_ref, zero_scr, sem, zsem):
    n = dest_ref.shape[1]
    tm = zero_scr.shape[0]

    @pl.when(pl.program_id(0) == 0)
    def _():
        zero_scr[...] = jnp.zeros(zero_scr.shape, zero_scr.dtype)

        def fill_copy(k):
            return pltpu.make_async_copy(zero_scr, dst_ref.at[pl.ds(pl.multiple_of(k * tm, tm), tm)], zsem)

        def start(k, carry):
            pl.when(fill_ref[k] == 1)(lambda: fill_copy(k).start())
            return carry

        def wait(k, carry):
            pl.when(fill_ref[k] == 1)(lambda: fill_copy(k).wait())
            return carry

        lax.fori_loop(0, fill_ref.shape[0], start, 0)
        lax.fori_loop(0, fill_ref.shape[0], wait, 0)

    def issue(j, carry):
        for r in range(TOP_K):
            pltpu.make_async_copy(src_ref.at[pl.ds(j, 1)], dst_ref.at[pl.ds(dest_ref[r, j], 1)], sem).start()
        return carry

    lax.fori_loop(0, n, issue, 0, unroll=8)
    for r in range(TOP_K):
        pltpu.make_async_copy(src_ref, dst_ref.at[pl.ds(0, n)], sem).wait()


def _dispatch(dest, blk_fill, hp):
    T, Wd = hp.shape
    n = min(DISPATCH_TOKENS, T)
    tm = EXPERT_ROWS
    grid_spec = pltpu.PrefetchScalarGridSpec(
        num_scalar_prefetch=1,
        grid=(T // n,),
        in_specs=[pl.BlockSpec((TOP_K, n), lambda i, fill: (0, i), memory_space=pltpu.SMEM),
                  pl.BlockSpec((n, Wd), lambda i, fill: (i, 0))],
        out_specs=pl.BlockSpec(memory_space=pl.ANY),
        scratch_shapes=[pltpu.VMEM((tm, Wd), hp.dtype), pltpu.SemaphoreType.DMA(()), pltpu.SemaphoreType.DMA(())],
    )
    return pl.pallas_call(
        _dispatch_kernel,
        grid_spec=grid_spec,
        out_shape=jax.ShapeDtypeStruct((_expert_blocks(T * TOP_K) * tm, Wd), hp.dtype),
        compiler_params=_cparams(("arbitrary",)),
        name="dispatch",
    )(blk_fill, dest, hp)


EXPERT_ROWS = 512
EXPERT_FEATURE_TILE = 512


def _experts_kernel(exp_ref, valid_ref, new_ref, x_ref, wgu_ref, bgu_ref, wdn_ref, bdn_ref, o_ref, wgu_bf, wdn_bf):
    k = pl.program_id(0)
    F = wdn_ref.shape[0]

    @pl.when(new_ref[k] == 1)
    def _():
        wgu_bf[...] = wgu_ref[...].astype(BF16)
        wdn_bf[...] = wdn_ref[...].astype(BF16)

    def ffn():
        word = x_ref[...]
        x = jnp.concatenate([pltpu.bitcast(word << 16, F32).astype(BF16),
                             pltpu.bitcast(word & jnp.uint32(0xFFFF0000), F32).astype(BF16)], axis=1)

        def gate_up(c):
            ft = slice(c, c + EXPERT_FEATURE_TILE)
            ut = slice(F + c, F + c + EXPERT_FEATURE_TILE)
            return (jnp.dot(x, wgu_bf[:, ft], preferred_element_type=F32) + bgu_ref[:, ft],
                    jnp.dot(x, wgu_bf[:, ut], preferred_element_type=F32) + bgu_ref[:, ut])

        tiles = list(range(0, F, EXPERT_FEATURE_TILE))
        y = None
        nxt = gate_up(tiles[0])
        for t, c in enumerate(tiles):
            gate, up = nxt
            if t + 1 < len(tiles):
                nxt = gate_up(tiles[t + 1])
            gate = jnp.minimum(gate, SWIGLU_LIMIT)
            up = jnp.clip(up, -SWIGLU_LIMIT, SWIGLU_LIMIT)
            act = (up + 1.0) * gate * _sigmoid(SWIGLU_ALPHA * gate)
            part = jnp.dot(act.astype(BF16), wdn_bf[c:c + EXPERT_FEATURE_TILE, :], preferred_element_type=F32)
            y = part + bdn_ref[...] if y is None else y + part
        return y

    @pl.when(valid_ref[k] == 1)
    def _():
        o_ref[:, 0, :] = ffn()

    @pl.when(valid_ref[k] == 0)
    def _():
        o_ref[...] = jnp.zeros(o_ref.shape, o_ref.dtype)


def _experts(xs, blk_exp, blk_valid, blk_new, w_gu, b_gu, w_down, b_down):
    R, half = xs.shape
    E, D, F2 = w_gu.shape
    F = F2 // 2
    tm = EXPERT_ROWS
    grid_spec = pltpu.PrefetchScalarGridSpec(
        num_scalar_prefetch=3,
        grid=(R // tm,),
        in_specs=[pl.BlockSpec((tm, half), lambda k, ex, va, nw: (k, 0)),
                  pl.BlockSpec((None, D, F2), lambda k, ex, va, nw: (ex[k], 0, 0)),
                  pl.BlockSpec((None, 1, F2), lambda k, ex, va, nw: (ex[k], 0, 0)),
                  pl.BlockSpec((None, F, D), lambda k, ex, va, nw: (ex[k], 0, 0)),
                  pl.BlockSpec((None, 1, D), lambda k, ex, va, nw: (ex[k], 0, 0))],
        out_specs=pl.BlockSpec((tm, 1, D), lambda k, ex, va, nw: (k, 0, 0)),
        scratch_shapes=[pltpu.VMEM((D, F2), BF16), pltpu.VMEM((F, D), BF16)],
    )
    return pl.pallas_call(
        _experts_kernel,
        grid_spec=grid_spec,
        out_shape=jax.ShapeDtypeStruct((R, 1, D), F32),
        compiler_params=_cparams(("arbitrary",)),
        name="experts",
    )(blk_exp, blk_valid, blk_new, xs, w_gu, b_gu.reshape(E, 1, F2).astype(F32),
      w_down, b_down.reshape(E, 1, D).astype(F32))


def _expert_blocks(A):
    return A // EXPERT_ROWS + N_EXPERTS


def _block_tables(counts, A):
    E = N_EXPERTS
    tm = EXPERT_ROWS
    n_blk = _expert_blocks(A)
    nblk_e = (counts + tm - 1) // tm
    blk_end = jnp.cumsum(nblk_e)
    blk_start = blk_end - nblk_e
    used = blk_end[-1]
    k = jnp.arange(n_blk, dtype=jnp.int32)
    kk = jnp.minimum(k, used - 1)
    e = jnp.sum((blk_end[None, :] <= kk[:, None]).astype(jnp.int32), axis=1)
    hot = e[:, None] == jnp.arange(E, dtype=jnp.int32)[None, :]
    pick = lambda table: jnp.sum(jnp.where(hot, table[None, :], 0), axis=1)
    valid = k < used
    new = valid & (k == pick(blk_start))
    fill = jnp.logical_not(valid) | (k == pick(blk_end) - 1)
    i32 = lambda v: v.astype(jnp.int32)
    return blk_start * tm, i32(e), i32(valid), i32(new), i32(fill)


COMBINE_TOKENS = 512


def _combine_kernel(dest_ref, nxt_ref, y_ref, wrow_ref, x1_ref, mod_ref, gpost_ref, o_ref, stage, moe_scr, sem):
    i = pl.program_id(0)
    last = pl.num_programs(0) - 1
    n = x1_ref.shape[0]
    SUB = 8

    def gather(idx_ref, j, s):
        for r in range(TOP_K):
            pltpu.make_async_copy(y_ref.at[idx_ref[r, j]], stage.at[s, r, j], sem.at[s]).start(priority=r % 2)

    @pl.when(i == 0)
    def _():
        def prime(j, carry):
            gather(dest_ref, j, 0)
            return carry
        lax.fori_loop(0, n, prime, 0, unroll=SUB)

    def step(slot):
        for r in range(TOP_K):
            pltpu.make_async_copy(y_ref.at[pl.ds(0, n)], stage.at[slot, r], sem.at[slot]).wait()

        def reduce_group(g):
            rows = pl.ds(pl.multiple_of(g * SUB, SUB), SUB)
            w = wrow_ref[rows, :]
            acc = w[:, 0:1] * stage[slot, 0, rows, 0, :]
            for r in range(1, TOP_K):
                acc = acc + w[:, r:r + 1] * stage[slot, r, rows, 0, :]
            moe_scr[rows, :] = acc

        @pl.when(i < last)
        def _():
            def body(g, carry):
                for t in range(SUB):
                    gather(nxt_ref, g * SUB + t, 1 - slot)
                reduce_group(g)
                return carry
            lax.fori_loop(0, n // SUB, body, 0)

        @pl.when(i == last)
        def _():
            def body(g, carry):
                reduce_group(g)
                return carry
            lax.fori_loop(0, n // SUB, body, 0)

    for parity in range(2):
        pl.when(i % 2 == parity)(functools.partial(step, parity))

    o_ref[...] = x1_ref[...] + mod_ref[0, 5:6, :] * (_rms(moe_scr[...]) * gpost_ref[...])


def _combine(dest, y, wrow, x1, mod3, g_post_ffn, S):
    T, D = x1.shape
    n = min(COMBINE_TOKENS, S)
    per_b = S // n
    steps = T // n
    return pl.pallas_call(
        _combine_kernel,
        grid=(steps,),
        in_specs=[pl.BlockSpec((TOP_K, n), lambda i: (0, i), memory_space=pltpu.SMEM),
                  pl.BlockSpec((TOP_K, n), lambda i: (0, jnp.minimum(i + 1, steps - 1)), memory_space=pltpu.SMEM),
                  pl.BlockSpec(memory_space=pl.ANY),
                  pl.BlockSpec((n, LANES), lambda i: (i, 0)),
                  pl.BlockSpec((n, D), lambda i: (i, 0)),
                  pl.BlockSpec((1, 6, D), lambda i: (i // per_b, 0, 0)),
                  pl.BlockSpec((1, D), lambda i: (0, 0))],
        out_specs=pl.BlockSpec((n, D), lambda i: (i, 0)),
        out_shape=jax.ShapeDtypeStruct((T, D), F32),
        scratch_shapes=[pltpu.VMEM((2, TOP_K, n, 1, D), F32), pltpu.VMEM((n, D), F32),
                        pltpu.SemaphoreType.DMA((2,))],
        compiler_params=_cparams(("arbitrary",)),
        name="combine",
    )(dest, dest, y, wrow, x1, mod3, g_post_ffn.reshape(1, D))


def _regroup_w_in(w_in):
    D = w_in.shape[0]
    dw = DN_HEADS * HEAD_DIM
    mw = MB_HEADS * HEAD_DIM
    cuts = np.cumsum([dw, dw, dw, dw, DN_HEADS, DN_HEADS, mw, mw, mw, D, D])[:-1]
    dq, dk, dv, dz, db, da, mq, mk, mv, ga, gb = jnp.split(w_in, [int(c) for c in cuts], axis=1)
    small = jnp.concatenate([db, da, jnp.zeros((D, LANES - 2 * DN_HEADS), w_in.dtype)], axis=1).astype(BF16)
    return jnp.concatenate([p.astype(BF16) for p in (ga, gb, dq, dk, dv, dz, mq, mk, mv)], axis=1), small


def kernel(x, c, w_ada, b_ada, g_pre_mix, g_post_mix, g_pre_ffn, g_post_ffn, w_in, conv_w, a_log, dt_bias,
           g_dn_out, w_br_a, w_br_b, w_o, w_router, b_router, w_gu, b_gu, w_down, b_down):
    B, S, D = x.shape
    l = 0
    mod = _adaln(c, w_ada[l], b_ada[l]).reshape(B, 6, D)
    proj, small = _inproj(x, mod, g_pre_mix[l], *_regroup_w_in(w_in[l]))
    qr, kr, vb, kmean = _mobaprep(proj, B, S)
    ob = _moba(qr, kr, vb, kmean, B, S)
    oa = _deltanet(proj, small, conv_w[l], a_log[l], dt_bias[l], g_dn_out[l], B, S)
    x1, hp, idx8, wrow, rank8, cnt = _merge(oa, ob, proj, x.reshape(B * S, D), mod, g_post_mix[l], g_pre_ffn[l],
                                            w_br_a[l], w_br_b[l], w_o[l], w_router[l], b_router[l], S)
    out = _moe(x1, hp, idx8, wrow, rank8, cnt, mod, g_post_ffn[l], w_gu[l], b_gu[l], w_down[l], b_down[l], S)
    return out.reshape(B, S, D)


def _moe(x1, hp, idx8, wrow, rank8, cnt, mod, g_post_ffn, w_gu, b_gu, w_down, b_down, S):
    T = x1.shape[0]
    counts = cnt[:, 0].astype(jnp.int32)
    start, blk_exp, blk_valid, blk_new, blk_fill = _block_tables(counts, T * TOP_K)
    hot = idx8[:TOP_K, :, None] == jnp.arange(N_EXPERTS, dtype=jnp.int32)
    dest = rank8[:TOP_K] + jnp.sum(jnp.where(hot, start, 0), axis=-1)
    xs = _dispatch(dest, blk_fill, hp)
    y = _experts(xs, blk_exp, blk_valid, blk_new, w_gu, b_gu, w_down, b_down)
    return _combine(dest, y, wrow, x1, mod, g_post_ffn, S)
```

```python
import functools

import jax
import jax.numpy as jnp
import numpy as np
from jax import lax
from jax.experimental import pallas as pl
from jax.experimental.pallas import tpu as pltpu

F32 = jnp.float32
BF16 = jnp.bfloat16
HIGHEST = lax.Precision.HIGHEST

HEAD_DIM = 128
DN_HEADS = 4
DN_CONV = 4
DN_CHUNK = 64
MB_HEADS = 4
MB_BLOCK = 256
MB_TOPK = 3
ROPE_THETA = 500000.0
ROPE_DIM = HEAD_DIM // 4
N_EXPERTS = 32
TOP_K = 4
SWIGLU_LIMIT = 7.0
SWIGLU_ALPHA = 1.702
NORM_EPS = 1e-6
LANES = 128
NEG_BIG = -1e30
MB_Q_SCALE = HEAD_DIM ** -0.5 * float(np.log2(np.e))

_W = DN_HEADS * HEAD_DIM // LANES
COL_GA = 0
COL_GB = 8
COL_DQ = 16
COL_DK = 20
COL_DV = 24
COL_DZ = 28
COL_MQ = 32
COL_MK = 36
COL_MV = 40
N_PROJ = 44 * LANES

VMEM_LIMIT = 56 * 1024 * 1024


def _cparams(sem):
    return pltpu.CompilerParams(dimension_semantics=sem, vmem_limit_bytes=VMEM_LIMIT)


def _sigmoid(v):
    return 0.5 * jnp.tanh(0.5 * v) + 0.5


def _silu(v):
    return v * _sigmoid(v)


def _adaln_kernel(c_ref, w_ref, b_ref, o_ref):
    a = _silu(c_ref[...])
    o_ref[...] = jnp.dot(a, w_ref[...], preferred_element_type=F32, precision=HIGHEST) + b_ref[...]


def _adaln(c, w_ada, b_ada):
    B, D = c.shape
    N = w_ada.shape[1]
    tn = D
    return pl.pallas_call(
        _adaln_kernel,
        grid=(N // tn,),
        in_specs=[pl.BlockSpec((B, D), lambda j: (0, 0)),
                  pl.BlockSpec((D, tn), lambda j: (0, j)),
                  pl.BlockSpec((1, tn), lambda j: (0, j))],
        out_specs=pl.BlockSpec((B, tn), lambda j: (0, j)),
        out_shape=jax.ShapeDtypeStruct((B, N), F32),
        compiler_params=_cparams(("arbitrary",)),
        name="adaln",
    )(c, w_ada, b_ada.reshape(1, N))


def _inproj_kernel(x_ref, mod_ref, g_ref, w_ref, ws_ref, o_ref, os_ref, h_scr):
    @pl.when(pl.program_id(2) == 0)
    def _():
        x = x_ref[0]
        y = x * lax.rsqrt(jnp.mean(x * x, axis=-1, keepdims=True) + NORM_EPS) * g_ref[...]
        h = y * (1.0 + mod_ref[0, 1:2, :]) + mod_ref[0, 0:1, :]
        h_scr[...] = h.astype(BF16)
        os_ref[...] = jnp.dot(h_scr[...], ws_ref[...], preferred_element_type=F32)

    o_ref[...] = jnp.dot(h_scr[...], w_ref[...], preferred_element_type=F32)


def _inproj(x, mod3, g_pre, w_main, w_small):
    B, S, D = x.shape
    tm = min(1024, S)
    tn = N_PROJ // 2
    nrow = S // tm
    return pl.pallas_call(
        _inproj_kernel,
        grid=(B, nrow, N_PROJ // tn),
        in_specs=[pl.BlockSpec((1, tm, D), lambda b, i, j: (b, i, 0)),
                  pl.BlockSpec((1, 6, D), lambda b, i, j: (b, 0, 0)),
                  pl.BlockSpec((1, D), lambda b, i, j: (0, 0)),
                  pl.BlockSpec((D, tn), lambda b, i, j: (0, j)),
                  pl.BlockSpec((D, LANES), lambda b, i, j: (0, 0))],
        out_specs=[pl.BlockSpec((tm, tn), lambda b, i, j: (b * nrow + i, j)),
                   pl.BlockSpec((tm, LANES), lambda b, i, j: (b * nrow + i, 0))],
        out_shape=[jax.ShapeDtypeStruct((B * S, N_PROJ), F32),
                   jax.ShapeDtypeStruct((B * S, LANES), F32)],
        scratch_shapes=[pltpu.VMEM((tm, D), BF16)],
        compiler_params=_cparams(("arbitrary", "arbitrary", "arbitrary")),
        name="inproj",
    )(x, mod3, g_pre.reshape(1, D), w_main, w_small)


def _rope(v, cosf, sinf, lane):
    rot = jnp.where(lane < ROPE_DIM // 2, pltpu.roll(v, LANES - ROPE_DIM // 2, 1), pltpu.roll(v, ROPE_DIM // 2, 1))
    return v * cosf + rot * sinf


def _mobaprep_kernel(q_ref, k_ref, v_ref, cos_ref, sin_ref, qo_ref, ko_ref, vo_ref, km_ref):
    cosf = cos_ref[...]
    sinf = sin_ref[...]
    lane = lax.broadcasted_iota(jnp.int32, cosf.shape, 1)
    for h in range(MB_HEADS):
        sl = slice(h * HEAD_DIM, (h + 1) * HEAD_DIM)
        qo_ref[:, sl] = (_rope(q_ref[:, sl], cosf, sinf, lane) * MB_Q_SCALE).astype(BF16)
        kr = _rope(k_ref[:, sl], cosf, sinf, lane)
        ko_ref[:, sl] = kr.astype(BF16)
        for blk in range(km_ref.shape[0]):
            km_ref[blk, :, sl] = jnp.mean(kr[blk * MB_BLOCK:(blk + 1) * MB_BLOCK], axis=0, keepdims=True)
    vo_ref[...] = v_ref[...].astype(BF16)


def _rope_tables(S):
    half = ROPE_DIM // 2
    inv_freq = ROPE_THETA ** (-jnp.arange(half, dtype=F32) / half)
    ang = jnp.arange(S, dtype=F32)[:, None] * inv_freq[None, :]
    cos, sin = jnp.cos(ang), jnp.sin(ang)
    rest = HEAD_DIM - ROPE_DIM
    cosf = jnp.concatenate([cos, cos, jnp.ones((S, rest), F32)], axis=1)
    sinf = jnp.concatenate([-sin, sin, jnp.zeros((S, rest), F32)], axis=1)
    return cosf, sinf


def _mobaprep(proj, B, S):
    T = B * S
    tb = min(4 * MB_BLOCK, S)
    nb = S // tb
    W = MB_HEADS * HEAD_DIM
    cosf, sinf = _rope_tables(S)
    col = lambda c: (lambda i: (i, c))
    return pl.pallas_call(
        _mobaprep_kernel,
        grid=(T // tb,),
        in_specs=[pl.BlockSpec((tb, W), col(COL_MQ // _W)),
                  pl.BlockSpec((tb, W), col(COL_MK // _W)),
                  pl.BlockSpec((tb, W), col(COL_MV // _W)),
                  pl.BlockSpec((tb, LANES), lambda i: (i % nb, 0)),
                  pl.BlockSpec((tb, LANES), lambda i: (i % nb, 0))],
        out_specs=[pl.BlockSpec((tb, W), lambda i: (i, 0)),
                   pl.BlockSpec((tb, W), lambda i: (i, 0)),
                   pl.BlockSpec((tb, W), lambda i: (i, 0)),
                   pl.BlockSpec((tb // MB_BLOCK, 1, W), lambda i: (i, 0, 0))],
        out_shape=[jax.ShapeDtypeStruct((T, W), BF16),
                   jax.ShapeDtypeStruct((T, W), BF16),
                   jax.ShapeDtypeStruct((T, W), BF16),
                   jax.ShapeDtypeStruct((T // MB_BLOCK, 1, W), F32)],
        compiler_params=_cparams(("arbitrary",)),
        name="mobaprep",
    )(proj, proj, proj, cosf, sinf)


MB_HPS = 4


def _moba_kernel(q_ref, k_ref, v_ref, km_ref, o_ref, s_scr, sd_scr, mx_scr, l_scr, acc_scr):
    qi = pl.program_id(1)
    tb = MB_BLOCK
    nb = km_ref.shape[0]
    nt = tb // LANES
    D = HEAD_DIM
    heads = range(MB_HPS)
    hs = [slice(h * D, (h + 1) * D) for h in heads]
    qs = [q_ref[:, hs[h]] for h in heads]

    sts = [lax.dot_general(km_ref[:, hs[h]], qs[h].astype(F32), (((1,), (1,)), ((), ())),
                           preferred_element_type=F32, precision=HIGHEST) for h in heads]
    blk = lax.broadcasted_iota(jnp.int32, (nb, tb), 0)
    rowid = lax.broadcasted_iota(jnp.int32, (LANES, tb), 0)
    sel_ts = []
    for h in heads:
        st = jnp.where(blk < qi, sts[h], -jnp.inf)
        sel_t = jnp.full((LANES, tb), -1.0, F32)
        for r in range(MB_TOPK):
            m = jnp.max(st, axis=0, keepdims=True)
            idx = jnp.min(jnp.where(st == m, blk, nb), axis=0, keepdims=True)
            sel_t = jnp.where(rowid == r, jnp.where(r < qi, idx, -1).astype(F32), sel_t)
            st = jnp.where(blk == idx, -jnp.inf, st)
        sel_ts.append(sel_t)
    sels = [jnp.transpose(t) for t in sel_ts]
    sel_rep = [[jnp.broadcast_to(sels[h][:, r:r + 1], (tb, LANES)) for r in range(MB_TOPK)] for h in heads]

    def logits(h, start, width):
        kslab = k_ref[pl.ds(pl.multiple_of(start, tb), width), hs[h]]
        return lax.dot_general(qs[h], kslab, (((1,), (1,)), ((), ())), preferred_element_type=F32)

    r_i = lax.broadcasted_iota(jnp.int32, (tb, LANES), 0)
    c_i = lax.broadcasted_iota(jnp.int32, (tb, LANES), 1)
    sds = [logits(h, qi * tb, tb) for h in heads]
    for h in heads:
        mx = jnp.full((tb, LANES), NEG_BIG, F32)
        for t in range(nt):
            piece = jnp.where(c_i + t * LANES <= r_i, sds[h][:, t * LANES:(t + 1) * LANES], NEG_BIG)
            sd_scr[h, :, t * LANES:(t + 1) * LANES] = piece
            mx = jnp.maximum(mx, piece)
        mx_scr[h] = mx

    n_pairs = (qi + 1) // 2

    def pass1(pairs):
        s2s = [[logits(h, 2 * i * tb, 2 * tb) for h in heads] for i in pairs]
        for h in heads:
            mx = mx_scr[h]
            for n, i in enumerate(pairs):
                for half in range(2):
                    kbf = (2 * i + half).astype(F32)
                    hit = (sel_rep[h][0] == kbf) | (sel_rep[h][1] == kbf) | (sel_rep[h][2] == kbf)
                    for t in range(nt):
                        c0 = (half * nt + t) * LANES
                        piece = jnp.where(hit, s2s[n][h][:, c0:c0 + LANES], NEG_BIG)
                        s_scr[h, i, :, c0:c0 + LANES] = piece
                        mx = jnp.maximum(mx, piece)
            mx_scr[h] = mx

    def run_pairs(fn):
        def body(j, carry):
            fn([4 * j + u for u in range(4)])
            return carry
        lax.fori_loop(0, n_pairs // 4, body, 0)
        rem = n_pairs % 4
        pl.when(rem >= 2)(lambda: fn([n_pairs - rem, n_pairs - rem + 1]))
        pl.when(rem % 2 == 1)(lambda: fn([n_pairs - 1]))

    run_pairs(pass1)
    m_reps = [jnp.broadcast_to(jnp.max(mx_scr[h], axis=-1, keepdims=True), (tb, LANES)) for h in heads]

    def probs(load, width, m_rep):
        ps, lsum = [], jnp.zeros((tb, LANES), F32)
        for t in range(width // LANES):
            p = jnp.exp2(load(t) - m_rep)
            lsum = lsum + p
            ps.append(p.astype(BF16))
        return jnp.concatenate(ps, axis=1), lsum

    pls = [probs(lambda t, h=h: sd_scr[h, :, t * LANES:(t + 1) * LANES], tb, m_reps[h]) for h in heads]
    for h in heads:
        l_scr[h] = pls[h][1]
        acc_scr[h] = jnp.dot(pls[h][0], v_ref[pl.ds(pl.multiple_of(qi * tb, tb), tb), hs[h]],
                             preferred_element_type=F32)

    def pass2(pairs):
        pls = [[probs(lambda t, h=h, i=i: s_scr[h, i, :, t * LANES:(t + 1) * LANES], 2 * tb, m_reps[h])
                for h in heads] for i in pairs]
        for h in heads:
            lsum, acc = l_scr[h], acc_scr[h]
            for n, i in enumerate(pairs):
                vslab = v_ref[pl.ds(pl.multiple_of(2 * i * tb, tb), 2 * tb), hs[h]]
                lsum = lsum + pls[n][h][1]
                acc = acc + jnp.dot(pls[n][h][0], vslab, preferred_element_type=F32)
            l_scr[h], acc_scr[h] = lsum, acc

    run_pairs(pass2)
    for h in heads:
        o_ref[:, hs[h]] = (acc_scr[h] / jnp.sum(l_scr[h], axis=-1, keepdims=True)).astype(o_ref.dtype)


def _moba(qr, kr, vb, kmean, B, S):
    tb = MB_BLOCK
    nb = S // tb
    T = B * S
    km = kmean.reshape(B, nb, MB_HEADS * HEAD_DIM)
    HW = MB_HPS * HEAD_DIM
    ng = MB_HEADS // MB_HPS
    return pl.pallas_call(
        _moba_kernel,
        grid=(B * ng, nb),
        in_specs=[pl.BlockSpec((tb, HW), lambda g, i: ((g // ng) * nb + i, g % ng)),
                  pl.BlockSpec((S, HW), lambda g, i: (g // ng, g % ng)),
                  pl.BlockSpec((S, HW), lambda g, i: (g // ng, g % ng)),
                  pl.BlockSpec((None, nb, HW), lambda g, i: (g // ng, 0, g % ng))],
        out_specs=pl.BlockSpec((tb, HW), lambda g, i: ((g // ng) * nb + i, g % ng)),
        out_shape=jax.ShapeDtypeStruct((T, MB_HEADS * HEAD_DIM), BF16),
        scratch_shapes=[pltpu.VMEM((MB_HPS, (nb + 1) // 2, tb, 2 * tb), F32),
                        pltpu.VMEM((MB_HPS, tb, tb), F32),
                        pltpu.VMEM((MB_HPS, tb, LANES), F32),
                        pltpu.VMEM((MB_HPS, tb, LANES), F32),
                        pltpu.VMEM((MB_HPS, tb, HEAD_DIM), F32)],
        compiler_params=_cparams(("arbitrary", "arbitrary")),
        name="moba",
    )(qr, kr, vb, km)


DN_TILE_CHUNKS = 8


def _softplus(v):
    return jnp.maximum(v, 0.0) + jnp.log1p(jnp.exp(-jnp.abs(v)))


DN_GROUP = 2
DN_HPS = 4


def _split(a):
    hi = pltpu.bitcast(pltpu.bitcast(a, jnp.uint32) & jnp.uint32(0xFFFF0000), F32)
    return hi.astype(BF16), (a - hi).astype(BF16)


def _dot3(ah, al, bh, bl):
    lhs = jnp.concatenate([ah, ah, al], axis=1)
    rhs = jnp.concatenate([bh, bl, bh], axis=0)
    return jnp.dot(lhs, rhs, preferred_element_type=F32)


def _dot_bf(a, b):
    return jnp.dot(a.astype(BF16), b.astype(BF16), preferred_element_type=F32)


def _deltanet_kernel(q_ref, k_ref, v_ref, z_ref, sm_ref, wq_ref, wk_ref, wv_ref, alog_ref, dtb_ref, gout_ref,
                     o_ref, xp_scr, state_scr):
    i = pl.program_id(2)
    C = DN_CHUNK
    TR = q_ref.shape[0]
    HALO = 8

    @pl.when(i == 0)
    def _():
        xp_scr[...] = jnp.zeros(xp_scr.shape, F32)
        state_scr[...] = jnp.zeros(state_scr.shape, F32)

    def conv_silu(slot, x_ref, w_ref, sl):
        xp_scr[slot, 0:HALO, sl] = xp_scr[slot, TR:TR + HALO, sl]
        xp_scr[slot, HALO:HALO + TR, sl] = x_ref[:, sl]
        acc = w_ref[DN_CONV - 1:DN_CONV, sl] * xp_scr[slot, HALO:HALO + TR, sl]
        for j in range(1, DN_CONV):
            acc = acc + w_ref[DN_CONV - 1 - j:DN_CONV - j, sl] * xp_scr[slot, HALO - j:HALO - j + TR, sl]
        return _silu(acc)

    sm = sm_ref[...]
    lane = lax.broadcasted_iota(jnp.int32, sm.shape, 1)
    row = lax.broadcasted_iota(jnp.int32, sm.shape, 0)
    beta_all = _sigmoid(sm)
    g_all = -jnp.exp(alog_ref[...]) * _softplus(sm + dtb_ref[...])
    pos = row % C
    gc_all = g_all
    shift = 1
    while shift < C:
        gc_all = gc_all + jnp.where(pos >= shift, pltpu.roll(gc_all, shift, 0), 0.0)
        shift *= 2
    gc_t = jnp.transpose(gc_all)
    row_t = lax.broadcasted_iota(jnp.int32, gc_t.shape, 0)

    G = DN_GROUP * C
    ri = lax.broadcasted_iota(jnp.int32, (G, G), 0)
    ci = lax.broadcasted_iota(jnp.int32, (G, G), 1)
    same = (ri // C) == (ci // C)
    tril = same & (ci <= ri)
    strict = same & (ci < ri)
    eye = (ci == ri).astype(F32)
    D = HEAD_DIM
    gout = gout_ref[...]

    def head_pipeline(hh):
        h = pl.program_id(1) * DN_HPS + hh
        sl = slice(hh * D, (hh + 1) * D)
        q = conv_silu(0, q_ref, wq_ref, sl)
        k = conv_silu(1, k_ref, wk_ref, sl)
        yield
        v = conv_silu(2, v_ref, wv_ref, sl)
        q = q * lax.rsqrt(jnp.sum(q * q, axis=-1, keepdims=True) + 1e-6) * (D ** -0.5)
        k = k * lax.rsqrt(jnp.sum(k * k, axis=-1, keepdims=True) + 1e-6)
        yield
        beta = jnp.sum(jnp.where(lane == h, beta_all, 0.0), axis=1, keepdims=True)
        gc = jnp.sum(jnp.where(lane == DN_HEADS + h, gc_all, 0.0), axis=1, keepdims=True)
        gc_row = jnp.sum(jnp.where(row_t == DN_HEADS + h, gc_t, 0.0), axis=0, keepdims=True)
        head = (q, k, v, beta, gc, gc_row, jnp.transpose(k))
        yield
        steps = [None] * (TR // C)
        yield from _round_robin([_deltanet_group(head, g, tril, strict, eye, steps) for g in range(TR // G)])
        outs = []
        yield from _deltanet_chain(hh, steps, state_scr, outs)
        o = jnp.concatenate(outs, axis=0)
        y = o * lax.rsqrt(jnp.mean(o * o, axis=-1, keepdims=True) + NORM_EPS) * gout
        o_ref[:, sl] = (y * _silu(z_ref[:, sl])).astype(o_ref.dtype)

    for _ in _round_robin([head_pipeline(hh) for hh in range(DN_HPS)]):
        pass


def _round_robin(gens):
    active = list(gens)
    while active:
        still = []
        for gen in active:
            try:
                next(gen)
                still.append(gen)
            except StopIteration:
                pass
        active = still
        yield


def _deltanet_group(head, g, tril, strict, eye, steps_out):
    C = DN_CHUNK
    D = HEAD_DIM
    G = DN_GROUP * C
    q, k, v, beta, gc, gc_row, k_t = head
    r0 = g * G
    qg, kg, vg = q[r0:r0 + G], k[r0:r0 + G], v[r0:r0 + G]
    bg = beta[r0:r0 + G]
    gcg = gc[r0:r0 + G]
    gcr = gc_row[:, r0:r0 + G]
    ktg = k_t[:, r0:r0 + G]
    decay = jnp.where(tril, jnp.exp(jnp.where(tril, gcg - gcr, 0.0)), 0.0)
    kb = kg * bg
    aq = _dot_bf(jnp.concatenate([kb, qg], axis=0), ktg)
    yield
    m_neg = jnp.where(strict, -(aq[:G] * decay), 0.0)
    qk = (aq[G:] * decay).astype(BF16)
    t_inv = eye + m_neg
    ph, pl_ = _split(m_neg)
    for _ in range(5):
        th, tl = _split(t_inv)
        ph, pl_ = _split(_dot3(ph, pl_, ph, pl_))
        yield
        t_inv = t_inv + _dot3(th, tl, ph, pl_)
        yield
    egc = jnp.exp(gcg)
    th, tl = _split(t_inv)
    rh, rl = _split(jnp.concatenate([kb * egc, vg * bg], axis=1))
    wu = _dot3(th, tl, rh, rl).astype(BF16)
    yield
    qr = jnp.dot(qk, wu, preferred_element_type=F32)
    qp = qg * egc - qr[:, :D]
    r_all = qr[:, D:]
    yield
    for c in range(DN_GROUP):
        c0 = c * C
        g_last = gcg[c0 + C - 1:c0 + C, :]
        kt_tail = ktg[:, c0:c0 + C] * jnp.exp(g_last - gcr[:, c0:c0 + C])
        gh = jnp.dot(kt_tail.astype(BF16), wu[c0:c0 + C, :], preferred_element_type=F32)
        lhs = jnp.concatenate([gh[:, :D], qp[c0:c0 + C]], axis=0).astype(BF16)
        steps_out[g * DN_GROUP + c] = (lhs, gh[:, D:], r_all[c0:c0 + C], jnp.exp(g_last))
        yield


def _deltanet_chain(hh, steps, state_scr, outs):
    D = HEAD_DIM
    state = state_scr[hh]
    for lhs, h_add, r_add, dec in steps:
        res = jnp.dot(lhs, state.astype(BF16), preferred_element_type=F32)
        outs.append(res[D:] + r_add)
        state = state * dec - res[:D] + h_add
        yield
    state_scr[hh] = state


def _deltanet(proj, small, conv_w, a_log, dt_bias, g_dn_out, B, S):
    T = B * S
    TR = min(DN_TILE_CHUNKS * DN_CHUNK, S)
    nt = S // TR
    pad = jnp.zeros((DN_HEADS,), F32)
    rest = jnp.zeros((LANES - 2 * DN_HEADS,), F32)
    alog_lane = jnp.concatenate([pad, a_log.astype(F32), rest]).reshape(1, LANES)
    dtb_lane = jnp.concatenate([pad, dt_bias.astype(F32), rest]).reshape(1, LANES)
    HW = DN_HPS * HEAD_DIM
    per = HW // LANES
    rows = lambda c0: (lambda b, h, i: (b * nt + i, c0 // per + h))
    wcol = lambda c0: (lambda b, h, i: (0, c0 // per + h))
    const = lambda b, h, i: (0, 0)
    return pl.pallas_call(
        _deltanet_kernel,
        grid=(B, DN_HEADS // DN_HPS, nt),
        in_specs=[pl.BlockSpec((TR, HW), rows(COL_DQ)),
                  pl.BlockSpec((TR, HW), rows(COL_DK)),
                  pl.BlockSpec((TR, HW), rows(COL_DV)),
                  pl.BlockSpec((TR, HW), rows(COL_DZ)),
                  pl.BlockSpec((TR, LANES), lambda b, h, i: (b * nt + i, 0)),
                  pl.BlockSpec((DN_CONV, HW), wcol(0)),
                  pl.BlockSpec((DN_CONV, HW), wcol(DN_HEADS)),
                  pl.BlockSpec((DN_CONV, HW), wcol(2 * DN_HEADS)),
                  pl.BlockSpec((1, LANES), const),
                  pl.BlockSpec((1, LANES), const),
                  pl.BlockSpec((1, HEAD_DIM), const)],
        out_specs=pl.BlockSpec((TR, HW), lambda b, h, i: (b * nt + i, h)),
        out_shape=jax.ShapeDtypeStruct((T, DN_HEADS * HEAD_DIM), BF16),
        scratch_shapes=[pltpu.VMEM((3, TR + 8, HW), F32), pltpu.VMEM((DN_HPS, HEAD_DIM, HEAD_DIM), F32)],
        compiler_params=_cparams(("arbitrary", "arbitrary", "arbitrary")),
        name="deltanet",
    )(proj, proj, proj, proj, small, conv_w, conv_w, conv_w, alog_lane, dtb_lane, g_dn_out.reshape(1, HEAD_DIM))


def _rms(v):
    return v * lax.rsqrt(jnp.mean(v * v, axis=-1, keepdims=True) + NORM_EPS)


def _merge_kernel(oa_ref, ob_ref, ga_ref, gb_ref, x_ref, mod_ref, gpost_ref, gpre_ref, wa_ref, wb_ref, wo_ref,
                  wrt_ref, br_ref, x1_ref, hp_ref, idx_ref, wrow_ref, rank_ref, cnt_ref, carry_scr):
    E = N_EXPERTS
    tm = x_ref.shape[0]
    half = x_ref.shape[1] // 2

    @pl.when(pl.program_id(0) == 0)
    def _():
        carry_scr[...] = jnp.zeros(carry_scr.shape, F32)

    ya = jnp.dot(oa_ref[...], wa_ref[...], preferred_element_type=F32)
    yb = jnp.dot(ob_ref[...], wb_ref[...], preferred_element_type=F32)
    merged = _sigmoid(ga_ref[...]) * ya + _sigmoid(gb_ref[...]) * yb
    mix = jnp.dot(merged.astype(BF16), wo_ref[...], preferred_element_type=F32)
    x1 = x_ref[...] + mod_ref[0, 2:3, :] * (_rms(mix) * gpost_ref[...])
    x1_ref[...] = x1
    h2 = (_rms(x1) * gpre_ref[...]) * (1.0 + mod_ref[0, 4:5, :]) + mod_ref[0, 3:4, :]

    lo_bits = pltpu.bitcast(h2[:, :half].astype(BF16).astype(F32), jnp.uint32) >> 16
    hi_bits = pltpu.bitcast(h2[:, half:].astype(BF16).astype(F32), jnp.uint32) & jnp.uint32(0xFFFF0000)
    hp_ref[...] = hi_bits | lo_bits

    lt = lax.dot_general(wrt_ref[...], h2, (((1,), (1,)), ((), ())), preferred_element_type=F32,
                         precision=HIGHEST) + br_ref[...]
    eid = lax.broadcasted_iota(jnp.int32, (E, tm), 0)
    vals, idxs = [], []
    for _ in range(TOP_K):
        m = jnp.max(lt, axis=0, keepdims=True)
        idx = jnp.min(jnp.where(lt == m, eid, E), axis=0, keepdims=True)
        vals.append(m)
        idxs.append(idx)
        lt = jnp.where(eid == idx, -jnp.inf, lt)
    exps = [jnp.exp(v - vals[0]) for v in vals]
    den = exps[0] + exps[1] + exps[2] + exps[3]
    wts = [e / den for e in exps]

    hot = jnp.zeros((E, tm), F32)
    for idx in idxs:
        hot = hot + (eid == idx).astype(F32)
    ti = lax.broadcasted_iota(jnp.int32, (tm, tm), 0)
    tj = lax.broadcasted_iota(jnp.int32, (tm, tm), 1)
    before = (ti < tj).astype(BF16)
    prior = carry_scr[...][:, 0:1] + jnp.dot(hot.astype(BF16), before, preferred_element_type=F32)
    row8 = lax.broadcasted_iota(jnp.int32, (8, tm), 0)
    row128 = lax.broadcasted_iota(jnp.int32, (LANES, tm), 0)
    idx8 = jnp.zeros((8, tm), jnp.int32)
    rank8 = jnp.zeros((8, tm), jnp.int32)
    w128 = jnp.zeros((LANES, tm), F32)
    for r in range(TOP_K):
        rank_r = jnp.sum(jnp.where(eid == idxs[r], prior, 0.0), axis=0, keepdims=True)
        idx8 = jnp.where(row8 == r, idxs[r], idx8)
        rank8 = jnp.where(row8 == r, rank_r.astype(jnp.int32), rank8)
        w128 = jnp.where(row128 == r, wts[r], w128)
    idx_ref[...] = idx8
    rank_ref[...] = rank8
    wrow_ref[...] = jnp.transpose(w128)
    carry = carry_scr[...] + jnp.sum(hot, axis=1, keepdims=True)
    carry_scr[...] = carry
    cnt_ref[...] = carry


def _merge(oa, ob, proj, x2, mod3, g_post_mix, g_pre_ffn, w_br_a, w_br_b, w_o, w_router, b_router, S):
    T, D = x2.shape
    E = N_EXPERTS
    tm = min(512, S)
    per_b = S // tm
    W = DN_HEADS * HEAD_DIM
    row = lambda i: (i, 0)
    const = lambda i: (0, 0)
    lane_t = lambda i: (0, i)
    return pl.pallas_call(
        _merge_kernel,
        grid=(T // tm,),
        in_specs=[pl.BlockSpec((tm, W), row),
                  pl.BlockSpec((tm, W), row),
                  pl.BlockSpec((tm, D), lambda i: (i, COL_GA * LANES // D)),
                  pl.BlockSpec((tm, D), lambda i: (i, COL_GB * LANES // D)),
                  pl.BlockSpec((tm, D), row),
                  pl.BlockSpec((1, 6, D), lambda i: (i // per_b, 0, 0)),
                  pl.BlockSpec((1, D), const),
                  pl.BlockSpec((1, D), const),
                  pl.BlockSpec((W, D), const),
                  pl.BlockSpec((W, D), const),
                  pl.BlockSpec((D, D), const),
                  pl.BlockSpec((E, D), const),
                  pl.BlockSpec((E, 1), const)],
        out_specs=[pl.BlockSpec((tm, D), row),
                   pl.BlockSpec((tm, D // 2), row),
                   pl.BlockSpec((8, tm), lane_t),
                   pl.BlockSpec((tm, LANES), row),
                   pl.BlockSpec((8, tm), lane_t),
                   pl.BlockSpec((E, LANES), const)],
        out_shape=[jax.ShapeDtypeStruct((T, D), F32),
                   jax.ShapeDtypeStruct((T, D // 2), jnp.uint32),
                   jax.ShapeDtypeStruct((8, T), jnp.int32),
                   jax.ShapeDtypeStruct((T, LANES), F32),
                   jax.ShapeDtypeStruct((8, T), jnp.int32),
                   jax.ShapeDtypeStruct((E, LANES), F32)],
        scratch_shapes=[pltpu.VMEM((E, LANES), F32)],
        compiler_params=_cparams(("arbitrary",)),
        name="merge",
    )(oa, ob, proj, proj, x2, mod3, g_post_mix.reshape(1, D), g_pre_ffn.reshape(1, D),
      w_br_a.astype(BF16), w_br_b.astype(BF16), w_o.astype(BF16),
      jnp.transpose(w_router).astype(F32), b_router.reshape(E, 1).astype(F32))


DISPATCH_TOKENS = 512


def _dispatch_kernel(fill_ref, dest_ref, src_ref, dst_ref, zero_scr, sem, zsem):
    n = dest_ref.shape[1]
    tm = zero_scr.shape[0]

    @pl.when(pl.program_id(0) == 0)
    def _():
        zero_scr[...] = jnp.zeros(zero_scr.shape, zero_scr.dtype)

        def fill_copy(k):
            return pltpu.make_async_copy(zero_scr, dst_ref.at[pl.ds(pl.multiple_of(k * tm, tm), tm)], zsem)

        def start(k, carry):
            pl.when(fill_ref[k] == 1)(lambda: fill_copy(k).start())
            return carry

        def wait(k, carry):
            pl.when(fill_ref[k] == 1)(lambda: fill_copy(k).wait())
            return carry

        lax.fori_loop(0, fill_ref.shape[0], start, 0)
        lax.fori_loop(0, fill_ref.shape[0], wait, 0)

    def issue(j, carry):
        for r in range(TOP_K):
            pltpu.make_async_copy(src_ref.at[pl.ds(j, 1)], dst_ref.at[pl.ds(dest_ref[r, j], 1)], sem).start()
        return carry

    lax.fori_loop(0, n, issue, 0, unroll=8)
    for r in range(TOP_K):
        pltpu.make_async_copy(src_ref, dst_ref.at[pl.ds(0, n)], sem).wait()


def _dispatch(dest, blk_fill, hp):
    T, Wd = hp.shape
    n = min(DISPATCH_TOKENS, T)
    tm = EXPERT_ROWS
    grid_spec = pltpu.PrefetchScalarGridSpec(
        num_scalar_prefetch=1,
        grid=(T // n,),
        in_specs=[pl.BlockSpec((TOP_K, n), lambda i, fill: (0, i), memory_space=pltpu.SMEM),
                  pl.BlockSpec((n, Wd), lambda i, fill: (i, 0))],
        out_specs=pl.BlockSpec(memory_space=pl.ANY),
        scratch_shapes=[pltpu.VMEM((tm, Wd), hp.dtype), pltpu.SemaphoreType.DMA(()), pltpu.SemaphoreType.DMA(())],
    )
    return pl.pallas_call(
        _dispatch_kernel,
        grid_spec=grid_spec,
        out_shape=jax.ShapeDtypeStruct((_expert_blocks(T * TOP_K) * tm, Wd), hp.dtype),
        compiler_params=_cparams(("arbitrary",)),
        name="dispatch",
    )(blk_fill, dest, hp)


EXPERT_ROWS = 512
EXPERT_FEATURE_TILE = 512


def _experts_kernel(exp_ref, valid_ref, new_ref, x_ref, wgu_ref, bgu_ref, wdn_ref, bdn_ref, o_ref, wgu_bf, wdn_bf):
    k = pl.program_id(0)
    F = wdn_ref.shape[0]

    @pl.when(new_ref[k] == 1)
    def _():
        wgu_bf[...] = wgu_ref[...].astype(BF16)
        wdn_bf[...] = wdn_ref[...].astype(BF16)

    def ffn():
        word = x_ref[...]
        x = jnp.concatenate([pltpu.bitcast(word << 16, F32).astype(BF16),
                             pltpu.bitcast(word & jnp.uint32(0xFFFF0000), F32).astype(BF16)], axis=1)

        def gate_up(c):
            ft = slice(c, c + EXPERT_FEATURE_TILE)
            ut = slice(F + c, F + c + EXPERT_FEATURE_TILE)
            return (jnp.dot(x, wgu_bf[:, ft], preferred_element_type=F32) + bgu_ref[:, ft],
                    jnp.dot(x, wgu_bf[:, ut], preferred_element_type=F32) + bgu_ref[:, ut])

        tiles = list(range(0, F, EXPERT_FEATURE_TILE))
        y = None
        nxt = gate_up(tiles[0])
        for t, c in enumerate(tiles):
            gate, up = nxt
            if t + 1 < len(tiles):
                nxt = gate_up(tiles[t + 1])
            gate = jnp.minimum(gate, SWIGLU_LIMIT)
            up = jnp.clip(up, -SWIGLU_LIMIT, SWIGLU_LIMIT)
            act = (up + 1.0) * gate * _sigmoid(SWIGLU_ALPHA * gate)
            part = jnp.dot(act.astype(BF16), wdn_bf[c:c + EXPERT_FEATURE_TILE, :], preferred_element_type=F32)
            y = part + bdn_ref[...] if y is None else y + part
        return y

    @pl.when(valid_ref[k] == 1)
    def _():
        o_ref[:, 0, :] = ffn()

    @pl.when(valid_ref[k] == 0)
    def _():
        o_ref[...] = jnp.zeros(o_ref.shape, o_ref.dtype)


def _experts(xs, blk_exp, blk_valid, blk_new, w_gu, b_gu, w_down, b_down):
    R, half = xs.shape
    E, D, F2 = w_gu.shape
    F = F2 // 2
    tm = EXPERT_ROWS
    grid_spec = pltpu.PrefetchScalarGridSpec(
        num_scalar_prefetch=3,
        grid=(R // tm,),
        in_specs=[pl.BlockSpec((tm, half), lambda k, ex, va, nw: (k, 0)),
                  pl.BlockSpec((None, D, F2), lambda k, ex, va, nw: (ex[k], 0, 0)),
                  pl.BlockSpec((None, 1, F2), lambda k, ex, va, nw: (ex[k], 0, 0)),
                  pl.BlockSpec((None, F, D), lambda k, ex, va, nw: (ex[k], 0, 0)),
                  pl.BlockSpec((None, 1, D), lambda k, ex, va, nw: (ex[k], 0, 0))],
        out_specs=pl.BlockSpec((tm, 1, D), lambda k, ex, va, nw: (k, 0, 0)),
        scratch_shapes=[pltpu.VMEM((D, F2), BF16), pltpu.VMEM((F, D), BF16)],
    )
    return pl.pallas_call(
        _experts_kernel,
        grid_spec=grid_spec,
        out_shape=jax.ShapeDtypeStruct((R, 1, D), F32),
        compiler_params=_cparams(("arbitrary",)),
        name="experts",
    )(blk_exp, blk_valid, blk_new, xs, w_gu, b_gu.reshape(E, 1, F2).astype(F32),
      w_down, b_down.reshape(E, 1, D).astype(F32))


def _expert_blocks(A):
    return A // EXPERT_ROWS + N_EXPERTS


def _block_tables(counts, A):
    E = N_EXPERTS
    tm = EXPERT_ROWS
    n_blk = _expert_blocks(A)
    nblk_e = (counts + tm - 1) // tm
    blk_end = jnp.cumsum(nblk_e)
    blk_start = blk_end - nblk_e
    used = blk_end[-1]
    k = jnp.arange(n_blk, dtype=jnp.int32)
    kk = jnp.minimum(k, used - 1)
    e = jnp.sum((blk_end[None, :] <= kk[:, None]).astype(jnp.int32), axis=1)
    hot = e[:, None] == jnp.arange(E, dtype=jnp.int32)[None, :]
    pick = lambda table: jnp.sum(jnp.where(hot, table[None, :], 0), axis=1)
    valid = k < used
    new = valid & (k == pick(blk_start))
    fill = jnp.logical_not(valid) | (k == pick(blk_end) - 1)
    i32 = lambda v: v.astype(jnp.int32)
    return blk_start * tm, i32(e), i32(valid), i32(new), i32(fill)


COMBINE_TOKENS = 512


def _combine_kernel(dest_ref, nxt_ref, y_ref, wrow_ref, x1_ref, mod_ref, gpost_ref, o_ref, stage, moe_scr, sem):
    i = pl.program_id(0)
    last = pl.num_programs(0) - 1
    n = x1_ref.shape[0]
    SUB = 8

    def gather(idx_ref, j, s):
        for r in range(TOP_K):
            pltpu.make_async_copy(y_ref.at[idx_ref[r, j]], stage.at[s, r, j], sem.at[s]).start(priority=r % 2)

    @pl.when(i == 0)
    def _():
        def prime(j, carry):
            gather(dest_ref, j, 0)
            return carry
        lax.fori_loop(0, n, prime, 0, unroll=SUB)

    def step(slot):
        for r in range(TOP_K):
            pltpu.make_async_copy(y_ref.at[pl.ds(0, n)], stage.at[slot, r], sem.at[slot]).wait()

        def reduce_group(g):
            rows = pl.ds(pl.multiple_of(g * SUB, SUB), SUB)
            w = wrow_ref[rows, :]
            acc = w[:, 0:1] * stage[slot, 0, rows, 0, :]
            for r in range(1, TOP_K):
                acc = acc + w[:, r:r + 1] * stage[slot, r, rows, 0, :]
            moe_scr[rows, :] = acc

        @pl.when(i < last)
        def _():
            def body(g, carry):
                for t in range(SUB):
                    gather(nxt_ref, g * SUB + t, 1 - slot)
                reduce_group(g)
                return carry
            lax.fori_loop(0, n // SUB, body, 0)

        @pl.when(i == last)
        def _():
            def body(g, carry):
                reduce_group(g)
                return carry
            lax.fori_loop(0, n // SUB, body, 0)

    for parity in range(2):
        pl.when(i % 2 == parity)(functools.partial(step, parity))

    o_ref[...] = x1_ref[...] + mod_ref[0, 5:6, :] * (_rms(moe_scr[...]) * gpost_ref[...])


def _combine(dest, y, wrow, x1, mod3, g_post_ffn, S):
    T, D = x1.shape
    n = min(COMBINE_TOKENS, S)
    per_b = S // n
    steps = T // n
    return pl.pallas_call(
        _combine_kernel,
        grid=(steps,),
        in_specs=[pl.BlockSpec((TOP_K, n), lambda i: (0, i), memory_space=pltpu.SMEM),
                  pl.BlockSpec((TOP_K, n), lambda i: (0, jnp.minimum(i + 1, steps - 1)), memory_space=pltpu.SMEM),
                  pl.BlockSpec(memory_space=pl.ANY),
                  pl.BlockSpec((n, LANES), lambda i: (i, 0)),
                  pl.BlockSpec((n, D), lambda i: (i, 0)),
                  pl.BlockSpec((1, 6, D), lambda i: (i // per_b, 0, 0)),
                  pl.BlockSpec((1, D), lambda i: (0, 0))],
        out_specs=pl.BlockSpec((n, D), lambda i: (i, 0)),
        out_shape=jax.ShapeDtypeStruct((T, D), F32),
        scratch_shapes=[pltpu.VMEM((2, TOP_K, n, 1, D), F32), pltpu.VMEM((n, D), F32),
                        pltpu.SemaphoreType.DMA((2,))],
        compiler_params=_cparams(("arbitrary",)),
        name="combine",
    )(dest, dest, y, wrow, x1, mod3, g_post_ffn.reshape(1, D))


def _regroup_w_in(w_in):
    D = w_in.shape[0]
    dw = DN_HEADS * HEAD_DIM
    mw = MB_HEADS * HEAD_DIM
    cuts = np.cumsum([dw, dw, dw, dw, DN_HEADS, DN_HEADS, mw, mw, mw, D, D])[:-1]
    dq, dk, dv, dz, db, da, mq, mk, mv, ga, gb = jnp.split(w_in, [int(c) for c in cuts], axis=1)
    small = jnp.concatenate([db, da, jnp.zeros((D, LANES - 2 * DN_HEADS), w_in.dtype)], axis=1).astype(BF16)
    return jnp.concatenate([p.astype(BF16) for p in (ga, gb, dq, dk, dv, dz, mq, mk, mv)], axis=1), small


def kernel(x, c, w_ada, b_ada, g_pre_mix, g_post_mix, g_pre_ffn, g_post_ffn, w_in, conv_w, a_log, dt_bias,
           g_dn_out, w_br_a, w_br_b, w_o, w_router, b_router, w_gu, b_gu, w_down, b_down):
    B, S, D = x.shape
    l = 0
    mod = _adaln(c, w_ada[l], b_ada[l]).reshape(B, 6, D)
    proj, small = _inproj(x, mod, g_pre_mix[l], *_regroup_w_in(w_in[l]))
    qr, kr, vb, kmean = _mobaprep(proj, B, S)
    ob = _moba(qr, kr, vb, kmean, B, S)
    oa = _deltanet(proj, small, conv_w[l], a_log[l], dt_bias[l], g_dn_out[l], B, S)
    x1, hp, idx8, wrow, rank8, cnt = _merge(oa, ob, proj, x.reshape(B * S, D), mod, g_post_mix[l], g_pre_ffn[l],
                                            w_br_a[l], w_br_b[l], w_o[l], w_router[l], b_router[l], S)
    out = _moe(x1, hp, idx8, wrow, rank8, cnt, mod, g_post_ffn[l], w_gu[l], b_gu[l], w_down[l], b_down[l], S)
    return out.reshape(B, S, D)


def _moe(x1, hp, idx8, wrow, rank8, cnt, mod, g_post_ffn, w_gu, b_gu, w_down, b_down, S):
    T = x1.shape[0]
    counts = cnt[:, 0].astype(jnp.int32)
    start, blk_exp, blk_valid, blk_new, blk_fill = _block_tables(counts, T * TOP_K)
    hot = idx8[:TOP_K, :, None] == jnp.arange(N_EXPERTS, dtype=jnp.int32)
    dest = rank8[:TOP_K] + jnp.sum(jnp.where(hot, start, 0), axis=-1)
    xs = _dispatch(dest, blk_fill, hp)
    y = _experts(xs, blk_exp, blk_valid, blk_new, w_gu, b_gu, w_down, b_down)
    return _combine(dest, y, wrow, x1, mod, g_post_ffn, S)
```

```python
import functools

import jax
import jax.numpy as jnp
import numpy as np
from jax import lax
from jax.experimental import pallas as pl
from jax.experimental.pallas import tpu as pltpu

F32 = jnp.float32
BF16 = jnp.bfloat16
HIGHEST = lax.Precision.HIGHEST

HEAD_DIM = 128
DN_HEADS = 4
DN_CONV = 4
DN_CHUNK = 64
MB_HEADS = 4
MB_BLOCK = 256
MB_TOPK = 3
ROPE_THETA = 500000.0
ROPE_DIM = HEAD_DIM // 4
N_EXPERTS = 32
TOP_K = 4
SWIGLU_LIMIT = 7.0
SWIGLU_ALPHA = 1.702
NORM_EPS = 1e-6
LANES = 128
NEG_BIG = -1e30
MB_Q_SCALE = HEAD_DIM ** -0.5 * float(np.log2(np.e))

_W = DN_HEADS * HEAD_DIM // LANES
COL_GA = 0
COL_GB = 8
COL_DQ = 16
COL_DK = 20
COL_DV = 24
COL_DZ = 28
COL_MQ = 32
COL_MK = 36
COL_MV = 40
N_PROJ = 44 * LANES

VMEM_LIMIT = 56 * 1024 * 1024


def _cparams(sem):
    return pltpu.CompilerParams(dimension_semantics=sem, vmem_limit_bytes=VMEM_LIMIT)


def _sigmoid(v):
    return 0.5 * jnp.tanh(0.5 * v) + 0.5


def _silu(v):
    return v * _sigmoid(v)


def _adaln_kernel(c_ref, w_ref, b_ref, o_ref):
    a = _silu(c_ref[...])
    o_ref[...] = jnp.dot(a, w_ref[...], preferred_element_type=F32, precision=HIGHEST) + b_ref[...]


def _adaln(c, w_ada, b_ada):
    B, D = c.shape
    N = w_ada.shape[1]
    tn = D
    return pl.pallas_call(
        _adaln_kernel,
        grid=(N // tn,),
        in_specs=[pl.BlockSpec((B, D), lambda j: (0, 0)),
                  pl.BlockSpec((D, tn), lambda j: (0, j)),
                  pl.BlockSpec((1, tn), lambda j: (0, j))],
        out_specs=pl.BlockSpec((B, tn), lambda j: (0, j)),
        out_shape=jax.ShapeDtypeStruct((B, N), F32),
        compiler_params=_cparams(("arbitrary",)),
        name="adaln",
    )(c, w_ada, b_ada.reshape(1, N))


def _inproj_kernel(x_ref, mod_ref, g_ref, w_ref, ws_ref, o_ref, os_ref, h_scr):
    @pl.when(pl.program_id(2) == 0)
    def _():
        x = x_ref[0]
        y = x * lax.rsqrt(jnp.mean(x * x, axis=-1, keepdims=True) + NORM_EPS) * g_ref[...]
        h = y * (1.0 + mod_ref[0, 1:2, :]) + mod_ref[0, 0:1, :]
        h_scr[...] = h.astype(BF16)
        os_ref[...] = jnp.dot(h_scr[...], ws_ref[...], preferred_element_type=F32)

    o_ref[...] = jnp.dot(h_scr[...], w_ref[...], preferred_element_type=F32)


def _inproj(x, mod3, g_pre, w_main, w_small):
    B, S, D = x.shape
    tm = min(1024, S)
    tn = N_PROJ // 2
    nrow = S // tm
    return pl.pallas_call(
        _inproj_kernel,
        grid=(B, nrow, N_PROJ // tn),
        in_specs=[pl.BlockSpec((1, tm, D), lambda b, i, j: (b, i, 0)),
                  pl.BlockSpec((1, 6, D), lambda b, i, j: (b, 0, 0)),
                  pl.BlockSpec((1, D), lambda b, i, j: (0, 0)),
                  pl.BlockSpec((D, tn), lambda b, i, j: (0, j)),
                  pl.BlockSpec((D, LANES), lambda b, i, j: (0, 0))],
        out_specs=[pl.BlockSpec((tm, tn), lambda b, i, j: (b * nrow + i, j)),
                   pl.BlockSpec((tm, LANES), lambda b, i, j: (b * nrow + i, 0))],
        out_shape=[jax.ShapeDtypeStruct((B * S, N_PROJ), F32),
                   jax.ShapeDtypeStruct((B * S, LANES), F32)],
        scratch_shapes=[pltpu.VMEM((tm, D), BF16)],
        compiler_params=_cparams(("arbitrary", "arbitrary", "arbitrary")),
        name="inproj",
    )(x, mod3, g_pre.reshape(1, D), w_main, w_small)


def _rope(v, cosf, sinf, lane):
    rot = jnp.where(lane < ROPE_DIM // 2, pltpu.roll(v, LANES - ROPE_DIM // 2, 1), pltpu.roll(v, ROPE_DIM // 2, 1))
    return v * cosf + rot * sinf


def _mobaprep_kernel(q_ref, k_ref, v_ref, cos_ref, sin_ref, qo_ref, ko_ref, vo_ref, km_ref):
    cosf = cos_ref[...]
    sinf = sin_ref[...]
    lane = lax.broadcasted_iota(jnp.int32, cosf.shape, 1)
    for h in range(MB_HEADS):
        sl = slice(h * HEAD_DIM, (h + 1) * HEAD_DIM)
        qo_ref[:, sl] = (_rope(q_ref[:, sl], cosf, sinf, lane) * MB_Q_SCALE).astype(BF16)
        kr = _rope(k_ref[:, sl], cosf, sinf, lane)
        ko_ref[:, sl] = kr.astype(BF16)
        for blk in range(km_ref.shape[0]):
            km_ref[blk, :, sl] = jnp.mean(kr[blk * MB_BLOCK:(blk + 1) * MB_BLOCK], axis=0, keepdims=True)
    vo_ref[...] = v_ref[...].astype(BF16)


def _rope_tables(S):
    half = ROPE_DIM // 2
    inv_freq = ROPE_THETA ** (-jnp.arange(half, dtype=F32) / half)
    ang = jnp.arange(S, dtype=F32)[:, None] * inv_freq[None, :]
    cos, sin = jnp.cos(ang), jnp.sin(ang)
    rest = HEAD_DIM - ROPE_DIM
    cosf = jnp.concatenate([cos, cos, jnp.ones((S, rest), F32)], axis=1)
    sinf = jnp.concatenate([-sin, sin, jnp.zeros((S, rest), F32)], axis=1)
    return cosf, sinf


def _mobaprep(proj, B, S):
    T = B * S
    tb = min(4 * MB_BLOCK, S)
    nb = S // tb
    W = MB_HEADS * HEAD_DIM
    cosf, sinf = _rope_tables(S)
    col = lambda c: (lambda i: (i, c))
    return pl.pallas_call(
        _mobaprep_kernel,
        grid=(T // tb,),
        in_specs=[pl.BlockSpec((tb, W), col(COL_MQ // _W)),
                  pl.BlockSpec((tb, W), col(COL_MK // _W)),
                  pl.BlockSpec((tb, W), col(COL_MV // _W)),
                  pl.BlockSpec((tb, LANES), lambda i: (i % nb, 0)),
                  pl.BlockSpec((tb, LANES), lambda i: (i % nb, 0))],
        out_specs=[pl.BlockSpec((tb, W), lambda i: (i, 0)),
                   pl.BlockSpec((tb, W), lambda i: (i, 0)),
                   pl.BlockSpec((tb, W), lambda i: (i, 0)),
                   pl.BlockSpec((tb // MB_BLOCK, 1, W), lambda i: (i, 0, 0))],
        out_shape=[jax.ShapeDtypeStruct((T, W), BF16),
                   jax.ShapeDtypeStruct((T, W), BF16),
                   jax.ShapeDtypeStruct((T, W), BF16),
                   jax.ShapeDtypeStruct((T // MB_BLOCK, 1, W), F32)],
        compiler_params=_cparams(("arbitrary",)),
        name="mobaprep",
    )(proj, proj, proj, cosf, sinf)


MB_HPS = 4


def _moba_kernel(q_ref, k_ref, v_ref, km_ref, o_ref, s_scr, sd_scr, mx_scr, l_scr, acc_scr):
    qi = pl.program_id(1)
    tb = MB_BLOCK
    nb = km_ref.shape[0]
    nt = tb // LANES
    D = HEAD_DIM
    heads = range(MB_HPS)
    hs = [slice(h * D, (h + 1) * D) for h in heads]
    qs = [q_ref[:, hs[h]] for h in heads]

    kms = [jnp.concatenate(_split(km_ref[:, hs[h]]), axis=1) for h in heads]
    sts = [lax.dot_general(kms[h], jnp.concatenate([qs[h], qs[h]], axis=1), (((1,), (1,)), ((), ())),
                           preferred_element_type=F32) for h in heads]
    blk = lax.broadcasted_iota(jnp.int32, (nb, tb), 0)
    rowid = lax.broadcasted_iota(jnp.int32, (LANES, tb), 0)
    sel_ts = []
    for h in heads:
        st = jnp.where(blk < qi, sts[h], -jnp.inf)
        sel_t = jnp.full((LANES, tb), -1.0, F32)
        for r in range(MB_TOPK):
            m = jnp.max(st, axis=0, keepdims=True)
            idx = jnp.min(jnp.where(st == m, blk, nb), axis=0, keepdims=True)
            sel_t = jnp.where(rowid == r, jnp.where(r < qi, idx, -1).astype(F32), sel_t)
            st = jnp.where(blk == idx, -jnp.inf, st)
        sel_ts.append(sel_t)
    sels = [jnp.transpose(t) for t in sel_ts]
    sel_rep = [[jnp.broadcast_to(sels[h][:, r:r + 1], (tb, LANES)) for r in range(MB_TOPK)] for h in heads]

    def logits(h, start, width):
        kslab = k_ref[pl.ds(pl.multiple_of(start, tb), width), hs[h]]
        return lax.dot_general(qs[h], kslab, (((1,), (1,)), ((), ())), preferred_element_type=F32)

    r_i = lax.broadcasted_iota(jnp.int32, (tb, LANES), 0)
    c_i = lax.broadcasted_iota(jnp.int32, (tb, LANES), 1)
    sds = [logits(h, qi * tb, tb) for h in heads]
    for h in heads:
        mx = jnp.full((tb, LANES), NEG_BIG, F32)
        for t in range(nt):
            piece = jnp.where(c_i + t * LANES <= r_i, sds[h][:, t * LANES:(t + 1) * LANES], NEG_BIG)
            sd_scr[h, :, t * LANES:(t + 1) * LANES] = piece
            mx = jnp.maximum(mx, piece)
        mx_scr[h] = mx

    n_pairs = (qi + 1) // 2

    def pass1(pairs):
        s2s = [[logits(h, 2 * i * tb, 2 * tb) for h in heads] for i in pairs]
        for h in heads:
            mx = mx_scr[h]
            for n, i in enumerate(pairs):
                for half in range(2):
                    kbf = (2 * i + half).astype(F32)
                    hit = (sel_rep[h][0] == kbf) | (sel_rep[h][1] == kbf) | (sel_rep[h][2] == kbf)
                    for t in range(nt):
                        c0 = (half * nt + t) * LANES
                        piece = jnp.where(hit, s2s[n][h][:, c0:c0 + LANES], NEG_BIG)
                        s_scr[h, i, :, c0:c0 + LANES] = piece
                        mx = jnp.maximum(mx, piece)
            mx_scr[h] = mx

    def run_pairs(fn):
        def body(j, carry):
            fn([4 * j + u for u in range(4)])
            return carry
        lax.fori_loop(0, n_pairs // 4, body, 0)
        rem = n_pairs % 4
        pl.when(rem >= 2)(lambda: fn([n_pairs - rem, n_pairs - rem + 1]))
        pl.when(rem % 2 == 1)(lambda: fn([n_pairs - 1]))

    run_pairs(pass1)
    m_reps = [jnp.broadcast_to(jnp.max(mx_scr[h], axis=-1, keepdims=True), (tb, LANES)) for h in heads]

    def probs(load, width, m_rep):
        ps, lsum = [], jnp.zeros((tb, LANES), F32)
        for t in range(width // LANES):
            p = jnp.exp2(load(t) - m_rep)
            lsum = lsum + p
            ps.append(p.astype(BF16))
        return jnp.concatenate(ps, axis=1), lsum

    pls = [probs(lambda t, h=h: sd_scr[h, :, t * LANES:(t + 1) * LANES], tb, m_reps[h]) for h in heads]
    for h in heads:
        l_scr[h] = pls[h][1]
        acc_scr[h] = jnp.dot(pls[h][0], v_ref[pl.ds(pl.multiple_of(qi * tb, tb), tb), hs[h]],
                             preferred_element_type=F32)

    def pass2(pairs):
        pls = [[probs(lambda t, h=h, i=i: s_scr[h, i, :, t * LANES:(t + 1) * LANES], 2 * tb, m_reps[h])
                for h in heads] for i in pairs]
        for h in heads:
            lsum, acc = l_scr[h], acc_scr[h]
            for n, i in enumerate(pairs):
                vslab = v_ref[pl.ds(pl.multiple_of(2 * i * tb, tb), 2 * tb), hs[h]]
                lsum = lsum + pls[n][h][1]
                acc = acc + jnp.dot(pls[n][h][0], vslab, preferred_element_type=F32)
            l_scr[h], acc_scr[h] = lsum, acc

    run_pairs(pass2)
    for h in heads:
        o_ref[:, hs[h]] = (acc_scr[h] / jnp.sum(l_scr[h], axis=-1, keepdims=True)).astype(o_ref.dtype)


def _moba(qr, kr, vb, kmean, B, S):
    tb = MB_BLOCK
    nb = S // tb
    T = B * S
    km = kmean.reshape(B, nb, MB_HEADS * HEAD_DIM)
    HW = MB_HPS * HEAD_DIM
    ng = MB_HEADS // MB_HPS
    return pl.pallas_call(
        _moba_kernel,
        grid=(B * ng, nb),
        in_specs=[pl.BlockSpec((tb, HW), lambda g, i: ((g // ng) * nb + i, g % ng)),
                  pl.BlockSpec((S, HW), lambda g, i: (g // ng, g % ng)),
                  pl.BlockSpec((S, HW), lambda g, i: (g // ng, g % ng)),
                  pl.BlockSpec((None, nb, HW), lambda g, i: (g // ng, 0, g % ng))],
        out_specs=pl.BlockSpec((tb, HW), lambda g, i: ((g // ng) * nb + i, g % ng)),
        out_shape=jax.ShapeDtypeStruct((T, MB_HEADS * HEAD_DIM), BF16),
        scratch_shapes=[pltpu.VMEM((MB_HPS, (nb + 1) // 2, tb, 2 * tb), F32),
                        pltpu.VMEM((MB_HPS, tb, tb), F32),
                        pltpu.VMEM((MB_HPS, tb, LANES), F32),
                        pltpu.VMEM((MB_HPS, tb, LANES), F32),
                        pltpu.VMEM((MB_HPS, tb, HEAD_DIM), F32)],
        compiler_params=_cparams(("arbitrary", "arbitrary")),
        name="moba",
    )(qr, kr, vb, km)


DN_TILE_CHUNKS = 8


def _softplus(v):
    return jnp.maximum(v, 0.0) + jnp.log1p(jnp.exp(-jnp.abs(v)))


DN_GROUP = 2
DN_HPS = 4


def _split(a):
    hi = pltpu.bitcast(pltpu.bitcast(a, jnp.uint32) & jnp.uint32(0xFFFF0000), F32)
    return hi.astype(BF16), (a - hi).astype(BF16)


def _dot3(ah, al, bh, bl):
    lhs = jnp.concatenate([ah, ah, al], axis=1)
    rhs = jnp.concatenate([bh, bl, bh], axis=0)
    return jnp.dot(lhs, rhs, preferred_element_type=F32)


def _dot_bf(a, b):
    return jnp.dot(a.astype(BF16), b.astype(BF16), preferred_element_type=F32)


def _deltanet_kernel(q_ref, k_ref, v_ref, z_ref, sm_ref, wq_ref, wk_ref, wv_ref, alog_ref, dtb_ref, gout_ref,
                     o_ref, xp_scr, state_scr):
    i = pl.program_id(2)
    C = DN_CHUNK
    TR = q_ref.shape[0]
    HALO = 8

    @pl.when(i == 0)
    def _():
        xp_scr[...] = jnp.zeros(xp_scr.shape, F32)
        state_scr[...] = jnp.zeros(state_scr.shape, F32)

    def conv_silu(slot, x_ref, w_ref, sl):
        xp_scr[slot, 0:HALO, sl] = xp_scr[slot, TR:TR + HALO, sl]
        xp_scr[slot, HALO:HALO + TR, sl] = x_ref[:, sl]
        acc = w_ref[DN_CONV - 1:DN_CONV, sl] * xp_scr[slot, HALO:HALO + TR, sl]
        for j in range(1, DN_CONV):
            acc = acc + w_ref[DN_CONV - 1 - j:DN_CONV - j, sl] * xp_scr[slot, HALO - j:HALO - j + TR, sl]
        return _silu(acc)

    sm = sm_ref[...]
    lane = lax.broadcasted_iota(jnp.int32, sm.shape, 1)
    row = lax.broadcasted_iota(jnp.int32, sm.shape, 0)
    beta_all = _sigmoid(sm)
    g_all = -jnp.exp(alog_ref[...]) * _softplus(sm + dtb_ref[...])
    pos = row % C
    gc_all = g_all
    shift = 1
    while shift < C:
        gc_all = gc_all + jnp.where(pos >= shift, pltpu.roll(gc_all, shift, 0), 0.0)
        shift *= 2
    gc_t = jnp.transpose(gc_all)
    row_t = lax.broadcasted_iota(jnp.int32, gc_t.shape, 0)

    G = DN_GROUP * C
    ri = lax.broadcasted_iota(jnp.int32, (G, G), 0)
    ci = lax.broadcasted_iota(jnp.int32, (G, G), 1)
    same = (ri // C) == (ci // C)
    tril = same & (ci <= ri)
    strict = same & (ci < ri)
    eye = (ci == ri).astype(F32)
    D = HEAD_DIM
    gout = gout_ref[...]

    def head_pipeline(hh):
        h = pl.program_id(1) * DN_HPS + hh
        sl = slice(hh * D, (hh + 1) * D)
        q = conv_silu(0, q_ref, wq_ref, sl)
        k = conv_silu(1, k_ref, wk_ref, sl)
        yield
        v = conv_silu(2, v_ref, wv_ref, sl)
        q = q * lax.rsqrt(jnp.sum(q * q, axis=-1, keepdims=True) + 1e-6) * (D ** -0.5)
        k = k * lax.rsqrt(jnp.sum(k * k, axis=-1, keepdims=True) + 1e-6)
        yield
        beta = jnp.sum(jnp.where(lane == h, beta_all, 0.0), axis=1, keepdims=True)
        gc = jnp.sum(jnp.where(lane == DN_HEADS + h, gc_all, 0.0), axis=1, keepdims=True)
        gc_row = jnp.sum(jnp.where(row_t == DN_HEADS + h, gc_t, 0.0), axis=0, keepdims=True)
        head = (q, k, v, beta, gc, gc_row, jnp.transpose(k))
        yield
        steps = [None] * (TR // C)
        yield from _round_robin([_deltanet_group(head, g, tril, strict, eye, steps) for g in range(TR // G)])
        outs = []
        yield from _deltanet_chain(hh, steps, state_scr, outs)
        o = jnp.concatenate(outs, axis=0)
        y = o * lax.rsqrt(jnp.mean(o * o, axis=-1, keepdims=True) + NORM_EPS) * gout
        o_ref[:, sl] = (y * _silu(z_ref[:, sl])).astype(o_ref.dtype)

    for _ in _round_robin([head_pipeline(hh) for hh in range(DN_HPS)]):
        pass


def _round_robin(gens):
    active = list(gens)
    while active:
        still = []
        for gen in active:
            try:
                next(gen)
                still.append(gen)
            except StopIteration:
                pass
        active = still
        yield


def _deltanet_group(head, g, tril, strict, eye, steps_out):
    C = DN_CHUNK
    D = HEAD_DIM
    G = DN_GROUP * C
    q, k, v, beta, gc, gc_row, k_t = head
    r0 = g * G
    qg, kg, vg = q[r0:r0 + G], k[r0:r0 + G], v[r0:r0 + G]
    bg = beta[r0:r0 + G]
    gcg = gc[r0:r0 + G]
    gcr = gc_row[:, r0:r0 + G]
    ktg = k_t[:, r0:r0 + G]
    decay = jnp.where(tril, jnp.exp(jnp.where(tril, gcg - gcr, 0.0)), 0.0)
    kb = kg * bg
    aq = _dot_bf(jnp.concatenate([kb, qg], axis=0), ktg)
    yield
    m_neg = jnp.where(strict, -(aq[:G] * decay), 0.0)
    qk = (aq[G:] * decay).astype(BF16)
    t_inv = eye + m_neg
    ph, pl_ = _split(m_neg)
    for _ in range(5):
        th, tl = _split(t_inv)
        ph, pl_ = _split(_dot3(ph, pl_, ph, pl_))
        yield
        t_inv = t_inv + _dot3(th, tl, ph, pl_)
        yield
    egc = jnp.exp(gcg)
    th, tl = _split(t_inv)
    rh, rl = _split(jnp.concatenate([kb * egc, vg * bg], axis=1))
    wu = _dot3(th, tl, rh, rl).astype(BF16)
    yield
    qr = jnp.dot(qk, wu, preferred_element_type=F32)
    qp = qg * egc - qr[:, :D]
    r_all = qr[:, D:]
    yield
    for c in range(DN_GROUP):
        c0 = c * C
        g_last = gcg[c0 + C - 1:c0 + C, :]
        kt_tail = ktg[:, c0:c0 + C] * jnp.exp(g_last - gcr[:, c0:c0 + C])
        gh = jnp.dot(kt_tail.astype(BF16), wu[c0:c0 + C, :], preferred_element_type=F32)
        lhs = jnp.concatenate([gh[:, :D], qp[c0:c0 + C]], axis=0).astype(BF16)
        steps_out[g * DN_GROUP + c] = (lhs, gh[:, D:], r_all[c0:c0 + C], jnp.exp(g_last))
        yield


def _deltanet_chain(hh, steps, state_scr, outs):
    D = HEAD_DIM
    state = state_scr[hh]
    for lhs, h_add, r_add, dec in steps:
        res = jnp.dot(lhs, state.astype(BF16), preferred_element_type=F32)
        outs.append(res[D:] + r_add)
        state = state * dec - res[:D] + h_add
        yield
    state_scr[hh] = state


def _deltanet(proj, small, conv_w, a_log, dt_bias, g_dn_out, B, S):
    T = B * S
    TR = min(DN_TILE_CHUNKS * DN_CHUNK, S)
    nt = S // TR
    pad = jnp.zeros((DN_HEADS,), F32)
    rest = jnp.zeros((LANES - 2 * DN_HEADS,), F32)
    alog_lane = jnp.concatenate([pad, a_log.astype(F32), rest]).reshape(1, LANES)
    dtb_lane = jnp.concatenate([pad, dt_bias.astype(F32), rest]).reshape(1, LANES)
    HW = DN_HPS * HEAD_DIM
    per = HW // LANES
    rows = lambda c0: (lambda b, h, i: (b * nt + i, c0 // per + h))
    wcol = lambda c0: (lambda b, h, i: (0, c0 // per + h))
    const = lambda b, h, i: (0, 0)
    return pl.pallas_call(
        _deltanet_kernel,
        grid=(B, DN_HEADS // DN_HPS, nt),
        in_specs=[pl.BlockSpec((TR, HW), rows(COL_DQ)),
                  pl.BlockSpec((TR, HW), rows(COL_DK)),
                  pl.BlockSpec((TR, HW), rows(COL_DV)),
                  pl.BlockSpec((TR, HW), rows(COL_DZ)),
                  pl.BlockSpec((TR, LANES), lambda b, h, i: (b * nt + i, 0)),
                  pl.BlockSpec((DN_CONV, HW), wcol(0)),
                  pl.BlockSpec((DN_CONV, HW), wcol(DN_HEADS)),
                  pl.BlockSpec((DN_CONV, HW), wcol(2 * DN_HEADS)),
                  pl.BlockSpec((1, LANES), const),
                  pl.BlockSpec((1, LANES), const),
                  pl.BlockSpec((1, HEAD_DIM), const)],
        out_specs=pl.BlockSpec((TR, HW), lambda b, h, i: (b * nt + i, h)),
        out_shape=jax.ShapeDtypeStruct((T, DN_HEADS * HEAD_DIM), BF16),
        scratch_shapes=[pltpu.VMEM((3, TR + 8, HW), F32), pltpu.VMEM((DN_HPS, HEAD_DIM, HEAD_DIM), F32)],
        compiler_params=_cparams(("arbitrary", "arbitrary", "arbitrary")),
        name="deltanet",
    )(proj, proj, proj, proj, small, conv_w, conv_w, conv_w, alog_lane, dtb_lane, g_dn_out.reshape(1, HEAD_DIM))


MERGE_PARTS = 4

def _rms(v):
    return v * lax.rsqrt(jnp.mean(v * v, axis=-1, keepdims=True) + NORM_EPS)


def _merge_kernel(oa_ref, ob_ref, ga_ref, gb_ref, x_ref, mod_ref, gpost_ref, gpre_ref, wa_ref, wb_ref, wo_ref,
                  wrt_ref, br_ref, x1_ref, hp_ref, idx_ref, wrow_ref, rank_ref, cnt_ref, carry_scr):
    E = N_EXPERTS
    tm = x_ref.shape[0]
    half = x_ref.shape[1] // 2

    @pl.when(pl.program_id(0) == 0)
    def _():
        carry_scr[...] = jnp.zeros(carry_scr.shape, F32)

    logits = [None] * MERGE_PARTS
    wrh, wrl = _split(wrt_ref[...])
    wr3 = jnp.concatenate([wrh, wrh, wrl], axis=1)

    def rows_pipeline(part):
        n = tm // MERGE_PARTS
        rs = slice(part * n, (part + 1) * n)
        ya = jnp.dot(oa_ref[rs, :], wa_ref[...], preferred_element_type=F32)
        yb = jnp.dot(ob_ref[rs, :], wb_ref[...], preferred_element_type=F32)
        yield
        merged = _sigmoid(ga_ref[rs, :]) * ya + _sigmoid(gb_ref[rs, :]) * yb
        yield
        mix = jnp.dot(merged.astype(BF16), wo_ref[...], preferred_element_type=F32)
        yield
        x1 = x_ref[rs, :] + mod_ref[0, 2:3, :] * (_rms(mix) * gpost_ref[...])
        x1_ref[rs, :] = x1
        h2 = (_rms(x1) * gpre_ref[...]) * (1.0 + mod_ref[0, 4:5, :]) + mod_ref[0, 3:4, :]
        lo_bits = pltpu.bitcast(h2[:, :half].astype(BF16).astype(F32), jnp.uint32) >> 16
        hi_bits = pltpu.bitcast(h2[:, half:].astype(BF16).astype(F32), jnp.uint32) & jnp.uint32(0xFFFF0000)
        hp_ref[rs, :] = hi_bits | lo_bits
        yield
        hh, hl = _split(h2)
        logits[part] = lax.dot_general(wr3, jnp.concatenate([hh, hl, hh], axis=1), (((1,), (1,)), ((), ())),
                                       preferred_element_type=F32)

    for _ in _round_robin([rows_pipeline(part) for part in range(MERGE_PARTS)]):
        pass
    lt = jnp.concatenate(logits, axis=1) + br_ref[...]
    eid = lax.broadcasted_iota(jnp.int32, (E, tm), 0)
    vals, idxs = [], []
    for _ in range(TOP_K):
        m = jnp.max(lt, axis=0, keepdims=True)
        idx = jnp.min(jnp.where(lt == m, eid, E), axis=0, keepdims=True)
        vals.append(m)
        idxs.append(idx)
        lt = jnp.where(eid == idx, -jnp.inf, lt)
    exps = [jnp.exp(v - vals[0]) for v in vals]
    den = exps[0] + exps[1] + exps[2] + exps[3]
    wts = [e / den for e in exps]

    hot = jnp.zeros((E, tm), F32)
    for idx in idxs:
        hot = hot + (eid == idx).astype(F32)
    ti = lax.broadcasted_iota(jnp.int32, (tm, tm), 0)
    tj = lax.broadcasted_iota(jnp.int32, (tm, tm), 1)
    before = (ti < tj).astype(BF16)
    prior = carry_scr[...][:, 0:1] + jnp.dot(hot.astype(BF16), before, preferred_element_type=F32)
    row8 = lax.broadcasted_iota(jnp.int32, (8, tm), 0)
    row128 = lax.broadcasted_iota(jnp.int32, (LANES, tm), 0)
    idx8 = jnp.zeros((8, tm), jnp.int32)
    rank8 = jnp.zeros((8, tm), jnp.int32)
    w128 = jnp.zeros((LANES, tm), F32)
    for r in range(TOP_K):
        rank_r = jnp.sum(jnp.where(eid == idxs[r], prior, 0.0), axis=0, keepdims=True)
        idx8 = jnp.where(row8 == r, idxs[r], idx8)
        rank8 = jnp.where(row8 == r, rank_r.astype(jnp.int32), rank8)
        w128 = jnp.where(row128 == r, wts[r], w128)
    idx_ref[...] = idx8
    rank_ref[...] = rank8
    wrow_ref[...] = jnp.transpose(w128)
    carry = carry_scr[...] + jnp.sum(hot, axis=1, keepdims=True)
    carry_scr[...] = carry
    cnt_ref[...] = carry


def _merge(oa, ob, proj, x2, mod3, g_post_mix, g_pre_ffn, w_br_a, w_br_b, w_o, w_router, b_router, S):
    T, D = x2.shape
    E = N_EXPERTS
    tm = min(512, S)
    per_b = S // tm
    W = DN_HEADS * HEAD_DIM
    row = lambda i: (i, 0)
    const = lambda i: (0, 0)
    lane_t = lambda i: (0, i)
    return pl.pallas_call(
        _merge_kernel,
        grid=(T // tm,),
        in_specs=[pl.BlockSpec((tm, W), row),
                  pl.BlockSpec((tm, W), row),
                  pl.BlockSpec((tm, D), lambda i: (i, COL_GA * LANES // D)),
                  pl.BlockSpec((tm, D), lambda i: (i, COL_GB * LANES // D)),
                  pl.BlockSpec((tm, D), row),
                  pl.BlockSpec((1, 6, D), lambda i: (i // per_b, 0, 0)),
                  pl.BlockSpec((1, D), const),
                  pl.BlockSpec((1, D), const),
                  pl.BlockSpec((W, D), const),
                  pl.BlockSpec((W, D), const),
                  pl.BlockSpec((D, D), const),
                  pl.BlockSpec((E, D), const),
                  pl.BlockSpec((E, 1), const)],
        out_specs=[pl.BlockSpec((tm, D), row),
                   pl.BlockSpec((tm, D // 2), row),
                   pl.BlockSpec((8, tm), lane_t),
                   pl.BlockSpec((tm, LANES), row),
                   pl.BlockSpec((8, tm), lane_t),
                   pl.BlockSpec((E, LANES), const)],
        out_shape=[jax.ShapeDtypeStruct((T, D), F32),
                   jax.ShapeDtypeStruct((T, D // 2), jnp.uint32),
                   jax.ShapeDtypeStruct((8, T), jnp.int32),
                   jax.ShapeDtypeStruct((T, LANES), F32),
                   jax.ShapeDtypeStruct((8, T), jnp.int32),
                   jax.ShapeDtypeStruct((E, LANES), F32)],
        scratch_shapes=[pltpu.VMEM((E, LANES), F32)],
        compiler_params=_cparams(("arbitrary",)),
        name="merge",
    )(oa, ob, proj, proj, x2, mod3, g_post_mix.reshape(1, D), g_pre_ffn.reshape(1, D),
      w_br_a.astype(BF16), w_br_b.astype(BF16), w_o.astype(BF16),
      jnp.transpose(w_router).astype(F32), b_router.reshape(E, 1).astype(F32))


DISPATCH_TOKENS = 512


def _dispatch_kernel(fill_ref, dest_ref, src_ref, dst_ref, zero_scr, sem, zsem):
    n = dest_ref.shape[1]
    tm = zero_scr.shape[0]

    @pl.when(pl.program_id(0) == 0)
    def _():
        zero_scr[...] = jnp.zeros(zero_scr.shape, zero_scr.dtype)

        def fill_copy(k):
            return pltpu.make_async_copy(zero_scr, dst_ref.at[pl.ds(pl.multiple_of(k * tm, tm), tm)], zsem)

        def start(k, carry):
            pl.when(fill_ref[k] == 1)(lambda: fill_copy(k).start())
            return carry

        def wait(k, carry):
            pl.when(fill_ref[k] == 1)(lambda: fill_copy(k).wait())
            return carry

        lax.fori_loop(0, fill_ref.shape[0], start, 0)
        lax.fori_loop(0, fill_ref.shape[0], wait, 0)

    def issue(j, carry):
        for r in range(TOP_K):
            pltpu.make_async_copy(src_ref.at[pl.ds(j, 1)], dst_ref.at[pl.ds(dest_ref[r, j], 1)], sem).start()
        return carry

    lax.fori_loop(0, n, issue, 0, unroll=8)
    for r in range(TOP_K):
        pltpu.make_async_copy(src_ref, dst_ref.at[pl.ds(0, n)], sem).wait()


def _dispatch(dest, blk_fill, hp):
    T, Wd = hp.shape
    n = min(DISPATCH_TOKENS, T)
    tm = EXPERT_ROWS
    grid_spec = pltpu.PrefetchScalarGridSpec(
        num_scalar_prefetch=1,
        grid=(T // n,),
        in_specs=[pl.BlockSpec((TOP_K, n), lambda i, fill: (0, i), memory_space=pltpu.SMEM),
                  pl.BlockSpec((n, Wd), lambda i, fill: (i, 0))],
        out_specs=pl.BlockSpec(memory_space=pl.ANY),
        scratch_shapes=[pltpu.VMEM((tm, Wd), hp.dtype), pltpu.SemaphoreType.DMA(()), pltpu.SemaphoreType.DMA(())],
    )
    return pl.pallas_call(
        _dispatch_kernel,
        grid_spec=grid_spec,
        out_shape=jax.ShapeDtypeStruct((_expert_blocks(T * TOP_K) * tm, Wd), hp.dtype),
        compiler_params=_cparams(("arbitrary",)),
        name="dispatch",
    )(blk_fill, dest, hp)


EXPERT_ROWS = 512
EXPERT_FEATURE_TILE = 512


def _experts_kernel(exp_ref, valid_ref, new_ref, x_ref, wgu_ref, bgu_ref, wdn_ref, bdn_ref, o_ref, wgu_bf, wdn_bf):
    k = pl.program_id(0)
    F = wdn_ref.shape[0]

    @pl.when(new_ref[k] == 1)
    def _():
        wgu_bf[...] = wgu_ref[...].astype(BF16)
        wdn_bf[...] = wdn_ref[...].astype(BF16)

    def ffn():
        word = x_ref[...]
        x = jnp.concatenate([pltpu.bitcast(word << 16, F32).astype(BF16),
                             pltpu.bitcast(word & jnp.uint32(0xFFFF0000), F32).astype(BF16)], axis=1)

        def gate_up(c):
            ft = slice(c, c + EXPERT_FEATURE_TILE)
            ut = slice(F + c, F + c + EXPERT_FEATURE_TILE)
            return (jnp.dot(x, wgu_bf[:, ft], preferred_element_type=F32) + bgu_ref[:, ft],
                    jnp.dot(x, wgu_bf[:, ut], preferred_element_type=F32) + bgu_ref[:, ut])

        tiles = list(range(0, F, EXPERT_FEATURE_TILE))
        y = None
        nxt = gate_up(tiles[0])
        for t, c in enumerate(tiles):
            gate, up = nxt
            if t + 1 < len(tiles):
                nxt = gate_up(tiles[t + 1])
            gate = jnp.minimum(gate, SWIGLU_LIMIT)
            up = jnp.clip(up, -SWIGLU_LIMIT, SWIGLU_LIMIT)
            act = (up + 1.0) * gate * _sigmoid(SWIGLU_ALPHA * gate)
            part = jnp.dot(act.astype(BF16), wdn_bf[c:c + EXPERT_FEATURE_TILE, :], preferred_element_type=F32)
            y = part + bdn_ref[...] if y is None else y + part
        return y

    @pl.when(valid_ref[k] == 1)
    def _():
        o_ref[:, 0, :] = ffn()

    @pl.when(valid_ref[k] == 0)
    def _():
        o_ref[...] = jnp.zeros(o_ref.shape, o_ref.dtype)


def _experts(xs, blk_exp, blk_valid, blk_new, w_gu, b_gu, w_down, b_down):
    R, half = xs.shape
    E, D, F2 = w_gu.shape
    F = F2 // 2
    tm = EXPERT_ROWS
    grid_spec = pltpu.PrefetchScalarGridSpec(
        num_scalar_prefetch=3,
        grid=(R // tm,),
        in_specs=[pl.BlockSpec((tm, half), lambda k, ex, va, nw: (k, 0)),
                  pl.BlockSpec((None, D, F2), lambda k, ex, va, nw: (ex[k], 0, 0)),
                  pl.BlockSpec((None, 1, F2), lambda k, ex, va, nw: (ex[k], 0, 0)),
                  pl.BlockSpec((None, F, D), lambda k, ex, va, nw: (ex[k], 0, 0)),
                  pl.BlockSpec((None, 1, D), lambda k, ex, va, nw: (ex[k], 0, 0))],
        out_specs=pl.BlockSpec((tm, 1, D), lambda k, ex, va, nw: (k, 0, 0)),
        scratch_shapes=[pltpu.VMEM((D, F2), BF16), pltpu.VMEM((F, D), BF16)],
    )
    return pl.pallas_call(
        _experts_kernel,
        grid_spec=grid_spec,
        out_shape=jax.ShapeDtypeStruct((R, 1, D), F32),
        compiler_params=_cparams(("arbitrary",)),
        name="experts",
    )(blk_exp, blk_valid, blk_new, xs, w_gu, b_gu.reshape(E, 1, F2).astype(F32),
      w_down, b_down.reshape(E, 1, D).astype(F32))


def _expert_blocks(A):
    return A // EXPERT_ROWS + N_EXPERTS


def _block_tables(counts, A):
    E = N_EXPERTS
    tm = EXPERT_ROWS
    n_blk = _expert_blocks(A)
    nblk_e = (counts + tm - 1) // tm
    blk_end = jnp.cumsum(nblk_e)
    blk_start = blk_end - nblk_e
    used = blk_end[-1]
    k = jnp.arange(n_blk, dtype=jnp.int32)
    kk = jnp.minimum(k, used - 1)
    e = jnp.sum((blk_end[None, :] <= kk[:, None]).astype(jnp.int32), axis=1)
    hot = e[:, None] == jnp.arange(E, dtype=jnp.int32)[None, :]
    pick = lambda table: jnp.sum(jnp.where(hot, table[None, :], 0), axis=1)
    valid = k < used
    new = valid & (k == pick(blk_start))
    fill = jnp.logical_not(valid) | (k == pick(blk_end) - 1)
    i32 = lambda v: v.astype(jnp.int32)
    return blk_start * tm, i32(e), i32(valid), i32(new), i32(fill)


COMBINE_TOKENS = 512


def _combine_kernel(dest_ref, nxt_ref, y_ref, wrow_ref, x1_ref, mod_ref, gpost_ref, o_ref, stage, moe_scr, sem):
    i = pl.program_id(0)
    last = pl.num_programs(0) - 1
    n = x1_ref.shape[0]
    SUB = 8

    def gather(idx_ref, j, s):
        for r in range(TOP_K):
            pltpu.make_async_copy(y_ref.at[idx_ref[r, j]], stage.at[s, r, j], sem.at[s]).start(priority=r % 2)

    @pl.when(i == 0)
    def _():
        def prime(j, carry):
            gather(dest_ref, j, 0)
            return carry
        lax.fori_loop(0, n, prime, 0, unroll=SUB)

    def step(slot):
        for r in range(TOP_K):
            pltpu.make_async_copy(y_ref.at[pl.ds(0, n)], stage.at[slot, r], sem.at[slot]).wait()

        def reduce_group(g):
            rows = pl.ds(pl.multiple_of(g * SUB, SUB), SUB)
            w = wrow_ref[rows, :]
            acc = w[:, 0:1] * stage[slot, 0, rows, 0, :]
            for r in range(1, TOP_K):
                acc = acc + w[:, r:r + 1] * stage[slot, r, rows, 0, :]
            moe_scr[rows, :] = acc

        @pl.when(i < last)
        def _():
            def body(g, carry):
                for t in range(SUB):
                    gather(nxt_ref, g * SUB + t, 1 - slot)
                reduce_group(g)
                return carry
            lax.fori_loop(0, n // SUB, body, 0)

        @pl.when(i == last)
        def _():
            def body(g, carry):
                reduce_group(g)
                return carry
            lax.fori_loop(0, n // SUB, body, 0)

    for parity in range(2):
        pl.when(i % 2 == parity)(functools.partial(step, parity))

    o_ref[...] = x1_ref[...] + mod_ref[0, 5:6, :] * (_rms(moe_scr[...]) * gpost_ref[...])


def _combine(dest, y, wrow, x1, mod3, g_post_ffn, S):
    T, D = x1.shape
    n = min(COMBINE_TOKENS, S)
    per_b = S // n
    steps = T // n
    return pl.pallas_call(
        _combine_kernel,
        grid=(steps,),
        in_specs=[pl.BlockSpec((TOP_K, n), lambda i: (0, i), memory_space=pltpu.SMEM),
                  pl.BlockSpec((TOP_K, n), lambda i: (0, jnp.minimum(i + 1, steps - 1)), memory_space=pltpu.SMEM),
                  pl.BlockSpec(memory_space=pl.ANY),
                  pl.BlockSpec((n, LANES), lambda i: (i, 0)),
                  pl.BlockSpec((n, D), lambda i: (i, 0)),
                  pl.BlockSpec((1, 6, D), lambda i: (i // per_b, 0, 0)),
                  pl.BlockSpec((1, D), lambda i: (0, 0))],
        out_specs=pl.BlockSpec((n, D), lambda i: (i, 0)),
        out_shape=jax.ShapeDtypeStruct((T, D), F32),
        scratch_shapes=[pltpu.VMEM((2, TOP_K, n, 1, D), F32), pltpu.VMEM((n, D), F32),
                        pltpu.SemaphoreType.DMA((2,))],
        compiler_params=_cparams(("arbitrary",)),
        name="combine",
    )(dest, dest, y, wrow, x1, mod3, g_post_ffn.reshape(1, D))


def _regroup_w_in(w_in):
    D = w_in.shape[0]
    dw = DN_HEADS * HEAD_DIM
    mw = MB_HEADS * HEAD_DIM
    cuts = np.cumsum([dw, dw, dw, dw, DN_HEADS, DN_HEADS, mw, mw, mw, D, D])[:-1]
    dq, dk, dv, dz, db, da, mq, mk, mv, ga, gb = jnp.split(w_in, [int(c) for c in cuts], axis=1)
    small = jnp.concatenate([db, da, jnp.zeros((D, LANES - 2 * DN_HEADS), w_in.dtype)], axis=1).astype(BF16)
    return jnp.concatenate([p.astype(BF16) for p in (ga, gb, dq, dk, dv, dz, mq, mk, mv)], axis=1), small


def kernel(x, c, w_ada, b_ada, g_pre_mix, g_post_mix, g_pre_ffn, g_post_ffn, w_in, conv_w, a_log, dt_bias,
           g_dn_out, w_br_a, w_br_b, w_o, w_router, b_router, w_gu, b_gu, w_down, b_down):
    B, S, D = x.shape
    l = 0
    mod = _adaln(c, w_ada[l], b_ada[l]).reshape(B, 6, D)
    proj, small = _inproj(x, mod, g_pre_mix[l], *_regroup_w_in(w_in[l]))
    qr, kr, vb, kmean = _mobaprep(proj, B, S)
    ob = _moba(qr, kr, vb, kmean, B, S)
    oa = _deltanet(proj, small, conv_w[l], a_log[l], dt_bias[l], g_dn_out[l], B, S)
    x1, hp, idx8, wrow, rank8, cnt = _merge(oa, ob, proj, x.reshape(B * S, D), mod, g_post_mix[l], g_pre_ffn[l],
                                            w_br_a[l], w_br_b[l], w_o[l], w_router[l], b_router[l], S)
    out = _moe(x1, hp, idx8, wrow, rank8, cnt, mod, g_post_ffn[l], w_gu[l], b_gu[l], w_down[l], b_down[l], S)
    return out.reshape(B, S, D)


def _moe(x1, hp, idx8, wrow, rank8, cnt, mod, g_post_ffn, w_gu, b_gu, w_down, b_down, S):
    T = x1.shape[0]
    counts = cnt[:, 0].astype(jnp.int32)
    start, blk_exp, blk_valid, blk_new, blk_fill = _block_tables(counts, T * TOP_K)
    hot = idx8[:TOP_K, :, None] == jnp.arange(N_EXPERTS, dtype=jnp.int32)
    dest = rank8[:TOP_K] + jnp.sum(jnp.where(hot, start, 0), axis=-1)
    xs = _dispatch(dest, blk_fill, hp)
    y = _experts(xs, blk_exp, blk_valid, blk_new, w_gu, b_gu, w_down, b_down)
    return _combine(dest, y, wrow, x1, mod, g_post_ffn, S)
```

```python
import functools

import jax
import jax.numpy as jnp
import numpy as np
from jax import lax
from jax.experimental import pallas as pl
from jax.experimental.pallas import tpu as pltpu

F32 = jnp.float32
BF16 = jnp.bfloat16
HIGHEST = lax.Precision.HIGHEST

HEAD_DIM = 128
DN_HEADS = 4
DN_CONV = 4
DN_CHUNK = 64
MB_HEADS = 4
MB_BLOCK = 256
MB_TOPK = 3
ROPE_THETA = 500000.0
ROPE_DIM = HEAD_DIM // 4
N_EXPERTS = 32
TOP_K = 4
SWIGLU_LIMIT = 7.0
SWIGLU_ALPHA = 1.702
NORM_EPS = 1e-6
LANES = 128
NEG_BIG = -1e30
MB_Q_SCALE = HEAD_DIM ** -0.5 * float(np.log2(np.e))

_W = DN_HEADS * HEAD_DIM // LANES
COL_GA = 0
COL_GB = 8
COL_DQ = 16
COL_DK = 20
COL_DV = 24
COL_DZ = 28
COL_MQ = 32
COL_MK = 36
COL_MV = 40
N_PROJ = 44 * LANES

VMEM_LIMIT = 56 * 1024 * 1024


def _cparams(sem):
    return pltpu.CompilerParams(dimension_semantics=sem, vmem_limit_bytes=VMEM_LIMIT)


def _sigmoid(v):
    return 0.5 * jnp.tanh(0.5 * v) + 0.5


def _silu(v):
    return v * _sigmoid(v)


def _adaln_kernel(c_ref, w_ref, b_ref, o_ref):
    a = _silu(c_ref[...])
    o_ref[...] = jnp.dot(a, w_ref[...], preferred_element_type=F32, precision=HIGHEST) + b_ref[...]


def _adaln(c, w_ada, b_ada):
    B, D = c.shape
    N = w_ada.shape[1]
    tn = D
    return pl.pallas_call(
        _adaln_kernel,
        grid=(N // tn,),
        in_specs=[pl.BlockSpec((B, D), lambda j: (0, 0)),
                  pl.BlockSpec((D, tn), lambda j: (0, j)),
                  pl.BlockSpec((1, tn), lambda j: (0, j))],
        out_specs=pl.BlockSpec((B, tn), lambda j: (0, j)),
        out_shape=jax.ShapeDtypeStruct((B, N), F32),
        compiler_params=_cparams(("arbitrary",)),
        name="adaln",
    )(c, w_ada, b_ada.reshape(1, N))


def _inproj_kernel(x_ref, mod_ref, g_ref, w_ref, ws_ref, o_ref, os_ref, h_scr):
    @pl.when(pl.program_id(2) == 0)
    def _():
        x = x_ref[0]
        y = x * lax.rsqrt(jnp.mean(x * x, axis=-1, keepdims=True) + NORM_EPS) * g_ref[...]
        h = y * (1.0 + mod_ref[0, 1:2, :]) + mod_ref[0, 0:1, :]
        h_scr[...] = h.astype(BF16)
        os_ref[...] = jnp.dot(h_scr[...], ws_ref[...], preferred_element_type=F32)

    o_ref[...] = jnp.dot(h_scr[...], w_ref[...], preferred_element_type=F32)


def _inproj(x, mod3, g_pre, w_main, w_small):
    B, S, D = x.shape
    tm = min(1024, S)
    tn = N_PROJ // 2
    nrow = S // tm
    return pl.pallas_call(
        _inproj_kernel,
        grid=(B, nrow, N_PROJ // tn),
        in_specs=[pl.BlockSpec((1, tm, D), lambda b, i, j: (b, i, 0)),
                  pl.BlockSpec((1, 6, D), lambda b, i, j: (b, 0, 0)),
                  pl.BlockSpec((1, D), lambda b, i, j: (0, 0)),
                  pl.BlockSpec((D, tn), lambda b, i, j: (0, j)),
                  pl.BlockSpec((D, LANES), lambda b, i, j: (0, 0))],
        out_specs=[pl.BlockSpec((tm, tn), lambda b, i, j: (b * nrow + i, j)),
                   pl.BlockSpec((tm, LANES), lambda b, i, j: (b * nrow + i, 0))],
        out_shape=[jax.ShapeDtypeStruct((B * S, N_PROJ), F32),
                   jax.ShapeDtypeStruct((B * S, LANES), F32)],
        scratch_shapes=[pltpu.VMEM((tm, D), BF16)],
        compiler_params=_cparams(("arbitrary", "arbitrary", "arbitrary")),
        name="inproj",
    )(x, mod3, g_pre.reshape(1, D), w_main, w_small)


def _rope(v, cosf, sinf, lane):
    rot = jnp.where(lane < ROPE_DIM // 2, pltpu.roll(v, LANES - ROPE_DIM // 2, 1), pltpu.roll(v, ROPE_DIM // 2, 1))
    return v * cosf + rot * sinf


def _mobaprep_kernel(q_ref, k_ref, v_ref, cos_ref, sin_ref, qo_ref, ko_ref, vo_ref, km_ref):
    cosf = cos_ref[...]
    sinf = sin_ref[...]
    lane = lax.broadcasted_iota(jnp.int32, cosf.shape, 1)
    for h in range(MB_HEADS):
        sl = slice(h * HEAD_DIM, (h + 1) * HEAD_DIM)
        qo_ref[:, sl] = (_rope(q_ref[:, sl], cosf, sinf, lane) * MB_Q_SCALE).astype(BF16)
        kr = _rope(k_ref[:, sl], cosf, sinf, lane)
        ko_ref[:, sl] = kr.astype(BF16)
        for blk in range(km_ref.shape[0]):
            km_ref[blk, :, sl] = jnp.mean(kr[blk * MB_BLOCK:(blk + 1) * MB_BLOCK], axis=0, keepdims=True)
    vo_ref[...] = v_ref[...].astype(BF16)


def _rope_tables(S):
    half = ROPE_DIM // 2
    inv_freq = ROPE_THETA ** (-jnp.arange(half, dtype=F32) / half)
    ang = jnp.arange(S, dtype=F32)[:, None] * inv_freq[None, :]
    cos, sin = jnp.cos(ang), jnp.sin(ang)
    rest = HEAD_DIM - ROPE_DIM
    cosf = jnp.concatenate([cos, cos, jnp.ones((S, rest), F32)], axis=1)
    sinf = jnp.concatenate([-sin, sin, jnp.zeros((S, rest), F32)], axis=1)
    return cosf, sinf


def _mobaprep(proj, B, S):
    T = B * S
    tb = min(4 * MB_BLOCK, S)
    nb = S // tb
    W = MB_HEADS * HEAD_DIM
    cosf, sinf = _rope_tables(S)
    col = lambda c: (lambda i: (i, c))
    return pl.pallas_call(
        _mobaprep_kernel,
        grid=(T // tb,),
        in_specs=[pl.BlockSpec((tb, W), col(COL_MQ // _W)),
                  pl.BlockSpec((tb, W), col(COL_MK // _W)),
                  pl.BlockSpec((tb, W), col(COL_MV // _W)),
                  pl.BlockSpec((tb, LANES), lambda i: (i % nb, 0)),
                  pl.BlockSpec((tb, LANES), lambda i: (i % nb, 0))],
        out_specs=[pl.BlockSpec((tb, W), lambda i: (i, 0)),
                   pl.BlockSpec((tb, W), lambda i: (i, 0)),
                   pl.BlockSpec((tb, W), lambda i: (i, 0)),
                   pl.BlockSpec((tb // MB_BLOCK, 1, W), lambda i: (i, 0, 0))],
        out_shape=[jax.ShapeDtypeStruct((T, W), BF16),
                   jax.ShapeDtypeStruct((T, W), BF16),
                   jax.ShapeDtypeStruct((T, W), BF16),
                   jax.ShapeDtypeStruct((T // MB_BLOCK, 1, W), F32)],
        compiler_params=_cparams(("arbitrary",)),
        name="mobaprep",
    )(proj, proj, proj, cosf, sinf)


MB_HPS = 4


def _moba_kernel(q_ref, k_ref, v_ref, km_ref, o_ref, s_scr, sd_scr, mx_scr, l_scr, acc_scr):
    qi = pl.program_id(1)
    tb = MB_BLOCK
    nb = km_ref.shape[0]
    nt = tb // LANES
    D = HEAD_DIM
    heads = range(MB_HPS)
    hs = [slice(h * D, (h + 1) * D) for h in heads]
    qs = [q_ref[:, hs[h]] for h in heads]

    kms = [jnp.concatenate(_split(km_ref[:, hs[h]]), axis=1) for h in heads]
    sts = [lax.dot_general(kms[h], jnp.concatenate([qs[h], qs[h]], axis=1), (((1,), (1,)), ((), ())),
                           preferred_element_type=F32) for h in heads]
    blk = lax.broadcasted_iota(jnp.int32, (nb, tb), 0)
    rowid = lax.broadcasted_iota(jnp.int32, (LANES, tb), 0)
    sel_ts = []
    for h in heads:
        st = jnp.where(blk < qi, sts[h], -jnp.inf)
        sel_t = jnp.full((LANES, tb), -1.0, F32)
        for r in range(MB_TOPK):
            m = jnp.max(st, axis=0, keepdims=True)
            idx = jnp.min(jnp.where(st == m, blk, nb), axis=0, keepdims=True)
            sel_t = jnp.where(rowid == r, jnp.where(r < qi, idx, -1).astype(F32), sel_t)
            st = jnp.where(blk == idx, -jnp.inf, st)
        sel_ts.append(sel_t)
    sels = [jnp.transpose(t) for t in sel_ts]
    sel_rep = [[jnp.broadcast_to(sels[h][:, r:r + 1], (tb, LANES)) for r in range(MB_TOPK)] for h in heads]

    def logits(h, start, width):
        kslab = k_ref[pl.ds(pl.multiple_of(start, tb), width), hs[h]]
        return lax.dot_general(qs[h], kslab, (((1,), (1,)), ((), ())), preferred_element_type=F32)

    r_i = lax.broadcasted_iota(jnp.int32, (tb, LANES), 0)
    c_i = lax.broadcasted_iota(jnp.int32, (tb, LANES), 1)
    sds = [logits(h, qi * tb, tb) for h in heads]
    for h in heads:
        mx = jnp.full((tb, LANES), NEG_BIG, F32)
        for t in range(nt):
            piece = jnp.where(c_i + t * LANES <= r_i, sds[h][:, t * LANES:(t + 1) * LANES], NEG_BIG)
            sd_scr[h, :, t * LANES:(t + 1) * LANES] = piece
            mx = jnp.maximum(mx, piece)
        mx_scr[h] = mx

    n_pairs = (qi + 1) // 2

    def pass1(pairs):
        s2s = [[logits(h, 2 * i * tb, 2 * tb) for h in heads] for i in pairs]
        for h in heads:
            mx = mx_scr[h]
            for n, i in enumerate(pairs):
                for half in range(2):
                    kbf = (2 * i + half).astype(F32)
                    hit = (sel_rep[h][0] == kbf) | (sel_rep[h][1] == kbf) | (sel_rep[h][2] == kbf)
                    for t in range(nt):
                        c0 = (half * nt + t) * LANES
                        piece = jnp.where(hit, s2s[n][h][:, c0:c0 + LANES], NEG_BIG)
                        s_scr[h, i, :, c0:c0 + LANES] = piece
                        mx = jnp.maximum(mx, piece)
            mx_scr[h] = mx

    def run_pairs(fn):
        def body(j, carry):
            fn([4 * j + u for u in range(4)])
            return carry
        lax.fori_loop(0, n_pairs // 4, body, 0)
        rem = n_pairs % 4
        pl.when(rem >= 2)(lambda: fn([n_pairs - rem, n_pairs - rem + 1]))
        pl.when(rem % 2 == 1)(lambda: fn([n_pairs - 1]))

    run_pairs(pass1)
    m_reps = [jnp.broadcast_to(jnp.max(mx_scr[h], axis=-1, keepdims=True), (tb, LANES)) for h in heads]

    def probs(load, width, m_rep):
        ps, lsum = [], jnp.zeros((tb, LANES), F32)
        for t in range(width // LANES):
            p = jnp.exp2(load(t) - m_rep)
            lsum = lsum + p
            ps.append(p.astype(BF16))
        return jnp.concatenate(ps, axis=1), lsum

    pls = [probs(lambda t, h=h: sd_scr[h, :, t * LANES:(t + 1) * LANES], tb, m_reps[h]) for h in heads]
    for h in heads:
        l_scr[h] = pls[h][1]
        acc_scr[h] = jnp.dot(pls[h][0], v_ref[pl.ds(pl.multiple_of(qi * tb, tb), tb), hs[h]],
                             preferred_element_type=F32)

    def pass2(pairs):
        pls = [[probs(lambda t, h=h, i=i: s_scr[h, i, :, t * LANES:(t + 1) * LANES], 2 * tb, m_reps[h])
                for h in heads] for i in pairs]
        for h in heads:
            lsum, acc = l_scr[h], acc_scr[h]
            for n, i in enumerate(pairs):
                vslab = v_ref[pl.ds(pl.multiple_of(2 * i * tb, tb), 2 * tb), hs[h]]
                lsum = lsum + pls[n][h][1]
                acc = acc + jnp.dot(pls[n][h][0], vslab, preferred_element_type=F32)
            l_scr[h], acc_scr[h] = lsum, acc

    run_pairs(pass2)
    for h in heads:
        o_ref[:, hs[h]] = (acc_scr[h] / jnp.sum(l_scr[h], axis=-1, keepdims=True)).astype(o_ref.dtype)


def _moba(qr, kr, vb, kmean, B, S):
    tb = MB_BLOCK
    nb = S // tb
    T = B * S
    km = kmean.reshape(B, nb, MB_HEADS * HEAD_DIM)
    HW = MB_HPS * HEAD_DIM
    ng = MB_HEADS // MB_HPS
    return pl.pallas_call(
        _moba_kernel,
        grid=(B * ng, nb),
        in_specs=[pl.BlockSpec((tb, HW), lambda g, i: ((g // ng) * nb + i, g % ng)),
                  pl.BlockSpec((S, HW), lambda g, i: (g // ng, g % ng)),
                  pl.BlockSpec((S, HW), lambda g, i: (g // ng, g % ng)),
                  pl.BlockSpec((None, nb, HW), lambda g, i: (g // ng, 0, g % ng))],
        out_specs=pl.BlockSpec((tb, HW), lambda g, i: ((g // ng) * nb + i, g % ng)),
        out_shape=jax.ShapeDtypeStruct((T, MB_HEADS * HEAD_DIM), BF16),
        scratch_shapes=[pltpu.VMEM((MB_HPS, (nb + 1) // 2, tb, 2 * tb), F32),
                        pltpu.VMEM((MB_HPS, tb, tb), F32),
                        pltpu.VMEM((MB_HPS, tb, LANES), F32),
                        pltpu.VMEM((MB_HPS, tb, LANES), F32),
                        pltpu.VMEM((MB_HPS, tb, HEAD_DIM), F32)],
        compiler_params=_cparams(("arbitrary", "arbitrary")),
        name="moba",
    )(qr, kr, vb, km)


DN_TILE_CHUNKS = 8


def _softplus(v):
    return jnp.maximum(v, 0.0) + jnp.log1p(jnp.exp(-jnp.abs(v)))


DN_GROUP = 2
DN_HPS = 4


def _split(a):
    hi = pltpu.bitcast(pltpu.bitcast(a, jnp.uint32) & jnp.uint32(0xFFFF0000), F32)
    return hi.astype(BF16), (a - hi).astype(BF16)


def _dot3(ah, al, bh, bl):
    lhs = jnp.concatenate([ah, ah, al], axis=1)
    rhs = jnp.concatenate([bh, bl, bh], axis=0)
    return jnp.dot(lhs, rhs, preferred_element_type=F32)


def _dot_bf(a, b):
    return jnp.dot(a.astype(BF16), b.astype(BF16), preferred_element_type=F32)


def _deltanet_kernel(q_ref, k_ref, v_ref, z_ref, sm_ref, wq_ref, wk_ref, wv_ref, alog_ref, dtb_ref, gout_ref,
                     o_ref, xp_scr, state_scr):
    i = pl.program_id(2)
    C = DN_CHUNK
    TR = q_ref.shape[0]
    HALO = 8

    @pl.when(i == 0)
    def _():
        xp_scr[...] = jnp.zeros(xp_scr.shape, F32)
        state_scr[...] = jnp.zeros(state_scr.shape, F32)

    def conv_silu(slot, x_ref, w_ref, sl):
        xp_scr[slot, 0:HALO, sl] = xp_scr[slot, TR:TR + HALO, sl]
        xp_scr[slot, HALO:HALO + TR, sl] = x_ref[:, sl]
        acc = w_ref[DN_CONV - 1:DN_CONV, sl] * xp_scr[slot, HALO:HALO + TR, sl]
        for j in range(1, DN_CONV):
            acc = acc + w_ref[DN_CONV - 1 - j:DN_CONV - j, sl] * xp_scr[slot, HALO - j:HALO - j + TR, sl]
        return _silu(acc)

    sm = sm_ref[...]
    lane = lax.broadcasted_iota(jnp.int32, sm.shape, 1)
    row = lax.broadcasted_iota(jnp.int32, sm.shape, 0)
    beta_all = _sigmoid(sm)
    g_all = -jnp.exp(alog_ref[...]) * _softplus(sm + dtb_ref[...])
    pos = row % C
    gc_all = g_all
    shift = 1
    while shift < C:
        gc_all = gc_all + jnp.where(pos >= shift, pltpu.roll(gc_all, shift, 0), 0.0)
        shift *= 2
    gc_t = jnp.transpose(gc_all)
    row_t = lax.broadcasted_iota(jnp.int32, gc_t.shape, 0)

    G = DN_GROUP * C
    ri = lax.broadcasted_iota(jnp.int32, (G, G), 0)
    ci = lax.broadcasted_iota(jnp.int32, (G, G), 1)
    same = (ri // C) == (ci // C)
    tril = same & (ci <= ri)
    strict = same & (ci < ri)
    eye = (ci == ri).astype(F32)
    D = HEAD_DIM
    gout = gout_ref[...]

    def head_pipeline(hh):
        h = pl.program_id(1) * DN_HPS + hh
        sl = slice(hh * D, (hh + 1) * D)
        q = conv_silu(0, q_ref, wq_ref, sl)
        k = conv_silu(1, k_ref, wk_ref, sl)
        yield
        v = conv_silu(2, v_ref, wv_ref, sl)
        q = q * lax.rsqrt(jnp.sum(q * q, axis=-1, keepdims=True) + 1e-6) * (D ** -0.5)
        k = k * lax.rsqrt(jnp.sum(k * k, axis=-1, keepdims=True) + 1e-6)
        yield
        beta = jnp.sum(jnp.where(lane == h, beta_all, 0.0), axis=1, keepdims=True)
        gc = jnp.sum(jnp.where(lane == DN_HEADS + h, gc_all, 0.0), axis=1, keepdims=True)
        gc_row = jnp.sum(jnp.where(row_t == DN_HEADS + h, gc_t, 0.0), axis=0, keepdims=True)
        head = (q, k, v, beta, gc, gc_row, jnp.transpose(k))
        yield
        steps = [None] * (TR // C)
        yield from _round_robin([_deltanet_group(head, g, tril, strict, eye, steps) for g in range(TR // G)])
        outs = []
        yield from _deltanet_chain(hh, steps, state_scr, outs)
        o = jnp.concatenate(outs, axis=0)
        y = o * lax.rsqrt(jnp.mean(o * o, axis=-1, keepdims=True) + NORM_EPS) * gout
        o_ref[:, sl] = (y * _silu(z_ref[:, sl])).astype(o_ref.dtype)

    for _ in _round_robin([head_pipeline(hh) for hh in range(DN_HPS)]):
        pass


def _round_robin(gens):
    active = list(gens)
    while active:
        still = []
        for gen in active:
            try:
                next(gen)
                still.append(gen)
            except StopIteration:
                pass
        active = still
        yield


def _deltanet_group(head, g, tril, strict, eye, steps_out):
    C = DN_CHUNK
    D = HEAD_DIM
    G = DN_GROUP * C
    q, k, v, beta, gc, gc_row, k_t = head
    r0 = g * G
    qg, kg, vg = q[r0:r0 + G], k[r0:r0 + G], v[r0:r0 + G]
    bg = beta[r0:r0 + G]
    gcg = gc[r0:r0 + G]
    gcr = gc_row[:, r0:r0 + G]
    ktg = k_t[:, r0:r0 + G]
    decay = jnp.where(tril, jnp.exp(jnp.where(tril, gcg - gcr, 0.0)), 0.0)
    kb = kg * bg
    aq = _dot_bf(jnp.concatenate([kb, qg], axis=0), ktg)
    yield
    m_neg = jnp.where(strict, -(aq[:G] * decay), 0.0)
    qk = (aq[G:] * decay).astype(BF16)
    t_inv = eye + m_neg
    ph, pl_ = _split(m_neg)
    for _ in range(5):
        th, tl = _split(t_inv)
        ph, pl_ = _split(_dot3(ph, pl_, ph, pl_))
        yield
        t_inv = t_inv + _dot3(th, tl, ph, pl_)
        yield
    egc = jnp.exp(gcg)
    th, tl = _split(t_inv)
    rh, rl = _split(jnp.concatenate([kb * egc, vg * bg], axis=1))
    wu = _dot3(th, tl, rh, rl).astype(BF16)
    yield
    qr = jnp.dot(qk, wu, preferred_element_type=F32)
    qp = qg * egc - qr[:, :D]
    r_all = qr[:, D:]
    yield
    for c in range(DN_GROUP):
        c0 = c * C
        g_last = gcg[c0 + C - 1:c0 + C, :]
        kt_tail = ktg[:, c0:c0 + C] * jnp.exp(g_last - gcr[:, c0:c0 + C])
        gh = jnp.dot(kt_tail.astype(BF16), wu[c0:c0 + C, :], preferred_element_type=F32)
        lhs = jnp.concatenate([gh[:, :D], qp[c0:c0 + C]], axis=0).astype(BF16)
        steps_out[g * DN_GROUP + c] = (lhs, gh[:, D:], r_all[c0:c0 + C], jnp.exp(g_last))
        yield


def _deltanet_chain(hh, steps, state_scr, outs):
    D = HEAD_DIM
    state = state_scr[hh]
    for lhs, h_add, r_add, dec in steps:
        res = jnp.dot(lhs, state.astype(BF16), preferred_element_type=F32)
        outs.append(res[D:] + r_add)
        state = state * dec - res[:D] + h_add
        yield
    state_scr[hh] = state


def _deltanet(proj, small, conv_w, a_log, dt_bias, g_dn_out, B, S):
    T = B * S
    TR = min(DN_TILE_CHUNKS * DN_CHUNK, S)
    nt = S // TR
    pad = jnp.zeros((DN_HEADS,), F32)
    rest = jnp.zeros((LANES - 2 * DN_HEADS,), F32)
    alog_lane = jnp.concatenate([pad, a_log.astype(F32), rest]).reshape(1, LANES)
    dtb_lane = jnp.concatenate([pad, dt_bias.astype(F32), rest]).reshape(1, LANES)
    HW = DN_HPS * HEAD_DIM
    per = HW // LANES
    rows = lambda c0: (lambda b, h, i: (b * nt + i, c0 // per + h))
    wcol = lambda c0: (lambda b, h, i: (0, c0 // per + h))
    const = lambda b, h, i: (0, 0)
    return pl.pallas_call(
        _deltanet_kernel,
        grid=(B, DN_HEADS // DN_HPS, nt),
        in_specs=[pl.BlockSpec((TR, HW), rows(COL_DQ)),
                  pl.BlockSpec((TR, HW), rows(COL_DK)),
                  pl.BlockSpec((TR, HW), rows(COL_DV)),
                  pl.BlockSpec((TR, HW), rows(COL_DZ)),
                  pl.BlockSpec((TR, LANES), lambda b, h, i: (b * nt + i, 0)),
                  pl.BlockSpec((DN_CONV, HW), wcol(0)),
                  pl.BlockSpec((DN_CONV, HW), wcol(DN_HEADS)),
                  pl.BlockSpec((DN_CONV, HW), wcol(2 * DN_HEADS)),
                  pl.BlockSpec((1, LANES), const),
                  pl.BlockSpec((1, LANES), const),
                  pl.BlockSpec((1, HEAD_DIM), const)],
        out_specs=pl.BlockSpec((TR, HW), lambda b, h, i: (b * nt + i, h)),
        out_shape=jax.ShapeDtypeStruct((T, DN_HEADS * HEAD_DIM), BF16),
        scratch_shapes=[pltpu.VMEM((3, TR + 8, HW), F32), pltpu.VMEM((DN_HPS, HEAD_DIM, HEAD_DIM), F32)],
        compiler_params=_cparams(("arbitrary", "arbitrary", "arbitrary")),
        name="deltanet",
    )(proj, proj, proj, proj, small, conv_w, conv_w, conv_w, alog_lane, dtb_lane, g_dn_out.reshape(1, HEAD_DIM))


MERGE_PARTS = 4

def _rms(v):
    return v * lax.rsqrt(jnp.mean(v * v, axis=-1, keepdims=True) + NORM_EPS)


def _merge_kernel(oa_ref, ob_ref, ga_ref, gb_ref, x_ref, mod_ref, gpost_ref, gpre_ref, wa_ref, wb_ref, wo_ref,
                  wrt_ref, br_ref, x1_ref, hp_ref, idx_ref, wrow_ref, rank_ref, cnt_ref, carry_scr):
    E = N_EXPERTS
    tm = x_ref.shape[0]
    half = x_ref.shape[1] // 2

    @pl.when(pl.program_id(0) == 0)
    def _():
        carry_scr[...] = jnp.zeros(carry_scr.shape, F32)

    logits = [None] * MERGE_PARTS
    wrh, wrl = _split(wrt_ref[...])
    wr3 = jnp.concatenate([wrh, wrh, wrl], axis=1)

    def rows_pipeline(part):
        n = tm // MERGE_PARTS
        rs = slice(part * n, (part + 1) * n)
        ya = jnp.dot(oa_ref[rs, :], wa_ref[...], preferred_element_type=F32)
        yb = jnp.dot(ob_ref[rs, :], wb_ref[...], preferred_element_type=F32)
        yield
        merged = _sigmoid(ga_ref[rs, :]) * ya + _sigmoid(gb_ref[rs, :]) * yb
        yield
        mix = jnp.dot(merged.astype(BF16), wo_ref[...], preferred_element_type=F32)
        yield
        x1 = x_ref[rs, :] + mod_ref[0, 2:3, :] * (_rms(mix) * gpost_ref[...])
        x1_ref[rs, :] = x1
        h2 = (_rms(x1) * gpre_ref[...]) * (1.0 + mod_ref[0, 4:5, :]) + mod_ref[0, 3:4, :]
        lo_bits = pltpu.bitcast(h2[:, :half].astype(BF16).astype(F32), jnp.uint32) >> 16
        hi_bits = pltpu.bitcast(h2[:, half:].astype(BF16).astype(F32), jnp.uint32) & jnp.uint32(0xFFFF0000)
        hp_ref[rs, :] = hi_bits | lo_bits
        yield
        hh, hl = _split(h2)
        logits[part] = lax.dot_general(wr3, jnp.concatenate([hh, hl, hh], axis=1), (((1,), (1,)), ((), ())),
                                       preferred_element_type=F32)

    for _ in _round_robin([rows_pipeline(part) for part in range(MERGE_PARTS)]):
        pass
    lt = jnp.concatenate(logits, axis=1) + br_ref[...]
    eid = lax.broadcasted_iota(jnp.int32, (E, tm), 0)
    vals, idxs = [], []
    for _ in range(TOP_K):
        m = jnp.max(lt, axis=0, keepdims=True)
        idx = jnp.min(jnp.where(lt == m, eid, E), axis=0, keepdims=True)
        vals.append(m)
        idxs.append(idx)
        lt = jnp.where(eid == idx, -jnp.inf, lt)
    exps = [jnp.exp(v - vals[0]) for v in vals]
    den = exps[0] + exps[1] + exps[2] + exps[3]
    wts = [e / den for e in exps]

    hot = jnp.zeros((E, tm), F32)
    for idx in idxs:
        hot = hot + (eid == idx).astype(F32)
    ti = lax.broadcasted_iota(jnp.int32, (tm, tm), 0)
    tj = lax.broadcasted_iota(jnp.int32, (tm, tm), 1)
    before = (ti < tj).astype(BF16)
    prior = carry_scr[...][:, 0:1] + jnp.dot(hot.astype(BF16), before, preferred_element_type=F32)
    row8 = lax.broadcasted_iota(jnp.int32, (8, tm), 0)
    row128 = lax.broadcasted_iota(jnp.int32, (LANES, tm), 0)
    idx8 = jnp.zeros((8, tm), jnp.int32)
    rank8 = jnp.zeros((8, tm), jnp.int32)
    w128 = jnp.zeros((LANES, tm), F32)
    for r in range(TOP_K):
        rank_r = jnp.sum(jnp.where(eid == idxs[r], prior, 0.0), axis=0, keepdims=True)
        idx8 = jnp.where(row8 == r, idxs[r], idx8)
        rank8 = jnp.where(row8 == r, rank_r.astype(jnp.int32), rank8)
        w128 = jnp.where(row128 == r, wts[r], w128)
    idx_ref[...] = idx8
    rank_ref[...] = rank8
    wrow_ref[...] = jnp.transpose(w128)
    carry = carry_scr[...] + jnp.sum(hot, axis=1, keepdims=True)
    carry_scr[...] = carry
    cnt_ref[...] = carry


def _merge(oa, ob, proj, x2, mod3, g_post_mix, g_pre_ffn, w_br_a, w_br_b, w_o, w_router, b_router, S):
    T, D = x2.shape
    E = N_EXPERTS
    tm = min(512, S)
    per_b = S // tm
    W = DN_HEADS * HEAD_DIM
    row = lambda i: (i, 0)
    const = lambda i: (0, 0)
    lane_t = lambda i: (0, i)
    return pl.pallas_call(
        _merge_kernel,
        grid=(T // tm,),
        in_specs=[pl.BlockSpec((tm, W), row),
                  pl.BlockSpec((tm, W), row),
                  pl.BlockSpec((tm, D), lambda i: (i, COL_GA * LANES // D)),
                  pl.BlockSpec((tm, D), lambda i: (i, COL_GB * LANES // D)),
                  pl.BlockSpec((tm, D), row),
                  pl.BlockSpec((1, 6, D), lambda i: (i // per_b, 0, 0)),
                  pl.BlockSpec((1, D), const),
                  pl.BlockSpec((1, D), const),
                  pl.BlockSpec((W, D), const),
                  pl.BlockSpec((W, D), const),
                  pl.BlockSpec((D, D), const),
                  pl.BlockSpec((E, D), const),
                  pl.BlockSpec((E, 1), const)],
        out_specs=[pl.BlockSpec((tm, D), row),
                   pl.BlockSpec((tm, D // 2), row),
                   pl.BlockSpec((8, tm), lane_t),
                   pl.BlockSpec((tm, LANES), row),
                   pl.BlockSpec((8, tm), lane_t),
                   pl.BlockSpec((E, LANES), const)],
        out_shape=[jax.ShapeDtypeStruct((T, D), F32),
                   jax.ShapeDtypeStruct((T, D // 2), jnp.uint32),
                   jax.ShapeDtypeStruct((8, T), jnp.int32),
                   jax.ShapeDtypeStruct((T, LANES), F32),
                   jax.ShapeDtypeStruct((8, T), jnp.int32),
                   jax.ShapeDtypeStruct((E, LANES), F32)],
        scratch_shapes=[pltpu.VMEM((E, LANES), F32)],
        compiler_params=_cparams(("arbitrary",)),
        name="merge",
    )(oa, ob, proj, proj, x2, mod3, g_post_mix.reshape(1, D), g_pre_ffn.reshape(1, D),
      w_br_a.astype(BF16), w_br_b.astype(BF16), w_o.astype(BF16),
      jnp.transpose(w_router).astype(F32), b_router.reshape(E, 1).astype(F32))


DISPATCH_TOKENS = 512


def _dispatch_kernel(fill_ref, dest_ref, src_ref, dst_ref, zero_scr, sem, zsem):
    n = dest_ref.shape[1]
    tm = zero_scr.shape[0]

    @pl.when(pl.program_id(0) == 0)
    def _():
        zero_scr[...] = jnp.zeros(zero_scr.shape, zero_scr.dtype)

        def fill_copy(k):
            return pltpu.make_async_copy(zero_scr, dst_ref.at[pl.ds(pl.multiple_of(k * tm, tm), tm)], zsem)

        def start(k, carry):
            pl.when(fill_ref[k] == 1)(lambda: fill_copy(k).start())
            return carry

        def wait(k, carry):
            pl.when(fill_ref[k] == 1)(lambda: fill_copy(k).wait())
            return carry

        lax.fori_loop(0, fill_ref.shape[0], start, 0)
        lax.fori_loop(0, fill_ref.shape[0], wait, 0)

    def issue(j, carry):
        for r in range(TOP_K):
            pltpu.make_async_copy(src_ref.at[pl.ds(j, 1)], dst_ref.at[pl.ds(dest_ref[r, j], 1)], sem).start()
        return carry

    lax.fori_loop(0, n, issue, 0, unroll=8)
    for r in range(TOP_K):
        pltpu.make_async_copy(src_ref, dst_ref.at[pl.ds(0, n)], sem).wait()


def _dispatch(dest, blk_fill, hp):
    T, Wd = hp.shape
    n = min(DISPATCH_TOKENS, T)
    tm = EXPERT_ROWS
    grid_spec = pltpu.PrefetchScalarGridSpec(
        num_scalar_prefetch=1,
        grid=(T // n,),
        in_specs=[pl.BlockSpec((TOP_K, n), lambda i, fill: (0, i), memory_space=pltpu.SMEM),
                  pl.BlockSpec((n, Wd), lambda i, fill: (i, 0))],
        out_specs=pl.BlockSpec(memory_space=pl.ANY),
        scratch_shapes=[pltpu.VMEM((tm, Wd), hp.dtype), pltpu.SemaphoreType.DMA(()), pltpu.SemaphoreType.DMA(())],
    )
    return pl.pallas_call(
        _dispatch_kernel,
        grid_spec=grid_spec,
        out_shape=jax.ShapeDtypeStruct((_expert_blocks(T * TOP_K) * tm, Wd), hp.dtype),
        compiler_params=_cparams(("arbitrary",)),
        name="dispatch",
    )(blk_fill, dest, hp)


EXPERT_ROWS = 512
EXPERT_FEATURE_TILE = 512


def _experts_kernel(exp_ref, valid_ref, new_ref, x_ref, wgu_ref, bgu_ref, wdn_ref, bdn_ref, o_ref, wgu_bf, wdn_bf):
    k = pl.program_id(0)
    F = wdn_ref.shape[0]

    @pl.when(new_ref[k] == 1)
    def _():
        wgu_bf[...] = wgu_ref[...].astype(BF16)
        wdn_bf[...] = wdn_ref[...].astype(BF16)

    def ffn():
        word = x_ref[...]
        x = jnp.concatenate([pltpu.bitcast(word << 16, F32).astype(BF16),
                             pltpu.bitcast(word & jnp.uint32(0xFFFF0000), F32).astype(BF16)], axis=1)

        def gate_up(c):
            ft = slice(c, c + EXPERT_FEATURE_TILE)
            ut = slice(F + c, F + c + EXPERT_FEATURE_TILE)
            return (jnp.dot(x, wgu_bf[:, ft], preferred_element_type=F32) + bgu_ref[:, ft],
                    jnp.dot(x, wgu_bf[:, ut], preferred_element_type=F32) + bgu_ref[:, ut])

        tiles = list(range(0, F, EXPERT_FEATURE_TILE))
        y = None
        nxt = gate_up(tiles[0])
        for t, c in enumerate(tiles):
            gate, up = nxt
            if t + 1 < len(tiles):
                nxt = gate_up(tiles[t + 1])
            gate = jnp.minimum(gate, SWIGLU_LIMIT)
            up = jnp.clip(up, -SWIGLU_LIMIT, SWIGLU_LIMIT)
            act = (up + 1.0) * gate * _sigmoid(SWIGLU_ALPHA * gate)
            part = jnp.dot(act.astype(BF16), wdn_bf[c:c + EXPERT_FEATURE_TILE, :], preferred_element_type=F32)
            y = part + bdn_ref[...] if y is None else y + part
        return y

    @pl.when(valid_ref[k] == 1)
    def _():
        o_ref[:, 0, :] = ffn()

    @pl.when(valid_ref[k] == 0)
    def _():
        o_ref[...] = jnp.zeros(o_ref.shape, o_ref.dtype)


def _experts(xs, blk_exp, blk_valid, blk_new, w_gu, b_gu, w_down, b_down):
    R, half = xs.shape
    E, D, F2 = w_gu.shape
    F = F2 // 2
    tm = EXPERT_ROWS
    grid_spec = pltpu.PrefetchScalarGridSpec(
        num_scalar_prefetch=3,
        grid=(R // tm,),
        in_specs=[pl.BlockSpec((tm, half), lambda k, ex, va, nw: (k, 0)),
                  pl.BlockSpec((None, D, F2), lambda k, ex, va, nw: (ex[k], 0, 0)),
                  pl.BlockSpec((None, 1, F2), lambda k, ex, va, nw: (ex[k], 0, 0)),
                  pl.BlockSpec((None, F, D), lambda k, ex, va, nw: (ex[k], 0, 0)),
                  pl.BlockSpec((None, 1, D), lambda k, ex, va, nw: (ex[k], 0, 0))],
        out_specs=pl.BlockSpec((tm, 1, D), lambda k, ex, va, nw: (k, 0, 0)),
        scratch_shapes=[pltpu.VMEM((D, F2), BF16), pltpu.VMEM((F, D), BF16)],
    )
    return pl.pallas_call(
        _experts_kernel,
        grid_spec=grid_spec,
        out_shape=jax.ShapeDtypeStruct((R, 1, D), F32),
        compiler_params=_cparams(("arbitrary",)),
        name="experts",
    )(blk_exp, blk_valid, blk_new, xs, w_gu, b_gu.reshape(E, 1, F2).astype(F32),
      w_down, b_down.reshape(E, 1, D).astype(F32))


def _expert_blocks(A):
    return A // EXPERT_ROWS + N_EXPERTS


def _block_tables(counts, A):
    E = N_EXPERTS
    tm = EXPERT_ROWS
    n_blk = _expert_blocks(A)
    nblk_e = (counts + tm - 1) // tm
    blk_end = jnp.cumsum(nblk_e)
    blk_start = blk_end - nblk_e
    used = blk_end[-1]
    k = jnp.arange(n_blk, dtype=jnp.int32)
    kk = jnp.minimum(k, used - 1)
    e = jnp.sum((blk_end[None, :] <= kk[:, None]).astype(jnp.int32), axis=1)
    hot = e[:, None] == jnp.arange(E, dtype=jnp.int32)[None, :]
    pick = lambda table: jnp.sum(jnp.where(hot, table[None, :], 0), axis=1)
    valid = k < used
    new = valid & (k == pick(blk_start))
    fill = jnp.logical_not(valid) | (k == pick(blk_end) - 1)
    i32 = lambda v: v.astype(jnp.int32)
    return blk_start * tm, i32(e), i32(valid), i32(new), i32(fill)


COMBINE_TOKENS = 512


def _combine_kernel(dest_ref, nxt_ref, y_ref, wrow_ref, x1_ref, mod_ref, gpost_ref, o_ref, stage, moe_scr, sem):
    i = pl.program_id(0)
    last = pl.num_programs(0) - 1
    n = x1_ref.shape[0]
    SUB = 8

    def gather(idx_ref, j, s):
        for r in range(TOP_K):
            pltpu.make_async_copy(y_ref.at[idx_ref[r, j]], stage.at[s, r, j], sem.at[s]).start(priority=r % 2)

    @pl.when(i == 0)
    def _():
        def prime(j, carry):
            gather(dest_ref, j, 0)
            return carry
        lax.fori_loop(0, n, prime, 0, unroll=SUB)

    def step(slot):
        for r in range(TOP_K):
            pltpu.make_async_copy(y_ref.at[pl.ds(0, n)], stage.at[slot, r], sem.at[slot]).wait()

        def reduce_group(g):
            rows = pl.ds(pl.multiple_of(g * SUB, SUB), SUB)
            w = wrow_ref[rows, :]
            acc = w[:, 0:1] * stage[slot, 0, rows, 0, :]
            for r in range(1, TOP_K):
                acc = acc + w[:, r:r + 1] * stage[slot, r, rows, 0, :]
            moe_scr[rows, :] = acc

        @pl.when(i < last)
        def _():
            def body(g, carry):
                for t in range(SUB):
                    gather(nxt_ref, g * SUB + t, 1 - slot)
                reduce_group(g)
                return carry
            lax.fori_loop(0, n // SUB, body, 0)

        @pl.when(i == last)
        def _():
            def body(g, carry):
                reduce_group(g)
                return carry
            lax.fori_loop(0, n // SUB, body, 0)

    for parity in range(2):
        pl.when(i % 2 == parity)(functools.partial(step, parity))

    o_ref[...] = x1_ref[...] + mod_ref[0, 5:6, :] * (_rms(moe_scr[...]) * gpost_ref[...])


def _combine(dest, y, wrow, x1, mod3, g_post_ffn, S):
    T, D = x1.shape
    n = min(COMBINE_TOKENS, S)
    per_b = S // n
    steps = T // n
    return pl.pallas_call(
        _combine_kernel,
        grid=(steps,),
        in_specs=[pl.BlockSpec((TOP_K, n), lambda i: (0, i), memory_space=pltpu.SMEM),
                  pl.BlockSpec((TOP_K, n), lambda i: (0, jnp.minimum(i + 1, steps - 1)), memory_space=pltpu.SMEM),
                  pl.BlockSpec(memory_space=pl.ANY),
                  pl.BlockSpec((n, LANES), lambda i: (i, 0)),
                  pl.BlockSpec((n, D), lambda i: (i, 0)),
                  pl.BlockSpec((1, 6, D), lambda i: (i // per_b, 0, 0)),
                  pl.BlockSpec((1, D), lambda i: (0, 0))],
        out_specs=pl.BlockSpec((n, D), lambda i: (i, 0)),
        out_shape=jax.ShapeDtypeStruct((T, D), F32),
        scratch_shapes=[pltpu.VMEM((2, TOP_K, n, 1, D), F32), pltpu.VMEM((n, D), F32),
                        pltpu.SemaphoreType.DMA((2,))],
        compiler_params=_cparams(("arbitrary",)),
        name="combine",
    )(dest, dest, y, wrow, x1, mod3, g_post_ffn.reshape(1, D))


def _regroup_w_in(w_in):
    D = w_in.shape[0]
    dw = DN_HEADS * HEAD_DIM
    mw = MB_HEADS * HEAD_DIM
    n_dn, n_small, n_mb = 4 * dw, 2 * DN_HEADS, 3 * mw
    dn, small, mb, gates = jnp.split(w_in, [n_dn, n_dn + n_small, n_dn + n_small + n_mb], axis=1)
    small = jnp.concatenate([small, jnp.zeros((D, LANES - n_small), w_in.dtype)], axis=1).astype(BF16)
    return jnp.concatenate([p.astype(BF16) for p in (gates, dn, mb)], axis=1), small


def kernel(x, c, w_ada, b_ada, g_pre_mix, g_post_mix, g_pre_ffn, g_post_ffn, w_in, conv_w, a_log, dt_bias,
           g_dn_out, w_br_a, w_br_b, w_o, w_router, b_router, w_gu, b_gu, w_down, b_down):
    B, S, D = x.shape
    l = 0
    mod = _adaln(c, w_ada[l], b_ada[l]).reshape(B, 6, D)
    proj, small = _inproj(x, mod, g_pre_mix[l], *_regroup_w_in(w_in[l]))
    qr, kr, vb, kmean = _mobaprep(proj, B, S)
    ob = _moba(qr, kr, vb, kmean, B, S)
    oa = _deltanet(proj, small, conv_w[l], a_log[l], dt_bias[l], g_dn_out[l], B, S)
    x1, hp, idx8, wrow, rank8, cnt = _merge(oa, ob, proj, x.reshape(B * S, D), mod, g_post_mix[l], g_pre_ffn[l],
                                            w_br_a[l], w_br_b[l], w_o[l], w_router[l], b_router[l], S)
    out = _moe(x1, hp, idx8, wrow, rank8, cnt, mod, g_post_ffn[l], w_gu[l], b_gu[l], w_down[l], b_down[l], S)
    return out.reshape(B, S, D)


def _moe(x1, hp, idx8, wrow, rank8, cnt, mod, g_post_ffn, w_gu, b_gu, w_down, b_down, S):
    T = x1.shape[0]
    counts = cnt[:, 0].astype(jnp.int32)
    start, blk_exp, blk_valid, blk_new, blk_fill = _block_tables(counts, T * TOP_K)
    hot = idx8[:TOP_K, :, None] == jnp.arange(N_EXPERTS, dtype=jnp.int32)
    dest = rank8[:TOP_K] + jnp.sum(jnp.where(hot, start, 0), axis=-1)
    xs = _dispatch(dest, blk_fill, hp)
    y = _experts(xs, blk_exp, blk_valid, blk_new, w_gu, b_gu, w_down, b_down)
    return _combine(dest, y, wrow, x1, mod, g_post_ffn, S)
```

```python
import functools

import jax
import jax.numpy as jnp
import numpy as np
from jax import lax
from jax.experimental import pallas as pl
from jax.experimental.pallas import tpu as pltpu

F32 = jnp.float32
BF16 = jnp.bfloat16
HIGHEST = lax.Precision.HIGHEST

HEAD_DIM = 128
DN_HEADS = 4
DN_CONV = 4
DN_CHUNK = 64
MB_HEADS = 4
MB_BLOCK = 256
MB_TOPK = 3
ROPE_THETA = 500000.0
ROPE_DIM = HEAD_DIM // 4
N_EXPERTS = 32
TOP_K = 4
SWIGLU_LIMIT = 7.0
SWIGLU_ALPHA = 1.702
NORM_EPS = 1e-6
LANES = 128
NEG_BIG = -1e30
MB_Q_SCALE = HEAD_DIM ** -0.5 * float(np.log2(np.e))

_W = DN_HEADS * HEAD_DIM // LANES
COL_GA = 0
COL_GB = 8
COL_DQ = 16
COL_DK = 20
COL_DV = 24
COL_DZ = 28
COL_MQ = 32
COL_MK = 36
COL_MV = 40
N_PROJ = 44 * LANES

VMEM_LIMIT = 56 * 1024 * 1024


def _cparams(sem):
    return pltpu.CompilerParams(dimension_semantics=sem, vmem_limit_bytes=VMEM_LIMIT)


def _sigmoid(v):
    return 0.5 * jnp.tanh(0.5 * v) + 0.5


def _silu(v):
    return v * _sigmoid(v)


def _adaln_kernel(c_ref, w_ref, b_ref, o_ref):
    a = _silu(c_ref[...])
    o_ref[...] = jnp.dot(a, w_ref[...], preferred_element_type=F32, precision=HIGHEST) + b_ref[...]


def _adaln(c, w_ada, b_ada):
    B, D = c.shape
    N = w_ada.shape[1]
    tn = D
    return pl.pallas_call(
        _adaln_kernel,
        grid=(N // tn,),
        in_specs=[pl.BlockSpec((B, D), lambda j: (0, 0)),
                  pl.BlockSpec((D, tn), lambda j: (0, j)),
                  pl.BlockSpec((1, tn), lambda j: (0, j))],
        out_specs=pl.BlockSpec((B, tn), lambda j: (0, j)),
        out_shape=jax.ShapeDtypeStruct((B, N), F32),
        compiler_params=_cparams(("arbitrary",)),
        name="adaln",
    )(c, w_ada, b_ada.reshape(1, N))


def _inproj_kernel(x_ref, mod_ref, g_ref, w_ref, ws_ref, o_ref, os_ref, h_scr):
    @pl.when(pl.program_id(2) == 0)
    def _():
        x = x_ref[0]
        y = x * lax.rsqrt(jnp.mean(x * x, axis=-1, keepdims=True) + NORM_EPS) * g_ref[...]
        h = y * (1.0 + mod_ref[0, 1:2, :]) + mod_ref[0, 0:1, :]
        h_scr[...] = h.astype(BF16)
        os_ref[...] = jnp.dot(h_scr[...], ws_ref[...], preferred_element_type=F32)

    o_ref[...] = jnp.dot(h_scr[...], w_ref[...], preferred_element_type=F32)


def _inproj(x, mod3, g_pre, w_main, w_small):
    B, S, D = x.shape
    tm = min(1024, S)
    tn = N_PROJ // 2
    nrow = S // tm
    return pl.pallas_call(
        _inproj_kernel,
        grid=(B, nrow, N_PROJ // tn),
        in_specs=[pl.BlockSpec((1, tm, D), lambda b, i, j: (b, i, 0)),
                  pl.BlockSpec((1, 6, D), lambda b, i, j: (b, 0, 0)),
                  pl.BlockSpec((1, D), lambda b, i, j: (0, 0)),
                  pl.BlockSpec((D, tn), lambda b, i, j: (0, j)),
                  pl.BlockSpec((D, LANES), lambda b, i, j: (0, 0))],
        out_specs=[pl.BlockSpec((tm, tn), lambda b, i, j: (b * nrow + i, j)),
                   pl.BlockSpec((tm, LANES), lambda b, i, j: (b * nrow + i, 0))],
        out_shape=[jax.ShapeDtypeStruct((B * S, N_PROJ), F32),
                   jax.ShapeDtypeStruct((B * S, LANES), F32)],
        scratch_shapes=[pltpu.VMEM((tm, D), BF16)],
        compiler_params=_cparams(("arbitrary", "arbitrary", "arbitrary")),
        name="inproj",
    )(x, mod3, g_pre.reshape(1, D), w_main, w_small)


def _rope(v, cosf, sinf, lane):
    rot = jnp.where(lane < ROPE_DIM // 2, pltpu.roll(v, LANES - ROPE_DIM // 2, 1), pltpu.roll(v, ROPE_DIM // 2, 1))
    return v * cosf + rot * sinf


def _mobaprep_kernel(q_ref, k_ref, v_ref, cos_ref, sin_ref, qo_ref, ko_ref, vo_ref, km_ref):
    cosf = cos_ref[...]
    sinf = sin_ref[...]
    lane = lax.broadcasted_iota(jnp.int32, cosf.shape, 1)
    for h in range(MB_HEADS):
        sl = slice(h * HEAD_DIM, (h + 1) * HEAD_DIM)
        qo_ref[:, sl] = (_rope(q_ref[:, sl], cosf, sinf, lane) * MB_Q_SCALE).astype(BF16)
        kr = _rope(k_ref[:, sl], cosf, sinf, lane)
        ko_ref[:, sl] = kr.astype(BF16)
        for blk in range(km_ref.shape[0]):
            km_ref[blk, :, sl] = jnp.mean(kr[blk * MB_BLOCK:(blk + 1) * MB_BLOCK], axis=0, keepdims=True)
    vo_ref[...] = v_ref[...].astype(BF16)


def _rope_tables(S):
    half = ROPE_DIM // 2
    inv_freq = ROPE_THETA ** (-jnp.arange(half, dtype=F32) / half)
    ang = jnp.arange(S, dtype=F32)[:, None] * inv_freq[None, :]
    cos, sin = jnp.cos(ang), jnp.sin(ang)
    rest = HEAD_DIM - ROPE_DIM
    cosf = jnp.concatenate([cos, cos, jnp.ones((S, rest), F32)], axis=1)
    sinf = jnp.concatenate([-sin, sin, jnp.zeros((S, rest), F32)], axis=1)
    return cosf, sinf


def _mobaprep(proj, B, S):
    T = B * S
    tb = min(4 * MB_BLOCK, S)
    nb = S // tb
    W = MB_HEADS * HEAD_DIM
    cosf, sinf = _rope_tables(S)
    col = lambda c: (lambda i: (i, c))
    return pl.pallas_call(
        _mobaprep_kernel,
        grid=(T // tb,),
        in_specs=[pl.BlockSpec((tb, W), col(COL_MQ // _W)),
                  pl.BlockSpec((tb, W), col(COL_MK // _W)),
                  pl.BlockSpec((tb, W), col(COL_MV // _W)),
                  pl.BlockSpec((tb, LANES), lambda i: (i % nb, 0)),
                  pl.BlockSpec((tb, LANES), lambda i: (i % nb, 0))],
        out_specs=[pl.BlockSpec((tb, W), lambda i: (i, 0)),
                   pl.BlockSpec((tb, W), lambda i: (i, 0)),
                   pl.BlockSpec((tb, W), lambda i: (i, 0)),
                   pl.BlockSpec((tb // MB_BLOCK, 1, W), lambda i: (i, 0, 0))],
        out_shape=[jax.ShapeDtypeStruct((T, W), BF16),
                   jax.ShapeDtypeStruct((T, W), BF16),
                   jax.ShapeDtypeStruct((T, W), BF16),
                   jax.ShapeDtypeStruct((T // MB_BLOCK, 1, W), F32)],
        compiler_params=_cparams(("arbitrary",)),
        name="mobaprep",
    )(proj, proj, proj, cosf, sinf)


MB_HPS = 4


def _moba_kernel(q_ref, k_ref, v_ref, km_ref, o_ref, s_scr, sd_scr, mx_scr, l_scr, acc_scr):
    qi = pl.program_id(1)
    tb = MB_BLOCK
    nb = km_ref.shape[0]
    nt = tb // LANES
    D = HEAD_DIM
    heads = range(MB_HPS)
    hs = [slice(h * D, (h + 1) * D) for h in heads]
    qs = [q_ref[:, hs[h]] for h in heads]

    kms = [jnp.concatenate(_split(km_ref[:, hs[h]]), axis=1) for h in heads]
    sts = [lax.dot_general(kms[h], jnp.concatenate([qs[h], qs[h]], axis=1), (((1,), (1,)), ((), ())),
                           preferred_element_type=F32) for h in heads]
    blk = lax.broadcasted_iota(jnp.int32, (nb, tb), 0)
    rowid = lax.broadcasted_iota(jnp.int32, (LANES, tb), 0)
    sel_ts = []
    for h in heads:
        st = jnp.where(blk < qi, sts[h], -jnp.inf)
        sel_t = jnp.full((LANES, tb), -1.0, F32)
        for r in range(MB_TOPK):
            m = jnp.max(st, axis=0, keepdims=True)
            idx = jnp.min(jnp.where(st == m, blk, nb), axis=0, keepdims=True)
            sel_t = jnp.where(rowid == r, jnp.where(r < qi, idx, -1).astype(F32), sel_t)
            st = jnp.where(blk == idx, -jnp.inf, st)
        sel_ts.append(sel_t)
    sels = [jnp.transpose(t) for t in sel_ts]
    sel_rep = [[jnp.broadcast_to(sels[h][:, r:r + 1], (tb, LANES)) for r in range(MB_TOPK)] for h in heads]

    def logits(h, start, width):
        kslab = k_ref[pl.ds(pl.multiple_of(start, tb), width), hs[h]]
        return lax.dot_general(qs[h], kslab, (((1,), (1,)), ((), ())), preferred_element_type=F32)

    r_i = lax.broadcasted_iota(jnp.int32, (tb, LANES), 0)
    c_i = lax.broadcasted_iota(jnp.int32, (tb, LANES), 1)
    sds = [logits(h, qi * tb, tb) for h in heads]
    for h in heads:
        mx = jnp.full((tb, LANES), NEG_BIG, F32)
        for t in range(nt):
            piece = jnp.where(c_i + t * LANES <= r_i, sds[h][:, t * LANES:(t + 1) * LANES], NEG_BIG)
            sd_scr[h, :, t * LANES:(t + 1) * LANES] = piece
            mx = jnp.maximum(mx, piece)
        mx_scr[h] = mx

    n_pairs = (qi + 1) // 2

    def pass1(pairs):
        s2s = [[logits(h, 2 * i * tb, 2 * tb) for h in heads] for i in pairs]
        for h in heads:
            mx = mx_scr[h]
            for n, i in enumerate(pairs):
                for half in range(2):
                    kbf = (2 * i + half).astype(F32)
                    hit = (sel_rep[h][0] == kbf) | (sel_rep[h][1] == kbf) | (sel_rep[h][2] == kbf)
                    for t in range(nt):
                        c0 = (half * nt + t) * LANES
                        piece = jnp.where(hit, s2s[n][h][:, c0:c0 + LANES], NEG_BIG)
                        s_scr[h, i, :, c0:c0 + LANES] = piece
                        mx = jnp.maximum(mx, piece)
            mx_scr[h] = mx

    def run_pairs(fn):
        def body(j, carry):
            fn([4 * j + u for u in range(4)])
            return carry
        lax.fori_loop(0, n_pairs // 4, body, 0)
        rem = n_pairs % 4
        pl.when(rem >= 2)(lambda: fn([n_pairs - rem, n_pairs - rem + 1]))
        pl.when(rem % 2 == 1)(lambda: fn([n_pairs - 1]))

    run_pairs(pass1)
    m_reps = [jnp.broadcast_to(jnp.max(mx_scr[h], axis=-1, keepdims=True), (tb, LANES)) for h in heads]

    def probs(load, width, m_rep):
        ps, lsum = [], jnp.zeros((tb, LANES), F32)
        for t in range(width // LANES):
            p = jnp.exp2(load(t) - m_rep)
            lsum = lsum + p
            ps.append(p.astype(BF16))
        return jnp.concatenate(ps, axis=1), lsum

    pls = [probs(lambda t, h=h: sd_scr[h, :, t * LANES:(t + 1) * LANES], tb, m_reps[h]) for h in heads]
    for h in heads:
        l_scr[h] = pls[h][1]
        acc_scr[h] = jnp.dot(pls[h][0], v_ref[pl.ds(pl.multiple_of(qi * tb, tb), tb), hs[h]],
                             preferred_element_type=F32)

    def pass2(pairs):
        pls = [[probs(lambda t, h=h, i=i: s_scr[h, i, :, t * LANES:(t + 1) * LANES], 2 * tb, m_reps[h])
                for h in heads] for i in pairs]
        for h in heads:
            lsum, acc = l_scr[h], acc_scr[h]
            for n, i in enumerate(pairs):
                vslab = v_ref[pl.ds(pl.multiple_of(2 * i * tb, tb), 2 * tb), hs[h]]
                lsum = lsum + pls[n][h][1]
                acc = acc + jnp.dot(pls[n][h][0], vslab, preferred_element_type=F32)
            l_scr[h], acc_scr[h] = lsum, acc

    run_pairs(pass2)
    for h in heads:
        o_ref[:, hs[h]] = (acc_scr[h] / jnp.sum(l_scr[h], axis=-1, keepdims=True)).astype(o_ref.dtype)


def _moba(qr, kr, vb, kmean, B, S):
    tb = MB_BLOCK
    nb = S // tb
    T = B * S
    km = kmean.reshape(B, nb, MB_HEADS * HEAD_DIM)
    HW = MB_HPS * HEAD_DIM
    ng = MB_HEADS // MB_HPS
    return pl.pallas_call(
        _moba_kernel,
        grid=(B * ng, nb),
        in_specs=[pl.BlockSpec((tb, HW), lambda g, i: ((g // ng) * nb + i, g % ng)),
                  pl.BlockSpec((S, HW), lambda g, i: (g // ng, g % ng)),
                  pl.BlockSpec((S, HW), lambda g, i: (g // ng, g % ng)),
                  pl.BlockSpec((None, nb, HW), lambda g, i: (g // ng, 0, g % ng))],
        out_specs=pl.BlockSpec((tb, HW), lambda g, i: ((g // ng) * nb + i, g % ng)),
        out_shape=jax.ShapeDtypeStruct((T, MB_HEADS * HEAD_DIM), BF16),
        scratch_shapes=[pltpu.VMEM((MB_HPS, (nb + 1) // 2, tb, 2 * tb), F32),
                        pltpu.VMEM((MB_HPS, tb, tb), F32),
                        pltpu.VMEM((MB_HPS, tb, LANES), F32),
                        pltpu.VMEM((MB_HPS, tb, LANES), F32),
                        pltpu.VMEM((MB_HPS, tb, HEAD_DIM), F32)],
        compiler_params=_cparams(("arbitrary", "arbitrary")),
        name="moba",
    )(qr, kr, vb, km)


DN_TILE_CHUNKS = 8


def _softplus(v):
    return jnp.maximum(v, 0.0) + jnp.log1p(jnp.exp(-jnp.abs(v)))


DN_GROUP = 2
DN_HPS = 4


def _split(a):
    hi = pltpu.bitcast(pltpu.bitcast(a, jnp.uint32) & jnp.uint32(0xFFFF0000), F32)
    return hi.astype(BF16), (a - hi).astype(BF16)


def _dot3(ah, al, bh, bl):
    lhs = jnp.concatenate([ah, ah, al], axis=1)
    rhs = jnp.concatenate([bh, bl, bh], axis=0)
    return jnp.dot(lhs, rhs, preferred_element_type=F32)


def _dot_bf(a, b):
    return jnp.dot(a.astype(BF16), b.astype(BF16), preferred_element_type=F32)


def _deltanet_kernel(q_ref, k_ref, v_ref, z_ref, sm_ref, wq_ref, wk_ref, wv_ref, alog_ref, dtb_ref, gout_ref,
                     o_ref, xp_scr, state_scr):
    i = pl.program_id(2)
    C = DN_CHUNK
    TR = q_ref.shape[0]
    HALO = 8

    @pl.when(i == 0)
    def _():
        xp_scr[...] = jnp.zeros(xp_scr.shape, F32)
        state_scr[...] = jnp.zeros(state_scr.shape, F32)

    def conv_silu(slot, x_ref, w_ref, sl):
        xp_scr[slot, 0:HALO, sl] = xp_scr[slot, TR:TR + HALO, sl]
        xp_scr[slot, HALO:HALO + TR, sl] = x_ref[:, sl]
        acc = w_ref[DN_CONV - 1:DN_CONV, sl] * xp_scr[slot, HALO:HALO + TR, sl]
        for j in range(1, DN_CONV):
            acc = acc + w_ref[DN_CONV - 1 - j:DN_CONV - j, sl] * xp_scr[slot, HALO - j:HALO - j + TR, sl]
        return _silu(acc)

    sm = sm_ref[...]
    lane = lax.broadcasted_iota(jnp.int32, sm.shape, 1)
    row = lax.broadcasted_iota(jnp.int32, sm.shape, 0)
    beta_all = _sigmoid(sm)
    g_all = -jnp.exp(alog_ref[...]) * _softplus(sm + dtb_ref[...])
    pos = row % C
    gc_all = g_all
    shift = 1
    while shift < C:
        gc_all = gc_all + jnp.where(pos >= shift, pltpu.roll(gc_all, shift, 0), 0.0)
        shift *= 2
    gc_t = jnp.transpose(gc_all)
    row_t = lax.broadcasted_iota(jnp.int32, gc_t.shape, 0)

    G = DN_GROUP * C
    ri = lax.broadcasted_iota(jnp.int32, (G, G), 0)
    ci = lax.broadcasted_iota(jnp.int32, (G, G), 1)
    same = (ri // C) == (ci // C)
    tril = same & (ci <= ri)
    strict = same & (ci < ri)
    eye = (ci == ri).astype(F32)
    D = HEAD_DIM
    gout = gout_ref[...]

    def head_pipeline(hh):
        h = pl.program_id(1) * DN_HPS + hh
        sl = slice(hh * D, (hh + 1) * D)
        q = conv_silu(0, q_ref, wq_ref, sl)
        k = conv_silu(1, k_ref, wk_ref, sl)
        yield
        v = conv_silu(2, v_ref, wv_ref, sl)
        q = q * lax.rsqrt(jnp.sum(q * q, axis=-1, keepdims=True) + 1e-6) * (D ** -0.5)
        k = k * lax.rsqrt(jnp.sum(k * k, axis=-1, keepdims=True) + 1e-6)
        yield
        beta = jnp.sum(jnp.where(lane == h, beta_all, 0.0), axis=1, keepdims=True)
        gc = jnp.sum(jnp.where(lane == DN_HEADS + h, gc_all, 0.0), axis=1, keepdims=True)
        gc_row = jnp.sum(jnp.where(row_t == DN_HEADS + h, gc_t, 0.0), axis=0, keepdims=True)
        head = (q, k, v, beta, gc, gc_row, jnp.transpose(k))
        yield
        steps = [None] * (TR // C)
        yield from _round_robin([_deltanet_group(head, g, tril, strict, eye, steps) for g in range(TR // G)])
        outs = []
        yield from _deltanet_chain(hh, steps, state_scr, outs)
        o = jnp.concatenate(outs, axis=0)
        y = o * lax.rsqrt(jnp.mean(o * o, axis=-1, keepdims=True) + NORM_EPS) * gout
        o_ref[:, sl] = (y * _silu(z_ref[:, sl])).astype(o_ref.dtype)

    for _ in _round_robin([head_pipeline(hh) for hh in range(DN_HPS)]):
        pass


def _round_robin(gens):
    active = list(gens)
    while active:
        still = []
        for gen in active:
            try:
                next(gen)
                still.append(gen)
            except StopIteration:
                pass
        active = still
        yield


def _deltanet_group(head, g, tril, strict, eye, steps_out):
    C = DN_CHUNK
    D = HEAD_DIM
    G = DN_GROUP * C
    q, k, v, beta, gc, gc_row, k_t = head
    r0 = g * G
    qg, kg, vg = q[r0:r0 + G], k[r0:r0 + G], v[r0:r0 + G]
    bg = beta[r0:r0 + G]
    gcg = gc[r0:r0 + G]
    gcr = gc_row[:, r0:r0 + G]
    ktg = k_t[:, r0:r0 + G]
    decay = jnp.where(tril, jnp.exp(jnp.where(tril, gcg - gcr, 0.0)), 0.0)
    kb = kg * bg
    aq = _dot_bf(jnp.concatenate([kb, qg], axis=0), ktg)
    yield
    m_neg = jnp.where(strict, -(aq[:G] * decay), 0.0)
    qk = (aq[G:] * decay).astype(BF16)
    t_inv = eye + m_neg
    ph, pl_ = _split(m_neg)
    for _ in range(5):
        th, tl = _split(t_inv)
        ph, pl_ = _split(_dot3(ph, pl_, ph, pl_))
        yield
        t_inv = t_inv + _dot3(th, tl, ph, pl_)
        yield
    egc = jnp.exp(gcg)
    th, tl = _split(t_inv)
    rh, rl = _split(jnp.concatenate([kb * egc, vg * bg], axis=1))
    wu = _dot3(th, tl, rh, rl).astype(BF16)
    yield
    qr = jnp.dot(qk, wu, preferred_element_type=F32)
    qp = qg * egc - qr[:, :D]
    r_all = qr[:, D:]
    yield
    for c in range(DN_GROUP):
        c0 = c * C
        g_last = gcg[c0 + C - 1:c0 + C, :]
        kt_tail = ktg[:, c0:c0 + C] * jnp.exp(g_last - gcr[:, c0:c0 + C])
        gh = jnp.dot(kt_tail.astype(BF16), wu[c0:c0 + C, :], preferred_element_type=F32)
        lhs = jnp.concatenate([gh[:, :D], qp[c0:c0 + C]], axis=0).astype(BF16)
        steps_out[g * DN_GROUP + c] = (lhs, gh[:, D:], r_all[c0:c0 + C], jnp.exp(g_last))
        yield


def _deltanet_chain(hh, steps, state_scr, outs):
    D = HEAD_DIM
    state = state_scr[hh]
    for lhs, h_add, r_add, dec in steps:
        res = jnp.dot(lhs, state.astype(BF16), preferred_element_type=F32)
        outs.append(res[D:] + r_add)
        state = state * dec - res[:D] + h_add
        yield
    state_scr[hh] = state


def _deltanet(proj, small, conv_w, a_log, dt_bias, g_dn_out, B, S):
    T = B * S
    TR = min(DN_TILE_CHUNKS * DN_CHUNK, S)
    nt = S // TR
    pad = jnp.zeros((DN_HEADS,), F32)
    rest = jnp.zeros((LANES - 2 * DN_HEADS,), F32)
    alog_lane = jnp.concatenate([pad, a_log.astype(F32), rest]).reshape(1, LANES)
    dtb_lane = jnp.concatenate([pad, dt_bias.astype(F32), rest]).reshape(1, LANES)
    HW = DN_HPS * HEAD_DIM
    per = HW // LANES
    rows = lambda c0: (lambda b, h, i: (b * nt + i, c0 // per + h))
    wcol = lambda c0: (lambda b, h, i: (0, c0 // per + h))
    const = lambda b, h, i: (0, 0)
    return pl.pallas_call(
        _deltanet_kernel,
        grid=(B, DN_HEADS // DN_HPS, nt),
        in_specs=[pl.BlockSpec((TR, HW), rows(COL_DQ)),
                  pl.BlockSpec((TR, HW), rows(COL_DK)),
                  pl.BlockSpec((TR, HW), rows(COL_DV)),
                  pl.BlockSpec((TR, HW), rows(COL_DZ)),
                  pl.BlockSpec((TR, LANES), lambda b, h, i: (b * nt + i, 0)),
                  pl.BlockSpec((DN_CONV, HW), wcol(0)),
                  pl.BlockSpec((DN_CONV, HW), wcol(DN_HEADS)),
                  pl.BlockSpec((DN_CONV, HW), wcol(2 * DN_HEADS)),
                  pl.BlockSpec((1, LANES), const),
                  pl.BlockSpec((1, LANES), const),
                  pl.BlockSpec((1, HEAD_DIM), const)],
        out_specs=pl.BlockSpec((TR, HW), lambda b, h, i: (b * nt + i, h)),
        out_shape=jax.ShapeDtypeStruct((T, DN_HEADS * HEAD_DIM), BF16),
        scratch_shapes=[pltpu.VMEM((3, TR + 8, HW), F32), pltpu.VMEM((DN_HPS, HEAD_DIM, HEAD_DIM), F32)],
        compiler_params=_cparams(("arbitrary", "arbitrary", "arbitrary")),
        name="deltanet",
    )(proj, proj, proj, proj, small, conv_w, conv_w, conv_w, alog_lane, dtb_lane, g_dn_out.reshape(1, HEAD_DIM))


MERGE_PARTS = 4

def _rms(v):
    return v * lax.rsqrt(jnp.mean(v * v, axis=-1, keepdims=True) + NORM_EPS)


def _merge_kernel(oa_ref, ob_ref, ga_ref, gb_ref, x_ref, mod_ref, gpost_ref, gpre_ref, wa_ref, wb_ref, wo_ref,
                  wrt_ref, br_ref, x1_ref, hp_ref, idx_ref, wrow_ref, rank_ref, cnt_ref, carry_scr):
    E = N_EXPERTS
    tm = x_ref.shape[0]
    half = x_ref.shape[1] // 2

    @pl.when(pl.program_id(0) == 0)
    def _():
        carry_scr[...] = jnp.zeros(carry_scr.shape, F32)

    logits = [None] * MERGE_PARTS
    wrh, wrl = _split(wrt_ref[...])
    wr3 = jnp.concatenate([wrh, wrh, wrl], axis=1)

    def rows_pipeline(part):
        n = tm // MERGE_PARTS
        rs = slice(part * n, (part + 1) * n)
        ya = jnp.dot(oa_ref[rs, :], wa_ref[...], preferred_element_type=F32)
        yb = jnp.dot(ob_ref[rs, :], wb_ref[...], preferred_element_type=F32)
        yield
        merged = _sigmoid(ga_ref[rs, :]) * ya + _sigmoid(gb_ref[rs, :]) * yb
        yield
        mix = jnp.dot(merged.astype(BF16), wo_ref[...], preferred_element_type=F32)
        yield
        x1 = x_ref[rs, :] + mod_ref[0, 2:3, :] * (_rms(mix) * gpost_ref[...])
        x1_ref[rs, :] = x1
        h2 = (_rms(x1) * gpre_ref[...]) * (1.0 + mod_ref[0, 4:5, :]) + mod_ref[0, 3:4, :]
        lo_bits = pltpu.bitcast(h2[:, :half].astype(BF16).astype(F32), jnp.uint32) >> 16
        hi_bits = pltpu.bitcast(h2[:, half:].astype(BF16).astype(F32), jnp.uint32) & jnp.uint32(0xFFFF0000)
        hp_ref[rs, :] = hi_bits | lo_bits
        yield
        hh, hl = _split(h2)
        logits[part] = lax.dot_general(wr3, jnp.concatenate([hh, hl, hh], axis=1), (((1,), (1,)), ((), ())),
                                       preferred_element_type=F32)

    for _ in _round_robin([rows_pipeline(part) for part in range(MERGE_PARTS)]):
        pass
    lt = jnp.concatenate(logits, axis=1) + br_ref[...]
    eid = lax.broadcasted_iota(jnp.int32, (E, tm), 0)
    vals, idxs = [], []
    for _ in range(TOP_K):
        m = jnp.max(lt, axis=0, keepdims=True)
        idx = jnp.min(jnp.where(lt == m, eid, E), axis=0, keepdims=True)
        vals.append(m)
        idxs.append(idx)
        lt = jnp.where(eid == idx, -jnp.inf, lt)
    exps = [jnp.exp(v - vals[0]) for v in vals]
    den = exps[0] + exps[1] + exps[2] + exps[3]
    wts = [e / den for e in exps]

    hot = jnp.zeros((E, tm), F32)
    for idx in idxs:
        hot = hot + (eid == idx).astype(F32)
    ti = lax.broadcasted_iota(jnp.int32, (tm, tm), 0)
    tj = lax.broadcasted_iota(jnp.int32, (tm, tm), 1)
    before = (ti < tj).astype(BF16)
    prior = carry_scr[...][:, 0:1] + jnp.dot(hot.astype(BF16), before, preferred_element_type=F32)
    row8 = lax.broadcasted_iota(jnp.int32, (8, tm), 0)
    row128 = lax.broadcasted_iota(jnp.int32, (LANES, tm), 0)
    idx8 = jnp.zeros((8, tm), jnp.int32)
    rank8 = jnp.zeros((8, tm), jnp.int32)
    w128 = jnp.zeros((LANES, tm), F32)
    for r in range(TOP_K):
        rank_r = jnp.sum(jnp.where(eid == idxs[r], prior, 0.0), axis=0, keepdims=True)
        idx8 = jnp.where(row8 == r, idxs[r], idx8)
        rank8 = jnp.where(row8 == r, rank_r.astype(jnp.int32), rank8)
        w128 = jnp.where(row128 == r, wts[r], w128)
    idx_ref[...] = idx8
    rank_ref[...] = rank8
    wrow_ref[...] = jnp.transpose(w128)
    carry = carry_scr[...] + jnp.sum(hot, axis=1, keepdims=True)
    carry_scr[...] = carry
    cnt_ref[...] = carry


def _merge(oa, ob, proj, x2, mod3, g_post_mix, g_pre_ffn, w_br_a, w_br_b, w_o, w_router, b_router, S):
    T, D = x2.shape
    E = N_EXPERTS
    tm = min(512, S)
    per_b = S // tm
    W = DN_HEADS * HEAD_DIM
    row = lambda i: (i, 0)
    const = lambda i: (0, 0)
    lane_t = lambda i: (0, i)
    return pl.pallas_call(
        _merge_kernel,
        grid=(T // tm,),
        in_specs=[pl.BlockSpec((tm, W), row),
                  pl.BlockSpec((tm, W), row),
                  pl.BlockSpec((tm, D), lambda i: (i, COL_GA * LANES // D)),
                  pl.BlockSpec((tm, D), lambda i: (i, COL_GB * LANES // D)),
                  pl.BlockSpec((tm, D), row),
                  pl.BlockSpec((1, 6, D), lambda i: (i // per_b, 0, 0)),
                  pl.BlockSpec((1, D), const),
                  pl.BlockSpec((1, D), const),
                  pl.BlockSpec((W, D), const),
                  pl.BlockSpec((W, D), const),
                  pl.BlockSpec((D, D), const),
                  pl.BlockSpec((E, D), const),
                  pl.BlockSpec((E, 1), const)],
        out_specs=[pl.BlockSpec((tm, D), row),
                   pl.BlockSpec((tm, D // 2), row),
                   pl.BlockSpec((8, tm), lane_t),
                   pl.BlockSpec((tm, LANES), row),
                   pl.BlockSpec((8, tm), lane_t),
                   pl.BlockSpec((E, LANES), const)],
        out_shape=[jax.ShapeDtypeStruct((T, D), F32),
                   jax.ShapeDtypeStruct((T, D // 2), jnp.uint32),
                   jax.ShapeDtypeStruct((8, T), jnp.int32),
                   jax.ShapeDtypeStruct((T, LANES), F32),
                   jax.ShapeDtypeStruct((8, T), jnp.int32),
                   jax.ShapeDtypeStruct((E, LANES), F32)],
        scratch_shapes=[pltpu.VMEM((E, LANES), F32)],
        compiler_params=_cparams(("arbitrary",)),
        name="merge",
    )(oa, ob, proj, proj, x2, mod3, g_post_mix.reshape(1, D), g_pre_ffn.reshape(1, D),
      w_br_a.astype(BF16), w_br_b.astype(BF16), w_o.astype(BF16),
      jnp.transpose(w_router).astype(F32), b_router.reshape(E, 1).astype(F32))


DISPATCH_TOKENS = 512


def _dispatch_kernel(fill_ref, dest_ref, src_ref, dst_ref, zero_scr, sem, zsem):
    n = dest_ref.shape[1]
    tm = zero_scr.shape[0]

    @pl.when(pl.program_id(0) == 0)
    def _():
        zero_scr[...] = jnp.zeros(zero_scr.shape, zero_scr.dtype)

        def fill_copy(k):
            return pltpu.make_async_copy(zero_scr, dst_ref.at[pl.ds(pl.multiple_of(k * tm, tm), tm)], zsem)

        def start(k, carry):
            pl.when(fill_ref[k] == 1)(lambda: fill_copy(k).start())
            return carry

        def wait(k, carry):
            pl.when(fill_ref[k] == 1)(lambda: fill_copy(k).wait())
            return carry

        lax.fori_loop(0, fill_ref.shape[0], start, 0)
        lax.fori_loop(0, fill_ref.shape[0], wait, 0)

    def issue(j, carry):
        for r in range(TOP_K):
            pltpu.make_async_copy(src_ref.at[pl.ds(j, 1)], dst_ref.at[pl.ds(dest_ref[r, j], 1)], sem).start()
        return carry

    lax.fori_loop(0, n, issue, 0, unroll=8)
    for r in range(TOP_K):
        pltpu.make_async_copy(src_ref, dst_ref.at[pl.ds(0, n)], sem).wait()


def _dispatch(dest, blk_fill, hp):
    T, Wd = hp.shape
    n = min(DISPATCH_TOKENS, T)
    tm = EXPERT_ROWS
    grid_spec = pltpu.PrefetchScalarGridSpec(
        num_scalar_prefetch=1,
        grid=(T // n,),
        in_specs=[pl.BlockSpec((TOP_K, n), lambda i, fill: (0, i), memory_space=pltpu.SMEM),
                  pl.BlockSpec((n, Wd), lambda i, fill: (i, 0))],
        out_specs=pl.BlockSpec(memory_space=pl.ANY),
        scratch_shapes=[pltpu.VMEM((tm, Wd), hp.dtype), pltpu.SemaphoreType.DMA(()), pltpu.SemaphoreType.DMA(())],
    )
    return pl.pallas_call(
        _dispatch_kernel,
        grid_spec=grid_spec,
        out_shape=jax.ShapeDtypeStruct((_expert_blocks(T * TOP_K) * tm, Wd), hp.dtype),
        compiler_params=_cparams(("arbitrary",)),
        name="dispatch",
    )(blk_fill, dest, hp)


EXPERT_ROWS = 512
EXPERT_FEATURE_TILE = 512


def _experts_kernel(exp_ref, valid_ref, new_ref, x_ref, wgu_ref, bgu_ref, wdn_ref, bdn_ref, o_ref, wgu_bf, wdn_bf):
    k = pl.program_id(0)
    F = wdn_ref.shape[0]
    tm = x_ref.shape[0]

    @pl.when(new_ref[k] == 1)
    def _():
        wgu_bf[...] = wgu_ref[...].astype(BF16)
        wdn_bf[...] = wdn_ref[...].astype(BF16)

    def ffn(rows):
        word = x_ref[0:rows, :]
        x = jnp.concatenate([pltpu.bitcast(word << 16, F32).astype(BF16),
                             pltpu.bitcast(word & jnp.uint32(0xFFFF0000), F32).astype(BF16)], axis=1)

        def gate_up(c):
            ft = slice(c, c + EXPERT_FEATURE_TILE)
            ut = slice(F + c, F + c + EXPERT_FEATURE_TILE)
            return (jnp.dot(x, wgu_bf[:, ft], preferred_element_type=F32) + bgu_ref[:, ft],
                    jnp.dot(x, wgu_bf[:, ut], preferred_element_type=F32) + bgu_ref[:, ut])

        tiles = list(range(0, F, EXPERT_FEATURE_TILE))
        y = None
        nxt = gate_up(tiles[0])
        for t, c in enumerate(tiles):
            gate, up = nxt
            if t + 1 < len(tiles):
                nxt = gate_up(tiles[t + 1])
            gate = jnp.minimum(gate, SWIGLU_LIMIT)
            up = jnp.clip(up, -SWIGLU_LIMIT, SWIGLU_LIMIT)
            act = (up + 1.0) * gate * _sigmoid(SWIGLU_ALPHA * gate)
            part = jnp.dot(act.astype(BF16), wdn_bf[c:c + EXPERT_FEATURE_TILE, :], preferred_element_type=F32)
            y = part + bdn_ref[...] if y is None else y + part
        return y

    @pl.when(valid_ref[k] == 2)
    def _():
        o_ref[:, 0, :] = ffn(tm)

    @pl.when(valid_ref[k] == 1)
    def _():
        o_ref[0:tm // 2, 0, :] = ffn(tm // 2)
        o_ref[tm // 2:tm, :, :] = jnp.zeros((tm // 2,) + o_ref.shape[1:], o_ref.dtype)

    @pl.when(valid_ref[k] == 0)
    def _():
        o_ref[...] = jnp.zeros(o_ref.shape, o_ref.dtype)


def _experts(xs, blk_exp, blk_valid, blk_new, w_gu, b_gu, w_down, b_down):
    R, half = xs.shape
    E, D, F2 = w_gu.shape
    F = F2 // 2
    tm = EXPERT_ROWS
    grid_spec = pltpu.PrefetchScalarGridSpec(
        num_scalar_prefetch=3,
        grid=(R // tm,),
        in_specs=[pl.BlockSpec((tm, half), lambda k, ex, va, nw: (k, 0)),
                  pl.BlockSpec((None, D, F2), lambda k, ex, va, nw: (ex[k], 0, 0)),
                  pl.BlockSpec((None, 1, F2), lambda k, ex, va, nw: (ex[k], 0, 0)),
                  pl.BlockSpec((None, F, D), lambda k, ex, va, nw: (ex[k], 0, 0)),
                  pl.BlockSpec((None, 1, D), lambda k, ex, va, nw: (ex[k], 0, 0))],
        out_specs=pl.BlockSpec((tm, 1, D), lambda k, ex, va, nw: (k, 0, 0)),
        scratch_shapes=[pltpu.VMEM((D, F2), BF16), pltpu.VMEM((F, D), BF16)],
    )
    return pl.pallas_call(
        _experts_kernel,
        grid_spec=grid_spec,
        out_shape=jax.ShapeDtypeStruct((R, 1, D), F32),
        compiler_params=_cparams(("arbitrary",)),
        name="experts",
    )(blk_exp, blk_valid, blk_new, xs, w_gu, b_gu.reshape(E, 1, F2).astype(F32),
      w_down, b_down.reshape(E, 1, D).astype(F32))


def _expert_blocks(A):
    return A // EXPERT_ROWS + N_EXPERTS


def _block_tables(counts, A):
    E = N_EXPERTS
    tm = EXPERT_ROWS
    n_blk = _expert_blocks(A)
    nblk_e = (counts + tm - 1) // tm
    blk_end = jnp.cumsum(nblk_e)
    blk_start = blk_end - nblk_e
    used = blk_end[-1]
    k = jnp.arange(n_blk, dtype=jnp.int32)
    kk = jnp.minimum(k, used - 1)
    e = jnp.sum((blk_end[None, :] <= kk[:, None]).astype(jnp.int32), axis=1)
    hot = e[:, None] == jnp.arange(E, dtype=jnp.int32)[None, :]
    pick = lambda table: jnp.sum(jnp.where(hot, table[None, :], 0), axis=1)
    valid = k < used
    new = valid & (k == pick(blk_start))
    fill = jnp.logical_not(valid) | (k == pick(blk_end) - 1)
    rows_used = pick(counts) - (k - pick(blk_start)) * tm
    i32 = lambda v: v.astype(jnp.int32)
    level = i32(valid) + i32(valid & (rows_used > tm // 2))
    return blk_start * tm, i32(e), level, i32(new), i32(fill)


COMBINE_TOKENS = 512


def _combine_kernel(dest_ref, nxt_ref, y_ref, wrow_ref, x1_ref, mod_ref, gpost_ref, o_ref, stage, moe_scr, sem):
    i = pl.program_id(0)
    last = pl.num_programs(0) - 1
    n = x1_ref.shape[0]
    SUB = 8

    def gather(idx_ref, j, s):
        for r in range(TOP_K):
            pltpu.make_async_copy(y_ref.at[idx_ref[r, j]], stage.at[s, r, j], sem.at[s]).start(priority=r % 2)

    @pl.when(i == 0)
    def _():
        def prime(j, carry):
            gather(dest_ref, j, 0)
            return carry
        lax.fori_loop(0, n, prime, 0, unroll=SUB)

    def step(slot):
        for r in range(TOP_K):
            pltpu.make_async_copy(y_ref.at[pl.ds(0, n)], stage.at[slot, r], sem.at[slot]).wait()

        def reduce_group(g):
            rows = pl.ds(pl.multiple_of(g * SUB, SUB), SUB)
            w = wrow_ref[rows, :]
            acc = w[:, 0:1] * stage[slot, 0, rows, 0, :]
            for r in range(1, TOP_K):
                acc = acc + w[:, r:r + 1] * stage[slot, r, rows, 0, :]
            moe_scr[rows, :] = acc

        @pl.when(i < last)
        def _():
            def body(g, carry):
                for t in range(SUB):
                    gather(nxt_ref, g * SUB + t, 1 - slot)
                reduce_group(g)
                return carry
            lax.fori_loop(0, n // SUB, body, 0)

        @pl.when(i == last)
        def _():
            def body(g, carry):
                reduce_group(g)
                return carry
            lax.fori_loop(0, n // SUB, body, 0)

    for parity in range(2):
        pl.when(i % 2 == parity)(functools.partial(step, parity))

    o_ref[...] = x1_ref[...] + mod_ref[0, 5:6, :] * (_rms(moe_scr[...]) * gpost_ref[...])


def _combine(dest, y, wrow, x1, mod3, g_post_ffn, S):
    T, D = x1.shape
    n = min(COMBINE_TOKENS, S)
    per_b = S // n
    steps = T // n
    return pl.pallas_call(
        _combine_kernel,
        grid=(steps,),
        in_specs=[pl.BlockSpec((TOP_K, n), lambda i: (0, i), memory_space=pltpu.SMEM),
                  pl.BlockSpec((TOP_K, n), lambda i: (0, jnp.minimum(i + 1, steps - 1)), memory_space=pltpu.SMEM),
                  pl.BlockSpec(memory_space=pl.ANY),
                  pl.BlockSpec((n, LANES), lambda i: (i, 0)),
                  pl.BlockSpec((n, D), lambda i: (i, 0)),
                  pl.BlockSpec((1, 6, D), lambda i: (i // per_b, 0, 0)),
                  pl.BlockSpec((1, D), lambda i: (0, 0))],
        out_specs=pl.BlockSpec((n, D), lambda i: (i, 0)),
        out_shape=jax.ShapeDtypeStruct((T, D), F32),
        scratch_shapes=[pltpu.VMEM((2, TOP_K, n, 1, D), F32), pltpu.VMEM((n, D), F32),
                        pltpu.SemaphoreType.DMA((2,))],
        compiler_params=_cparams(("arbitrary",)),
        name="combine",
    )(dest, dest, y, wrow, x1, mod3, g_post_ffn.reshape(1, D))


def _regroup_w_in(w_in):
    D = w_in.shape[0]
    dw = DN_HEADS * HEAD_DIM
    mw = MB_HEADS * HEAD_DIM
    n_dn, n_small, n_mb = 4 * dw, 2 * DN_HEADS, 3 * mw
    dn, small, mb, gates = jnp.split(w_in, [n_dn, n_dn + n_small, n_dn + n_small + n_mb], axis=1)
    small = jnp.concatenate([small, jnp.zeros((D, LANES - n_small), w_in.dtype)], axis=1).astype(BF16)
    return jnp.concatenate([p.astype(BF16) for p in (gates, dn, mb)], axis=1), small


def kernel(x, c, w_ada, b_ada, g_pre_mix, g_post_mix, g_pre_ffn, g_post_ffn, w_in, conv_w, a_log, dt_bias,
           g_dn_out, w_br_a, w_br_b, w_o, w_router, b_router, w_gu, b_gu, w_down, b_down):
    B, S, D = x.shape
    l = 0
    mod = _adaln(c, w_ada[l], b_ada[l]).reshape(B, 6, D)
    proj, small = _inproj(x, mod, g_pre_mix[l], *_regroup_w_in(w_in[l]))
    qr, kr, vb, kmean = _mobaprep(proj, B, S)
    ob = _moba(qr, kr, vb, kmean, B, S)
    oa = _deltanet(proj, small, conv_w[l], a_log[l], dt_bias[l], g_dn_out[l], B, S)
    x1, hp, idx8, wrow, rank8, cnt = _merge(oa, ob, proj, x.reshape(B * S, D), mod, g_post_mix[l], g_pre_ffn[l],
                                            w_br_a[l], w_br_b[l], w_o[l], w_router[l], b_router[l], S)
    out = _moe(x1, hp, idx8, wrow, rank8, cnt, mod, g_post_ffn[l], w_gu[l], b_gu[l], w_down[l], b_down[l], S)
    return out.reshape(B, S, D)


def _moe(x1, hp, idx8, wrow, rank8, cnt, mod, g_post_ffn, w_gu, b_gu, w_down, b_down, S):
    T = x1.shape[0]
    counts = cnt[:, 0].astype(jnp.int32)
    start, blk_exp, blk_valid, blk_new, blk_fill = _block_tables(counts, T * TOP_K)
    hot = idx8[:TOP_K, :, None] == jnp.arange(N_EXPERTS, dtype=jnp.int32)
    dest = rank8[:TOP_K] + jnp.sum(jnp.where(hot, start, 0), axis=-1)
    xs = _dispatch(dest, blk_fill, hp)
    y = _experts(xs, blk_exp, blk_valid, blk_new, w_gu, b_gu, w_down, b_down)
    return _combine(dest, y, wrow, x1, mod, g_post_ffn, S)
```

```python
import functools

import jax
import jax.numpy as jnp
import numpy as np
from jax import lax
from jax.experimental import pallas as pl
from jax.experimental.pallas import tpu as pltpu

F32 = jnp.float32
BF16 = jnp.bfloat16
HIGHEST = lax.Precision.HIGHEST

HEAD_DIM = 128
DN_HEADS = 4
DN_CONV = 4
DN_CHUNK = 64
MB_HEADS = 4
MB_BLOCK = 256
MB_TOPK = 3
ROPE_THETA = 500000.0
ROPE_DIM = HEAD_DIM // 4
N_EXPERTS = 32
TOP_K = 4
SWIGLU_LIMIT = 7.0
SWIGLU_ALPHA = 1.702
NORM_EPS = 1e-6
LANES = 128
NEG_BIG = -1e30
MB_Q_SCALE = HEAD_DIM ** -0.5 * float(np.log2(np.e))

_W = DN_HEADS * HEAD_DIM // LANES
COL_GA = 0
COL_GB = 8
COL_DQ = 16
COL_DK = 20
COL_DV = 24
COL_DZ = 28
COL_MQ = 32
COL_MK = 36
COL_MV = 40
N_PROJ = 44 * LANES

VMEM_LIMIT = 56 * 1024 * 1024


def _cparams(sem):
    return pltpu.CompilerParams(dimension_semantics=sem, vmem_limit_bytes=VMEM_LIMIT)


def _sigmoid(v):
    return 0.5 * jnp.tanh(0.5 * v) + 0.5


def _silu(v):
    return v * _sigmoid(v)


def _adaln_kernel(c_ref, w_ref, b_ref, o_ref):
    a = _silu(c_ref[...])
    o_ref[...] = jnp.dot(a, w_ref[...], preferred_element_type=F32, precision=HIGHEST) + b_ref[...]


def _adaln(c, w_ada, b_ada):
    B, D = c.shape
    N = w_ada.shape[1]
    tn = D
    return pl.pallas_call(
        _adaln_kernel,
        grid=(N // tn,),
        in_specs=[pl.BlockSpec((B, D), lambda j: (0, 0)),
                  pl.BlockSpec((D, tn), lambda j: (0, j)),
                  pl.BlockSpec((1, tn), lambda j: (0, j))],
        out_specs=pl.BlockSpec((B, tn), lambda j: (0, j)),
        out_shape=jax.ShapeDtypeStruct((B, N), F32),
        compiler_params=_cparams(("arbitrary",)),
        name="adaln",
    )(c, w_ada, b_ada.reshape(1, N))


def _inproj_kernel(x_ref, mod_ref, g_ref, w_ref, ws_ref, o_ref, os_ref, h_scr):
    @pl.when(pl.program_id(2) == 0)
    def _():
        x = x_ref[0]
        y = x * lax.rsqrt(jnp.mean(x * x, axis=-1, keepdims=True) + NORM_EPS) * g_ref[...]
        h = y * (1.0 + mod_ref[0, 1:2, :]) + mod_ref[0, 0:1, :]
        h_scr[...] = h.astype(BF16)
        os_ref[...] = jnp.dot(h_scr[...], ws_ref[...], preferred_element_type=F32)

    o_ref[...] = jnp.dot(h_scr[...], w_ref[...], preferred_element_type=F32)


def _inproj(x, mod3, g_pre, w_main, w_small):
    B, S, D = x.shape
    tm = min(1024, S)
    tn = N_PROJ // 2
    nrow = S // tm
    return pl.pallas_call(
        _inproj_kernel,
        grid=(B, nrow, N_PROJ // tn),
        in_specs=[pl.BlockSpec((1, tm, D), lambda b, i, j: (b, i, 0)),
                  pl.BlockSpec((1, 6, D), lambda b, i, j: (b, 0, 0)),
                  pl.BlockSpec((1, D), lambda b, i, j: (0, 0)),
                  pl.BlockSpec((D, tn), lambda b, i, j: (0, j)),
                  pl.BlockSpec((D, LANES), lambda b, i, j: (0, 0))],
        out_specs=[pl.BlockSpec((tm, tn), lambda b, i, j: (b * nrow + i, j)),
                   pl.BlockSpec((tm, LANES), lambda b, i, j: (b * nrow + i, 0))],
        out_shape=[jax.ShapeDtypeStruct((B * S, N_PROJ), F32),
                   jax.ShapeDtypeStruct((B * S, LANES), F32)],
        scratch_shapes=[pltpu.VMEM((tm, D), BF16)],
        compiler_params=_cparams(("arbitrary", "arbitrary", "arbitrary")),
        name="inproj",
    )(x, mod3, g_pre.reshape(1, D), w_main, w_small)


def _rope(v, cosf, sinf, lane):
    rot = jnp.where(lane < ROPE_DIM // 2, pltpu.roll(v, LANES - ROPE_DIM // 2, 1), pltpu.roll(v, ROPE_DIM // 2, 1))
    return v * cosf + rot * sinf


def _mobaprep_kernel(q_ref, k_ref, v_ref, cos_ref, sin_ref, qo_ref, ko_ref, vo_ref, km_ref):
    cosf = cos_ref[...]
    sinf = sin_ref[...]
    lane = lax.broadcasted_iota(jnp.int32, cosf.shape, 1)
    for h in range(MB_HEADS):
        sl = slice(h * HEAD_DIM, (h + 1) * HEAD_DIM)
        qo_ref[:, sl] = (_rope(q_ref[:, sl], cosf, sinf, lane) * MB_Q_SCALE).astype(BF16)
        kr = _rope(k_ref[:, sl], cosf, sinf, lane)
        ko_ref[:, sl] = kr.astype(BF16)
        for blk in range(km_ref.shape[0]):
            km_ref[blk, :, sl] = jnp.mean(kr[blk * MB_BLOCK:(blk + 1) * MB_BLOCK], axis=0, keepdims=True)
    vo_ref[...] = v_ref[...].astype(BF16)


def _rope_tables(S):
    half = ROPE_DIM // 2
    inv_freq = ROPE_THETA ** (-jnp.arange(half, dtype=F32) / half)
    ang = jnp.arange(S, dtype=F32)[:, None] * inv_freq[None, :]
    cos, sin = jnp.cos(ang), jnp.sin(ang)
    rest = HEAD_DIM - ROPE_DIM
    cosf = jnp.concatenate([cos, cos, jnp.ones((S, rest), F32)], axis=1)
    sinf = jnp.concatenate([-sin, sin, jnp.zeros((S, rest), F32)], axis=1)
    return cosf, sinf


def _mobaprep(proj, B, S):
    T = B * S
    tb = min(8 * MB_BLOCK, S)
    nb = S // tb
    W = MB_HEADS * HEAD_DIM
    cosf, sinf = _rope_tables(S)
    col = lambda c: (lambda i: (i, c))
    return pl.pallas_call(
        _mobaprep_kernel,
        grid=(T // tb,),
        in_specs=[pl.BlockSpec((tb, W), col(COL_MQ // _W)),
                  pl.BlockSpec((tb, W), col(COL_MK // _W)),
                  pl.BlockSpec((tb, W), col(COL_MV // _W)),
                  pl.BlockSpec((tb, LANES), lambda i: (i % nb, 0)),
                  pl.BlockSpec((tb, LANES), lambda i: (i % nb, 0))],
        out_specs=[pl.BlockSpec((tb, W), lambda i: (i, 0)),
                   pl.BlockSpec((tb, W), lambda i: (i, 0)),
                   pl.BlockSpec((tb, W), lambda i: (i, 0)),
                   pl.BlockSpec((tb // MB_BLOCK, 1, W), lambda i: (i, 0, 0))],
        out_shape=[jax.ShapeDtypeStruct((T, W), BF16),
                   jax.ShapeDtypeStruct((T, W), BF16),
                   jax.ShapeDtypeStruct((T, W), BF16),
                   jax.ShapeDtypeStruct((T // MB_BLOCK, 1, W), F32)],
        compiler_params=_cparams(("arbitrary",)),
        name="mobaprep",
    )(proj, proj, proj, cosf, sinf)


MB_HPS = 4


def _moba_kernel(q_ref, k_ref, v_ref, km_ref, o_ref, s_scr, sd_scr, mx_scr, l_scr, acc_scr):
    qi = pl.program_id(1)
    tb = MB_BLOCK
    nb = km_ref.shape[0]
    nt = tb // LANES
    D = HEAD_DIM
    heads = range(MB_HPS)
    hs = [slice(h * D, (h + 1) * D) for h in heads]
    qs = [q_ref[:, hs[h]] for h in heads]

    kms = [jnp.concatenate(_split(km_ref[:, hs[h]]), axis=1) for h in heads]
    sts = [lax.dot_general(kms[h], jnp.concatenate([qs[h], qs[h]], axis=1), (((1,), (1,)), ((), ())),
                           preferred_element_type=F32) for h in heads]
    blk = lax.broadcasted_iota(jnp.int32, (nb, tb), 0)
    rowid = lax.broadcasted_iota(jnp.int32, (LANES, tb), 0)
    sel_ts = []
    for h in heads:
        st = jnp.where(blk < qi, sts[h], -jnp.inf)
        sel_t = jnp.full((LANES, tb), -1.0, F32)
        for r in range(MB_TOPK):
            m = jnp.max(st, axis=0, keepdims=True)
            idx = jnp.min(jnp.where(st == m, blk, nb), axis=0, keepdims=True)
            sel_t = jnp.where(rowid == r, jnp.where(r < qi, idx, -1).astype(F32), sel_t)
            st = jnp.where(blk == idx, -jnp.inf, st)
        sel_ts.append(sel_t)
    sels = [jnp.transpose(t) for t in sel_ts]
    sel_rep = [[jnp.broadcast_to(sels[h][:, r:r + 1], (tb, LANES)) for r in range(MB_TOPK)] for h in heads]

    def logits(h, start, width):
        kslab = k_ref[pl.ds(pl.multiple_of(start, tb), width), hs[h]]
        return lax.dot_general(qs[h], kslab, (((1,), (1,)), ((), ())), preferred_element_type=F32)

    r_i = lax.broadcasted_iota(jnp.int32, (tb, LANES), 0)
    c_i = lax.broadcasted_iota(jnp.int32, (tb, LANES), 1)
    sds = [logits(h, qi * tb, tb) for h in heads]
    for h in heads:
        mx = jnp.full((tb, LANES), NEG_BIG, F32)
        for t in range(nt):
            piece = jnp.where(c_i + t * LANES <= r_i, sds[h][:, t * LANES:(t + 1) * LANES], NEG_BIG)
            sd_scr[h, :, t * LANES:(t + 1) * LANES] = piece
            mx = jnp.maximum(mx, piece)
        mx_scr[h] = mx

    n_pairs = (qi + 1) // 2

    def pass1(pairs):
        s2s = [[logits(h, 2 * i * tb, 2 * tb) for h in heads] for i in pairs]
        for h in heads:
            mx = mx_scr[h]
            for n, i in enumerate(pairs):
                for half in range(2):
                    kbf = (2 * i + half).astype(F32)
                    hit = (sel_rep[h][0] == kbf) | (sel_rep[h][1] == kbf) | (sel_rep[h][2] == kbf)
                    for t in range(nt):
                        c0 = (half * nt + t) * LANES
                        piece = jnp.where(hit, s2s[n][h][:, c0:c0 + LANES], NEG_BIG)
                        s_scr[h, i, :, c0:c0 + LANES] = piece
                        mx = jnp.maximum(mx, piece)
            mx_scr[h] = mx

    def run_pairs(fn):
        def body(j, carry):
            fn([4 * j + u for u in range(4)])
            return carry
        lax.fori_loop(0, n_pairs // 4, body, 0)
        rem = n_pairs % 4
        pl.when(rem >= 2)(lambda: fn([n_pairs - rem, n_pairs - rem + 1]))
        pl.when(rem % 2 == 1)(lambda: fn([n_pairs - 1]))

    run_pairs(pass1)
    m_reps = [jnp.broadcast_to(jnp.max(mx_scr[h], axis=-1, keepdims=True), (tb, LANES)) for h in heads]

    def probs(load, width, m_rep):
        ps, lsum = [], jnp.zeros((tb, LANES), F32)
        for t in range(width // LANES):
            p = jnp.exp2(load(t) - m_rep)
            lsum = lsum + p
            ps.append(p.astype(BF16))
        return jnp.concatenate(ps, axis=1), lsum

    pls = [probs(lambda t, h=h: sd_scr[h, :, t * LANES:(t + 1) * LANES], tb, m_reps[h]) for h in heads]
    for h in heads:
        l_scr[h] = pls[h][1]
        acc_scr[h] = jnp.dot(pls[h][0], v_ref[pl.ds(pl.multiple_of(qi * tb, tb), tb), hs[h]],
                             preferred_element_type=F32)

    def pass2(pairs):
        pls = [[probs(lambda t, h=h, i=i: s_scr[h, i, :, t * LANES:(t + 1) * LANES], 2 * tb, m_reps[h])
                for h in heads] for i in pairs]
        for h in heads:
            lsum, acc = l_scr[h], acc_scr[h]
            for n, i in enumerate(pairs):
                vslab = v_ref[pl.ds(pl.multiple_of(2 * i * tb, tb), 2 * tb), hs[h]]
                lsum = lsum + pls[n][h][1]
                acc = acc + jnp.dot(pls[n][h][0], vslab, preferred_element_type=F32)
            l_scr[h], acc_scr[h] = lsum, acc

    run_pairs(pass2)
    for h in heads:
        o_ref[:, hs[h]] = (acc_scr[h] / jnp.sum(l_scr[h], axis=-1, keepdims=True)).astype(o_ref.dtype)


def _moba(qr, kr, vb, kmean, B, S):
    tb = MB_BLOCK
    nb = S // tb
    T = B * S
    km = kmean.reshape(B, nb, MB_HEADS * HEAD_DIM)
    HW = MB_HPS * HEAD_DIM
    ng = MB_HEADS // MB_HPS
    return pl.pallas_call(
        _moba_kernel,
        grid=(B * ng, nb),
        in_specs=[pl.BlockSpec((tb, HW), lambda g, i: ((g // ng) * nb + i, g % ng)),
                  pl.BlockSpec((S, HW), lambda g, i: (g // ng, g % ng)),
                  pl.BlockSpec((S, HW), lambda g, i: (g // ng, g % ng)),
                  pl.BlockSpec((None, nb, HW), lambda g, i: (g // ng, 0, g % ng))],
        out_specs=pl.BlockSpec((tb, HW), lambda g, i: ((g // ng) * nb + i, g % ng)),
        out_shape=jax.ShapeDtypeStruct((T, MB_HEADS * HEAD_DIM), BF16),
        scratch_shapes=[pltpu.VMEM((MB_HPS, (nb + 1) // 2, tb, 2 * tb), F32),
                        pltpu.VMEM((MB_HPS, tb, tb), F32),
                        pltpu.VMEM((MB_HPS, tb, LANES), F32),
                        pltpu.VMEM((MB_HPS, tb, LANES), F32),
                        pltpu.VMEM((MB_HPS, tb, HEAD_DIM), F32)],
        compiler_params=_cparams(("arbitrary", "arbitrary")),
        name="moba",
    )(qr, kr, vb, km)


DN_TILE_CHUNKS = 8


def _softplus(v):
    return jnp.maximum(v, 0.0) + jnp.log1p(jnp.exp(-jnp.abs(v)))


DN_GROUP = 2
DN_HPS = 4


def _split(a):
    hi = pltpu.bitcast(pltpu.bitcast(a, jnp.uint32) & jnp.uint32(0xFFFF0000), F32)
    return hi.astype(BF16), (a - hi).astype(BF16)


def _dot3(ah, al, bh, bl):
    lhs = jnp.concatenate([ah, ah, al], axis=1)
    rhs = jnp.concatenate([bh, bl, bh], axis=0)
    return jnp.dot(lhs, rhs, preferred_element_type=F32)


def _dot_bf(a, b):
    return jnp.dot(a.astype(BF16), b.astype(BF16), preferred_element_type=F32)


def _deltanet_kernel(q_ref, k_ref, v_ref, z_ref, sm_ref, wq_ref, wk_ref, wv_ref, alog_ref, dtb_ref, gout_ref,
                     o_ref, xp_scr, state_scr):
    i = pl.program_id(2)
    C = DN_CHUNK
    TR = q_ref.shape[0]
    HALO = 8

    @pl.when(i == 0)
    def _():
        xp_scr[...] = jnp.zeros(xp_scr.shape, F32)
        state_scr[...] = jnp.zeros(state_scr.shape, F32)

    def conv_silu(slot, x_ref, w_ref, sl):
        xp_scr[slot, 0:HALO, sl] = xp_scr[slot, TR:TR + HALO, sl]
        xp_scr[slot, HALO:HALO + TR, sl] = x_ref[:, sl]
        acc = w_ref[DN_CONV - 1:DN_CONV, sl] * xp_scr[slot, HALO:HALO + TR, sl]
        for j in range(1, DN_CONV):
            acc = acc + w_ref[DN_CONV - 1 - j:DN_CONV - j, sl] * xp_scr[slot, HALO - j:HALO - j + TR, sl]
        return _silu(acc)

    sm = sm_ref[...]
    lane = lax.broadcasted_iota(jnp.int32, sm.shape, 1)
    row = lax.broadcasted_iota(jnp.int32, sm.shape, 0)
    beta_all = _sigmoid(sm)
    g_all = -jnp.exp(alog_ref[...]) * _softplus(sm + dtb_ref[...])
    pos = row % C
    gc_all = g_all
    shift = 1
    while shift < C:
        gc_all = gc_all + jnp.where(pos >= shift, pltpu.roll(gc_all, shift, 0), 0.0)
        shift *= 2
    gc_t = jnp.transpose(gc_all)
    row_t = lax.broadcasted_iota(jnp.int32, gc_t.shape, 0)

    G = DN_GROUP * C
    ri = lax.broadcasted_iota(jnp.int32, (G, G), 0)
    ci = lax.broadcasted_iota(jnp.int32, (G, G), 1)
    same = (ri // C) == (ci // C)
    tril = same & (ci <= ri)
    strict = same & (ci < ri)
    eye = (ci == ri).astype(F32)
    D = HEAD_DIM
    gout = gout_ref[...]

    def head_pipeline(hh):
        h = pl.program_id(1) * DN_HPS + hh
        sl = slice(hh * D, (hh + 1) * D)
        q = conv_silu(0, q_ref, wq_ref, sl)
        k = conv_silu(1, k_ref, wk_ref, sl)
        yield
        v = conv_silu(2, v_ref, wv_ref, sl)
        q = q * lax.rsqrt(jnp.sum(q * q, axis=-1, keepdims=True) + 1e-6) * (D ** -0.5)
        k = k * lax.rsqrt(jnp.sum(k * k, axis=-1, keepdims=True) + 1e-6)
        yield
        beta = jnp.sum(jnp.where(lane == h, beta_all, 0.0), axis=1, keepdims=True)
        gc = jnp.sum(jnp.where(lane == DN_HEADS + h, gc_all, 0.0), axis=1, keepdims=True)
        gc_row = jnp.sum(jnp.where(row_t == DN_HEADS + h, gc_t, 0.0), axis=0, keepdims=True)
        head = (q, k, v, beta, gc, gc_row, jnp.transpose(k))
        yield
        steps = [None] * (TR // C)
        yield from _round_robin([_deltanet_group(head, g, tril, strict, eye, steps) for g in range(TR // G)])
        outs = []
        yield from _deltanet_chain(hh, steps, state_scr, outs)
        o = jnp.concatenate(outs, axis=0)
        y = o * lax.rsqrt(jnp.mean(o * o, axis=-1, keepdims=True) + NORM_EPS) * gout
        o_ref[:, sl] = (y * _silu(z_ref[:, sl])).astype(o_ref.dtype)

    for _ in _round_robin([head_pipeline(hh) for hh in range(DN_HPS)]):
        pass


def _round_robin(gens):
    active = list(gens)
    while active:
        still = []
        for gen in active:
            try:
                next(gen)
                still.append(gen)
            except StopIteration:
                pass
        active = still
        yield


def _deltanet_group(head, g, tril, strict, eye, steps_out):
    C = DN_CHUNK
    D = HEAD_DIM
    G = DN_GROUP * C
    q, k, v, beta, gc, gc_row, k_t = head
    r0 = g * G
    qg, kg, vg = q[r0:r0 + G], k[r0:r0 + G], v[r0:r0 + G]
    bg = beta[r0:r0 + G]
    gcg = gc[r0:r0 + G]
    gcr = gc_row[:, r0:r0 + G]
    ktg = k_t[:, r0:r0 + G]
    decay = jnp.where(tril, jnp.exp(jnp.where(tril, gcg - gcr, 0.0)), 0.0)
    kb = kg * bg
    aq = _dot_bf(jnp.concatenate([kb, qg], axis=0), ktg)
    yield
    m_neg = jnp.where(strict, -(aq[:G] * decay), 0.0)
    qk = (aq[G:] * decay).astype(BF16)
    t_inv = eye + m_neg
    ph, pl_ = _split(m_neg)
    for _ in range(5):
        th, tl = _split(t_inv)
        ph, pl_ = _split(_dot3(ph, pl_, ph, pl_))
        yield
        t_inv = t_inv + _dot3(th, tl, ph, pl_)
        yield
    egc = jnp.exp(gcg)
    th, tl = _split(t_inv)
    rh, rl = _split(jnp.concatenate([kb * egc, vg * bg], axis=1))
    wu = _dot3(th, tl, rh, rl).astype(BF16)
    yield
    qr = jnp.dot(qk, wu, preferred_element_type=F32)
    qp = qg * egc - qr[:, :D]
    r_all = qr[:, D:]
    yield
    for c in range(DN_GROUP):
        c0 = c * C
        g_last = gcg[c0 + C - 1:c0 + C, :]
        kt_tail = ktg[:, c0:c0 + C] * jnp.exp(g_last - gcr[:, c0:c0 + C])
        gh = jnp.dot(kt_tail.astype(BF16), wu[c0:c0 + C, :], preferred_element_type=F32)
        lhs = jnp.concatenate([gh[:, :D], qp[c0:c0 + C]], axis=0).astype(BF16)
        steps_out[g * DN_GROUP + c] = (lhs, gh[:, D:], r_all[c0:c0 + C], jnp.exp(g_last))
        yield


def _deltanet_chain(hh, steps, state_scr, outs):
    D = HEAD_DIM
    state = state_scr[hh]
    for lhs, h_add, r_add, dec in steps:
        res = jnp.dot(lhs, state.astype(BF16), preferred_element_type=F32)
        outs.append(res[D:] + r_add)
        state = state * dec - res[:D] + h_add
        yield
    state_scr[hh] = state


def _deltanet(proj, small, conv_w, a_log, dt_bias, g_dn_out, B, S):
    T = B * S
    TR = min(DN_TILE_CHUNKS * DN_CHUNK, S)
    nt = S // TR
    pad = jnp.zeros((DN_HEADS,), F32)
    rest = jnp.zeros((LANES - 2 * DN_HEADS,), F32)
    alog_lane = jnp.concatenate([pad, a_log.astype(F32), rest]).reshape(1, LANES)
    dtb_lane = jnp.concatenate([pad, dt_bias.astype(F32), rest]).reshape(1, LANES)
    HW = DN_HPS * HEAD_DIM
    per = HW // LANES
    rows = lambda c0: (lambda b, h, i: (b * nt + i, c0 // per + h))
    wcol = lambda c0: (lambda b, h, i: (0, c0 // per + h))
    const = lambda b, h, i: (0, 0)
    return pl.pallas_call(
        _deltanet_kernel,
        grid=(B, DN_HEADS // DN_HPS, nt),
        in_specs=[pl.BlockSpec((TR, HW), rows(COL_DQ)),
                  pl.BlockSpec((TR, HW), rows(COL_DK)),
                  pl.BlockSpec((TR, HW), rows(COL_DV)),
                  pl.BlockSpec((TR, HW), rows(COL_DZ)),
                  pl.BlockSpec((TR, LANES), lambda b, h, i: (b * nt + i, 0)),
                  pl.BlockSpec((DN_CONV, HW), wcol(0)),
                  pl.BlockSpec((DN_CONV, HW), wcol(DN_HEADS)),
                  pl.BlockSpec((DN_CONV, HW), wcol(2 * DN_HEADS)),
                  pl.BlockSpec((1, LANES), const),
                  pl.BlockSpec((1, LANES), const),
                  pl.BlockSpec((1, HEAD_DIM), const)],
        out_specs=pl.BlockSpec((TR, HW), lambda b, h, i: (b * nt + i, h)),
        out_shape=jax.ShapeDtypeStruct((T, DN_HEADS * HEAD_DIM), BF16),
        scratch_shapes=[pltpu.VMEM((3, TR + 8, HW), F32), pltpu.VMEM((DN_HPS, HEAD_DIM, HEAD_DIM), F32)],
        compiler_params=_cparams(("arbitrary", "arbitrary", "arbitrary")),
        name="deltanet",
    )(proj, proj, proj, proj, small, conv_w, conv_w, conv_w, alog_lane, dtb_lane, g_dn_out.reshape(1, HEAD_DIM))


MERGE_PARTS = 4

def _rms(v):
    return v * lax.rsqrt(jnp.mean(v * v, axis=-1, keepdims=True) + NORM_EPS)


def _merge_kernel(oa_ref, ob_ref, ga_ref, gb_ref, x_ref, mod_ref, gpost_ref, gpre_ref, wa_ref, wb_ref, wo_ref,
                  wrt_ref, br_ref, x1_ref, hp_ref, idx_ref, wrow_ref, rank_ref, cnt_ref, carry_scr):
    E = N_EXPERTS
    tm = x_ref.shape[0]
    half = x_ref.shape[1] // 2

    @pl.when(pl.program_id(0) == 0)
    def _():
        carry_scr[...] = jnp.zeros(carry_scr.shape, F32)

    logits = [None] * MERGE_PARTS
    wrh, wrl = _split(wrt_ref[...])
    wr3 = jnp.concatenate([wrh, wrh, wrl], axis=1)

    def rows_pipeline(part):
        n = tm // MERGE_PARTS
        rs = slice(part * n, (part + 1) * n)
        ya = jnp.dot(oa_ref[rs, :], wa_ref[...], preferred_element_type=F32)
        yb = jnp.dot(ob_ref[rs, :], wb_ref[...], preferred_element_type=F32)
        yield
        merged = _sigmoid(ga_ref[rs, :]) * ya + _sigmoid(gb_ref[rs, :]) * yb
        yield
        mix = jnp.dot(merged.astype(BF16), wo_ref[...], preferred_element_type=F32)
        yield
        x1 = x_ref[rs, :] + mod_ref[0, 2:3, :] * (_rms(mix) * gpost_ref[...])
        x1_ref[rs, :] = x1
        h2 = (_rms(x1) * gpre_ref[...]) * (1.0 + mod_ref[0, 4:5, :]) + mod_ref[0, 3:4, :]
        lo_bits = pltpu.bitcast(h2[:, :half].astype(BF16).astype(F32), jnp.uint32) >> 16
        hi_bits = pltpu.bitcast(h2[:, half:].astype(BF16).astype(F32), jnp.uint32) & jnp.uint32(0xFFFF0000)
        hp_ref[rs, :] = hi_bits | lo_bits
        yield
        hh, hl = _split(h2)
        logits[part] = lax.dot_general(wr3, jnp.concatenate([hh, hl, hh], axis=1), (((1,), (1,)), ((), ())),
                                       preferred_element_type=F32)

    for _ in _round_robin([rows_pipeline(part) for part in range(MERGE_PARTS)]):
        pass
    lt = jnp.concatenate(logits, axis=1) + br_ref[...]
    eid = lax.broadcasted_iota(jnp.int32, (E, tm), 0)
    vals, idxs = [], []
    for _ in range(TOP_K):
        m = jnp.max(lt, axis=0, keepdims=True)
        idx = jnp.min(jnp.where(lt == m, eid, E), axis=0, keepdims=True)
        vals.append(m)
        idxs.append(idx)
        lt = jnp.where(eid == idx, -jnp.inf, lt)
    exps = [jnp.exp(v - vals[0]) for v in vals]
    den = exps[0] + exps[1] + exps[2] + exps[3]
    wts = [e / den for e in exps]

    hot = jnp.zeros((E, tm), F32)
    for idx in idxs:
        hot = hot + (eid == idx).astype(F32)
    ti = lax.broadcasted_iota(jnp.int32, (tm, tm), 0)
    tj = lax.broadcasted_iota(jnp.int32, (tm, tm), 1)
    before = (ti < tj).astype(BF16)
    prior = carry_scr[...][:, 0:1] + jnp.dot(hot.astype(BF16), before, preferred_element_type=F32)
    row8 = lax.broadcasted_iota(jnp.int32, (8, tm), 0)
    row128 = lax.broadcasted_iota(jnp.int32, (LANES, tm), 0)
    idx8 = jnp.zeros((8, tm), jnp.int32)
    rank8 = jnp.zeros((8, tm), jnp.int32)
    w128 = jnp.zeros((LANES, tm), F32)
    for r in range(TOP_K):
        rank_r = jnp.sum(jnp.where(eid == idxs[r], prior, 0.0), axis=0, keepdims=True)
        idx8 = jnp.where(row8 == r, idxs[r], idx8)
        rank8 = jnp.where(row8 == r, rank_r.astype(jnp.int32), rank8)
        w128 = jnp.where(row128 == r, wts[r], w128)
    idx_ref[...] = idx8
    rank_ref[...] = rank8
    wrow_ref[...] = jnp.transpose(w128)
    carry = carry_scr[...] + jnp.sum(hot, axis=1, keepdims=True)
    carry_scr[...] = carry
    cnt_ref[...] = carry


def _merge(oa, ob, proj, x2, mod3, g_post_mix, g_pre_ffn, w_br_a, w_br_b, w_o, w_router, b_router, S):
    T, D = x2.shape
    E = N_EXPERTS
    tm = min(512, S)
    per_b = S // tm
    W = DN_HEADS * HEAD_DIM
    row = lambda i: (i, 0)
    const = lambda i: (0, 0)
    lane_t = lambda i: (0, i)
    return pl.pallas_call(
        _merge_kernel,
        grid=(T // tm,),
        in_specs=[pl.BlockSpec((tm, W), row),
                  pl.BlockSpec((tm, W), row),
                  pl.BlockSpec((tm, D), lambda i: (i, COL_GA * LANES // D)),
                  pl.BlockSpec((tm, D), lambda i: (i, COL_GB * LANES // D)),
                  pl.BlockSpec((tm, D), row),
                  pl.BlockSpec((1, 6, D), lambda i: (i // per_b, 0, 0)),
                  pl.BlockSpec((1, D), const),
                  pl.BlockSpec((1, D), const),
                  pl.BlockSpec((W, D), const),
                  pl.BlockSpec((W, D), const),
                  pl.BlockSpec((D, D), const),
                  pl.BlockSpec((E, D), const),
                  pl.BlockSpec((E, 1), const)],
        out_specs=[pl.BlockSpec((tm, D), row),
                   pl.BlockSpec((tm, D // 2), row),
                   pl.BlockSpec((8, tm), lane_t),
                   pl.BlockSpec((tm, LANES), row),
                   pl.BlockSpec((8, tm), lane_t),
                   pl.BlockSpec((E, LANES), const)],
        out_shape=[jax.ShapeDtypeStruct((T, D), F32),
                   jax.ShapeDtypeStruct((T, D // 2), jnp.uint32),
                   jax.ShapeDtypeStruct((8, T), jnp.int32),
                   jax.ShapeDtypeStruct((T, LANES), F32),
                   jax.ShapeDtypeStruct((8, T), jnp.int32),
                   jax.ShapeDtypeStruct((E, LANES), F32)],
        scratch_shapes=[pltpu.VMEM((E, LANES), F32)],
        compiler_params=_cparams(("arbitrary",)),
        name="merge",
    )(oa, ob, proj, proj, x2, mod3, g_post_mix.reshape(1, D), g_pre_ffn.reshape(1, D),
      w_br_a.astype(BF16), w_br_b.astype(BF16), w_o.astype(BF16),
      jnp.transpose(w_router).astype(F32), b_router.reshape(E, 1).astype(F32))


DISPATCH_TOKENS = 1024


def _dispatch_kernel(fill_ref, dest_ref, src_ref, dst_ref, zero_scr, sem, zsem):
    n = dest_ref.shape[1]
    tm = zero_scr.shape[0]

    @pl.when(pl.program_id(0) == 0)
    def _():
        zero_scr[...] = jnp.zeros(zero_scr.shape, zero_scr.dtype)

        def fill_copy(k):
            return pltpu.make_async_copy(zero_scr, dst_ref.at[pl.ds(pl.multiple_of(k * tm, tm), tm)], zsem)

        def start(k, carry):
            pl.when(fill_ref[k] == 1)(lambda: fill_copy(k).start())
            return carry

        def wait(k, carry):
            pl.when(fill_ref[k] == 1)(lambda: fill_copy(k).wait())
            return carry

        lax.fori_loop(0, fill_ref.shape[0], start, 0)
        lax.fori_loop(0, fill_ref.shape[0], wait, 0)

    def issue(j, carry):
        for r in range(TOP_K):
            pltpu.make_async_copy(src_ref.at[pl.ds(j, 1)], dst_ref.at[pl.ds(dest_ref[r, j], 1)], sem).start()
        return carry

    lax.fori_loop(0, n, issue, 0, unroll=8)
    for r in range(TOP_K):
        pltpu.make_async_copy(src_ref, dst_ref.at[pl.ds(0, n)], sem).wait()


def _dispatch(dest, blk_fill, hp):
    T, Wd = hp.shape
    n = min(DISPATCH_TOKENS, T)
    tm = EXPERT_ROWS
    grid_spec = pltpu.PrefetchScalarGridSpec(
        num_scalar_prefetch=1,
        grid=(T // n,),
        in_specs=[pl.BlockSpec((TOP_K, n), lambda i, fill: (0, i), memory_space=pltpu.SMEM),
                  pl.BlockSpec((n, Wd), lambda i, fill: (i, 0))],
        out_specs=pl.BlockSpec(memory_space=pl.ANY),
        scratch_shapes=[pltpu.VMEM((tm, Wd), hp.dtype), pltpu.SemaphoreType.DMA(()), pltpu.SemaphoreType.DMA(())],
    )
    return pl.pallas_call(
        _dispatch_kernel,
        grid_spec=grid_spec,
        out_shape=jax.ShapeDtypeStruct((_expert_blocks(T * TOP_K) * tm, Wd), hp.dtype),
        compiler_params=_cparams(("arbitrary",)),
        name="dispatch",
    )(blk_fill, dest, hp)


EXPERT_ROWS = 704
EXPERT_FEATURE_TILE = 512


def _experts_kernel(exp_ref, valid_ref, new_ref, x_ref, wgu_ref, bgu_ref, wdn_ref, bdn_ref, o_ref, wgu_bf, wdn_bf):
    k = pl.program_id(0)
    F = wdn_ref.shape[0]

    @pl.when(new_ref[k] == 1)
    def _():
        wgu_bf[...] = wgu_ref[...].astype(BF16)
        wdn_bf[...] = wdn_ref[...].astype(BF16)

    def ffn():
        word = x_ref[...]
        x = jnp.concatenate([pltpu.bitcast(word << 16, F32).astype(BF16),
                             pltpu.bitcast(word & jnp.uint32(0xFFFF0000), F32).astype(BF16)], axis=1)

        def gate_up(c):
            ft = slice(c, c + EXPERT_FEATURE_TILE)
            ut = slice(F + c, F + c + EXPERT_FEATURE_TILE)
            return (jnp.dot(x, wgu_bf[:, ft], preferred_element_type=F32) + bgu_ref[:, ft],
                    jnp.dot(x, wgu_bf[:, ut], preferred_element_type=F32) + bgu_ref[:, ut])

        tiles = list(range(0, F, EXPERT_FEATURE_TILE))
        y = None
        nxt = gate_up(tiles[0])
        for t, c in enumerate(tiles):
            gate, up = nxt
            if t + 1 < len(tiles):
                nxt = gate_up(tiles[t + 1])
            gate = jnp.minimum(gate, SWIGLU_LIMIT)
            up = jnp.clip(up, -SWIGLU_LIMIT, SWIGLU_LIMIT)
            act = (up + 1.0) * gate * _sigmoid(SWIGLU_ALPHA * gate)
            part = jnp.dot(act.astype(BF16), wdn_bf[c:c + EXPERT_FEATURE_TILE, :], preferred_element_type=F32)
            y = part + bdn_ref[...] if y is None else y + part
        return y

    @pl.when(valid_ref[k] == 1)
    def _():
        o_ref[:, 0, :] = ffn()

    @pl.when(valid_ref[k] == 0)
    def _():
        o_ref[...] = jnp.zeros(o_ref.shape, o_ref.dtype)


def _experts(xs, blk_exp, blk_valid, blk_new, w_gu, b_gu, w_down, b_down):
    R, half = xs.shape
    E, D, F2 = w_gu.shape
    F = F2 // 2
    tm = EXPERT_ROWS
    grid_spec = pltpu.PrefetchScalarGridSpec(
        num_scalar_prefetch=3,
        grid=(R // tm,),
        in_specs=[pl.BlockSpec((tm, half), lambda k, ex, va, nw: (k, 0)),
                  pl.BlockSpec((None, D, F2), lambda k, ex, va, nw: (ex[k], 0, 0)),
                  pl.BlockSpec((None, 1, F2), lambda k, ex, va, nw: (ex[k], 0, 0)),
                  pl.BlockSpec((None, F, D), lambda k, ex, va, nw: (ex[k], 0, 0)),
                  pl.BlockSpec((None, 1, D), lambda k, ex, va, nw: (ex[k], 0, 0))],
        out_specs=pl.BlockSpec((tm, 1, D), lambda k, ex, va, nw: (k, 0, 0)),
        scratch_shapes=[pltpu.VMEM((D, F2), BF16), pltpu.VMEM((F, D), BF16)],
    )
    return pl.pallas_call(
        _experts_kernel,
        grid_spec=grid_spec,
        out_shape=jax.ShapeDtypeStruct((R, 1, D), F32),
        compiler_params=_cparams(("arbitrary",)),
        name="experts",
    )(blk_exp, blk_valid, blk_new, xs, w_gu, b_gu.reshape(E, 1, F2).astype(F32),
      w_down, b_down.reshape(E, 1, D).astype(F32))


def _expert_blocks(A):
    return A // EXPERT_ROWS + N_EXPERTS


def _block_tables(counts, A):
    E = N_EXPERTS
    tm = EXPERT_ROWS
    n_blk = _expert_blocks(A)
    nblk_e = (counts + tm - 1) // tm
    blk_end = jnp.cumsum(nblk_e)
    blk_start = blk_end - nblk_e
    used = blk_end[-1]
    k = jnp.arange(n_blk, dtype=jnp.int32)
    kk = jnp.minimum(k, used - 1)
    e = jnp.sum((blk_end[None, :] <= kk[:, None]).astype(jnp.int32), axis=1)
    hot = e[:, None] == jnp.arange(E, dtype=jnp.int32)[None, :]
    pick = lambda table: jnp.sum(jnp.where(hot, table[None, :], 0), axis=1)
    valid = k < used
    new = valid & (k == pick(blk_start))
    fill = jnp.logical_not(valid) | (k == pick(blk_end) - 1)
    i32 = lambda v: v.astype(jnp.int32)
    return blk_start * tm, i32(e), i32(valid), i32(new), i32(fill)


COMBINE_TOKENS = 512


def _combine_kernel(dest_ref, nxt_ref, y_ref, wrow_ref, x1_ref, mod_ref, gpost_ref, o_ref, stage, moe_scr, sem):
    i = pl.program_id(0)
    last = pl.num_programs(0) - 1
    n = x1_ref.shape[0]
    SUB = 8

    def gather(idx_ref, j, s):
        for r in range(TOP_K):
            pltpu.make_async_copy(y_ref.at[idx_ref[r, j]], stage.at[s, r, j], sem.at[s]).start(priority=r % 2)

    @pl.when(i == 0)
    def _():
        def prime(j, carry):
            gather(dest_ref, j, 0)
            return carry
        lax.fori_loop(0, n, prime, 0, unroll=SUB)

    def step(slot):
        for r in range(TOP_K):
            pltpu.make_async_copy(y_ref.at[pl.ds(0, n)], stage.at[slot, r], sem.at[slot]).wait()

        def reduce_group(g):
            rows = pl.ds(pl.multiple_of(g * SUB, SUB), SUB)
            w = wrow_ref[rows, :]
            acc = w[:, 0:1] * stage[slot, 0, rows, 0, :]
            for r in range(1, TOP_K):
                acc = acc + w[:, r:r + 1] * stage[slot, r, rows, 0, :]
            moe_scr[rows, :] = acc

        @pl.when(i < last)
        def _():
            def body(g, carry):
                for t in range(SUB):
                    gather(nxt_ref, g * SUB + t, 1 - slot)
                reduce_group(g)
                return carry
            lax.fori_loop(0, n // SUB, body, 0)

        @pl.when(i == last)
        def _():
            def body(g, carry):
                reduce_group(g)
                return carry
            lax.fori_loop(0, n // SUB, body, 0)

    for parity in range(2):
        pl.when(i % 2 == parity)(functools.partial(step, parity))

    o_ref[...] = x1_ref[...] + mod_ref[0, 5:6, :] * (_rms(moe_scr[...]) * gpost_ref[...])


def _combine(dest, y, wrow, x1, mod3, g_post_ffn, S):
    T, D = x1.shape
    n = min(COMBINE_TOKENS, S)
    per_b = S // n
    steps = T // n
    return pl.pallas_call(
        _combine_kernel,
        grid=(steps,),
        in_specs=[pl.BlockSpec((TOP_K, n), lambda i: (0, i), memory_space=pltpu.SMEM),
                  pl.BlockSpec((TOP_K, n), lambda i: (0, jnp.minimum(i + 1, steps - 1)), memory_space=pltpu.SMEM),
                  pl.BlockSpec(memory_space=pl.ANY),
                  pl.BlockSpec((n, LANES), lambda i: (i, 0)),
                  pl.BlockSpec((n, D), lambda i: (i, 0)),
                  pl.BlockSpec((1, 6, D), lambda i: (i // per_b, 0, 0)),
                  pl.BlockSpec((1, D), lambda i: (0, 0))],
        out_specs=pl.BlockSpec((n, D), lambda i: (i, 0)),
        out_shape=jax.ShapeDtypeStruct((T, D), F32),
        scratch_shapes=[pltpu.VMEM((2, TOP_K, n, 1, D), F32), pltpu.VMEM((n, D), F32),
                        pltpu.SemaphoreType.DMA((2,))],
        compiler_params=_cparams(("arbitrary",)),
        name="combine",
    )(dest, dest, y, wrow, x1, mod3, g_post_ffn.reshape(1, D))


def _regroup_w_in(w_in):
    D = w_in.shape[0]
    dw = DN_HEADS * HEAD_DIM
    mw = MB_HEADS * HEAD_DIM
    n_dn, n_small, n_mb = 4 * dw, 2 * DN_HEADS, 3 * mw
    dn, small, mb, gates = jnp.split(w_in, [n_dn, n_dn + n_small, n_dn + n_small + n_mb], axis=1)
    small = jnp.concatenate([small, jnp.zeros((D, LANES - n_small), w_in.dtype)], axis=1).astype(BF16)
    return jnp.concatenate([p.astype(BF16) for p in (gates, dn, mb)], axis=1), small


def kernel(x, c, w_ada, b_ada, g_pre_mix, g_post_mix, g_pre_ffn, g_post_ffn, w_in, conv_w, a_log, dt_bias,
           g_dn_out, w_br_a, w_br_b, w_o, w_router, b_router, w_gu, b_gu, w_down, b_down):
    B, S, D = x.shape
    l = 0
    mod = _adaln(c, w_ada[l], b_ada[l]).reshape(B, 6, D)
    proj, small = _inproj(x, mod, g_pre_mix[l], *_regroup_w_in(w_in[l]))
    qr, kr, vb, kmean = _mobaprep(proj, B, S)
    ob = _moba(qr, kr, vb, kmean, B, S)
    oa = _deltanet(proj, small, conv_w[l], a_log[l], dt_bias[l], g_dn_out[l], B, S)
    x1, hp, idx8, wrow, rank8, cnt = _merge(oa, ob, proj, x.reshape(B * S, D), mod, g_post_mix[l], g_pre_ffn[l],
                                            w_br_a[l], w_br_b[l], w_o[l], w_router[l], b_router[l], S)
    out = _moe(x1, hp, idx8, wrow, rank8, cnt, mod, g_post_ffn[l], w_gu[l], b_gu[l], w_down[l], b_down[l], S)
    return out.reshape(B, S, D)


def _moe(x1, hp, idx8, wrow, rank8, cnt, mod, g_post_ffn, w_gu, b_gu, w_down, b_down, S):
    T = x1.shape[0]
    counts = cnt[:, 0].astype(jnp.int32)
    start, blk_exp, blk_valid, blk_new, blk_fill = _block_tables(counts, T * TOP_K)
    hot = idx8[:TOP_K, :, None] == jnp.arange(N_EXPERTS, dtype=jnp.int32)
    dest = rank8[:TOP_K] + jnp.sum(jnp.where(hot, start, 0), axis=-1)
    xs = _dispatch(dest, blk_fill, hp)
    y = _experts(xs, blk_exp, blk_valid, blk_new, w_gu, b_gu, w_down, b_down)
    return _combine(dest, y, wrow, x1, mod, g_post_ffn, S)
```
